```python
import jax, jax.numpy as jnp
from jax import lax
import numpy as np

D_MODEL = 1024
BATCH = 8
SEQ = 4096
DEPTH = 1

POOL_WINDOWS = (2, 4, 8, 16)
POOL_WIDTH = D_MODEL // 4
POOL_GROUP = POOL_WIDTH // len(POOL_WINDOWS)
HEAD_DIM = 64
FOX_WIDTH = D_MODEL // 2
FOX_HEADS = FOX_WIDTH // HEAD_DIM
MEM_WIDTH = D_MODEL // 4
MEM_HEADS = 4
MEM_HEAD_DIM = MEM_WIDTH // MEM_HEADS
MEM_LEN = 256
MIX_WIDTH = POOL_WIDTH + FOX_WIDTH + MEM_WIDTH
IN_COLS = POOL_WIDTH + 3 * FOX_WIDTH + FOX_HEADS + MEM_WIDTH
Q_BLOCK = 128
N_EXPERTS = 256
TOP_K = 8
N_GROUPS = 8
TOPK_GROUPS = 4
EXPERT_HIDDEN = D_MODEL // 4
SHARED_HIDDEN = D_MODEL // 4
ROUTED_SCALE = 2.5
EXPERT_BLOCK = 128
ALPHA = (2 * DEPTH) ** 0.25
BETA = (8 * DEPTH) ** -0.25
LN_EPS = 1e-5

kernel_name = "hybrid_pool_fox_memory_moe_deepnorm"


def layer_norm(x, g, b):
    xf = x.astype(jnp.float32)
    mu = jnp.mean(xf, axis=-1, keepdims=True)
    var = jnp.mean(jnp.square(xf - mu), axis=-1, keepdims=True)
    y = (xf - mu) * lax.rsqrt(var + LN_EPS)
    return (y * g + b).astype(x.dtype)


def pool_mixer(u, w_pool, pool_scale):
    B, S, _ = u.shape
    ug = u.reshape(B, S, len(POOL_WINDOWS), POOL_GROUP).astype(jnp.float32)
    cs = jnp.cumsum(ug, axis=1)
    pos = jnp.arange(S)
    outs = []
    for g, w in enumerate(POOL_WINDOWS):
        c = cs[:, :, g]
        lag = jnp.pad(c[:, :-w], ((0, 0), (w, 0), (0, 0)))
        cnt = jnp.minimum(pos + 1, w).astype(jnp.float32)[None, :, None]
        outs.append((c - lag) / cnt - ug[:, :, g])
    d = jnp.stack(outs, axis=2).astype(u.dtype)
    y = jnp.einsum('bsgc,gcd->bsgd', d, w_pool)
    return y.reshape(B, S, POOL_WIDTH) * pool_scale


def forgetting_attention(q, k, v, f_logit):
    B, S, H, hd = q.shape
    logf = jax.nn.log_sigmoid(f_logit.astype(jnp.float32))
    F = jnp.cumsum(logf, axis=1)
    Fk = jnp.transpose(F, (0, 2, 1))[:, :, None, :]
    nb = S // Q_BLOCK
    qb = jnp.transpose(q.reshape(B, nb, Q_BLOCK, H, hd), (1, 0, 2, 3, 4))
    Fb = jnp.transpose(F.reshape(B, nb, Q_BLOCK, H), (1, 0, 3, 2))
    kpos = jnp.arange(S)
    scale = hd ** -0.5

    def block(args):
        i, q_i, F_i = args
        s = jnp.einsum('bqhd,bkhd->bhqk', q_i, k).astype(jnp.float32) * scale
        s = s + F_i[..., None] - Fk
        qpos = i * Q_BLOCK + jnp.arange(Q_BLOCK)
        s = jnp.where(kpos[None, :] <= qpos[:, None], s, -jnp.inf)
        p = jax.nn.softmax(s, axis=-1)
        return jnp.einsum('bhqk,bkhd->bqhd', p.astype(v.dtype), v)

    out = lax.map(block, (jnp.arange(nb, dtype=jnp.int32), qb, Fb))
    return jnp.transpose(out, (1, 0, 2, 3, 4)).reshape(B, S, H * hd)


def memory_attention(q, mem, w_mem_kv):
    B, S, _ = q.shape
    kv = mem @ w_mem_kv
    k, v = jnp.split(kv, 2, axis=-1)
    M = mem.shape[1]
    qh = q.reshape(B, S, MEM_HEADS, MEM_HEAD_DIM)
    kh = k.reshape(B, M, MEM_HEADS, MEM_HEAD_DIM)
    vh = v.reshape(B, M, MEM_HEADS, MEM_HEAD_DIM)
    s = jnp.einsum('bshd,bmhd->bhsm', qh, kh).astype(jnp.float32) * (MEM_HEAD_DIM ** -0.5)
    p = jax.nn.softmax(s, axis=-1)
    o = jnp.einsum('bhsm,bmhd->bshd', p.astype(vh.dtype), vh)
    return o.reshape(B, S, MEM_WIDTH)


def token_mixer(x, mem, w_in, b_f, w_pool, pool_scale, w_mem_kv, w_out):
    B, S, _ = x.shape
    proj = x @ w_in
    P, Fw = POOL_WIDTH, FOX_WIDTH
    u_pool, q, k, v, f_logit, q_mem = jnp.split(
        proj, [P, P + Fw, P + 2 * Fw, P + 3 * Fw, P + 3 * Fw + FOX_HEADS], axis=-1)
    y_pool = pool_mixer(u_pool, w_pool, pool_scale)
    shp = (B, S, FOX_HEADS, HEAD_DIM)
    y_fox = forgetting_attention(q.reshape(shp), k.reshape(shp), v.reshape(shp), f_logit + b_f)
    y_mem = memory_attention(q_mem, mem, w_mem_kv)
    return jnp.concatenate([y_pool, y_fox, y_mem], axis=-1) @ w_out


def swiglu(x, wg, wu, wd):
    return (jax.nn.silu(x @ wg) * (x @ wu)) @ wd


def routed_experts(xt, eidx, gates, w_gate, w_up, w_down):
    T, D = xt.shape
    A = T * TOP_K
    flat_e = eidx.reshape(A)
    flat_tok = jnp.arange(A, dtype=jnp.int32) // TOP_K
    flat_g = gates.reshape(A)
    order = jnp.argsort(flat_e)
    se, stok, sg = flat_e[order], flat_tok[order], flat_g[order]
    counts = jnp.bincount(flat_e, length=N_EXPERTS)
    padded = (counts + EXPERT_BLOCK - 1) // EXPERT_BLOCK * EXPERT_BLOCK
    start_sorted = jnp.cumsum(counts) - counts
    pad_end = jnp.cumsum(padded)
    start_pad = pad_end - padded
    dest = start_pad[se] + (jnp.arange(A, dtype=jnp.int32) - start_sorted[se])
    n_rows = EXPERT_BLOCK * (-(-(A + N_EXPERTS * (EXPERT_BLOCK - 1)) // EXPERT_BLOCK))
    n_blk = n_rows // EXPERT_BLOCK
    row_tok = jnp.full((n_rows,), T, jnp.int32).at[dest].set(stok)
    row_g = jnp.zeros((n_rows,), gates.dtype).at[dest].set(sg)
    blk_e = jnp.minimum(
        jnp.searchsorted(pad_end, jnp.arange(n_blk, dtype=jnp.int32) * EXPERT_BLOCK, side='right'),
        N_EXPERTS - 1)
    x_pad = jnp.concatenate([xt, jnp.zeros((1, D), xt.dtype)], axis=0)

    def step(acc, args):
        e, tok, g = args
        xb = x_pad[tok]
        h = jax.nn.silu(xb @ w_gate[e]) * (xb @ w_up[e])
        y = ((h @ w_down[e]) * g[:, None]).astype(acc.dtype)
        return acc.at[tok].add(y), None

    acc0 = jnp.zeros((T + 1, D), xt.dtype)
    acc, _ = lax.scan(step, acc0, (blk_e, row_tok.reshape(n_blk, EXPERT_BLOCK),
                                   row_g.reshape(n_blk, EXPERT_BLOCK)))
    return acc[:T]


def moe(x, w_router, router_bias, w_gate, w_up, w_down, ws_gate, ws_up, ws_down):
    B, S, D = x.shape
    T = B * S
    xt = x.reshape(T, D)
    scores = jax.nn.sigmoid((xt @ w_router).astype(jnp.float32))
    biased = scores + router_bias.astype(jnp.float32)
    grp = biased.reshape(T, N_GROUPS, N_EXPERTS // N_GROUPS)
    gscore = lax.top_k(grp, 2)[0].sum(-1)
    _, gidx = lax.top_k(gscore, TOPK_GROUPS)
    gmask = jax.nn.one_hot(gidx, N_GROUPS, dtype=jnp.int32).sum(-2) > 0
    emask = jnp.repeat(gmask, N_EXPERTS // N_GROUPS, axis=-1)
    _, eidx = lax.top_k(jnp.where(emask, biased, -jnp.inf), TOP_K)
    sel = jnp.take_along_axis(scores, eidx, axis=-1)
    gates = sel / jnp.sum(sel, axis=-1, keepdims=True) * ROUTED_SCALE
    routed = routed_experts(xt, eidx, gates, w_gate, w_up, w_down)
    shared = swiglu(xt, ws_gate, ws_up, ws_down)
    return (routed + shared).reshape(B, S, D)


def setup_inputs(seed: int = 0) -> dict:
    key = jax.random.key(seed)
    ks = jax.random.split(key, 20)
    f32 = jnp.float32
    nrm = lambda k, shape, s: jax.random.normal(k, shape, f32) * s
    L, D, E, H = DEPTH, D_MODEL, N_EXPERTS, EXPERT_HIDDEN
    return {
        "x": nrm(ks[0], (BATCH, SEQ, D), 1.0),
        "mem": nrm(ks[1], (BATCH, MEM_LEN, D), 1.0),
        "w_in": nrm(ks[2], (L, D, IN_COLS), D ** -0.5),
        "b_f": 1.0 + 2.0 * jax.random.uniform(ks[3], (L, FOX_HEADS), f32),
        "w_pool": nrm(ks[4], (L, len(POOL_WINDOWS), POOL_GROUP, POOL_GROUP), POOL_GROUP ** -0.5),
        "pool_scale": 1.0 + nrm(ks[5], (L, POOL_WIDTH), 0.02),
        "w_mem_kv": nrm(ks[6], (L, D, 2 * MEM_WIDTH), D ** -0.5),
        "w_out": nrm(ks[7], (L, MIX_WIDTH, D), MIX_WIDTH ** -0.5 * BETA),
        "ln1_g": 1.0 + nrm(ks[8], (L, D), 0.02),
        "ln1_b": nrm(ks[9], (L, D), 0.02),
        "w_router": nrm(ks[10], (L, D, E), D ** -0.5),
        "router_bias": nrm(ks[11], (L, E), 0.01),
        "w_gate": nrm(ks[12], (L, E, D, H), D ** -0.5),
        "w_up": nrm(ks[13], (L, E, D, H), D ** -0.5),
        "w_down": nrm(ks[14], (L, E, H, D), H ** -0.5 * BETA),
        "ws_gate": nrm(ks[15], (L, D, SHARED_HIDDEN), D ** -0.5),
        "ws_up": nrm(ks[16], (L, D, SHARED_HIDDEN), D ** -0.5),
        "ws_down": nrm(ks[17], (L, SHARED_HIDDEN, D), SHARED_HIDDEN ** -0.5 * BETA),
        "ln2_g": 1.0 + nrm(ks[18], (L, D), 0.02),
        "ln2_b": nrm(ks[19], (L, D), 0.02),
    }


def reference(x, mem, w_in, b_f, w_pool, pool_scale, w_mem_kv, w_out, ln1_g, ln1_b,
              w_router, router_bias, w_gate, w_up, w_down, ws_gate, ws_up, ws_down,
              ln2_g, ln2_b):
    for l in range(DEPTH):
        h = token_mixer(x, mem, w_in[l], b_f[l], w_pool[l], pool_scale[l], w_mem_kv[l], w_out[l])
        x = layer_norm(ALPHA * x + h, ln1_g[l], ln1_b[l])
        h = moe(x, w_router[l], router_bias[l], w_gate[l], w_up[l], w_down[l],
                ws_gate[l], ws_up[l], ws_down[l])
        x = layer_norm(ALPHA * x + h, ln2_g[l], ln2_b[l])
    return x
```

```python
import functools

import jax
import jax.numpy as jnp
from jax import lax
from jax.experimental import pallas as pl
from jax.experimental.pallas import tpu as pltpu

F32 = jnp.float32
BF16 = jnp.bfloat16
I32 = jnp.int32

LANES = 128
POOL_WINDOWS = (2, 4, 8, 16)
POOL_HALO = 16
HEAD_DIM = 64
FOX_HEADS = 8
MEM_HEADS = 4
TOP_K = 8
N_GROUPS = 8
TOPK_GROUPS = 4
ROUTED_SCALE = 2.5
LN_EPS = 1e-5
ROW_BLOCK = 256
VMEM_LIMIT = 48 * 1024 * 1024


def _cparams(sem):
    return pltpu.CompilerParams(dimension_semantics=sem, vmem_limit_bytes=VMEM_LIMIT)


def _layer_norm(z, g, b):
    mu = jnp.mean(z, axis=-1, keepdims=True)
    zc = z - mu
    var = jnp.mean(zc * zc, axis=-1, keepdims=True)
    return zc * lax.rsqrt(var + LN_EPS) * g + b


def _silu(x):
    return x * (1.0 / (1.0 + jnp.exp(-x)))


def _inproj_kernel(x_ref, w_ref, wf_ref, up_ref, q_ref, k_ref, v_ref, qm_ref, f_ref, *, pw, fw, mw, scale):
    xb = x_ref[...].astype(BF16)

    def proj(lo, hi):
        return jnp.dot(xb, w_ref[:, lo:hi], preferred_element_type=F32)

    up_ref[...] = proj(0, pw)
    q_ref[...] = (proj(pw, pw + fw) * scale).astype(BF16)
    k_ref[...] = proj(pw + fw, pw + 2 * fw).astype(BF16)
    v_ref[...] = proj(pw + 2 * fw, pw + 3 * fw).astype(BF16)
    qm_ref[...] = (proj(pw + 3 * fw, pw + 3 * fw + mw) * scale).astype(BF16)
    f_ref[...] = jnp.dot(xb, wf_ref[...], preferred_element_type=F32)


def _inproj(x2, w_main, w_f, pw, fw, mw, tm=512):
    T, D = x2.shape
    kern = functools.partial(_inproj_kernel, pw=pw, fw=fw, mw=mw, scale=HEAD_DIM ** -0.5)
    row = lambda i: (i, 0)
    fixed = lambda i: (0, 0)
    return pl.pallas_call(
        kern,
        grid=(T // tm,),
        in_specs=[pl.BlockSpec((tm, D), row),
                  pl.BlockSpec(w_main.shape, fixed),
                  pl.BlockSpec(w_f.shape, fixed)],
        out_specs=[pl.BlockSpec((tm, pw), row), pl.BlockSpec((tm, fw), row), pl.BlockSpec((tm, fw), row),
                   pl.BlockSpec((tm, fw), row), pl.BlockSpec((tm, mw), row), pl.BlockSpec((tm, LANES), row)],
        out_shape=[jax.ShapeDtypeStruct((T, pw), F32), jax.ShapeDtypeStruct((T, fw), BF16),
                   jax.ShapeDtypeStruct((T, fw), BF16), jax.ShapeDtypeStruct((T, fw), BF16),
                   jax.ShapeDtypeStruct((T, mw), BF16), jax.ShapeDtypeStruct((T, LANES), F32)],
        compiler_params=_cparams(("parallel",)),
        name="inproj",
    )(x2, w_main, w_f)


def _pool_kernel(u_ref, wbd_ref, sc_ref, o_ref, ext_sc, *, chunk, group):
    S, W = u_ref.shape
    ext_sc[0:POOL_HALO, :] = jnp.zeros((POOL_HALO, W), F32)
    ext_sc[POOL_HALO:, :] = u_ref[...]
    rows = chunk + POOL_HALO
    lrow = lax.broadcasted_iota(I32, (rows, W), 0)
    lane = lax.broadcasted_iota(I32, (rows, W), 1)

    def body(c, carry):
        start = pl.multiple_of(c * chunk, chunk)
        e = ext_sc[pl.ds(start, rows), :]
        posf = (lrow + (start - POOL_HALO + 1)).astype(F32)
        acc = e
        d = jnp.zeros_like(e)
        shift = 1
        for g, w in enumerate(POOL_WINDOWS):
            while shift < w:
                acc = acc + pltpu.roll(acc, shift, axis=0)
                shift *= 2
            mean = acc / jnp.minimum(posf, float(w))
            d = jnp.where((lane >= g * group) & (lane < (g + 1) * group), mean, d)
        d = (d - e)[POOL_HALO:, :]
        y = jnp.dot(d.astype(BF16), wbd_ref[...], preferred_element_type=F32) * sc_ref[...]
        o_ref[pl.ds(start, chunk), :] = y.astype(BF16)
        return carry

    lax.fori_loop(0, S // chunk, body, 0)


def _pool(u, wbd, pscale, B, S, chunk=512):
    T, W = u.shape
    kern = functools.partial(_pool_kernel, chunk=chunk, group=W // len(POOL_WINDOWS))
    return pl.pallas_call(
        kern,
        grid=(B,),
        in_specs=[pl.BlockSpec((S, W), lambda b: (b, 0)),
                  pl.BlockSpec((W, W), lambda b: (0, 0)),
                  pl.BlockSpec((1, W), lambda b: (0, 0))],
        out_specs=pl.BlockSpec((S, W), lambda b: (b, 0)),
        out_shape=jax.ShapeDtypeStruct((T, W), BF16),
        scratch_shapes=[pltpu.VMEM((S + POOL_HALO, W), F32)],
        compiler_params=_cparams(("parallel",)),
        name="pool_mixer",
    )(u, wbd, pscale)


def _fgate_kernel(f_ref, bf_ref, o_ref):
    S = f_ref.shape[0]
    z = f_ref[...] + bf_ref[...]
    logf = jnp.minimum(z, 0.0) - jnp.log(1.0 + jnp.exp(-jnp.abs(z)))
    x = logf.T[0:FOX_HEADS, :]
    lane = lax.broadcasted_iota(I32, x.shape, 1)
    shift = 1
    while shift < S:
        x = x + jnp.where(lane >= shift, pltpu.roll(x, shift, axis=1), 0.0)
        shift *= 2
    for p in range(FOX_HEADS // 2):
        o_ref[0, p] = x[2 * p:2 * p + 2, :]


def _fgate(f_logit, bf_pad, B, S):
    return pl.pallas_call(
        _fgate_kernel,
        grid=(B,),
        in_specs=[pl.BlockSpec((S, LANES), lambda b: (b, 0)),
                  pl.BlockSpec((1, LANES), lambda b: (0, 0))],
        out_specs=pl.BlockSpec((1, FOX_HEADS // 2, 2, S), lambda b: (b, 0, 0, 0)),
        out_shape=jax.ShapeDtypeStruct((B, FOX_HEADS // 2, 2, S), F32),
        compiler_params=_cparams(("parallel",)),
        name="forget_cumsum",
    )(f_logit, bf_pad)


def _fox_kernel(q_ref, k_ref, v_ref, f_ref, o_ref, m_sc, l_sc, acc_sc, *, blk):
    qi = pl.program_id(2)
    q = q_ref[...]
    lane = lax.broadcasted_iota(I32, q.shape, 1)
    first = lane < HEAD_DIM
    zero = jnp.zeros_like(q)
    qh = (jnp.where(first, q, zero), jnp.where(first, zero, q))
    m_sc[...] = jnp.full(m_sc.shape, -jnp.inf, F32)
    l_sc[...] = jnp.zeros(l_sc.shape, F32)
    acc_sc[...] = jnp.zeros(acc_sc.shape, F32)

    def step(ki, causal):
        ks = pl.multiple_of(ki * blk, blk)
        kb = k_ref[pl.ds(ks, blk), :]
        vb = v_ref[pl.ds(ks, blk), :]
        fb = f_ref[0, 0, :, pl.ds(ks, blk)]
        alphas, pvs = [], []
        for h in range(2):
            s = lax.dot_general(qh[h], kb, (((1,), (1,)), ((), ())), preferred_element_type=F32)
            s = s - fb[h:h + 1, :]
            if causal:
                r = lax.broadcasted_iota(I32, s.shape, 0)
                c = lax.broadcasted_iota(I32, s.shape, 1)
                s = jnp.where(c <= r, s, -jnp.inf)
            m_prev = m_sc[h]
            m_new = jnp.maximum(m_prev, jnp.max(s, axis=1, keepdims=True))
            alpha = jnp.exp(m_prev - m_new)
            p = jnp.exp(s - m_new)
            l_sc[h] = alpha * l_sc[h] + jnp.sum(p, axis=1, keepdims=True)
            m_sc[h] = m_new
            alphas.append(alpha)
            pvs.append(jnp.dot(p.astype(BF16), vb, preferred_element_type=F32))
        acc_sc[...] = acc_sc[...] * jnp.where(first, alphas[0], alphas[1]) + jnp.where(first, pvs[0], pvs[1])

    def body(ki, carry):
        step(ki, False)
        return carry

    lax.fori_loop(0, qi, body, 0)
    step(qi, True)
    o_ref[...] = (acc_sc[...] / jnp.where(first, l_sc[0], l_sc[1])).astype(BF16)


def _fox(q, k, v, fcum, B, S, blk=512):
    T, FW = q.shape
    nq = S // blk
    pairs = FW // LANES
    kern = functools.partial(_fox_kernel, blk=blk)
    return pl.pallas_call(
        kern,
        grid=(B, pairs, nq),
        in_specs=[pl.BlockSpec((blk, LANES), lambda b, p, i: (b * nq + i, p)),
                  pl.BlockSpec((S, LANES), lambda b, p, i: (b, p)),
                  pl.BlockSpec((S, LANES), lambda b, p, i: (b, p)),
                  pl.BlockSpec((1, 1, 2, S), lambda b, p, i: (b, p, 0, 0))],
        out_specs=pl.BlockSpec((blk, LANES), lambda b, p, i: (b * nq + i, p)),
        out_shape=jax.ShapeDtypeStruct((T, FW), BF16),
        scratch_shapes=[pltpu.VMEM((2, blk, 1), F32), pltpu.VMEM((2, blk, 1), F32),
                        pltpu.VMEM((blk, LANES), F32)],
        compiler_params=_cparams(("parallel", "parallel", "parallel")),
        name="fox_attention",
    )(q, k, v, fcum)


def _memattn_kernel(qm_ref, mem_ref, wkv_ref, o_ref, k_sc, v_sc):
    MW = qm_ref.shape[1]

    @pl.when(pl.program_id(1) == 0)
    def _():
        kv = jnp.dot(mem_ref[0].astype(BF16), wkv_ref[...], preferred_element_type=F32)
        k_sc[...] = kv[:, :MW].astype(BF16)
        v_sc[...] = kv[:, MW:].astype(BF16)

    q = qm_ref[...]
    lane = lax.broadcasted_iota(I32, q.shape, 1)
    hd = MW // MEM_HEADS
    out = jnp.zeros(q.shape, F32)
    for h in range(MEM_HEADS):
        mine = (lane >= h * hd) & (lane < (h + 1) * hd)
        qh = jnp.where(mine, q, jnp.zeros_like(q))
        s = lax.dot_general(qh, k_sc[...], (((1,), (1,)), ((), ())), preferred_element_type=F32)
        p = jnp.exp(s - jnp.max(s, axis=1, keepdims=True))
        l = jnp.sum(p, axis=1, keepdims=True)
        o = jnp.dot(p.astype(BF16), v_sc[...], preferred_element_type=F32)
        out = jnp.where(mine, o / l, out)
    o_ref[...] = out.astype(BF16)


def _memattn(qm, mem, wkv, B, S, tq=512):
    T, MW = qm.shape
    M, D = mem.shape[1], mem.shape[2]
    nq = S // tq
    return pl.pallas_call(
        _memattn_kernel,
        grid=(B, nq),
        in_specs=[pl.BlockSpec((tq, MW), lambda b, i: (b * nq + i, 0)),
                  pl.BlockSpec((1, M, D), lambda b, i: (b, 0, 0)),
                  pl.BlockSpec(wkv.shape, lambda b, i: (0, 0))],
        out_specs=pl.BlockSpec((tq, MW), lambda b, i: (b * nq + i, 0)),
        out_shape=jax.ShapeDtypeStruct((T, MW), BF16),
        scratch_shapes=[pltpu.VMEM((M, MW), BF16), pltpu.VMEM((M, MW), BF16)],
        compiler_params=_cparams(("parallel", "arbitrary")),
        name="memory_attention",
    )(qm, mem, wkv)


def _outproj_kernel(yp_ref, yf_ref, ym_ref, x_ref, w_ref, g_ref, b_ref, o_ref, *, alpha):
    pw, fw = yp_ref.shape[1], yf_ref.shape[1]
    h = jnp.dot(yp_ref[...], w_ref[0:pw, :], preferred_element_type=F32)
    h = h + jnp.dot(yf_ref[...], w_ref[pw:pw + fw, :], preferred_element_type=F32)
    h = h + jnp.dot(ym_ref[...], w_ref[pw + fw:, :], preferred_element_type=F32)
    o_ref[...] = _layer_norm(alpha * x_ref[...] + h, g_ref[...], b_ref[...])


def _outproj(yp, yf, ym, x2, w_out, g, b, alpha, tm=512):
    T, D = x2.shape
    row = lambda i: (i, 0)
    fixed = lambda i: (0, 0)
    return pl.pallas_call(
        functools.partial(_outproj_kernel, alpha=alpha),
        grid=(T // tm,),
        in_specs=[pl.BlockSpec((tm, yp.shape[1]), row), pl.BlockSpec((tm, yf.shape[1]), row),
                  pl.BlockSpec((tm, ym.shape[1]), row), pl.BlockSpec((tm, D), row),
                  pl.BlockSpec(w_out.shape, fixed), pl.BlockSpec((1, D), fixed), pl.BlockSpec((1, D), fixed)],
        out_specs=pl.BlockSpec((tm, D), row),
        out_shape=jax.ShapeDtypeStruct((T, D), F32),
        compiler_params=_cparams(("parallel",)),
        name="outproj_ln1",
    )(yp, yf, ym, x2, w_out, g, b)


def _router_kernel(x_ref, wh_ref, wl_ref, bias_ref, eidx_ref, gate_ref, rank_ref, cnt_ref, carry_sc):
    tt = x_ref.shape[0]
    E = wh_ref.shape[1]
    gsz = E // N_GROUPS

    @pl.when(pl.program_id(0) == 0)
    def _():
        carry_sc[...] = jnp.zeros(carry_sc.shape, F32)

    x = x_ref[...]
    xh = x.astype(BF16)
    xl = (x - xh.astype(F32)).astype(BF16)
    wh = wh_ref[...]
    logits = jnp.dot(xh, wh, preferred_element_type=F32) + (
        jnp.dot(xh, wl_ref[...], preferred_element_type=F32) + jnp.dot(xl, wh, preferred_element_type=F32))
    scores = 1.0 / (1.0 + jnp.exp(-logits))
    biased = scores + bias_ref[...]
    lane = lax.broadcasted_iota(I32, (tt, E), 1)
    ninf = jnp.full((tt, E), -jnp.inf, F32)

    def rmax(a):
        return jnp.max(a, axis=1, keepdims=True)

    def first_at(a, m):
        return jnp.min(jnp.where(a == m, lane, E), axis=1, keepdims=True)

    gscore = []
    for g in range(N_GROUPS):
        mg = jnp.where((lane >= g * gsz) & (lane < (g + 1) * gsz), biased, ninf)
        m1 = rmax(mg)
        m2 = rmax(jnp.where(lane == first_at(mg, m1), ninf, mg))
        gscore.append(m1 + m2)

    masked = ninf
    for g in range(N_GROUPS):
        ahead = jnp.zeros((tt, 1), I32)
        for o in range(N_GROUPS):
            if o == g:
                continue
            beats = gscore[o] > gscore[g]
            if o < g:
                beats = beats | (gscore[o] == gscore[g])
            ahead = ahead + beats.astype(I32)
        keep = (ahead < TOPK_GROUPS) & (lane >= g * gsz) & (lane < (g + 1) * gsz)
        masked = jnp.where(keep, biased, masked)

    picks, sels = [], []
    chosen = jnp.zeros((tt, E), F32)
    for _ in range(TOP_K):
        ik = first_at(masked, rmax(masked))
        oh = lane == ik
        picks.append(ik)
        sels.append(jnp.sum(jnp.where(oh, scores, 0.0), axis=1, keepdims=True))
        chosen = jnp.where(oh, 1.0, chosen)
        masked = jnp.where(oh, ninf, masked)
    denom = sels[0]
    for sk in sels[1:]:
        denom = denom + sk

    r = lax.broadcasted_iota(I32, (tt, tt), 0)
    c = lax.broadcasted_iota(I32, (tt, tt), 1)
    lower = jnp.where(c < r, 1.0, 0.0).astype(BF16)
    pos = carry_sc[...] + jnp.dot(lower, chosen.astype(BF16), preferred_element_type=F32)
    carry_sc[...] = carry_sc[...] + jnp.sum(chosen, axis=0, keepdims=True)
    cnt_ref[...] = carry_sc[...]

    lane_o = lax.broadcasted_iota(I32, (tt, LANES), 1)
    e_out = jnp.zeros((tt, LANES), I32)
    g_out = jnp.zeros((tt, LANES), F32)
    r_out = jnp.zeros((tt, LANES), I32)
    for kk in range(TOP_K):
        rk = jnp.sum(jnp.where(lane == picks[kk], pos, 0.0), axis=1, keepdims=True)
        here = lane_o == kk
        e_out = jnp.where(here, picks[kk], e_out)
        g_out = jnp.where(here, sels[kk] / denom * ROUTED_SCALE, g_out)
        r_out = jnp.where(here, rk.astype(I32), r_out)
    eidx_ref[...] = e_out
    gate_ref[...] = g_out
    rank_ref[...] = r_out


def _router(x1, wr_hi, wr_lo, rbias, tt=256):
    T, D = x1.shape
    E = wr_hi.shape[1]
    row = lambda i: (i, 0)
    fixed = lambda i: (0, 0)
    return pl.pallas_call(
        _router_kernel,
        grid=(T // tt,),
        in_specs=[pl.BlockSpec((tt, D), row), pl.BlockSpec((D, E), fixed), pl.BlockSpec((D, E), fixed),
                  pl.BlockSpec((1, E), fixed)],
        out_specs=[pl.BlockSpec((tt, LANES), row), pl.BlockSpec((tt, LANES), row), pl.BlockSpec((tt, LANES), row),
                   pl.BlockSpec((1, E), fixed)],
        out_shape=[jax.ShapeDtypeStruct((T, LANES), I32), jax.ShapeDtypeStruct((T, LANES), F32),
                   jax.ShapeDtypeStruct((T, LANES), I32), jax.ShapeDtypeStruct((1, E), F32)],
        scratch_shapes=[pltpu.VMEM((1, E), F32)],
        compiler_params=_cparams(("arbitrary",)),
        name="router_topk",
    )(x1, wr_hi, wr_lo, rbias)


def _row_copy(src, dst, sem):
    return pltpu.make_async_copy(src, dst, sem)


def _dispatch_kernel(sp_ref, zb_ref, e_ref, r_ref, x_ref, xs_ref, zero_sc, sem, *, tt, n_exp):
    @pl.when(pl.program_id(0) == 0)
    def _():
        zero_sc[...] = jnp.zeros(zero_sc.shape, F32)

        def zstart(e, c):
            @pl.when(zb_ref[e] >= 0)
            def _():
                dst = xs_ref.at[pl.ds(pl.multiple_of(zb_ref[e], ROW_BLOCK), ROW_BLOCK)]
                _row_copy(zero_sc, dst, sem).start()
            return c

        def zwait(e, c):
            @pl.when(zb_ref[e] >= 0)
            def _():
                _row_copy(zero_sc, xs_ref.at[pl.ds(0, ROW_BLOCK)], sem).wait()
            return c

        lax.fori_loop(0, n_exp, zstart, 0)
        lax.fori_loop(0, n_exp, zwait, 0)

    def start(j, c):
        for kk in range(TOP_K):
            a = j * TOP_K + kk
            d = sp_ref[e_ref[a]] + r_ref[a]
            _row_copy(x_ref.at[pl.ds(j, 1)], xs_ref.at[pl.ds(d, 1)], sem).start()
        return c

    def wait(j, c):
        for kk in range(TOP_K):
            _row_copy(x_ref.at[pl.ds(0, 1)], xs_ref.at[pl.ds(0, 1)], sem).wait()
        return c

    lax.fori_loop(0, tt, start, 0)
    lax.fori_loop(0, tt, wait, 0)


def _dispatch(start_pad, zero_blk, eflat, rflat, x1, n_rows, tt=256):
    T, D = x1.shape
    n_exp = start_pad.shape[0]
    grid_spec = pltpu.PrefetchScalarGridSpec(
        num_scalar_prefetch=2,
        grid=(T // tt,),
        in_specs=[pl.BlockSpec((tt * TOP_K,), lambda i, sp, zb: (i,), memory_space=pltpu.SMEM),
                  pl.BlockSpec((tt * TOP_K,), lambda i, sp, zb: (i,), memory_space=pltpu.SMEM),
                  pl.BlockSpec((tt, D), lambda i, sp, zb: (i, 0))],
        out_specs=pl.BlockSpec(memory_space=pl.ANY),
        scratch_shapes=[pltpu.VMEM((ROW_BLOCK, D), F32), pltpu.SemaphoreType.DMA],
    )
    return pl.pallas_call(
        functools.partial(_dispatch_kernel, tt=tt, n_exp=n_exp),
        grid_spec=grid_spec,
        out_shape=jax.ShapeDtypeStruct((n_rows, D), F32),
        compiler_params=_cparams(("arbitrary",)),
        name="moe_dispatch",
    )(start_pad, zero_blk, eflat, rflat, x1)


def _expert_kernel(be_ref, nu_ref, xs_ref, wg_ref, wu_ref, wd_ref, y_ref, wg_sc, wu_sc, wd_sc):
    i = pl.program_id(0)

    @pl.when(i < nu_ref[0])
    def _():
        e = be_ref[i]
        prev = be_ref[jnp.maximum(i - 1, 0)]

        @pl.when((i == 0) | (e != prev))
        def _():
            wg_sc[...] = wg_ref[0].astype(BF16)
            wu_sc[...] = wu_ref[0].astype(BF16)
            wd_sc[...] = wd_ref[0].astype(BF16)

        xb = xs_ref[...].astype(BF16)
        g = jnp.dot(xb, wg_sc[...], preferred_element_type=F32)
        u = jnp.dot(xb, wu_sc[...], preferred_element_type=F32)
        h = (_silu(g) * u).astype(BF16)
        y_ref[...] = jnp.dot(h, wd_sc[...], preferred_element_type=F32)


def _experts(blk_e, n_used, xs, w_gate, w_up, w_down):
    n_rows, D = xs.shape
    E, _, H = w_gate.shape
    n_blk = n_rows // ROW_BLOCK
    rows = lambda i, be, nu: (jnp.minimum(i, nu[0] - 1), 0)
    wsel = lambda i, be, nu: (be[i], 0, 0)
    grid_spec = pltpu.PrefetchScalarGridSpec(
        num_scalar_prefetch=2,
        grid=(n_blk,),
        in_specs=[pl.BlockSpec((ROW_BLOCK, D), rows),
                  pl.BlockSpec((1, D, H), wsel), pl.BlockSpec((1, D, H), wsel), pl.BlockSpec((1, H, D), wsel)],
        out_specs=pl.BlockSpec((ROW_BLOCK, D), rows),
        scratch_shapes=[pltpu.VMEM((D, H), BF16), pltpu.VMEM((D, H), BF16), pltpu.VMEM((H, D), BF16)],
    )
    return pl.pallas_call(
        _expert_kernel,
        grid_spec=grid_spec,
        out_shape=jax.ShapeDtypeStruct((n_rows, D), F32),
        compiler_params=_cparams(("arbitrary",)),
        name="moe_experts",
    )(blk_e, n_used, xs, w_gate, w_up, w_down)


def _combine_kernel(sp_ref, e_ref, r_ref, x_ref, gate_ref, wsg_ref, wsu_ref, wsd_ref, g_ref, b_ref, y_ref,
                    o_ref, buf, sem, *, tt, alpha):
    def start(j, c):
        for kk in range(TOP_K):
            a = j * TOP_K + kk
            d = sp_ref[e_ref[a]] + r_ref[a]
            _row_copy(y_ref.at[pl.ds(d, 1)], buf.at[kk, pl.ds(j, 1)], sem).start()
        return c

    def wait(j, c):
        for kk in range(TOP_K):
            _row_copy(y_ref.at[pl.ds(0, 1)], buf.at[0, pl.ds(0, 1)], sem).wait()
        return c

    lax.fori_loop(0, tt, start, 0)
    x = x_ref[...]
    xb = x.astype(BF16)
    hs = _silu(jnp.dot(xb, wsg_ref[...], preferred_element_type=F32)) * jnp.dot(
        xb, wsu_ref[...], preferred_element_type=F32)
    moe = jnp.dot(hs.astype(BF16), wsd_ref[...], preferred_element_type=F32)
    lax.fori_loop(0, tt, wait, 0)
    gates = gate_ref[...]
    for kk in range(TOP_K):
        moe = moe + gates[:, kk:kk + 1] * buf[kk]
    o_ref[...] = _layer_norm(alpha * x + moe, g_ref[...], b_ref[...])


def _combine(start_pad, eflat, rflat, x1, gates, wsg, wsu, wsd, g, b, y, alpha, tt=128):
    T, D = x1.shape
    row = lambda i, sp: (i, 0)
    fixed = lambda i, sp: (0, 0)
    grid_spec = pltpu.PrefetchScalarGridSpec(
        num_scalar_prefetch=1,
        grid=(T // tt,),
        in_specs=[pl.BlockSpec((tt * TOP_K,), lambda i, sp: (i,), memory_space=pltpu.SMEM),
                  pl.BlockSpec((tt * TOP_K,), lambda i, sp: (i,), memory_space=pltpu.SMEM),
                  pl.BlockSpec((tt, D), row), pl.BlockSpec((tt, LANES), row),
                  pl.BlockSpec(wsg.shape, fixed), pl.BlockSpec(wsu.shape, fixed), pl.BlockSpec(wsd.shape, fixed),
                  pl.BlockSpec((1, D), fixed), pl.BlockSpec((1, D), fixed),
                  pl.BlockSpec(memory_space=pl.ANY)],
        out_specs=pl.BlockSpec((tt, D), row),
        scratch_shapes=[pltpu.VMEM((TOP_K, tt, D), F32), pltpu.SemaphoreType.DMA],
    )
    return pl.pallas_call(
        functools.partial(_combine_kernel, tt=tt, alpha=alpha),
        grid_spec=grid_spec,
        out_shape=jax.ShapeDtypeStruct((T, D), F32),
        compiler_params=_cparams(("arbitrary",)),
        name="moe_combine_ln2",
    )(start_pad, eflat, rflat, x1, gates, wsg, wsu, wsd, g, b, y)


def _layer(x2, mem, B, S, depth, w_in, b_f, w_pool, pool_scale, w_mem_kv, w_out, ln1_g, ln1_b,
           w_router, router_bias, w_gate, w_up, w_down, ws_gate, ws_up, ws_down, ln2_g, ln2_b):
    T, D = x2.shape
    n_win, grp = w_pool.shape[0], w_pool.shape[1]
    pw = n_win * grp
    fw = FOX_HEADS * HEAD_DIM
    mw = w_mem_kv.shape[1] // 2
    E = w_router.shape[1]
    alpha = (2 * depth) ** 0.25

    f_lo = pw + 3 * fw
    w_main = jnp.concatenate([w_in[:, :f_lo], w_in[:, f_lo + FOX_HEADS:]], axis=1).astype(BF16)
    w_f = jnp.pad(w_in[:, f_lo:f_lo + FOX_HEADS], ((0, 0), (0, LANES - FOX_HEADS))).astype(BF16)
    bf_pad = jnp.pad(b_f, (0, LANES - FOX_HEADS)).reshape(1, LANES)
    wbd = jnp.zeros((pw, pw), F32)
    for g in range(n_win):
        wbd = wbd.at[g * grp:(g + 1) * grp, g * grp:(g + 1) * grp].set(w_pool[g])
    wr_hi = w_router.astype(BF16)
    wr_lo = (w_router - wr_hi.astype(F32)).astype(BF16)

    u_pool, q, k, v, q_mem, f_logit = _inproj(x2, w_main, w_f, pw, fw, mw)
    y_pool = _pool(u_pool, wbd.astype(BF16), pool_scale.reshape(1, pw), B, S)
    fcum = _fgate(f_logit, bf_pad, B, S)
    y_fox = _fox(q, k, v, fcum, B, S)
    y_mem = _memattn(q_mem, mem, w_mem_kv.astype(BF16), B, S)
    x1 = _outproj(y_pool, y_fox, y_mem, x2, w_out.astype(BF16), ln1_g.reshape(1, D), ln1_b.reshape(1, D), alpha)

    eidx, gates, rank, counts = _router(x1, wr_hi, wr_lo, router_bias.reshape(1, E))

    cnt = counts.reshape(E).astype(I32)
    padded = (cnt + ROW_BLOCK - 1) // ROW_BLOCK * ROW_BLOCK
    pad_end = jnp.cumsum(padded)
    start_pad = (pad_end - padded).astype(I32)
    n_rows = (T * TOP_K + E * (ROW_BLOCK - 1)) // ROW_BLOCK * ROW_BLOCK
    n_blk = n_rows // ROW_BLOCK
    n_used = (pad_end[-1] // ROW_BLOCK).astype(I32)
    blk_ids = jnp.minimum(jnp.arange(n_blk, dtype=I32), n_used - 1)
    blk_e = jnp.sum((pad_end[None, :] <= (blk_ids * ROW_BLOCK)[:, None]).astype(I32), axis=1)
    blk_e = jnp.minimum(blk_e, E - 1)
    zero_blk = jnp.where(padded > 0, pad_end - ROW_BLOCK, -1).astype(I32)
    eflat = eidx[:, :TOP_K].reshape(T * TOP_K)
    rflat = rank[:, :TOP_K].reshape(T * TOP_K)

    xs = _dispatch(start_pad, zero_blk, eflat, rflat, x1, n_rows)
    y = _experts(blk_e, n_used.reshape(1), xs, w_gate, w_up, w_down)
    return _combine(start_pad, eflat, rflat, x1, gates, ws_gate.astype(BF16), ws_up.astype(BF16),
                    ws_down.astype(BF16), ln2_g.reshape(1, D), ln2_b.reshape(1, D), y, alpha)


def kernel(x, mem, w_in, b_f, w_pool, pool_scale, w_mem_kv, w_out, ln1_g, ln1_b, w_router, router_bias,
           w_gate, w_up, w_down, ws_gate, ws_up, ws_down, ln2_g, ln2_b):
    B, S, D = x.shape
    depth = w_in.shape[0]
    x2 = x.reshape(B * S, D)
    for l in range(depth):
        x2 = _layer(x2, mem, B, S, depth, w_in[l], b_f[l], w_pool[l], pool_scale[l], w_mem_kv[l], w_out[l],
                    ln1_g[l], ln1_b[l], w_router[l], router_bias[l], w_gate[l], w_up[l], w_down[l],
                    ws_gate[l], ws_up[l], ws_down[l], ln2_g[l], ln2_b[l])
    return x2.reshape(B, S, D)
```

```python
import functools

import jax
import jax.numpy as jnp
from jax import lax
from jax.experimental import pallas as pl
from jax.experimental.pallas import tpu as pltpu

F32 = jnp.float32
BF16 = jnp.bfloat16
I32 = jnp.int32

LANES = 128
SUBLANES = 8
POOL_WINDOWS = (2, 4, 8, 16)
POOL_HALO = 16
HEAD_DIM = 64
FOX_HEADS = 8
MEM_HEADS = 4
TOP_K = 8
N_GROUPS = 8
TOPK_GROUPS = 4
ROUTED_SCALE = 2.5
LN_EPS = 1e-5
ROW_BLOCK = 256
VMEM_LIMIT = 48 * 1024 * 1024


def _cparams(sem):
    return pltpu.CompilerParams(dimension_semantics=sem, vmem_limit_bytes=VMEM_LIMIT)


def _layer_norm(z, g, b):
    mu = jnp.mean(z, axis=-1, keepdims=True)
    zc = z - mu
    var = jnp.mean(zc * zc, axis=-1, keepdims=True)
    return zc * lax.rsqrt(var + LN_EPS) * g + b


def _silu(x):
    return x * (1.0 / (1.0 + jnp.exp(-x)))


def _load_rows(ref, n):
    return jnp.concatenate([ref[pl.ds(j, n, stride=SUBLANES), :] for j in range(SUBLANES)], axis=1)


def _store_rows(ref, val):
    n = val.shape[0]
    for j in range(SUBLANES):
        ref[pl.ds(j, n, stride=SUBLANES), :] = val[:, j * LANES:(j + 1) * LANES]


def _tile_rows(ref, r):
    return ref.at[pl.ds(pl.multiple_of(r * SUBLANES, SUBLANES), SUBLANES), :]


def _inproj_kernel(x_ref, w_ref, wf_ref, up_ref, q_ref, k_ref, v_ref, qm_ref, f_ref, *, pw, fw, mw, scale):
    xb = x_ref[...].astype(BF16)

    def proj(lo, hi):
        return jnp.dot(xb, w_ref[:, lo:hi], preferred_element_type=F32)

    up_ref[...] = proj(0, pw)
    q_ref[...] = (proj(pw, pw + fw) * scale).astype(BF16)
    k_ref[...] = proj(pw + fw, pw + 2 * fw).astype(BF16)
    v_ref[...] = proj(pw + 2 * fw, pw + 3 * fw).astype(BF16)
    qm_ref[...] = (proj(pw + 3 * fw, pw + 3 * fw + mw) * scale).astype(BF16)
    f_ref[...] = jnp.dot(xb, wf_ref[...], preferred_element_type=F32)


def _inproj(x2, w_main, w_f, pw, fw, mw, tm=512):
    T, D = x2.shape
    kern = functools.partial(_inproj_kernel, pw=pw, fw=fw, mw=mw, scale=HEAD_DIM ** -0.5)
    row = lambda i: (i, 0)
    fixed = lambda i: (0, 0)
    return pl.pallas_call(
        kern,
        grid=(T // tm,),
        in_specs=[pl.BlockSpec((tm, D), row),
                  pl.BlockSpec(w_main.shape, fixed),
                  pl.BlockSpec(w_f.shape, fixed)],
        out_specs=[pl.BlockSpec((tm, pw), row), pl.BlockSpec((tm, fw), row), pl.BlockSpec((tm, fw), row),
                   pl.BlockSpec((tm, fw), row), pl.BlockSpec((tm, mw), row), pl.BlockSpec((tm, LANES), row)],
        out_shape=[jax.ShapeDtypeStruct((T, pw), F32), jax.ShapeDtypeStruct((T, fw), BF16),
                   jax.ShapeDtypeStruct((T, fw), BF16), jax.ShapeDtypeStruct((T, fw), BF16),
                   jax.ShapeDtypeStruct((T, mw), BF16), jax.ShapeDtypeStruct((T, LANES), F32)],
        compiler_params=_cparams(("parallel",)),
        name="inproj",
    )(x2, w_main, w_f)


def _pool_kernel(u_ref, wbd_ref, sc_ref, o_ref, ext_sc, *, chunk, group):
    S, W = u_ref.shape
    ext_sc[0:POOL_HALO, :] = jnp.zeros((POOL_HALO, W), F32)
    ext_sc[POOL_HALO:, :] = u_ref[...]
    rows = chunk + POOL_HALO
    lrow = lax.broadcasted_iota(I32, (rows, W), 0)
    lane = lax.broadcasted_iota(I32, (rows, W), 1)

    def body(c, carry):
        start = pl.multiple_of(c * chunk, chunk)
        e = ext_sc[pl.ds(start, rows), :]
        posf = (lrow + (start - POOL_HALO + 1)).astype(F32)
        acc = e
        d = jnp.zeros_like(e)
        shift = 1
        for g, w in enumerate(POOL_WINDOWS):
            while shift < w:
                acc = acc + pltpu.roll(acc, shift, axis=0)
                shift *= 2
            mean = acc / jnp.minimum(posf, float(w))
            d = jnp.where((lane >= g * group) & (lane < (g + 1) * group), mean, d)
        d = (d - e)[POOL_HALO:, :]
        y = jnp.dot(d.astype(BF16), wbd_ref[...], preferred_element_type=F32) * sc_ref[...]
        o_ref[pl.ds(start, chunk), :] = y.astype(BF16)
        return carry

    lax.fori_loop(0, S // chunk, body, 0)


def _pool(u, wbd, pscale, B, S, chunk=512):
    T, W = u.shape
    kern = functools.partial(_pool_kernel, chunk=chunk, group=W // len(POOL_WINDOWS))
    return pl.pallas_call(
        kern,
        grid=(B,),
        in_specs=[pl.BlockSpec((S, W), lambda b: (b, 0)),
                  pl.BlockSpec((W, W), lambda b: (0, 0)),
                  pl.BlockSpec((1, W), lambda b: (0, 0))],
        out_specs=pl.BlockSpec((S, W), lambda b: (b, 0)),
        out_shape=jax.ShapeDtypeStruct((T, W), BF16),
        scratch_shapes=[pltpu.VMEM((S + POOL_HALO, W), F32)],
        compiler_params=_cparams(("parallel",)),
        name="pool_mixer",
    )(u, wbd, pscale)


def _fgate_kernel(f_ref, bf_ref, o_ref):
    S = f_ref.shape[0]
    z = f_ref[...] + bf_ref[...]
    logf = jnp.minimum(z, 0.0) - jnp.log(1.0 + jnp.exp(-jnp.abs(z)))
    x = logf.T[0:FOX_HEADS, :]
    lane = lax.broadcasted_iota(I32, x.shape, 1)
    shift = 1
    while shift < S:
        x = x + jnp.where(lane >= shift, pltpu.roll(x, shift, axis=1), 0.0)
        shift *= 2
    for p in range(FOX_HEADS // 2):
        o_ref[0, p] = x[2 * p:2 * p + 2, :]


def _fgate(f_logit, bf_pad, B, S):
    return pl.pallas_call(
        _fgate_kernel,
        grid=(B,),
        in_specs=[pl.BlockSpec((S, LANES), lambda b: (b, 0)),
                  pl.BlockSpec((1, LANES), lambda b: (0, 0))],
        out_specs=pl.BlockSpec((1, FOX_HEADS // 2, 2, S), lambda b: (b, 0, 0, 0)),
        out_shape=jax.ShapeDtypeStruct((B, FOX_HEADS // 2, 2, S), F32),
        compiler_params=_cparams(("parallel",)),
        name="forget_cumsum",
    )(f_logit, bf_pad)


def _fox_kernel(q_ref, k_ref, v_ref, f_ref, o_ref, m_sc, l_sc, acc_sc, *, blk):
    qi = pl.program_id(2)
    q = q_ref[...]
    lane = lax.broadcasted_iota(I32, q.shape, 1)
    first = lane < HEAD_DIM
    zero = jnp.zeros_like(q)
    qh = (jnp.where(first, q, zero), jnp.where(first, zero, q))
    m_sc[...] = jnp.full(m_sc.shape, -jnp.inf, F32)
    l_sc[...] = jnp.zeros(l_sc.shape, F32)
    acc_sc[...] = jnp.zeros(acc_sc.shape, F32)

    def step(ki, causal):
        ks = pl.multiple_of(ki * blk, blk)
        kb = k_ref[pl.ds(ks, blk), :]
        vb = v_ref[pl.ds(ks, blk), :]
        fb = f_ref[0, 0, :, pl.ds(ks, blk)]
        alphas, pvs = [], []
        for h in range(2):
            s = lax.dot_general(qh[h], kb, (((1,), (1,)), ((), ())), preferred_element_type=F32)
            s = s - fb[h:h + 1, :]
            if causal:
                r = lax.broadcasted_iota(I32, s.shape, 0)
                c = lax.broadcasted_iota(I32, s.shape, 1)
                s = jnp.where(c <= r, s, -jnp.inf)
            m_prev = m_sc[h]
            m_new = jnp.maximum(m_prev, jnp.max(s, axis=1, keepdims=True))
            alpha = jnp.exp(m_prev - m_new)
            p = jnp.exp(s - m_new)
            l_sc[h] = alpha * l_sc[h] + jnp.sum(p, axis=1, keepdims=True)
            m_sc[h] = m_new
            alphas.append(alpha)
            pvs.append(jnp.dot(p.astype(BF16), vb, preferred_element_type=F32))
        acc_sc[...] = acc_sc[...] * jnp.where(first, alphas[0], alphas[1]) + jnp.where(first, pvs[0], pvs[1])

    def body(ki, carry):
        step(ki, False)
        return carry

    lax.fori_loop(0, qi, body, 0)
    step(qi, True)
    o_ref[...] = (acc_sc[...] / jnp.where(first, l_sc[0], l_sc[1])).astype(BF16)


def _fox(q, k, v, fcum, B, S, blk=512):
    T, FW = q.shape
    nq = S // blk
    pairs = FW // LANES
    kern = functools.partial(_fox_kernel, blk=blk)
    return pl.pallas_call(
        kern,
        grid=(B, pairs, nq),
        in_specs=[pl.BlockSpec((blk, LANES), lambda b, p, i: (b * nq + i, p)),
                  pl.BlockSpec((S, LANES), lambda b, p, i: (b, p)),
                  pl.BlockSpec((S, LANES), lambda b, p, i: (b, p)),
                  pl.BlockSpec((1, 1, 2, S), lambda b, p, i: (b, p, 0, 0))],
        out_specs=pl.BlockSpec((blk, LANES), lambda b, p, i: (b * nq + i, p)),
        out_shape=jax.ShapeDtypeStruct((T, FW), BF16),
        scratch_shapes=[pltpu.VMEM((2, blk, 1), F32), pltpu.VMEM((2, blk, 1), F32),
                        pltpu.VMEM((blk, LANES), F32)],
        compiler_params=_cparams(("parallel", "parallel", "parallel")),
        name="fox_attention",
    )(q, k, v, fcum)


def _memattn_kernel(qm_ref, mem_ref, wkv_ref, o_ref, k_sc, v_sc):
    MW = qm_ref.shape[1]

    @pl.when(pl.program_id(1) == 0)
    def _():
        kv = jnp.dot(mem_ref[0].astype(BF16), wkv_ref[...], preferred_element_type=F32)
        k_sc[...] = kv[:, :MW].astype(BF16)
        v_sc[...] = kv[:, MW:].astype(BF16)

    q = qm_ref[...]
    lane = lax.broadcasted_iota(I32, q.shape, 1)
    hd = MW // MEM_HEADS
    out = jnp.zeros(q.shape, F32)
    for h in range(MEM_HEADS):
        mine = (lane >= h * hd) & (lane < (h + 1) * hd)
        qh = jnp.where(mine, q, jnp.zeros_like(q))
        s = lax.dot_general(qh, k_sc[...], (((1,), (1,)), ((), ())), preferred_element_type=F32)
        p = jnp.exp(s - jnp.max(s, axis=1, keepdims=True))
        l = jnp.sum(p, axis=1, keepdims=True)
        o = jnp.dot(p.astype(BF16), v_sc[...], preferred_element_type=F32)
        out = jnp.where(mine, o / l, out)
    o_ref[...] = out.astype(BF16)


def _memattn(qm, mem, wkv, B, S, tq=512):
    T, MW = qm.shape
    M, D = mem.shape[1], mem.shape[2]
    nq = S // tq
    return pl.pallas_call(
        _memattn_kernel,
        grid=(B, nq),
        in_specs=[pl.BlockSpec((tq, MW), lambda b, i: (b * nq + i, 0)),
                  pl.BlockSpec((1, M, D), lambda b, i: (b, 0, 0)),
                  pl.BlockSpec(wkv.shape, lambda b, i: (0, 0))],
        out_specs=pl.BlockSpec((tq, MW), lambda b, i: (b * nq + i, 0)),
        out_shape=jax.ShapeDtypeStruct((T, MW), BF16),
        scratch_shapes=[pltpu.VMEM((M, MW), BF16), pltpu.VMEM((M, MW), BF16)],
        compiler_params=_cparams(("parallel", "arbitrary")),
        name="memory_attention",
    )(qm, mem, wkv)


def _outproj_kernel(yp_ref, yf_ref, ym_ref, x_ref, w_ref, g_ref, b_ref, o_ref, *, alpha):
    pw, fw = yp_ref.shape[1], yf_ref.shape[1]
    h = jnp.dot(yp_ref[...], w_ref[0:pw, :], preferred_element_type=F32)
    h = h + jnp.dot(yf_ref[...], w_ref[pw:pw + fw, :], preferred_element_type=F32)
    h = h + jnp.dot(ym_ref[...], w_ref[pw + fw:, :], preferred_element_type=F32)
    _store_rows(o_ref, _layer_norm(alpha * x_ref[...] + h, g_ref[...], b_ref[...]))


def _outproj(yp, yf, ym, x2, w_out, g, b, alpha, tm=512):
    T, D = x2.shape
    assert D == SUBLANES * LANES
    row = lambda i: (i, 0)
    fixed = lambda i: (0, 0)
    return pl.pallas_call(
        functools.partial(_outproj_kernel, alpha=alpha),
        grid=(T // tm,),
        in_specs=[pl.BlockSpec((tm, yp.shape[1]), row), pl.BlockSpec((tm, yf.shape[1]), row),
                  pl.BlockSpec((tm, ym.shape[1]), row), pl.BlockSpec((tm, D), row),
                  pl.BlockSpec(w_out.shape, fixed), pl.BlockSpec((1, D), fixed), pl.BlockSpec((1, D), fixed)],
        out_specs=pl.BlockSpec((tm * SUBLANES, LANES), row),
        out_shape=jax.ShapeDtypeStruct((T * SUBLANES, LANES), F32),
        compiler_params=_cparams(("parallel",)),
        name="outproj_ln1",
    )(yp, yf, ym, x2, w_out, g, b)


def _router_kernel(x_ref, wh_ref, wl_ref, bias_ref, eidx_ref, gate_ref, rank_ref, cnt_ref, carry_sc):
    tt = x_ref.shape[0] // SUBLANES
    E = wh_ref.shape[1]
    gsz = E // N_GROUPS

    @pl.when(pl.program_id(0) == 0)
    def _():
        carry_sc[...] = jnp.zeros(carry_sc.shape, F32)

    x = _load_rows(x_ref, tt)
    xh = x.astype(BF16)
    xl = (x - xh.astype(F32)).astype(BF16)
    wh = wh_ref[...]
    logits = jnp.dot(xh, wh, preferred_element_type=F32) + (
        jnp.dot(xh, wl_ref[...], preferred_element_type=F32) + jnp.dot(xl, wh, preferred_element_type=F32))
    scores = 1.0 / (1.0 + jnp.exp(-logits))
    biased = scores + bias_ref[...]
    lane = lax.broadcasted_iota(I32, (tt, E), 1)
    ninf = jnp.full((tt, E), -jnp.inf, F32)

    def rmax(a):
        return jnp.max(a, axis=1, keepdims=True)

    def first_at(a, m):
        return jnp.min(jnp.where(a == m, lane, E), axis=1, keepdims=True)

    gscore = []
    for g in range(N_GROUPS):
        mg = jnp.where((lane >= g * gsz) & (lane < (g + 1) * gsz), biased, ninf)
        m1 = rmax(mg)
        m2 = rmax(jnp.where(lane == first_at(mg, m1), ninf, mg))
        gscore.append(m1 + m2)

    masked = ninf
    for g in range(N_GROUPS):
        ahead = jnp.zeros((tt, 1), I32)
        for o in range(N_GROUPS):
            if o == g:
                continue
            beats = gscore[o] > gscore[g]
            if o < g:
                beats = beats | (gscore[o] == gscore[g])
            ahead = ahead + beats.astype(I32)
        keep = (ahead < TOPK_GROUPS) & (lane >= g * gsz) & (lane < (g + 1) * gsz)
        masked = jnp.where(keep, biased, masked)

    picks, sels = [], []
    chosen = jnp.zeros((tt, E), F32)
    for _ in range(TOP_K):
        ik = first_at(masked, rmax(masked))
        oh = lane == ik
        picks.append(ik)
        sels.append(jnp.sum(jnp.where(oh, scores, 0.0), axis=1, keepdims=True))
        chosen = jnp.where(oh, 1.0, chosen)
        masked = jnp.where(oh, ninf, masked)
    denom = sels[0]
    for sk in sels[1:]:
        denom = denom + sk

    r = lax.broadcasted_iota(I32, (tt, tt), 0)
    c = lax.broadcasted_iota(I32, (tt, tt), 1)
    lower = jnp.where(c < r, 1.0, 0.0).astype(BF16)
    pos = carry_sc[...] + jnp.dot(lower, chosen.astype(BF16), preferred_element_type=F32)
    carry_sc[...] = carry_sc[...] + jnp.sum(chosen, axis=0, keepdims=True)
    cnt_ref[...] = carry_sc[...]

    lane_o = lax.broadcasted_iota(I32, (tt, LANES), 1)
    e_out = jnp.zeros((tt, LANES), I32)
    g_out = jnp.zeros((tt, LANES), F32)
    r_out = jnp.zeros((tt, LANES), I32)
    for kk in range(TOP_K):
        rk = jnp.sum(jnp.where(lane == picks[kk], pos, 0.0), axis=1, keepdims=True)
        here = lane_o == kk
        e_out = jnp.where(here, picks[kk], e_out)
        g_out = jnp.where(here, sels[kk] / denom * ROUTED_SCALE, g_out)
        r_out = jnp.where(here, rk.astype(I32), r_out)
    eidx_ref[...] = e_out
    gate_ref[...] = g_out
    rank_ref[...] = r_out


def _router(x1t, wr_hi, wr_lo, rbias, tt=256):
    T = x1t.shape[0] // SUBLANES
    D, E = wr_hi.shape
    row = lambda i: (i, 0)
    fixed = lambda i: (0, 0)
    return pl.pallas_call(
        _router_kernel,
        grid=(T // tt,),
        in_specs=[pl.BlockSpec((tt * SUBLANES, LANES), row), pl.BlockSpec((D, E), fixed),
                  pl.BlockSpec((D, E), fixed), pl.BlockSpec((1, E), fixed)],
        out_specs=[pl.BlockSpec((tt, LANES), row), pl.BlockSpec((tt, LANES), row), pl.BlockSpec((tt, LANES), row),
                   pl.BlockSpec((1, E), fixed)],
        out_shape=[jax.ShapeDtypeStruct((T, LANES), I32), jax.ShapeDtypeStruct((T, LANES), F32),
                   jax.ShapeDtypeStruct((T, LANES), I32), jax.ShapeDtypeStruct((1, E), F32)],
        scratch_shapes=[pltpu.VMEM((1, E), F32)],
        compiler_params=_cparams(("arbitrary",)),
        name="router_topk",
    )(x1t, wr_hi, wr_lo, rbias)


def _dispatch_kernel(zb_ref, d_ref, x_ref, xs_ref, zero_sc, sem, *, tt, n_exp):
    blk_rows = ROW_BLOCK * SUBLANES

    @pl.when(pl.program_id(0) == 0)
    def _():
        zero_sc[...] = jnp.zeros(zero_sc.shape, F32)

        def zstart(e, c):
            @pl.when(zb_ref[e] >= 0)
            def _():
                dst = xs_ref.at[pl.ds(pl.multiple_of(zb_ref[e] * SUBLANES, blk_rows), blk_rows), :]
                pltpu.make_async_copy(zero_sc, dst, sem).start()
            return c

        def zwait(e, c):
            @pl.when(zb_ref[e] >= 0)
            def _():
                pltpu.make_async_copy(zero_sc, xs_ref.at[pl.ds(0, blk_rows), :], sem).wait()
            return c

        lax.fori_loop(0, n_exp, zstart, 0)
        lax.fori_loop(0, n_exp, zwait, 0)

    def start(j, c):
        src = _tile_rows(x_ref, j)
        for kk in range(TOP_K):
            pltpu.make_async_copy(src, _tile_rows(xs_ref, d_ref[j * TOP_K + kk]), sem).start(priority=kk % 2)
        return c

    lax.fori_loop(0, tt, start, 0)
    for kk in range(TOP_K):
        pltpu.make_async_copy(x_ref, xs_ref.at[pl.ds(0, tt * SUBLANES), :], sem).wait()


def _dispatch(zero_blk, dest, x1t, n_rows, tt=256):
    T = x1t.shape[0] // SUBLANES
    n_exp = zero_blk.shape[0]
    grid_spec = pltpu.PrefetchScalarGridSpec(
        num_scalar_prefetch=1,
        grid=(T // tt,),
        in_specs=[pl.BlockSpec((tt * TOP_K,), lambda i, zb: (i,), memory_space=pltpu.SMEM),
                  pl.BlockSpec((tt * SUBLANES, LANES), lambda i, zb: (i, 0))],
        out_specs=pl.BlockSpec(memory_space=pl.ANY),
        scratch_shapes=[pltpu.VMEM((ROW_BLOCK * SUBLANES, LANES), F32), pltpu.SemaphoreType.DMA],
    )
    return pl.pallas_call(
        functools.partial(_dispatch_kernel, tt=tt, n_exp=n_exp),
        grid_spec=grid_spec,
        out_shape=jax.ShapeDtypeStruct((n_rows * SUBLANES, LANES), F32),
        compiler_params=_cparams(("arbitrary",)),
        name="moe_dispatch",
    )(zero_blk, dest, x1t)


def _expert_kernel(be_ref, nu_ref, xs_ref, wg_ref, wu_ref, wd_ref, y_ref, wg_sc, wu_sc, wd_sc):
    i = pl.program_id(0)

    @pl.when(i < nu_ref[0])
    def _():
        e = be_ref[i]
        prev = be_ref[jnp.maximum(i - 1, 0)]

        @pl.when((i == 0) | (e != prev))
        def _():
            wg_sc[...] = wg_ref[0].astype(BF16)
            wu_sc[...] = wu_ref[0].astype(BF16)
            wd_sc[...] = wd_ref[0].astype(BF16)

        xb = _load_rows(xs_ref, ROW_BLOCK).astype(BF16)
        g = jnp.dot(xb, wg_sc[...], preferred_element_type=F32)
        u = jnp.dot(xb, wu_sc[...], preferred_element_type=F32)
        h = (_silu(g) * u).astype(BF16)
        _store_rows(y_ref, jnp.dot(h, wd_sc[...], preferred_element_type=F32))


def _experts(blk_e, n_used, xs, w_gate, w_up, w_down):
    n_rows = xs.shape[0] // SUBLANES
    E, D, H = w_gate.shape
    n_blk = n_rows // ROW_BLOCK
    rows = lambda i, be, nu: (jnp.minimum(i, nu[0] - 1), 0)
    wsel = lambda i, be, nu: (be[i], 0, 0)
    grid_spec = pltpu.PrefetchScalarGridSpec(
        num_scalar_prefetch=2,
        grid=(n_blk,),
        in_specs=[pl.BlockSpec((ROW_BLOCK * SUBLANES, LANES), rows),
                  pl.BlockSpec((1, D, H), wsel), pl.BlockSpec((1, D, H), wsel), pl.BlockSpec((1, H, D), wsel)],
        out_specs=pl.BlockSpec((ROW_BLOCK * SUBLANES, LANES), rows),
        scratch_shapes=[pltpu.VMEM((D, H), BF16), pltpu.VMEM((D, H), BF16), pltpu.VMEM((H, D), BF16)],
    )
    return pl.pallas_call(
        _expert_kernel,
        grid_spec=grid_spec,
        out_shape=jax.ShapeDtypeStruct((n_rows * SUBLANES, LANES), F32),
        compiler_params=_cparams(("arbitrary",)),
        name="moe_experts",
    )(blk_e, n_used, xs, w_gate, w_up, w_down)


def _combine_kernel(dcur_ref, dnxt_ref, x_ref, gate_ref, wsg_ref, wsu_ref, wsd_ref, g_ref, b_ref, y_ref,
                    o_ref, buf, sem, *, tt, alpha):
    i = pl.program_id(0)
    n = pl.num_programs(0)

    def fetch(d_ref, slot):
        def body(j, c):
            for kk in range(TOP_K):
                src = _tile_rows(y_ref, d_ref[j * TOP_K + kk])
                pltpu.make_async_copy(src, _tile_rows(buf.at[slot, kk], j), sem.at[slot]).start(priority=kk % 2)
            return c

        lax.fori_loop(0, tt, body, 0)

    def reduce(slot):
        for kk in range(TOP_K):
            pltpu.make_async_copy(y_ref.at[pl.ds(0, tt * SUBLANES), :], buf.at[slot, kk], sem.at[slot]).wait()
        x = _load_rows(x_ref, tt)
        xb = x.astype(BF16)
        hs = _silu(jnp.dot(xb, wsg_ref[...], preferred_element_type=F32)) * jnp.dot(
            xb, wsu_ref[...], preferred_element_type=F32)
        moe = jnp.dot(hs.astype(BF16), wsd_ref[...], preferred_element_type=F32)
        gates = gate_ref[...]
        for kk in range(TOP_K):
            moe = moe + gates[:, kk:kk + 1] * _load_rows(buf.at[slot, kk], tt)
        o_ref[...] = _layer_norm(alpha * x + moe, g_ref[...], b_ref[...])

    @pl.when(i == 0)
    def _():
        fetch(dcur_ref, 0)

    for slot in range(2):
        @pl.when((i % 2 == slot) & (i + 1 < n))
        def _():
            fetch(dnxt_ref, 1 - slot)

    for slot in range(2):
        @pl.when(i % 2 == slot)
        def _():
            reduce(slot)


def _combine(dest, x1t, gates, wsg, wsu, wsd, g, b, y, alpha, tt=128):
    T = x1t.shape[0] // SUBLANES
    D = SUBLANES * LANES
    n = T // tt
    row = lambda i: (i, 0)
    fixed = lambda i: (0, 0)
    return pl.pallas_call(
        functools.partial(_combine_kernel, tt=tt, alpha=alpha),
        grid=(n,),
        in_specs=[pl.BlockSpec((tt * TOP_K,), lambda i: (i,), memory_space=pltpu.SMEM),
                  pl.BlockSpec((tt * TOP_K,), lambda i: (jnp.minimum(i + 1, n - 1),), memory_space=pltpu.SMEM),
                  pl.BlockSpec((tt * SUBLANES, LANES), row), pl.BlockSpec((tt, LANES), row),
                  pl.BlockSpec(wsg.shape, fixed), pl.BlockSpec(wsu.shape, fixed), pl.BlockSpec(wsd.shape, fixed),
                  pl.BlockSpec((1, D), fixed), pl.BlockSpec((1, D), fixed),
                  pl.BlockSpec(memory_space=pl.ANY)],
        out_specs=pl.BlockSpec((tt, D), row),
        out_shape=jax.ShapeDtypeStruct((T, D), F32),
        scratch_shapes=[pltpu.VMEM((2, TOP_K, tt * SUBLANES, LANES), F32), pltpu.SemaphoreType.DMA((2,))],
        compiler_params=_cparams(("arbitrary",)),
        name="moe_combine_ln2",
    )(dest, dest, x1t, gates, wsg, wsu, wsd, g, b, y)


def _layer(x2, mem, B, S, depth, w_in, b_f, w_pool, pool_scale, w_mem_kv, w_out, ln1_g, ln1_b,
           w_router, router_bias, w_gate, w_up, w_down, ws_gate, ws_up, ws_down, ln2_g, ln2_b):
    T, D = x2.shape
    n_win, grp = w_pool.shape[0], w_pool.shape[1]
    pw = n_win * grp
    fw = FOX_HEADS * HEAD_DIM
    mw = w_mem_kv.shape[1] // 2
    E = w_router.shape[1]
    alpha = (2 * depth) ** 0.25

    f_lo = pw + 3 * fw
    w_main = jnp.concatenate([w_in[:, :f_lo], w_in[:, f_lo + FOX_HEADS:]], axis=1).astype(BF16)
    w_f = jnp.pad(w_in[:, f_lo:f_lo + FOX_HEADS], ((0, 0), (0, LANES - FOX_HEADS))).astype(BF16)
    bf_pad = jnp.pad(b_f, (0, LANES - FOX_HEADS)).reshape(1, LANES)
    wbd = jnp.zeros((pw, pw), F32)
    for g in range(n_win):
        wbd = wbd.at[g * grp:(g + 1) * grp, g * grp:(g + 1) * grp].set(w_pool[g])
    wr_hi = w_router.astype(BF16)
    wr_lo = (w_router - wr_hi.astype(F32)).astype(BF16)

    u_pool, q, k, v, q_mem, f_logit = _inproj(x2, w_main, w_f, pw, fw, mw)
    y_pool = _pool(u_pool, wbd.astype(BF16), pool_scale.reshape(1, pw), B, S)
    fcum = _fgate(f_logit, bf_pad, B, S)
    y_fox = _fox(q, k, v, fcum, B, S)
    y_mem = _memattn(q_mem, mem, w_mem_kv.astype(BF16), B, S)
    x1 = _outproj(y_pool, y_fox, y_mem, x2, w_out.astype(BF16), ln1_g.reshape(1, D), ln1_b.reshape(1, D), alpha)

    eidx, gates, rank, counts = _router(x1, wr_hi, wr_lo, router_bias.reshape(1, E))

    cnt = counts.reshape(E).astype(I32)
    padded = (cnt + ROW_BLOCK - 1) // ROW_BLOCK * ROW_BLOCK
    pad_end = jnp.cumsum(padded)
    start_pad = (pad_end - padded).astype(I32)
    n_rows = (T * TOP_K + E * (ROW_BLOCK - 1)) // ROW_BLOCK * ROW_BLOCK
    n_blk = n_rows // ROW_BLOCK
    n_used = (pad_end[-1] // ROW_BLOCK).astype(I32)
    blk_ids = jnp.minimum(jnp.arange(n_blk, dtype=I32), n_used - 1)
    blk_e = jnp.sum((pad_end[None, :] <= (blk_ids * ROW_BLOCK)[:, None]).astype(I32), axis=1)
    blk_e = jnp.minimum(blk_e, E - 1)
    zero_blk = jnp.where(padded > 0, pad_end - ROW_BLOCK, -1).astype(I32)
    dest = (jnp.take(start_pad, eidx[:, :TOP_K], axis=0) + rank[:, :TOP_K]).reshape(T * TOP_K)

    xs = _dispatch(zero_blk, dest, x1, n_rows)
    y = _experts(blk_e, n_used.reshape(1), xs, w_gate, w_up, w_down)
    return _combine(dest, x1, gates, ws_gate.astype(BF16), ws_up.astype(BF16),
                    ws_down.astype(BF16), ln2_g.reshape(1, D), ln2_b.reshape(1, D), y, alpha)


def kernel(x, mem, w_in, b_f, w_pool, pool_scale, w_mem_kv, w_out, ln1_g, ln1_b, w_router, router_bias,
           w_gate, w_up, w_down, ws_gate, ws_up, ws_down, ln2_g, ln2_b):
    B, S, D = x.shape
    depth = w_in.shape[0]
    x2 = x.reshape(B * S, D)
    for l in range(depth):
        x2 = _layer(x2, mem, B, S, depth, w_in[l], b_f[l], w_pool[l], pool_scale[l], w_mem_kv[l], w_out[l],
                    ln1_g[l], ln1_b[l], w_router[l], router_bias[l], w_gate[l], w_up[l], w_down[l],
                    ws_gate[l], ws_up[l], ws_down[l], ln2_g[l], ln2_b[l])
    return x2.reshape(B, S, D)
```

```python
import functools

import jax
import jax.numpy as jnp
from jax import lax
from jax.experimental import pallas as pl
from jax.experimental.pallas import tpu as pltpu

F32 = jnp.float32
BF16 = jnp.bfloat16
I32 = jnp.int32

LANES = 128
SUBLANES = 8
POOL_WINDOWS = (2, 4, 8, 16)
POOL_HALO = 16
HEAD_DIM = 64
FOX_HEADS = 8
BIAS_TERMS = 3
MEM_HEADS = 4
TOP_K = 8
N_GROUPS = 8
TOPK_GROUPS = 4
ROUTED_SCALE = 2.5
LN_EPS = 1e-5
ROW_BLOCK = 256
VMEM_LIMIT = 48 * 1024 * 1024


def _cparams(sem):
    return pltpu.CompilerParams(dimension_semantics=sem, vmem_limit_bytes=VMEM_LIMIT)


def _layer_norm(z, g, b):
    mu = jnp.mean(z, axis=-1, keepdims=True)
    zc = z - mu
    var = jnp.mean(zc * zc, axis=-1, keepdims=True)
    return zc * lax.rsqrt(var + LN_EPS) * g + b


def _silu(x):
    return x * (1.0 / (1.0 + jnp.exp(-x)))


def _load_rows(ref, n):
    return jnp.concatenate([ref[pl.ds(j, n, stride=SUBLANES), :] for j in range(SUBLANES)], axis=1)


def _store_rows(ref, val):
    n = val.shape[0]
    for j in range(SUBLANES):
        ref[pl.ds(j, n, stride=SUBLANES), :] = val[:, j * LANES:(j + 1) * LANES]


def _tile_rows(ref, r):
    return ref.at[pl.ds(pl.multiple_of(r * SUBLANES, SUBLANES), SUBLANES), :]


def _inproj_kernel(x_ref, w_ref, wf_ref, up_ref, q_ref, k_ref, v_ref, qm_ref, f_ref, *, pw, fw, mw, scale):
    xb = x_ref[...].astype(BF16)

    def proj(lo, hi):
        return jnp.dot(xb, w_ref[:, lo:hi], preferred_element_type=F32)

    up_ref[...] = proj(0, pw)
    q_ref[...] = (proj(pw, pw + fw) * scale).astype(BF16)
    k_ref[...] = proj(pw + fw, pw + 2 * fw).astype(BF16)
    v_ref[...] = proj(pw + 2 * fw, pw + 3 * fw).astype(BF16)
    qm_ref[...] = (proj(pw + 3 * fw, pw + 3 * fw + mw) * scale).astype(BF16)
    f_ref[...] = jnp.dot(xb, wf_ref[...], preferred_element_type=F32)


def _inproj(x2, w_main, w_f, pw, fw, mw, tm=512):
    T, D = x2.shape
    kern = functools.partial(_inproj_kernel, pw=pw, fw=fw, mw=mw, scale=HEAD_DIM ** -0.5)
    row = lambda i: (i, 0)
    fixed = lambda i: (0, 0)
    return pl.pallas_call(
        kern,
        grid=(T // tm,),
        in_specs=[pl.BlockSpec((tm, D), row),
                  pl.BlockSpec(w_main.shape, fixed),
                  pl.BlockSpec(w_f.shape, fixed)],
        out_specs=[pl.BlockSpec((tm, pw), row), pl.BlockSpec((tm, fw), row), pl.BlockSpec((tm, fw), row),
                   pl.BlockSpec((tm, fw), row), pl.BlockSpec((tm, mw), row), pl.BlockSpec((tm, LANES), row)],
        out_shape=[jax.ShapeDtypeStruct((T, pw), F32), jax.ShapeDtypeStruct((T, fw), BF16),
                   jax.ShapeDtypeStruct((T, fw), BF16), jax.ShapeDtypeStruct((T, fw), BF16),
                   jax.ShapeDtypeStruct((T, mw), BF16), jax.ShapeDtypeStruct((T, LANES), F32)],
        compiler_params=_cparams(("parallel",)),
        name="inproj",
    )(x2, w_main, w_f)


def _pool_kernel(u_ref, wbd_ref, sc_ref, o_ref, ext_sc, *, chunk, group):
    S, W = u_ref.shape
    ext_sc[0:POOL_HALO, :] = jnp.zeros((POOL_HALO, W), F32)
    ext_sc[POOL_HALO:, :] = u_ref[...]
    rows = chunk + POOL_HALO
    lrow = lax.broadcasted_iota(I32, (rows, W), 0)
    lane = lax.broadcasted_iota(I32, (rows, W), 1)

    def body(c, carry):
        start = pl.multiple_of(c * chunk, chunk)
        e = ext_sc[pl.ds(start, rows), :]
        posf = (lrow + (start - POOL_HALO + 1)).astype(F32)
        acc = e
        d = jnp.zeros_like(e)
        shift = 1
        for g, w in enumerate(POOL_WINDOWS):
            while shift < w:
                acc = acc + pltpu.roll(acc, shift, axis=0)
                shift *= 2
            mean = acc / jnp.minimum(posf, float(w))
            d = jnp.where((lane >= g * group) & (lane < (g + 1) * group), mean, d)
        d = (d - e)[POOL_HALO:, :]
        y = jnp.dot(d.astype(BF16), wbd_ref[...], preferred_element_type=F32) * sc_ref[...]
        o_ref[pl.ds(start, chunk), :] = y.astype(BF16)
        return carry

    lax.fori_loop(0, S // chunk, body, 0)


def _pool(u, wbd, pscale, B, S, chunk=512):
    T, W = u.shape
    kern = functools.partial(_pool_kernel, chunk=chunk, group=W // len(POOL_WINDOWS))
    return pl.pallas_call(
        kern,
        grid=(B,),
        in_specs=[pl.BlockSpec((S, W), lambda b: (b, 0)),
                  pl.BlockSpec((W, W), lambda b: (0, 0)),
                  pl.BlockSpec((1, W), lambda b: (0, 0))],
        out_specs=pl.BlockSpec((S, W), lambda b: (b, 0)),
        out_shape=jax.ShapeDtypeStruct((T, W), BF16),
        scratch_shapes=[pltpu.VMEM((S + POOL_HALO, W), F32)],
        compiler_params=_cparams(("parallel",)),
        name="pool_mixer",
    )(u, wbd, pscale)


def _fgate_kernel(f_ref, bf_ref, o_ref):
    S = f_ref.shape[0]
    z = f_ref[...] + bf_ref[...]
    x = jnp.minimum(z, 0.0) - jnp.log(1.0 + jnp.exp(-jnp.abs(z)))
    row = lax.broadcasted_iota(I32, x.shape, 0)
    lane = lax.broadcasted_iota(I32, x.shape, 1)
    shift = 1
    while shift < S:
        x = x + jnp.where(row >= shift, pltpu.roll(x, shift, axis=0), 0.0)
        shift *= 2
    hi = x.astype(BF16).astype(F32)
    mid = (x - hi).astype(BF16).astype(F32)
    lo = (x - hi) - mid
    for p in range(FOX_HEADS // 2):
        out = jnp.zeros(x.shape, F32)
        for hh in range(2):
            c = 2 * p + hh
            for j, part in enumerate((hi, mid, lo)):
                out = jnp.where(lane == BIAS_TERMS * hh + j, -part[:, c:c + 1], out)
        o_ref[0, p] = out.astype(BF16)


def _fgate(f_logit, bf_pad, B, S):
    return pl.pallas_call(
        _fgate_kernel,
        grid=(B,),
        in_specs=[pl.BlockSpec((S, LANES), lambda b: (b, 0)),
                  pl.BlockSpec((1, LANES), lambda b: (0, 0))],
        out_specs=pl.BlockSpec((1, FOX_HEADS // 2, S, LANES), lambda b: (b, 0, 0, 0)),
        out_shape=jax.ShapeDtypeStruct((B, FOX_HEADS // 2, S, LANES), BF16),
        compiler_params=_cparams(("parallel",)),
        name="forget_cumsum",
    )(f_logit, bf_pad)


def _fox_kernel(q_ref, k_ref, a_ref, v_ref, o_ref, vt_sc, m_sc, l_sc, acc_sc, *, blk):
    qi = pl.program_id(2)
    nk = vt_sc.shape[0]

    @pl.when(qi == 0)
    def _():
        for c in range(nk):
            vt_sc[c] = v_ref[c * blk:(c + 1) * blk, :].astype(F32).T.astype(BF16)

    q = q_ref[...].astype(F32)
    lane = lax.broadcasted_iota(I32, q.shape, 1)
    qa = []
    for h in range(2):
        mine = (lane >= h * HEAD_DIM) & (lane < (h + 1) * HEAD_DIM)
        bias_rows = (lane >= h * BIAS_TERMS) & (lane < (h + 1) * BIAS_TERMS)
        qa.append(jnp.concatenate([jnp.where(mine, q, 0.0), jnp.where(bias_rows, 1.0, 0.0)], axis=1).astype(BF16))
    m_sc[...] = jnp.full(m_sc.shape, -jnp.inf, F32)
    l_sc[...] = jnp.zeros(l_sc.shape, F32)
    acc_sc[...] = jnp.zeros(acc_sc.shape, F32)

    def step(ki, causal):
        ks = pl.multiple_of(ki * blk, blk)
        kb = jnp.concatenate([k_ref[pl.ds(ks, blk), :], a_ref[0, 0, pl.ds(ks, blk), :]], axis=1)
        for h in range(2):
            st = lax.dot_general(kb, qa[h], (((1,), (1,)), ((), ())), preferred_element_type=F32)
            if causal:
                kpos = lax.broadcasted_iota(I32, st.shape, 0)
                qpos = lax.broadcasted_iota(I32, st.shape, 1)
                st = jnp.where(kpos <= qpos, st, -jnp.inf)
            m_prev = m_sc[h]
            m_new = jnp.maximum(m_prev, jnp.max(st, axis=0, keepdims=True))
            alpha = jnp.exp(m_prev - m_new)
            p = jnp.exp(st - m_new)
            l_sc[h] = alpha * l_sc[h] + jnp.sum(p, axis=0, keepdims=True)
            m_sc[h] = m_new
            vt = vt_sc[ki, h * HEAD_DIM:(h + 1) * HEAD_DIM, :]
            pv = jnp.dot(vt, p.astype(BF16), preferred_element_type=F32)
            rows = slice(h * HEAD_DIM, (h + 1) * HEAD_DIM)
            acc_sc[rows, :] = acc_sc[rows, :] * alpha + pv

    def body(ki, carry):
        step(ki, False)
        return carry

    lax.fori_loop(0, qi, body, 0)
    step(qi, True)
    out_t = jnp.concatenate([acc_sc[h * HEAD_DIM:(h + 1) * HEAD_DIM, :] / l_sc[h] for h in range(2)], axis=0)
    o_ref[...] = out_t.T.astype(BF16)


def _fox(q, k, v, fbias, B, S, blk=512):
    T, FW = q.shape
    nq = S // blk
    pairs = FW // LANES
    kern = functools.partial(_fox_kernel, blk=blk)
    return pl.pallas_call(
        kern,
        grid=(B, pairs, nq),
        in_specs=[pl.BlockSpec((blk, LANES), lambda b, p, i: (b * nq + i, p)),
                  pl.BlockSpec((S, LANES), lambda b, p, i: (b, p)),
                  pl.BlockSpec((1, 1, S, LANES), lambda b, p, i: (b, p, 0, 0)),
                  pl.BlockSpec((S, LANES), lambda b, p, i: (b, p))],
        out_specs=pl.BlockSpec((blk, LANES), lambda b, p, i: (b * nq + i, p)),
        out_shape=jax.ShapeDtypeStruct((T, FW), BF16),
        scratch_shapes=[pltpu.VMEM((nq, LANES, blk), BF16), pltpu.VMEM((2, 1, blk), F32),
                        pltpu.VMEM((2, 1, blk), F32), pltpu.VMEM((LANES, blk), F32)],
        compiler_params=_cparams(("parallel", "parallel", "arbitrary")),
        name="fox_attention",
    )(q, k, fbias, v)


def _memattn_kernel(qm_ref, mem_ref, wkv_ref, o_ref, k_sc, v_sc):
    MW = qm_ref.shape[1]

    @pl.when(pl.program_id(1) == 0)
    def _():
        kv = jnp.dot(mem_ref[0].astype(BF16), wkv_ref[...], preferred_element_type=F32)
        k_sc[...] = kv[:, :MW].astype(BF16)
        v_sc[...] = kv[:, MW:].astype(BF16)

    q = qm_ref[...]
    lane = lax.broadcasted_iota(I32, q.shape, 1)
    hd = MW // MEM_HEADS
    out = jnp.zeros(q.shape, F32)
    for h in range(MEM_HEADS):
        mine = (lane >= h * hd) & (lane < (h + 1) * hd)
        qh = jnp.where(mine, q, jnp.zeros_like(q))
        s = lax.dot_general(qh, k_sc[...], (((1,), (1,)), ((), ())), preferred_element_type=F32)
        p = jnp.exp(s - jnp.max(s, axis=1, keepdims=True))
        l = jnp.sum(p, axis=1, keepdims=True)
        o = jnp.dot(p.astype(BF16), v_sc[...], preferred_element_type=F32)
        out = jnp.where(mine, o / l, out)
    o_ref[...] = out.astype(BF16)


def _memattn(qm, mem, wkv, B, S, tq=512):
    T, MW = qm.shape
    M, D = mem.shape[1], mem.shape[2]
    nq = S // tq
    return pl.pallas_call(
        _memattn_kernel,
        grid=(B, nq),
        in_specs=[pl.BlockSpec((tq, MW), lambda b, i: (b * nq + i, 0)),
                  pl.BlockSpec((1, M, D), lambda b, i: (b, 0, 0)),
                  pl.BlockSpec(wkv.shape, lambda b, i: (0, 0))],
        out_specs=pl.BlockSpec((tq, MW), lambda b, i: (b * nq + i, 0)),
        out_shape=jax.ShapeDtypeStruct((T, MW), BF16),
        scratch_shapes=[pltpu.VMEM((M, MW), BF16), pltpu.VMEM((M, MW), BF16)],
        compiler_params=_cparams(("parallel", "arbitrary")),
        name="memory_attention",
    )(qm, mem, wkv)


def _outproj_kernel(yp_ref, yf_ref, ym_ref, x_ref, w_ref, g_ref, b_ref, o_ref, *, alpha):
    pw, fw = yp_ref.shape[1], yf_ref.shape[1]
    h = jnp.dot(yp_ref[...], w_ref[0:pw, :], preferred_element_type=F32)
    h = h + jnp.dot(yf_ref[...], w_ref[pw:pw + fw, :], preferred_element_type=F32)
    h = h + jnp.dot(ym_ref[...], w_ref[pw + fw:, :], preferred_element_type=F32)
    _store_rows(o_ref, _layer_norm(alpha * x_ref[...] + h, g_ref[...], b_ref[...]))


def _outproj(yp, yf, ym, x2, w_out, g, b, alpha, tm=512):
    T, D = x2.shape
    assert D == SUBLANES * LANES
    row = lambda i: (i, 0)
    fixed = lambda i: (0, 0)
    return pl.pallas_call(
        functools.partial(_outproj_kernel, alpha=alpha),
        grid=(T // tm,),
        in_specs=[pl.BlockSpec((tm, yp.shape[1]), row), pl.BlockSpec((tm, yf.shape[1]), row),
                  pl.BlockSpec((tm, ym.shape[1]), row), pl.BlockSpec((tm, D), row),
                  pl.BlockSpec(w_out.shape, fixed), pl.BlockSpec((1, D), fixed), pl.BlockSpec((1, D), fixed)],
        out_specs=pl.BlockSpec((tm * SUBLANES, LANES), row),
        out_shape=jax.ShapeDtypeStruct((T * SUBLANES, LANES), F32),
        compiler_params=_cparams(("parallel",)),
        name="outproj_ln1",
    )(yp, yf, ym, x2, w_out, g, b)


def _router_kernel(x_ref, wh_ref, wl_ref, bias_ref, eidx_ref, gate_ref, rank_ref, cnt_ref, carry_sc):
    tt = x_ref.shape[0] // SUBLANES
    E = wh_ref.shape[1]
    gsz = E // N_GROUPS

    @pl.when(pl.program_id(0) == 0)
    def _():
        carry_sc[...] = jnp.zeros(carry_sc.shape, F32)

    x = _load_rows(x_ref, tt)
    xh = x.astype(BF16)
    xl = (x - xh.astype(F32)).astype(BF16)
    wh = wh_ref[...]
    logits = jnp.dot(xh, wh, preferred_element_type=F32) + (
        jnp.dot(xh, wl_ref[...], preferred_element_type=F32) + jnp.dot(xl, wh, preferred_element_type=F32))
    scores = 1.0 / (1.0 + jnp.exp(-logits))
    biased = scores + bias_ref[...]
    lane = lax.broadcasted_iota(I32, (tt, E), 1)
    ninf = jnp.full((tt, E), -jnp.inf, F32)

    def rmax(a):
        return jnp.max(a, axis=1, keepdims=True)

    def first_at(a, m):
        return jnp.min(jnp.where(a == m, lane, E), axis=1, keepdims=True)

    gscore = []
    for g in range(N_GROUPS):
        mg = jnp.where((lane >= g * gsz) & (lane < (g + 1) * gsz), biased, ninf)
        m1 = rmax(mg)
        m2 = rmax(jnp.where(lane == first_at(mg, m1), ninf, mg))
        gscore.append(m1 + m2)

    masked = ninf
    for g in range(N_GROUPS):
        ahead = jnp.zeros((tt, 1), I32)
        for o in range(N_GROUPS):
            if o == g:
                continue
            beats = gscore[o] > gscore[g]
            if o < g:
                beats = beats | (gscore[o] == gscore[g])
            ahead = ahead + beats.astype(I32)
        keep = (ahead < TOPK_GROUPS) & (lane >= g * gsz) & (lane < (g + 1) * gsz)
        masked = jnp.where(keep, biased, masked)

    picks, sels = [], []
    chosen = jnp.zeros((tt, E), F32)
    for _ in range(TOP_K):
        ik = first_at(masked, rmax(masked))
        oh = lane == ik
        picks.append(ik)
        sels.append(jnp.sum(jnp.where(oh, scores, 0.0), axis=1, keepdims=True))
        chosen = jnp.where(oh, 1.0, chosen)
        masked = jnp.where(oh, ninf, masked)
    denom = sels[0]
    for sk in sels[1:]:
        denom = denom + sk

    r = lax.broadcasted_iota(I32, (tt, tt), 0)
    c = lax.broadcasted_iota(I32, (tt, tt), 1)
    lower = jnp.where(c < r, 1.0, 0.0).astype(BF16)
    pos = carry_sc[...] + jnp.dot(lower, chosen.astype(BF16), preferred_element_type=F32)
    carry_sc[...] = carry_sc[...] + jnp.sum(chosen, axis=0, keepdims=True)
    cnt_ref[...] = carry_sc[...]

    lane_o = lax.broadcasted_iota(I32, (tt, LANES), 1)
    e_out = jnp.zeros((tt, LANES), I32)
    g_out = jnp.zeros((tt, LANES), F32)
    r_out = jnp.zeros((tt, LANES), I32)
    for kk in range(TOP_K):
        rk = jnp.sum(jnp.where(lane == picks[kk], pos, 0.0), axis=1, keepdims=True)
        here = lane_o == kk
        e_out = jnp.where(here, picks[kk], e_out)
        g_out = jnp.where(here, sels[kk] / denom * ROUTED_SCALE, g_out)
        r_out = jnp.where(here, rk.astype(I32), r_out)
    eidx_ref[...] = e_out
    gate_ref[...] = g_out
    rank_ref[...] = r_out


def _router(x1t, wr_hi, wr_lo, rbias, tt=256):
    T = x1t.shape[0] // SUBLANES
    D, E = wr_hi.shape
    row = lambda i: (i, 0)
    fixed = lambda i: (0, 0)
    return pl.pallas_call(
        _router_kernel,
        grid=(T // tt,),
        in_specs=[pl.BlockSpec((tt * SUBLANES, LANES), row), pl.BlockSpec((D, E), fixed),
                  pl.BlockSpec((D, E), fixed), pl.BlockSpec((1, E), fixed)],
        out_specs=[pl.BlockSpec((tt, LANES), row), pl.BlockSpec((tt, LANES), row), pl.BlockSpec((tt, LANES), row),
                   pl.BlockSpec((1, E), fixed)],
        out_shape=[jax.ShapeDtypeStruct((T, LANES), I32), jax.ShapeDtypeStruct((T, LANES), F32),
                   jax.ShapeDtypeStruct((T, LANES), I32), jax.ShapeDtypeStruct((1, E), F32)],
        scratch_shapes=[pltpu.VMEM((1, E), F32)],
        compiler_params=_cparams(("arbitrary",)),
        name="router_topk",
    )(x1t, wr_hi, wr_lo, rbias)


def _dispatch_kernel(zb_ref, d_ref, x_ref, xs_ref, zero_sc, sem, *, tt, n_exp):
    blk_rows = ROW_BLOCK * SUBLANES

    @pl.when(pl.program_id(0) == 0)
    def _():
        zero_sc[...] = jnp.zeros(zero_sc.shape, F32)

        def zstart(e, c):
            @pl.when(zb_ref[e] >= 0)
            def _():
                dst = xs_ref.at[pl.ds(pl.multiple_of(zb_ref[e] * SUBLANES, blk_rows), blk_rows), :]
                pltpu.make_async_copy(zero_sc, dst, sem).start()
            return c

        def zwait(e, c):
            @pl.when(zb_ref[e] >= 0)
            def _():
                pltpu.make_async_copy(zero_sc, xs_ref.at[pl.ds(0, blk_rows), :], sem).wait()
            return c

        lax.fori_loop(0, n_exp, zstart, 0)
        lax.fori_loop(0, n_exp, zwait, 0)

    def start(j, c):
        src = _tile_rows(x_ref, j)
        for kk in range(TOP_K):
            pltpu.make_async_copy(src, _tile_rows(xs_ref, d_ref[j * TOP_K + kk]), sem).start(priority=kk % 2)
        return c

    lax.fori_loop(0, tt, start, 0)
    for kk in range(TOP_K):
        pltpu.make_async_copy(x_ref, xs_ref.at[pl.ds(0, tt * SUBLANES), :], sem).wait()


def _dispatch(zero_blk, dest, x1t, n_rows, tt=256):
    T = x1t.shape[0] // SUBLANES
    n_exp = zero_blk.shape[0]
    grid_spec = pltpu.PrefetchScalarGridSpec(
        num_scalar_prefetch=1,
        grid=(T // tt,),
        in_specs=[pl.BlockSpec((tt * TOP_K,), lambda i, zb: (i,), memory_space=pltpu.SMEM),
                  pl.BlockSpec((tt * SUBLANES, LANES), lambda i, zb: (i, 0))],
        out_specs=pl.BlockSpec(memory_space=pl.ANY),
        scratch_shapes=[pltpu.VMEM((ROW_BLOCK * SUBLANES, LANES), F32), pltpu.SemaphoreType.DMA],
    )
    return pl.pallas_call(
        functools.partial(_dispatch_kernel, tt=tt, n_exp=n_exp),
        grid_spec=grid_spec,
        out_shape=jax.ShapeDtypeStruct((n_rows * SUBLANES, LANES), F32),
        compiler_params=_cparams(("arbitrary",)),
        name="moe_dispatch",
    )(zero_blk, dest, x1t)


def _expert_kernel(be_ref, nu_ref, xs_ref, wg_ref, wu_ref, wd_ref, y_ref, wg_sc, wu_sc, wd_sc):
    i = pl.program_id(0)

    @pl.when(i < nu_ref[0])
    def _():
        e = be_ref[i]
        prev = be_ref[jnp.maximum(i - 1, 0)]

        @pl.when((i == 0) | (e != prev))
        def _():
            wg_sc[...] = wg_ref[0].astype(BF16)
            wu_sc[...] = wu_ref[0].astype(BF16)
            wd_sc[...] = wd_ref[0].astype(BF16)

        xb = _load_rows(xs_ref, ROW_BLOCK).astype(BF16)
        g = jnp.dot(xb, wg_sc[...], preferred_element_type=F32)
        u = jnp.dot(xb, wu_sc[...], preferred_element_type=F32)
        h = (_silu(g) * u).astype(BF16)
        _store_rows(y_ref, jnp.dot(h, wd_sc[...], preferred_element_type=F32))


def _experts(blk_e, n_used, xs, w_gate, w_up, w_down):
    n_rows = xs.shape[0] // SUBLANES
    E, D, H = w_gate.shape
    n_blk = n_rows // ROW_BLOCK
    rows = lambda i, be, nu: (jnp.minimum(i, nu[0] - 1), 0)
    wsel = lambda i, be, nu: (be[i], 0, 0)
    grid_spec = pltpu.PrefetchScalarGridSpec(
        num_scalar_prefetch=2,
        grid=(n_blk,),
        in_specs=[pl.BlockSpec((ROW_BLOCK * SUBLANES, LANES), rows),
                  pl.BlockSpec((1, D, H), wsel), pl.BlockSpec((1, D, H), wsel), pl.BlockSpec((1, H, D), wsel)],
        out_specs=pl.BlockSpec((ROW_BLOCK * SUBLANES, LANES), rows),
        scratch_shapes=[pltpu.VMEM((D, H), BF16), pltpu.VMEM((D, H), BF16), pltpu.VMEM((H, D), BF16)],
    )
    return pl.pallas_call(
        _expert_kernel,
        grid_spec=grid_spec,
        out_shape=jax.ShapeDtypeStruct((n_rows * SUBLANES, LANES), F32),
        compiler_params=_cparams(("arbitrary",)),
        name="moe_experts",
    )(blk_e, n_used, xs, w_gate, w_up, w_down)


def _combine_kernel(dcur_ref, dnxt_ref, x_ref, gate_ref, wsg_ref, wsu_ref, wsd_ref, g_ref, b_ref, y_ref,
                    o_ref, buf, sem, *, tt, alpha):
    i = pl.program_id(0)
    n = pl.num_programs(0)

    def fetch(d_ref, slot):
        def body(j, c):
            for kk in range(TOP_K):
                src = _tile_rows(y_ref, d_ref[j * TOP_K + kk])
                pltpu.make_async_copy(src, _tile_rows(buf.at[slot, kk], j), sem.at[slot]).start(priority=kk % 2)
            return c

        lax.fori_loop(0, tt, body, 0)

    def reduce(slot):
        for kk in range(TOP_K):
            pltpu.make_async_copy(y_ref.at[pl.ds(0, tt * SUBLANES), :], buf.at[slot, kk], sem.at[slot]).wait()
        x = _load_rows(x_ref, tt)
        xb = x.astype(BF16)
        hs = _silu(jnp.dot(xb, wsg_ref[...], preferred_element_type=F32)) * jnp.dot(
            xb, wsu_ref[...], preferred_element_type=F32)
        moe = jnp.dot(hs.astype(BF16), wsd_ref[...], preferred_element_type=F32)
        gates = gate_ref[...]
        for kk in range(TOP_K):
            moe = moe + gates[:, kk:kk + 1] * _load_rows(buf.at[slot, kk], tt)
        o_ref[...] = _layer_norm(alpha * x + moe, g_ref[...], b_ref[...])

    @pl.when(i == 0)
    def _():
        fetch(dcur_ref, 0)

    for slot in range(2):
        @pl.when((i % 2 == slot) & (i + 1 < n))
        def _():
            fetch(dnxt_ref, 1 - slot)

    for slot in range(2):
        @pl.when(i % 2 == slot)
        def _():
            reduce(slot)


def _combine(dest, x1t, gates, wsg, wsu, wsd, g, b, y, alpha, tt=128):
    T = x1t.shape[0] // SUBLANES
    D = SUBLANES * LANES
    n = T // tt
    row = lambda i: (i, 0)
    fixed = lambda i: (0, 0)
    return pl.pallas_call(
        functools.partial(_combine_kernel, tt=tt, alpha=alpha),
        grid=(n,),
        in_specs=[pl.BlockSpec((tt * TOP_K,), lambda i: (i,), memory_space=pltpu.SMEM),
                  pl.BlockSpec((tt * TOP_K,), lambda i: (jnp.minimum(i + 1, n - 1),), memory_space=pltpu.SMEM),
                  pl.BlockSpec((tt * SUBLANES, LANES), row), pl.BlockSpec((tt, LANES), row),
                  pl.BlockSpec(wsg.shape, fixed), pl.BlockSpec(wsu.shape, fixed), pl.BlockSpec(wsd.shape, fixed),
                  pl.BlockSpec((1, D), fixed), pl.BlockSpec((1, D), fixed),
                  pl.BlockSpec(memory_space=pl.ANY)],
        out_specs=pl.BlockSpec((tt, D), row),
        out_shape=jax.ShapeDtypeStruct((T, D), F32),
        scratch_shapes=[pltpu.VMEM((2, TOP_K, tt * SUBLANES, LANES), F32), pltpu.SemaphoreType.DMA((2,))],
        compiler_params=_cparams(("arbitrary",)),
        name="moe_combine_ln2",
    )(dest, dest, x1t, gates, wsg, wsu, wsd, g, b, y)


def _layer(x2, mem, B, S, depth, w_in, b_f, w_pool, pool_scale, w_mem_kv, w_out, ln1_g, ln1_b,
           w_router, router_bias, w_gate, w_up, w_down, ws_gate, ws_up, ws_down, ln2_g, ln2_b):
    T, D = x2.shape
    n_win, grp = w_pool.shape[0], w_pool.shape[1]
    pw = n_win * grp
    fw = FOX_HEADS * HEAD_DIM
    mw = w_mem_kv.shape[1] // 2
    E = w_router.shape[1]
    alpha = (2 * depth) ** 0.25

    f_lo = pw + 3 * fw
    w_main = jnp.concatenate([w_in[:, :f_lo], w_in[:, f_lo + FOX_HEADS:]], axis=1).astype(BF16)
    w_f = jnp.pad(w_in[:, f_lo:f_lo + FOX_HEADS], ((0, 0), (0, LANES - FOX_HEADS))).astype(BF16)
    bf_pad = jnp.pad(b_f, (0, LANES - FOX_HEADS)).reshape(1, LANES)
    wbd = jnp.zeros((pw, pw), F32)
    for g in range(n_win):
        wbd = wbd.at[g * grp:(g + 1) * grp, g * grp:(g + 1) * grp].set(w_pool[g])
    wr_hi = w_router.astype(BF16)
    wr_lo = (w_router - wr_hi.astype(F32)).astype(BF16)

    u_pool, q, k, v, q_mem, f_logit = _inproj(x2, w_main, w_f, pw, fw, mw)
    y_pool = _pool(u_pool, wbd.astype(BF16), pool_scale.reshape(1, pw), B, S)
    fcum = _fgate(f_logit, bf_pad, B, S)
    y_fox = _fox(q, k, v, fcum, B, S)
    y_mem = _memattn(q_mem, mem, w_mem_kv.astype(BF16), B, S)
    x1 = _outproj(y_pool, y_fox, y_mem, x2, w_out.astype(BF16), ln1_g.reshape(1, D), ln1_b.reshape(1, D), alpha)

    eidx, gates, rank, counts = _router(x1, wr_hi, wr_lo, router_bias.reshape(1, E))

    cnt = counts.reshape(E).astype(I32)
    padded = (cnt + ROW_BLOCK - 1) // ROW_BLOCK * ROW_BLOCK
    pad_end = jnp.cumsum(padded)
    start_pad = (pad_end - padded).astype(I32)
    n_rows = (T * TOP_K + E * (ROW_BLOCK - 1)) // ROW_BLOCK * ROW_BLOCK
    n_blk = n_rows // ROW_BLOCK
    n_used = (pad_end[-1] // ROW_BLOCK).astype(I32)
    blk_ids = jnp.minimum(jnp.arange(n_blk, dtype=I32), n_used - 1)
    blk_e = jnp.sum((pad_end[None, :] <= (blk_ids * ROW_BLOCK)[:, None]).astype(I32), axis=1)
    blk_e = jnp.minimum(blk_e, E - 1)
    zero_blk = jnp.where(padded > 0, pad_end - ROW_BLOCK, -1).astype(I32)
    e_sel = eidx[:, :TOP_K]
    sel_start = jnp.sum(jnp.where(e_sel[:, :, None] == jnp.arange(E, dtype=I32)[None, None, :],
                                  start_pad[None, None, :], 0), axis=-1)
    dest = (sel_start + rank[:, :TOP_K]).reshape(T * TOP_K)

    xs = _dispatch(zero_blk, dest, x1, n_rows)
    y = _experts(blk_e, n_used.reshape(1), xs, w_gate, w_up, w_down)
    return _combine(dest, x1, gates, ws_gate.astype(BF16), ws_up.astype(BF16),
                    ws_down.astype(BF16), ln2_g.reshape(1, D), ln2_b.reshape(1, D), y, alpha)


def kernel(x, mem, w_in, b_f, w_pool, pool_scale, w_mem_kv, w_out, ln1_g, ln1_b, w_router, router_bias,
           w_gate, w_up, w_down, ws_gate, ws_up, ws_down, ln2_g, ln2_b):
    B, S, D = x.shape
    depth = w_in.shape[0]
    x2 = x.reshape(B * S, D)
    for l in range(depth):
        x2 = _layer(x2, mem, B, S, depth, w_in[l], b_f[l], w_pool[l], pool_scale[l], w_mem_kv[l], w_out[l],
                    ln1_g[l], ln1_b[l], w_router[l], router_bias[l], w_gate[l], w_up[l], w_down[l],
                    ws_gate[l], ws_up[l], ws_down[l], ln2_g[l], ln2_b[l])
    return x2.reshape(B, S, D)
```

```python
import functools

import jax
import jax.numpy as jnp
from jax import lax
from jax.experimental import pallas as pl
from jax.experimental.pallas import tpu as pltpu

F32 = jnp.float32
BF16 = jnp.bfloat16
I32 = jnp.int32

LANES = 128
SUBLANES = 8
POOL_WINDOWS = (2, 4, 8, 16)
POOL_HALO = 16
HEAD_DIM = 64
FOX_HEADS = 8
BIAS_TERMS = 3
MEM_HEADS = 4
TOP_K = 8
N_GROUPS = 8
TOPK_GROUPS = 4
ROUTED_SCALE = 2.5
LN_EPS = 1e-5
ROW_BLOCK = 256
VMEM_LIMIT = 48 * 1024 * 1024


def _cparams(sem):
    return pltpu.CompilerParams(dimension_semantics=sem, vmem_limit_bytes=VMEM_LIMIT)


def _layer_norm(z, g, b):
    mu = jnp.mean(z, axis=-1, keepdims=True)
    zc = z - mu
    var = jnp.mean(zc * zc, axis=-1, keepdims=True)
    return zc * lax.rsqrt(var + LN_EPS) * g + b


def _silu(x):
    return x * (1.0 / (1.0 + jnp.exp(-x)))


def _load_rows(ref, n):
    return jnp.concatenate([ref[pl.ds(j, n, stride=SUBLANES), :] for j in range(SUBLANES)], axis=1)


def _store_rows(ref, val):
    n = val.shape[0]
    for j in range(SUBLANES):
        ref[pl.ds(j, n, stride=SUBLANES), :] = val[:, j * LANES:(j + 1) * LANES]


def _tile_rows(ref, r):
    return ref.at[pl.ds(pl.multiple_of(r * SUBLANES, SUBLANES), SUBLANES), :]


def _inproj_kernel(x_ref, w_ref, wf_ref, up_ref, q_ref, k_ref, v_ref, qm_ref, f_ref, *, pw, fw, mw, scale):
    xb = x_ref[...].astype(BF16)

    def proj(lo, hi):
        return jnp.dot(xb, w_ref[:, lo:hi], preferred_element_type=F32)

    up_ref[...] = proj(0, pw)
    q_ref[...] = (proj(pw, pw + fw) * scale).astype(BF16)
    k_ref[...] = proj(pw + fw, pw + 2 * fw).astype(BF16)
    v_ref[...] = proj(pw + 2 * fw, pw + 3 * fw).astype(BF16)
    qm_ref[...] = (proj(pw + 3 * fw, pw + 3 * fw + mw) * scale).astype(BF16)
    f_ref[...] = jnp.dot(xb, wf_ref[...], preferred_element_type=F32)


def _inproj(x2, w_main, w_f, pw, fw, mw, tm=512):
    T, D = x2.shape
    kern = functools.partial(_inproj_kernel, pw=pw, fw=fw, mw=mw, scale=HEAD_DIM ** -0.5)
    row = lambda i: (i, 0)
    fixed = lambda i: (0, 0)
    return pl.pallas_call(
        kern,
        grid=(T // tm,),
        in_specs=[pl.BlockSpec((tm, D), row),
                  pl.BlockSpec(w_main.shape, fixed),
                  pl.BlockSpec(w_f.shape, fixed)],
        out_specs=[pl.BlockSpec((tm, pw), row), pl.BlockSpec((tm, fw), row), pl.BlockSpec((tm, fw), row),
                   pl.BlockSpec((tm, fw), row), pl.BlockSpec((tm, mw), row), pl.BlockSpec((tm, LANES), row)],
        out_shape=[jax.ShapeDtypeStruct((T, pw), F32), jax.ShapeDtypeStruct((T, fw), BF16),
                   jax.ShapeDtypeStruct((T, fw), BF16), jax.ShapeDtypeStruct((T, fw), BF16),
                   jax.ShapeDtypeStruct((T, mw), BF16), jax.ShapeDtypeStruct((T, LANES), F32)],
        compiler_params=_cparams(("parallel",)),
        name="inproj",
    )(x2, w_main, w_f)


def _pool_kernel(u_ref, wbd_ref, sc_ref, o_ref, ext_sc, *, chunk, group):
    S, W = u_ref.shape
    ext_sc[0:POOL_HALO, :] = jnp.zeros((POOL_HALO, W), F32)
    ext_sc[POOL_HALO:, :] = u_ref[...]
    rows = chunk + POOL_HALO
    lrow = lax.broadcasted_iota(I32, (rows, W), 0)
    lane = lax.broadcasted_iota(I32, (rows, W), 1)

    def body(c, carry):
        start = pl.multiple_of(c * chunk, chunk)
        e = ext_sc[pl.ds(start, rows), :]
        posf = (lrow + (start - POOL_HALO + 1)).astype(F32)
        acc = e
        d = jnp.zeros_like(e)
        shift = 1
        for g, w in enumerate(POOL_WINDOWS):
            while shift < w:
                acc = acc + pltpu.roll(acc, shift, axis=0)
                shift *= 2
            mean = acc / jnp.minimum(posf, float(w))
            d = jnp.where((lane >= g * group) & (lane < (g + 1) * group), mean, d)
        d = (d - e)[POOL_HALO:, :]
        y = jnp.dot(d.astype(BF16), wbd_ref[...], preferred_element_type=F32) * sc_ref[...]
        o_ref[pl.ds(start, chunk), :] = y.astype(BF16)
        return carry

    lax.fori_loop(0, S // chunk, body, 0)


def _pool(u, wbd, pscale, B, S, chunk=512):
    T, W = u.shape
    kern = functools.partial(_pool_kernel, chunk=chunk, group=W // len(POOL_WINDOWS))
    return pl.pallas_call(
        kern,
        grid=(B,),
        in_specs=[pl.BlockSpec((S, W), lambda b: (b, 0)),
                  pl.BlockSpec((W, W), lambda b: (0, 0)),
                  pl.BlockSpec((1, W), lambda b: (0, 0))],
        out_specs=pl.BlockSpec((S, W), lambda b: (b, 0)),
        out_shape=jax.ShapeDtypeStruct((T, W), BF16),
        scratch_shapes=[pltpu.VMEM((S + POOL_HALO, W), F32)],
        compiler_params=_cparams(("parallel",)),
        name="pool_mixer",
    )(u, wbd, pscale)


def _fgate_kernel(f_ref, bf_ref, o_ref):
    S = f_ref.shape[0]
    z = f_ref[...] + bf_ref[...]
    x = jnp.minimum(z, 0.0) - jnp.log(1.0 + jnp.exp(-jnp.abs(z)))
    row = lax.broadcasted_iota(I32, x.shape, 0)
    lane = lax.broadcasted_iota(I32, x.shape, 1)
    shift = 1
    while shift < S:
        x = x + jnp.where(row >= shift, pltpu.roll(x, shift, axis=0), 0.0)
        shift *= 2
    hi = x.astype(BF16).astype(F32)
    mid = (x - hi).astype(BF16).astype(F32)
    lo = (x - hi) - mid
    for p in range(FOX_HEADS // 2):
        out = jnp.zeros(x.shape, F32)
        for hh in range(2):
            c = 2 * p + hh
            for j, part in enumerate((hi, mid, lo)):
                out = jnp.where(lane == BIAS_TERMS * hh + j, -part[:, c:c + 1], out)
        o_ref[0, p] = out.astype(BF16)


def _fgate(f_logit, bf_pad, B, S):
    return pl.pallas_call(
        _fgate_kernel,
        grid=(B,),
        in_specs=[pl.BlockSpec((S, LANES), lambda b: (b, 0)),
                  pl.BlockSpec((1, LANES), lambda b: (0, 0))],
        out_specs=pl.BlockSpec((1, FOX_HEADS // 2, S, LANES), lambda b: (b, 0, 0, 0)),
        out_shape=jax.ShapeDtypeStruct((B, FOX_HEADS // 2, S, LANES), BF16),
        compiler_params=_cparams(("parallel",)),
        name="forget_cumsum",
    )(f_logit, bf_pad)


def _fox_kernel(q_ref, k_ref, a_ref, v_ref, o_ref, vt_sc, m_sc, l_sc, acc_sc, *, blk):
    qi = pl.program_id(2)
    nk = vt_sc.shape[0]

    @pl.when(qi == 0)
    def _():
        for c in range(nk):
            vt_sc[c] = v_ref[c * blk:(c + 1) * blk, :].astype(F32).T.astype(BF16)

    q = q_ref[...].astype(F32)
    lane = lax.broadcasted_iota(I32, q.shape, 1)
    qa = []
    for h in range(2):
        mine = (lane >= h * HEAD_DIM) & (lane < (h + 1) * HEAD_DIM)
        bias_rows = (lane >= h * BIAS_TERMS) & (lane < (h + 1) * BIAS_TERMS)
        qa.append(jnp.concatenate([jnp.where(mine, q, 0.0), jnp.where(bias_rows, 1.0, 0.0)], axis=1).astype(BF16))
    m_sc[...] = jnp.full(m_sc.shape, -jnp.inf, F32)
    l_sc[...] = jnp.zeros(l_sc.shape, F32)
    acc_sc[...] = jnp.zeros(acc_sc.shape, F32)

    def step(ki, causal):
        ks = pl.multiple_of(ki * blk, blk)
        kb = jnp.concatenate([k_ref[pl.ds(ks, blk), :], a_ref[0, 0, pl.ds(ks, blk), :]], axis=1)
        for h in range(2):
            st = lax.dot_general(kb, qa[h], (((1,), (1,)), ((), ())), preferred_element_type=F32)
            if causal:
                kpos = lax.broadcasted_iota(I32, st.shape, 0)
                qpos = lax.broadcasted_iota(I32, st.shape, 1)
                st = jnp.where(kpos <= qpos, st, -jnp.inf)
            m_prev = m_sc[h]
            m_new = jnp.maximum(m_prev, jnp.max(st, axis=0, keepdims=True))
            alpha = jnp.exp(m_prev - m_new)
            p = jnp.exp(st - m_new)
            l_sc[h] = alpha * l_sc[h] + jnp.sum(p, axis=0, keepdims=True)
            m_sc[h] = m_new
            vt = vt_sc[ki, h * HEAD_DIM:(h + 1) * HEAD_DIM, :]
            pv = jnp.dot(vt, p.astype(BF16), preferred_element_type=F32)
            rows = slice(h * HEAD_DIM, (h + 1) * HEAD_DIM)
            acc_sc[rows, :] = acc_sc[rows, :] * alpha + pv

    def body(ki, carry):
        step(ki, False)
        return carry

    lax.fori_loop(0, qi, body, 0)
    step(qi, True)
    out_t = jnp.concatenate([acc_sc[h * HEAD_DIM:(h + 1) * HEAD_DIM, :] / l_sc[h] for h in range(2)], axis=0)
    o_ref[...] = out_t.T.astype(BF16)


def _fox(q, k, v, fbias, B, S, blk=512):
    T, FW = q.shape
    nq = S // blk
    pairs = FW // LANES
    kern = functools.partial(_fox_kernel, blk=blk)
    return pl.pallas_call(
        kern,
        grid=(B, pairs, nq),
        in_specs=[pl.BlockSpec((blk, LANES), lambda b, p, i: (b * nq + i, p)),
                  pl.BlockSpec((S, LANES), lambda b, p, i: (b, p)),
                  pl.BlockSpec((1, 1, S, LANES), lambda b, p, i: (b, p, 0, 0)),
                  pl.BlockSpec((S, LANES), lambda b, p, i: (b, p))],
        out_specs=pl.BlockSpec((blk, LANES), lambda b, p, i: (b * nq + i, p)),
        out_shape=jax.ShapeDtypeStruct((T, FW), BF16),
        scratch_shapes=[pltpu.VMEM((nq, LANES, blk), BF16), pltpu.VMEM((2, 1, blk), F32),
                        pltpu.VMEM((2, 1, blk), F32), pltpu.VMEM((LANES, blk), F32)],
        compiler_params=_cparams(("parallel", "parallel", "arbitrary")),
        name="fox_attention",
    )(q, k, fbias, v)


def _memattn_kernel(qm_ref, mem_ref, wkv_ref, o_ref, k_sc, v_sc):
    MW = qm_ref.shape[1]

    @pl.when(pl.program_id(1) == 0)
    def _():
        kv = jnp.dot(mem_ref[0].astype(BF16), wkv_ref[...], preferred_element_type=F32)
        k_sc[...] = kv[:, :MW].astype(BF16)
        v_sc[...] = kv[:, MW:].astype(BF16)

    q = qm_ref[...]
    lane = lax.broadcasted_iota(I32, q.shape, 1)
    hd = MW // MEM_HEADS
    out = jnp.zeros(q.shape, F32)
    for h in range(MEM_HEADS):
        mine = (lane >= h * hd) & (lane < (h + 1) * hd)
        qh = jnp.where(mine, q, jnp.zeros_like(q))
        s = lax.dot_general(qh, k_sc[...], (((1,), (1,)), ((), ())), preferred_element_type=F32)
        p = jnp.exp(s - jnp.max(s, axis=1, keepdims=True))
        l = jnp.sum(p, axis=1, keepdims=True)
        o = jnp.dot(p.astype(BF16), v_sc[...], preferred_element_type=F32)
        out = jnp.where(mine, o / l, out)
    o_ref[...] = out.astype(BF16)


def _memattn(qm, mem, wkv, B, S, tq=512):
    T, MW = qm.shape
    M, D = mem.shape[1], mem.shape[2]
    nq = S // tq
    return pl.pallas_call(
        _memattn_kernel,
        grid=(B, nq),
        in_specs=[pl.BlockSpec((tq, MW), lambda b, i: (b * nq + i, 0)),
                  pl.BlockSpec((1, M, D), lambda b, i: (b, 0, 0)),
                  pl.BlockSpec(wkv.shape, lambda b, i: (0, 0))],
        out_specs=pl.BlockSpec((tq, MW), lambda b, i: (b * nq + i, 0)),
        out_shape=jax.ShapeDtypeStruct((T, MW), BF16),
        scratch_shapes=[pltpu.VMEM((M, MW), BF16), pltpu.VMEM((M, MW), BF16)],
        compiler_params=_cparams(("parallel", "arbitrary")),
        name="memory_attention",
    )(qm, mem, wkv)


def _outproj_kernel(yp_ref, yf_ref, ym_ref, x_ref, w_ref, g_ref, b_ref, o_ref, *, alpha):
    pw, fw = yp_ref.shape[1], yf_ref.shape[1]
    h = jnp.dot(yp_ref[...], w_ref[0:pw, :], preferred_element_type=F32)
    h = h + jnp.dot(yf_ref[...], w_ref[pw:pw + fw, :], preferred_element_type=F32)
    h = h + jnp.dot(ym_ref[...], w_ref[pw + fw:, :], preferred_element_type=F32)
    _store_rows(o_ref, _layer_norm(alpha * x_ref[...] + h, g_ref[...], b_ref[...]))


def _outproj(yp, yf, ym, x2, w_out, g, b, alpha, tm=512):
    T, D = x2.shape
    assert D == SUBLANES * LANES
    row = lambda i: (i, 0)
    fixed = lambda i: (0, 0)
    return pl.pallas_call(
        functools.partial(_outproj_kernel, alpha=alpha),
        grid=(T // tm,),
        in_specs=[pl.BlockSpec((tm, yp.shape[1]), row), pl.BlockSpec((tm, yf.shape[1]), row),
                  pl.BlockSpec((tm, ym.shape[1]), row), pl.BlockSpec((tm, D), row),
                  pl.BlockSpec(w_out.shape, fixed), pl.BlockSpec((1, D), fixed), pl.BlockSpec((1, D), fixed)],
        out_specs=pl.BlockSpec((tm * SUBLANES, LANES), row),
        out_shape=jax.ShapeDtypeStruct((T * SUBLANES, LANES), F32),
        compiler_params=_cparams(("parallel",)),
        name="outproj_ln1",
    )(yp, yf, ym, x2, w_out, g, b)


ROUTE_TILE = 512


def _per_expert_column(v, dtype=F32):
    return jnp.broadcast_to(v.astype(dtype)[:, None], (v.shape[0], ROUTE_TILE))


def _router_kernel(x_ref, wh_ref, wl_ref, bias_ref, eidx_ref, gate_ref, rank_ref, cnt_ref, carry_sc, *, tt):
    E = wh_ref.shape[0]
    gsz = E // N_GROUPS
    ninf = -jnp.inf

    @pl.when(pl.program_id(0) == 0)
    def _():
        carry_sc[...] = jnp.zeros(carry_sc.shape, F32)

    x = _load_rows(x_ref, tt)
    xh = x.astype(BF16)
    xl = (x - xh.astype(F32)).astype(BF16)
    wh = wh_ref[...]
    nt = (((1,), (1,)), ((), ()))
    logits = lax.dot_general(wh, xh, nt, preferred_element_type=F32) + (
        lax.dot_general(wl_ref[...], xh, nt, preferred_element_type=F32)
        + lax.dot_general(wh, xl, nt, preferred_element_type=F32))
    scores = 1.0 / (1.0 + jnp.exp(-logits))
    biased = scores + bias_ref[...]
    eiota = lax.broadcasted_iota(I32, (E, tt), 0).astype(F32)
    giota = lax.broadcasted_iota(I32, (gsz, tt), 0).astype(F32)

    def cmax(a):
        return jnp.max(a, axis=0, keepdims=True)

    def first_at(a, m, iota, n):
        return jnp.min(jnp.where(a == m, iota, float(n)), axis=0, keepdims=True)

    groups = [biased[g * gsz:(g + 1) * gsz, :] for g in range(N_GROUPS)]
    gscore = []
    for blk in groups:
        m1 = cmax(blk)
        m2 = cmax(jnp.where(giota == first_at(blk, m1, giota, gsz), ninf, blk))
        gscore.append(m1 + m2)

    kept = []
    for g in range(N_GROUPS):
        ahead = jnp.zeros((1, tt), F32)
        for o in range(N_GROUPS):
            if o == g:
                continue
            beats = gscore[o] > gscore[g]
            if o < g:
                beats = beats | (gscore[o] == gscore[g])
            ahead = ahead + jnp.where(beats, 1.0, 0.0)
        kept.append(groups[g] + jnp.where(ahead < float(TOPK_GROUPS), 0.0, ninf))
    masked = jnp.concatenate(kept, axis=0)

    picks, sels = [], []
    chosen = jnp.zeros((E, tt), F32)
    for _ in range(TOP_K):
        ik = first_at(masked, cmax(masked), eiota, E)
        oh = eiota == ik
        picks.append(ik)
        sels.append(jnp.sum(jnp.where(oh, scores, 0.0), axis=0, keepdims=True))
        chosen = jnp.where(oh, 1.0, chosen)
        masked = jnp.where(oh, ninf, masked)
    denom = sels[0]
    for sk in sels[1:]:
        denom = denom + sk

    r = lax.broadcasted_iota(I32, (tt, tt), 0)
    c = lax.broadcasted_iota(I32, (tt, tt), 1)
    earlier = jnp.where(r < c, 1.0, 0.0).astype(BF16)
    chosen_b = chosen.astype(BF16)
    pos = carry_sc[...] + jnp.dot(chosen_b, earlier, preferred_element_type=F32)
    carry_sc[...] = carry_sc[...] + jnp.dot(chosen_b, jnp.ones((tt, tt), BF16), preferred_element_type=F32)
    cnt_ref[...] = carry_sc[...]

    ranks = [jnp.sum(jnp.where(eiota == ik, pos, 0.0), axis=0, keepdims=True) for ik in picks]
    eidx_ref[...] = jnp.concatenate(picks, axis=0).astype(I32)
    gate_ref[...] = jnp.concatenate([sk / denom * ROUTED_SCALE for sk in sels], axis=0)
    rank_ref[...] = jnp.concatenate(ranks, axis=0).astype(I32)


def _router(x1t, wrt_hi, wrt_lo, rbias):
    T = x1t.shape[0] // SUBLANES
    E, D = wrt_hi.shape
    tt = ROUTE_TILE
    tok = lambda i: (0, i)
    fixed = lambda i: (0, 0)
    return pl.pallas_call(
        functools.partial(_router_kernel, tt=tt),
        grid=(T // tt,),
        in_specs=[pl.BlockSpec((tt * SUBLANES, LANES), lambda i: (i, 0)), pl.BlockSpec((E, D), fixed),
                  pl.BlockSpec((E, D), fixed), pl.BlockSpec((E, tt), fixed)],
        out_specs=[pl.BlockSpec((TOP_K, tt), tok), pl.BlockSpec((TOP_K, tt), tok), pl.BlockSpec((TOP_K, tt), tok),
                   pl.BlockSpec((E, tt), fixed)],
        out_shape=[jax.ShapeDtypeStruct((TOP_K, T), I32), jax.ShapeDtypeStruct((TOP_K, T), F32),
                   jax.ShapeDtypeStruct((TOP_K, T), I32), jax.ShapeDtypeStruct((E, tt), F32)],
        scratch_shapes=[pltpu.VMEM((E, tt), F32)],
        compiler_params=_cparams(("arbitrary",)),
        name="router_topk",
    )(x1t, wrt_hi, wrt_lo, rbias)


def _dest_kernel(e_ref, r_ref, sp_ref, d_ref):
    E = sp_ref.shape[0]
    tt = e_ref.shape[1]
    eiota = lax.broadcasted_iota(I32, (E, tt), 0)
    sp = sp_ref[...]
    rows = []
    for kk in range(TOP_K):
        start = jnp.sum(jnp.where(eiota == e_ref[kk:kk + 1, :], sp, 0.0), axis=0, keepdims=True)
        rows.append(start.astype(I32) + r_ref[kk:kk + 1, :])
    d_ref[...] = jnp.concatenate(rows, axis=0)


def _dest(eidx, rank, start_pad_b):
    K, T = eidx.shape
    tt = ROUTE_TILE
    tok = lambda i: (0, i)
    return pl.pallas_call(
        _dest_kernel,
        grid=(T // tt,),
        in_specs=[pl.BlockSpec((K, tt), tok), pl.BlockSpec((K, tt), tok),
                  pl.BlockSpec(start_pad_b.shape, lambda i: (0, 0))],
        out_specs=pl.BlockSpec((K, tt), tok),
        out_shape=jax.ShapeDtypeStruct((K, T), I32),
        compiler_params=_cparams(("parallel",)),
        name="moe_dest",
    )(eidx, rank, start_pad_b)


def _dispatch_kernel(zb_ref, d_ref, x_ref, xs_ref, zero_sc, sem, *, tt, n_exp):
    blk_rows = ROW_BLOCK * SUBLANES

    @pl.when(pl.program_id(0) == 0)
    def _():
        zero_sc[...] = jnp.zeros(zero_sc.shape, F32)

        def zstart(e, c):
            @pl.when(zb_ref[e] >= 0)
            def _():
                dst = xs_ref.at[pl.ds(pl.multiple_of(zb_ref[e] * SUBLANES, blk_rows), blk_rows), :]
                pltpu.make_async_copy(zero_sc, dst, sem).start()
            return c

        def zwait(e, c):
            @pl.when(zb_ref[e] >= 0)
            def _():
                pltpu.make_async_copy(zero_sc, xs_ref.at[pl.ds(0, blk_rows), :], sem).wait()
            return c

        lax.fori_loop(0, n_exp, zstart, 0)
        lax.fori_loop(0, n_exp, zwait, 0)

    def start(j, c):
        src = _tile_rows(x_ref, j)
        for kk in range(TOP_K):
            pltpu.make_async_copy(src, _tile_rows(xs_ref, d_ref[kk, j]), sem).start(priority=kk % 2)
        return c

    lax.fori_loop(0, tt, start, 0)
    for kk in range(TOP_K):
        pltpu.make_async_copy(x_ref, xs_ref.at[pl.ds(0, tt * SUBLANES), :], sem).wait()


def _dispatch(zero_blk, dest, x1t, n_rows, tt=256):
    T = x1t.shape[0] // SUBLANES
    n_exp = zero_blk.shape[0]
    grid_spec = pltpu.PrefetchScalarGridSpec(
        num_scalar_prefetch=1,
        grid=(T // tt,),
        in_specs=[pl.BlockSpec((TOP_K, tt), lambda i, zb: (0, i), memory_space=pltpu.SMEM),
                  pl.BlockSpec((tt * SUBLANES, LANES), lambda i, zb: (i, 0))],
        out_specs=pl.BlockSpec(memory_space=pl.ANY),
        scratch_shapes=[pltpu.VMEM((ROW_BLOCK * SUBLANES, LANES), F32), pltpu.SemaphoreType.DMA],
    )
    return pl.pallas_call(
        functools.partial(_dispatch_kernel, tt=tt, n_exp=n_exp),
        grid_spec=grid_spec,
        out_shape=jax.ShapeDtypeStruct((n_rows * SUBLANES, LANES), F32),
        compiler_params=_cparams(("arbitrary",)),
        name="moe_dispatch",
    )(zero_blk, dest, x1t)


def _expert_kernel(be_ref, nu_ref, xs_ref, wg_ref, wu_ref, wd_ref, y_ref, wg_sc, wu_sc, wd_sc):
    i = pl.program_id(0)

    @pl.when(i < nu_ref[0])
    def _():
        e = be_ref[i]
        prev = be_ref[jnp.maximum(i - 1, 0)]

        @pl.when((i == 0) | (e != prev))
        def _():
            wg_sc[...] = wg_ref[0].astype(BF16)
            wu_sc[...] = wu_ref[0].astype(BF16)
            wd_sc[...] = wd_ref[0].astype(BF16)

        xb = _load_rows(xs_ref, ROW_BLOCK).astype(BF16)
        g = jnp.dot(xb, wg_sc[...], preferred_element_type=F32)
        u = jnp.dot(xb, wu_sc[...], preferred_element_type=F32)
        h = (_silu(g) * u).astype(BF16)
        _store_rows(y_ref, jnp.dot(h, wd_sc[...], preferred_element_type=F32))


def _experts(blk_e, n_used, xs, w_gate, w_up, w_down):
    n_rows = xs.shape[0] // SUBLANES
    E, D, H = w_gate.shape
    n_blk = n_rows // ROW_BLOCK
    rows = lambda i, be, nu: (jnp.minimum(i, nu[0] - 1), 0)
    wsel = lambda i, be, nu: (be[i], 0, 0)
    grid_spec = pltpu.PrefetchScalarGridSpec(
        num_scalar_prefetch=2,
        grid=(n_blk,),
        in_specs=[pl.BlockSpec((ROW_BLOCK * SUBLANES, LANES), rows),
                  pl.BlockSpec((1, D, H), wsel), pl.BlockSpec((1, D, H), wsel), pl.BlockSpec((1, H, D), wsel)],
        out_specs=pl.BlockSpec((ROW_BLOCK * SUBLANES, LANES), rows),
        scratch_shapes=[pltpu.VMEM((D, H), BF16), pltpu.VMEM((D, H), BF16), pltpu.VMEM((H, D), BF16)],
    )
    return pl.pallas_call(
        _expert_kernel,
        grid_spec=grid_spec,
        out_shape=jax.ShapeDtypeStruct((n_rows * SUBLANES, LANES), F32),
        compiler_params=_cparams(("arbitrary",)),
        name="moe_experts",
    )(blk_e, n_used, xs, w_gate, w_up, w_down)


def _combine_kernel(dcur_ref, dnxt_ref, x_ref, gate_ref, wsg_ref, wsu_ref, wsd_ref, g_ref, b_ref, y_ref,
                    o_ref, buf, sem, *, tt, alpha):
    i = pl.program_id(0)
    n = pl.num_programs(0)

    def fetch(d_ref, slot):
        def body(j, c):
            for kk in range(TOP_K):
                src = _tile_rows(y_ref, d_ref[kk, j])
                pltpu.make_async_copy(src, _tile_rows(buf.at[slot, kk], j), sem.at[slot]).start(priority=kk % 2)
            return c

        lax.fori_loop(0, tt, body, 0)

    def reduce(slot):
        for kk in range(TOP_K):
            pltpu.make_async_copy(y_ref.at[pl.ds(0, tt * SUBLANES), :], buf.at[slot, kk], sem.at[slot]).wait()
        x = _load_rows(x_ref, tt)
        xb = x.astype(BF16)
        hs = _silu(jnp.dot(xb, wsg_ref[...], preferred_element_type=F32)) * jnp.dot(
            xb, wsu_ref[...], preferred_element_type=F32)
        moe = jnp.dot(hs.astype(BF16), wsd_ref[...], preferred_element_type=F32)
        gates = gate_ref[...]
        for kk in range(TOP_K):
            moe = moe + gates[:, kk:kk + 1] * _load_rows(buf.at[slot, kk], tt)
        o_ref[...] = _layer_norm(alpha * x + moe, g_ref[...], b_ref[...])

    @pl.when(i == 0)
    def _():
        fetch(dcur_ref, 0)

    for slot in range(2):
        @pl.when((i % 2 == slot) & (i + 1 < n))
        def _():
            fetch(dnxt_ref, 1 - slot)

    for slot in range(2):
        @pl.when(i % 2 == slot)
        def _():
            reduce(slot)


def _combine(dest, x1t, gates, wsg, wsu, wsd, g, b, y, alpha, tt=128):
    T = x1t.shape[0] // SUBLANES
    D = SUBLANES * LANES
    n = T // tt
    row = lambda i: (i, 0)
    fixed = lambda i: (0, 0)
    return pl.pallas_call(
        functools.partial(_combine_kernel, tt=tt, alpha=alpha),
        grid=(n,),
        in_specs=[pl.BlockSpec((TOP_K, tt), lambda i: (0, i), memory_space=pltpu.SMEM),
                  pl.BlockSpec((TOP_K, tt), lambda i: (0, jnp.minimum(i + 1, n - 1)), memory_space=pltpu.SMEM),
                  pl.BlockSpec((tt * SUBLANES, LANES), row), pl.BlockSpec((tt, TOP_K), row),
                  pl.BlockSpec(wsg.shape, fixed), pl.BlockSpec(wsu.shape, fixed), pl.BlockSpec(wsd.shape, fixed),
                  pl.BlockSpec((1, D), fixed), pl.BlockSpec((1, D), fixed),
                  pl.BlockSpec(memory_space=pl.ANY)],
        out_specs=pl.BlockSpec((tt, D), row),
        out_shape=jax.ShapeDtypeStruct((T, D), F32),
        scratch_shapes=[pltpu.VMEM((2, TOP_K, tt * SUBLANES, LANES), F32), pltpu.SemaphoreType.DMA((2,))],
        compiler_params=_cparams(("arbitrary",)),
        name="moe_combine_ln2",
    )(dest, dest, x1t, gates, wsg, wsu, wsd, g, b, y)


def _layer(x2, mem, B, S, depth, w_in, b_f, w_pool, pool_scale, w_mem_kv, w_out, ln1_g, ln1_b,
           w_router, router_bias, w_gate, w_up, w_down, ws_gate, ws_up, ws_down, ln2_g, ln2_b):
    T, D = x2.shape
    n_win, grp = w_pool.shape[0], w_pool.shape[1]
    pw = n_win * grp
    fw = FOX_HEADS * HEAD_DIM
    mw = w_mem_kv.shape[1] // 2
    E = w_router.shape[1]
    alpha = (2 * depth) ** 0.25

    f_lo = pw + 3 * fw
    w_main = jnp.concatenate([w_in[:, :f_lo], w_in[:, f_lo + FOX_HEADS:]], axis=1).astype(BF16)
    w_f = jnp.pad(w_in[:, f_lo:f_lo + FOX_HEADS], ((0, 0), (0, LANES - FOX_HEADS))).astype(BF16)
    bf_pad = jnp.pad(b_f, (0, LANES - FOX_HEADS)).reshape(1, LANES)
    wbd = jnp.zeros((pw, pw), F32)
    for g in range(n_win):
        wbd = wbd.at[g * grp:(g + 1) * grp, g * grp:(g + 1) * grp].set(w_pool[g])
    wrt = w_router.T
    wrt_hi = wrt.astype(BF16)
    wrt_lo = (wrt - wrt_hi.astype(F32)).astype(BF16)

    u_pool, q, k, v, q_mem, f_logit = _inproj(x2, w_main, w_f, pw, fw, mw)
    y_pool = _pool(u_pool, wbd.astype(BF16), pool_scale.reshape(1, pw), B, S)
    fcum = _fgate(f_logit, bf_pad, B, S)
    y_fox = _fox(q, k, v, fcum, B, S)
    y_mem = _memattn(q_mem, mem, w_mem_kv.astype(BF16), B, S)
    x1 = _outproj(y_pool, y_fox, y_mem, x2, w_out.astype(BF16), ln1_g.reshape(1, D), ln1_b.reshape(1, D), alpha)

    eidx, gates, rank, counts = _router(x1, wrt_hi, wrt_lo, _per_expert_column(router_bias))

    cnt = counts[:, 0].astype(I32)
    padded = (cnt + ROW_BLOCK - 1) // ROW_BLOCK * ROW_BLOCK
    pad_end = jnp.cumsum(padded)
    start_pad = (pad_end - padded).astype(I32)
    n_rows = (T * TOP_K + E * (ROW_BLOCK - 1)) // ROW_BLOCK * ROW_BLOCK
    n_blk = n_rows // ROW_BLOCK
    n_used = (pad_end[-1] // ROW_BLOCK).astype(I32)
    blk_ids = jnp.minimum(jnp.arange(n_blk, dtype=I32), n_used - 1)
    blk_e = jnp.sum((pad_end[None, :] <= (blk_ids * ROW_BLOCK)[:, None]).astype(I32), axis=1)
    blk_e = jnp.minimum(blk_e, E - 1)
    zero_blk = jnp.where(padded > 0, pad_end - ROW_BLOCK, -1).astype(I32)
    dest = _dest(eidx, rank, _per_expert_column(start_pad))

    xs = _dispatch(zero_blk, dest, x1, n_rows)
    y = _experts(blk_e, n_used.reshape(1), xs, w_gate, w_up, w_down)
    return _combine(dest, x1, gates.T, ws_gate.astype(BF16), ws_up.astype(BF16),
                    ws_down.astype(BF16), ln2_g.reshape(1, D), ln2_b.reshape(1, D), y, alpha)


def kernel(x, mem, w_in, b_f, w_pool, pool_scale, w_mem_kv, w_out, ln1_g, ln1_b, w_router, router_bias,
           w_gate, w_up, w_down, ws_gate, ws_up, ws_down, ln2_g, ln2_b):
    B, S, D = x.shape
    depth = w_in.shape[0]
    x2 = x.reshape(B * S, D)
    for l in range(depth):
        x2 = _layer(x2, mem, B, S, depth, w_in[l], b_f[l], w_pool[l], pool_scale[l], w_mem_kv[l], w_out[l],
                    ln1_g[l], ln1_b[l], w_router[l], router_bias[l], w_gate[l], w_up[l], w_down[l],
                    ws_gate[l], ws_up[l], ws_down[l], ln2_g[l], ln2_b[l])
    return x2.reshape(B, S, D)
```

```python
import functools

import jax
import jax.numpy as jnp
from jax import lax
from jax.experimental import pallas as pl
from jax.experimental.pallas import tpu as pltpu

F32 = jnp.float32
BF16 = jnp.bfloat16
I32 = jnp.int32
U32 = jnp.uint32

LANES = 128
SUBLANES = 8
POOL_WINDOWS = (2, 4, 8, 16)
POOL_HALO = 16
HEAD_DIM = 64
FOX_HEADS = 8
BIAS_TERMS = 3
MEM_HEADS = 4
TOP_K = 8
N_GROUPS = 8
TOPK_GROUPS = 4
ROUTED_SCALE = 2.5
LN_EPS = 1e-5
ROW_BLOCK = 256
VMEM_LIMIT = 48 * 1024 * 1024


def _cparams(sem):
    return pltpu.CompilerParams(dimension_semantics=sem, vmem_limit_bytes=VMEM_LIMIT)


def _layer_norm(z, g, b):
    mu = jnp.mean(z, axis=-1, keepdims=True)
    zc = z - mu
    var = jnp.mean(zc * zc, axis=-1, keepdims=True)
    return zc * lax.rsqrt(var + LN_EPS) * g + b


def _silu(x):
    return x * (1.0 / (1.0 + jnp.exp(-x)))


def _load_rows(ref, n):
    return jnp.concatenate([ref[pl.ds(j, n, stride=SUBLANES), :] for j in range(SUBLANES)], axis=1)


def _store_rows(ref, val):
    n = val.shape[0]
    for j in range(SUBLANES):
        ref[pl.ds(j, n, stride=SUBLANES), :] = val[:, j * LANES:(j + 1) * LANES]


def _tile_rows(ref, r):
    return ref.at[pl.ds(pl.multiple_of(r * SUBLANES, SUBLANES), SUBLANES), :]


def _inproj_kernel(x_ref, w_ref, wf_ref, up_ref, q_ref, k_ref, v_ref, qm_ref, f_ref, *, pw, fw, mw, scale):
    xb = x_ref[...].astype(BF16)

    def proj(lo, hi):
        return jnp.dot(xb, w_ref[:, lo:hi], preferred_element_type=F32)

    up_ref[...] = proj(0, pw)
    q_ref[...] = (proj(pw, pw + fw) * scale).astype(BF16)
    k_ref[...] = proj(pw + fw, pw + 2 * fw).astype(BF16)
    v_ref[...] = proj(pw + 2 * fw, pw + 3 * fw).astype(BF16)
    qm_ref[...] = (proj(pw + 3 * fw, pw + 3 * fw + mw) * scale).astype(BF16)
    f_ref[...] = jnp.dot(xb, wf_ref[...], preferred_element_type=F32)


def _inproj(x2, w_main, w_f, pw, fw, mw, tm=512):
    T, D = x2.shape
    kern = functools.partial(_inproj_kernel, pw=pw, fw=fw, mw=mw, scale=HEAD_DIM ** -0.5)
    row = lambda i: (i, 0)
    fixed = lambda i: (0, 0)
    return pl.pallas_call(
        kern,
        grid=(T // tm,),
        in_specs=[pl.BlockSpec((tm, D), row),
                  pl.BlockSpec(w_main.shape, fixed),
                  pl.BlockSpec(w_f.shape, fixed)],
        out_specs=[pl.BlockSpec((tm, pw), row), pl.BlockSpec((tm, fw), row), pl.BlockSpec((tm, fw), row),
                   pl.BlockSpec((tm, fw), row), pl.BlockSpec((tm, mw), row), pl.BlockSpec((tm, LANES), row)],
        out_shape=[jax.ShapeDtypeStruct((T, pw), F32), jax.ShapeDtypeStruct((T, fw), BF16),
                   jax.ShapeDtypeStruct((T, fw), BF16), jax.ShapeDtypeStruct((T, fw), BF16),
                   jax.ShapeDtypeStruct((T, mw), BF16), jax.ShapeDtypeStruct((T, LANES), F32)],
        compiler_params=_cparams(("parallel",)),
        name="inproj",
    )(x2, w_main, w_f)


def _pool_kernel(u_ref, wbd_ref, sc_ref, o_ref, ext_sc, *, chunk, group):
    S, W = u_ref.shape
    ext_sc[0:POOL_HALO, :] = jnp.zeros((POOL_HALO, W), F32)
    ext_sc[POOL_HALO:, :] = u_ref[...]
    rows = chunk + POOL_HALO
    lrow = lax.broadcasted_iota(I32, (rows, W), 0)
    lane = lax.broadcasted_iota(I32, (rows, W), 1)

    def body(c, carry):
        start = pl.multiple_of(c * chunk, chunk)
        e = ext_sc[pl.ds(start, rows), :]
        posf = (lrow + (start - POOL_HALO + 1)).astype(F32)
        acc = e
        d = jnp.zeros_like(e)
        shift = 1
        for g, w in enumerate(POOL_WINDOWS):
            while shift < w:
                acc = acc + pltpu.roll(acc, shift, axis=0)
                shift *= 2
            mean = acc / jnp.minimum(posf, float(w))
            d = jnp.where((lane >= g * group) & (lane < (g + 1) * group), mean, d)
        d = (d - e)[POOL_HALO:, :]
        y = jnp.dot(d.astype(BF16), wbd_ref[...], preferred_element_type=F32) * sc_ref[...]
        o_ref[pl.ds(start, chunk), :] = y.astype(BF16)
        return carry

    lax.fori_loop(0, S // chunk, body, 0)


def _pool(u, wbd, pscale, B, S, chunk=512):
    T, W = u.shape
    kern = functools.partial(_pool_kernel, chunk=chunk, group=W // len(POOL_WINDOWS))
    return pl.pallas_call(
        kern,
        grid=(B,),
        in_specs=[pl.BlockSpec((S, W), lambda b: (b, 0)),
                  pl.BlockSpec((W, W), lambda b: (0, 0)),
                  pl.BlockSpec((1, W), lambda b: (0, 0))],
        out_specs=pl.BlockSpec((S, W), lambda b: (b, 0)),
        out_shape=jax.ShapeDtypeStruct((T, W), BF16),
        scratch_shapes=[pltpu.VMEM((S + POOL_HALO, W), F32)],
        compiler_params=_cparams(("parallel",)),
        name="pool_mixer",
    )(u, wbd, pscale)


def _fgate_kernel(f_ref, bf_ref, o_ref):
    S = f_ref.shape[0]
    z = f_ref[...] + bf_ref[...]
    x = jnp.minimum(z, 0.0) - jnp.log(1.0 + jnp.exp(-jnp.abs(z)))
    row = lax.broadcasted_iota(I32, x.shape, 0)
    lane = lax.broadcasted_iota(I32, x.shape, 1)
    shift = 1
    while shift < S:
        x = x + jnp.where(row >= shift, pltpu.roll(x, shift, axis=0), 0.0)
        shift *= 2
    hi = x.astype(BF16).astype(F32)
    mid = (x - hi).astype(BF16).astype(F32)
    lo = (x - hi) - mid
    for p in range(FOX_HEADS // 2):
        out = jnp.zeros(x.shape, F32)
        for hh in range(2):
            c = 2 * p + hh
            for j, part in enumerate((hi, mid, lo)):
                out = jnp.where(lane == BIAS_TERMS * hh + j, -part[:, c:c + 1], out)
        o_ref[0, p] = out.astype(BF16)


def _fgate(f_logit, bf_pad, B, S):
    return pl.pallas_call(
        _fgate_kernel,
        grid=(B,),
        in_specs=[pl.BlockSpec((S, LANES), lambda b: (b, 0)),
                  pl.BlockSpec((1, LANES), lambda b: (0, 0))],
        out_specs=pl.BlockSpec((1, FOX_HEADS // 2, S, LANES), lambda b: (b, 0, 0, 0)),
        out_shape=jax.ShapeDtypeStruct((B, FOX_HEADS // 2, S, LANES), BF16),
        compiler_params=_cparams(("parallel",)),
        name="forget_cumsum",
    )(f_logit, bf_pad)


def _fox_kernel(q_ref, k_ref, a_ref, v_ref, o_ref, vt_sc, m_sc, l_sc, acc_sc, *, blk):
    qi = pl.program_id(2)
    nk = vt_sc.shape[0]

    @pl.when(qi == 0)
    def _():
        for c in range(nk):
            vt_sc[c] = v_ref[c * blk:(c + 1) * blk, :].astype(F32).T.astype(BF16)

    q = q_ref[...].astype(F32)
    lane = lax.broadcasted_iota(I32, q.shape, 1)
    qa = []
    for h in range(2):
        mine = (lane >= h * HEAD_DIM) & (lane < (h + 1) * HEAD_DIM)
        bias_rows = (lane >= h * BIAS_TERMS) & (lane < (h + 1) * BIAS_TERMS)
        qa.append(jnp.concatenate([jnp.where(mine, q, 0.0), jnp.where(bias_rows, 1.0, 0.0)], axis=1).astype(BF16))
    m_sc[...] = jnp.full(m_sc.shape, -jnp.inf, F32)
    l_sc[...] = jnp.zeros(l_sc.shape, F32)
    acc_sc[...] = jnp.zeros(acc_sc.shape, F32)

    def step(ki, causal):
        ks = pl.multiple_of(ki * blk, blk)
        kb = jnp.concatenate([k_ref[pl.ds(ks, blk), :], a_ref[0, 0, pl.ds(ks, blk), :]], axis=1)
        for h in range(2):
            st = lax.dot_general(kb, qa[h], (((1,), (1,)), ((), ())), preferred_element_type=F32)
            if causal:
                kpos = lax.broadcasted_iota(I32, st.shape, 0)
                qpos = lax.broadcasted_iota(I32, st.shape, 1)
                st = jnp.where(kpos <= qpos, st, -jnp.inf)
            m_prev = m_sc[h]
            m_new = jnp.maximum(m_prev, jnp.max(st, axis=0, keepdims=True))
            alpha = jnp.exp(m_prev - m_new)
            p = jnp.exp(st - m_new)
            l_sc[h] = alpha * l_sc[h] + jnp.sum(p, axis=0, keepdims=True)
            m_sc[h] = m_new
            vt = vt_sc[ki, h * HEAD_DIM:(h + 1) * HEAD_DIM, :]
            pv = jnp.dot(vt, p.astype(BF16), preferred_element_type=F32)
            rows = slice(h * HEAD_DIM, (h + 1) * HEAD_DIM)
            acc_sc[rows, :] = acc_sc[rows, :] * alpha + pv

    def body(ki, carry):
        step(ki, False)
        return carry

    lax.fori_loop(0, qi, body, 0)
    step(qi, True)
    out_t = jnp.concatenate([acc_sc[h * HEAD_DIM:(h + 1) * HEAD_DIM, :] / l_sc[h] for h in range(2)], axis=0)
    o_ref[...] = out_t.T.astype(BF16)


def _fox(q, k, v, fbias, B, S, blk=512):
    T, FW = q.shape
    nq = S // blk
    pairs = FW // LANES
    kern = functools.partial(_fox_kernel, blk=blk)
    return pl.pallas_call(
        kern,
        grid=(B, pairs, nq),
        in_specs=[pl.BlockSpec((blk, LANES), lambda b, p, i: (b * nq + i, p)),
                  pl.BlockSpec((S, LANES), lambda b, p, i: (b, p)),
                  pl.BlockSpec((1, 1, S, LANES), lambda b, p, i: (b, p, 0, 0)),
                  pl.BlockSpec((S, LANES), lambda b, p, i: (b, p))],
        out_specs=pl.BlockSpec((blk, LANES), lambda b, p, i: (b * nq + i, p)),
        out_shape=jax.ShapeDtypeStruct((T, FW), BF16),
        scratch_shapes=[pltpu.VMEM((nq, LANES, blk), BF16), pltpu.VMEM((2, 1, blk), F32),
                        pltpu.VMEM((2, 1, blk), F32), pltpu.VMEM((LANES, blk), F32)],
        compiler_params=_cparams(("parallel", "parallel", "arbitrary")),
        name="fox_attention",
    )(q, k, fbias, v)


def _memattn_kernel(qm_ref, mem_ref, wkv_ref, o_ref, k_sc, v_sc):
    MW = qm_ref.shape[1]

    @pl.when(pl.program_id(1) == 0)
    def _():
        kv = jnp.dot(mem_ref[0].astype(BF16), wkv_ref[...], preferred_element_type=F32)
        k_sc[...] = kv[:, :MW].astype(BF16)
        v_sc[...] = kv[:, MW:].astype(BF16)

    q = qm_ref[...]
    lane = lax.broadcasted_iota(I32, q.shape, 1)
    hd = MW // MEM_HEADS
    out = jnp.zeros(q.shape, F32)
    for h in range(MEM_HEADS):
        mine = (lane >= h * hd) & (lane < (h + 1) * hd)
        qh = jnp.where(mine, q, jnp.zeros_like(q))
        s = lax.dot_general(qh, k_sc[...], (((1,), (1,)), ((), ())), preferred_element_type=F32)
        p = jnp.exp(s - jnp.max(s, axis=1, keepdims=True))
        l = jnp.sum(p, axis=1, keepdims=True)
        o = jnp.dot(p.astype(BF16), v_sc[...], preferred_element_type=F32)
        out = jnp.where(mine, o / l, out)
    o_ref[...] = out.astype(BF16)


def _memattn(qm, mem, wkv, B, S, tq=512):
    T, MW = qm.shape
    M, D = mem.shape[1], mem.shape[2]
    nq = S // tq
    return pl.pallas_call(
        _memattn_kernel,
        grid=(B, nq),
        in_specs=[pl.BlockSpec((tq, MW), lambda b, i: (b * nq + i, 0)),
                  pl.BlockSpec((1, M, D), lambda b, i: (b, 0, 0)),
                  pl.BlockSpec(wkv.shape, lambda b, i: (0, 0))],
        out_specs=pl.BlockSpec((tq, MW), lambda b, i: (b * nq + i, 0)),
        out_shape=jax.ShapeDtypeStruct((T, MW), BF16),
        scratch_shapes=[pltpu.VMEM((M, MW), BF16), pltpu.VMEM((M, MW), BF16)],
        compiler_params=_cparams(("parallel", "arbitrary")),
        name="memory_attention",
    )(qm, mem, wkv)


PACK_SUB = 4


def _pack_rows(ref, val):
    bits = pltpu.bitcast(val, U32)
    r = (bits + jnp.uint32(0x7FFF) + ((bits >> 16) & jnp.uint32(1))) >> 16
    half = val.shape[1] // 2
    words = (r[:, half:] << 16) | r[:, :half]
    for j in range(PACK_SUB):
        ref[:, j, :] = words[:, j * LANES:(j + 1) * LANES]


def _unpack_rows(ref):
    slabs = [ref[:, j, :] for j in range(PACK_SUB)]
    lo = [pltpu.bitcast(s << 16, F32) for s in slabs]
    hi = [pltpu.bitcast(s & jnp.uint32(0xFFFF0000), F32) for s in slabs]
    return jnp.concatenate(lo + hi, axis=1).astype(BF16)


def _outproj_kernel(yp_ref, yf_ref, ym_ref, x_ref, w_ref, g_ref, b_ref, o_ref, pk_ref, *, alpha):
    pw, fw = yp_ref.shape[1], yf_ref.shape[1]
    h = jnp.dot(yp_ref[...], w_ref[0:pw, :], preferred_element_type=F32)
    h = h + jnp.dot(yf_ref[...], w_ref[pw:pw + fw, :], preferred_element_type=F32)
    h = h + jnp.dot(ym_ref[...], w_ref[pw + fw:, :], preferred_element_type=F32)
    y = _layer_norm(alpha * x_ref[...] + h, g_ref[...], b_ref[...])
    _store_rows(o_ref, y)
    _pack_rows(pk_ref, y)


def _outproj(yp, yf, ym, x2, w_out, g, b, alpha, tm=512):
    T, D = x2.shape
    assert D == SUBLANES * LANES == 2 * PACK_SUB * LANES
    row = lambda i: (i, 0)
    fixed = lambda i: (0, 0)
    return pl.pallas_call(
        functools.partial(_outproj_kernel, alpha=alpha),
        grid=(T // tm,),
        in_specs=[pl.BlockSpec((tm, yp.shape[1]), row), pl.BlockSpec((tm, yf.shape[1]), row),
                  pl.BlockSpec((tm, ym.shape[1]), row), pl.BlockSpec((tm, D), row),
                  pl.BlockSpec(w_out.shape, fixed), pl.BlockSpec((1, D), fixed), pl.BlockSpec((1, D), fixed)],
        out_specs=[pl.BlockSpec((tm * SUBLANES, LANES), row),
                   pl.BlockSpec((tm, PACK_SUB, LANES), lambda i: (i, 0, 0))],
        out_shape=[jax.ShapeDtypeStruct((T * SUBLANES, LANES), F32),
                   jax.ShapeDtypeStruct((T, PACK_SUB, LANES), U32)],
        compiler_params=_cparams(("parallel",)),
        name="outproj_ln1",
    )(yp, yf, ym, x2, w_out, g, b)


ROUTE_TILE = 512


def _per_expert_column(v, dtype=F32):
    return jnp.broadcast_to(v.astype(dtype)[:, None], (v.shape[0], ROUTE_TILE))


def _router_kernel(x_ref, wh_ref, wl_ref, bias_ref, eidx_ref, gate_ref, rank_ref, cnt_ref, carry_sc, *, tt):
    E = wh_ref.shape[0]
    gsz = E // N_GROUPS
    ninf = -jnp.inf

    @pl.when(pl.program_id(0) == 0)
    def _():
        carry_sc[...] = jnp.zeros(carry_sc.shape, F32)

    x = _load_rows(x_ref, tt)
    xh = x.astype(BF16)
    xl = (x - xh.astype(F32)).astype(BF16)
    wh = wh_ref[...]
    nt = (((1,), (1,)), ((), ()))
    logits = lax.dot_general(wh, xh, nt, preferred_element_type=F32) + (
        lax.dot_general(wl_ref[...], xh, nt, preferred_element_type=F32)
        + lax.dot_general(wh, xl, nt, preferred_element_type=F32))
    scores = 1.0 / (1.0 + jnp.exp(-logits))
    biased = scores + bias_ref[...]
    eiota = lax.broadcasted_iota(I32, (E, tt), 0).astype(F32)
    giota = lax.broadcasted_iota(I32, (gsz, tt), 0).astype(F32)

    def cmax(a):
        return jnp.max(a, axis=0, keepdims=True)

    def first_at(a, m, iota, n):
        return jnp.min(jnp.where(a == m, iota, float(n)), axis=0, keepdims=True)

    groups = [biased[g * gsz:(g + 1) * gsz, :] for g in range(N_GROUPS)]
    gscore = []
    for blk in groups:
        m1 = cmax(blk)
        m2 = cmax(jnp.where(giota == first_at(blk, m1, giota, gsz), ninf, blk))
        gscore.append(m1 + m2)

    kept = []
    for g in range(N_GROUPS):
        ahead = jnp.zeros((1, tt), F32)
        for o in range(N_GROUPS):
            if o == g:
                continue
            beats = gscore[o] > gscore[g]
            if o < g:
                beats = beats | (gscore[o] == gscore[g])
            ahead = ahead + jnp.where(beats, 1.0, 0.0)
        kept.append(groups[g] + jnp.where(ahead < float(TOPK_GROUPS), 0.0, ninf))
    masked = jnp.concatenate(kept, axis=0)

    picks, sels = [], []
    chosen = jnp.zeros((E, tt), F32)
    for _ in range(TOP_K):
        ik = first_at(masked, cmax(masked), eiota, E)
        oh = eiota == ik
        picks.append(ik)
        sels.append(jnp.sum(jnp.where(oh, scores, 0.0), axis=0, keepdims=True))
        chosen = jnp.where(oh, 1.0, chosen)
        masked = jnp.where(oh, ninf, masked)
    denom = sels[0]
    for sk in sels[1:]:
        denom = denom + sk

    r = lax.broadcasted_iota(I32, (tt, tt), 0)
    c = lax.broadcasted_iota(I32, (tt, tt), 1)
    earlier = jnp.where(r < c, 1.0, 0.0).astype(BF16)
    chosen_b = chosen.astype(BF16)
    pos = carry_sc[...] + jnp.dot(chosen_b, earlier, preferred_element_type=F32)
    carry_sc[...] = carry_sc[...] + jnp.dot(chosen_b, jnp.ones((tt, tt), BF16), preferred_element_type=F32)
    cnt_ref[...] = carry_sc[...]

    ranks = [jnp.sum(jnp.where(eiota == ik, pos, 0.0), axis=0, keepdims=True) for ik in picks]
    eidx_ref[...] = jnp.concatenate(picks, axis=0).astype(I32)
    gate_ref[...] = jnp.concatenate([sk / denom * ROUTED_SCALE for sk in sels], axis=0)
    rank_ref[...] = jnp.concatenate(ranks, axis=0).astype(I32)


def _router(x1t, wrt_hi, wrt_lo, rbias):
    T = x1t.shape[0] // SUBLANES
    E, D = wrt_hi.shape
    tt = ROUTE_TILE
    tok = lambda i: (0, i)
    fixed = lambda i: (0, 0)
    return pl.pallas_call(
        functools.partial(_router_kernel, tt=tt),
        grid=(T // tt,),
        in_specs=[pl.BlockSpec((tt * SUBLANES, LANES), lambda i: (i, 0)), pl.BlockSpec((E, D), fixed),
                  pl.BlockSpec((E, D), fixed), pl.BlockSpec((E, tt), fixed)],
        out_specs=[pl.BlockSpec((TOP_K, tt), tok), pl.BlockSpec((TOP_K, tt), tok), pl.BlockSpec((TOP_K, tt), tok),
                   pl.BlockSpec((E, tt), fixed)],
        out_shape=[jax.ShapeDtypeStruct((TOP_K, T), I32), jax.ShapeDtypeStruct((TOP_K, T), F32),
                   jax.ShapeDtypeStruct((TOP_K, T), I32), jax.ShapeDtypeStruct((E, tt), F32)],
        scratch_shapes=[pltpu.VMEM((E, tt), F32)],
        compiler_params=_cparams(("arbitrary",)),
        name="router_topk",
    )(x1t, wrt_hi, wrt_lo, rbias)


def _dest_kernel(e_ref, r_ref, sp_ref, d_ref):
    E = sp_ref.shape[0]
    tt = e_ref.shape[1]
    eiota = lax.broadcasted_iota(I32, (E, tt), 0)
    sp = sp_ref[...]
    rows = []
    for kk in range(TOP_K):
        start = jnp.sum(jnp.where(eiota == e_ref[kk:kk + 1, :], sp, 0.0), axis=0, keepdims=True)
        rows.append(start.astype(I32) + r_ref[kk:kk + 1, :])
    d_ref[...] = jnp.concatenate(rows, axis=0)


def _dest(eidx, rank, start_pad_b):
    K, T = eidx.shape
    tt = ROUTE_TILE
    tok = lambda i: (0, i)
    return pl.pallas_call(
        _dest_kernel,
        grid=(T // tt,),
        in_specs=[pl.BlockSpec((K, tt), tok), pl.BlockSpec((K, tt), tok),
                  pl.BlockSpec(start_pad_b.shape, lambda i: (0, 0))],
        out_specs=pl.BlockSpec((K, tt), tok),
        out_shape=jax.ShapeDtypeStruct((K, T), I32),
        compiler_params=_cparams(("parallel",)),
        name="moe_dest",
    )(eidx, rank, start_pad_b)


def _dispatch_kernel(zb_ref, d_ref, x_ref, xs_ref, zero_sc, sem, *, tt, n_exp):
    @pl.when(pl.program_id(0) == 0)
    def _():
        zero_sc[...] = jnp.zeros(zero_sc.shape, U32)

        def zstart(e, c):
            @pl.when(zb_ref[e] >= 0)
            def _():
                dst = xs_ref.at[pl.ds(pl.multiple_of(zb_ref[e], ROW_BLOCK), ROW_BLOCK)]
                pltpu.make_async_copy(zero_sc, dst, sem).start()
            return c

        def zwait(e, c):
            @pl.when(zb_ref[e] >= 0)
            def _():
                pltpu.make_async_copy(zero_sc, xs_ref.at[pl.ds(0, ROW_BLOCK)], sem).wait()
            return c

        lax.fori_loop(0, n_exp, zstart, 0)
        lax.fori_loop(0, n_exp, zwait, 0)

    def start(j, c):
        src = x_ref.at[j]
        for kk in range(TOP_K):
            pltpu.make_async_copy(src, xs_ref.at[d_ref[kk, j]], sem).start(priority=kk % 2)
        return c

    lax.fori_loop(0, tt, start, 0)
    for kk in range(TOP_K):
        pltpu.make_async_copy(x_ref, xs_ref.at[pl.ds(0, tt)], sem).wait()


def _dispatch(zero_blk, dest, xpk, n_rows, tt=256):
    T = xpk.shape[0]
    n_exp = zero_blk.shape[0]
    grid_spec = pltpu.PrefetchScalarGridSpec(
        num_scalar_prefetch=1,
        grid=(T // tt,),
        in_specs=[pl.BlockSpec((TOP_K, tt), lambda i, zb: (0, i), memory_space=pltpu.SMEM),
                  pl.BlockSpec((tt, PACK_SUB, LANES), lambda i, zb: (i, 0, 0))],
        out_specs=pl.BlockSpec(memory_space=pl.ANY),
        scratch_shapes=[pltpu.VMEM((ROW_BLOCK, PACK_SUB, LANES), U32), pltpu.SemaphoreType.DMA],
    )
    return pl.pallas_call(
        functools.partial(_dispatch_kernel, tt=tt, n_exp=n_exp),
        grid_spec=grid_spec,
        out_shape=jax.ShapeDtypeStruct((n_rows, PACK_SUB, LANES), U32),
        compiler_params=_cparams(("arbitrary",)),
        name="moe_dispatch",
    )(zero_blk, dest, xpk)


def _expert_kernel(sb_ref, nb_ref, nu_ref, wg_ref, wu_ref, wd_ref, xs_ref, y_ref,
                   xbuf, ybuf, wg_sc, wu_sc, wd_sc, isem, osem):
    e = pl.program_id(0)
    nb = nb_ref[e]
    base = sb_ref[e]
    out_rows = ROW_BLOCK * SUBLANES

    def fetch(blk, slot):
        src = xs_ref.at[pl.ds(pl.multiple_of(blk * ROW_BLOCK, ROW_BLOCK), ROW_BLOCK)]
        pltpu.make_async_copy(src, xbuf.at[slot], isem.at[slot]).start()

    def wait_in(slot):
        pltpu.make_async_copy(xs_ref.at[pl.ds(0, ROW_BLOCK)], xbuf.at[slot], isem.at[slot]).wait()

    def store(blk, slot):
        dst = y_ref.at[pl.ds(pl.multiple_of(blk * out_rows, out_rows), out_rows), :]
        pltpu.make_async_copy(ybuf.at[slot], dst, osem.at[slot]).start()

    def wait_out(slot):
        pltpu.make_async_copy(ybuf.at[slot], y_ref.at[pl.ds(0, out_rows), :], osem.at[slot]).wait()

    @pl.when(nb > 0)
    def _():
        @pl.when(base == 0)
        def _():
            fetch(0, 0)

        wg_sc[...] = wg_ref[0].astype(BF16)
        wu_sc[...] = wu_ref[0].astype(BF16)
        wd_sc[...] = wd_ref[0].astype(BF16)

        def step(b, slot):
            wait_in(slot)

            @pl.when(b + 1 < nb)
            def _():
                fetch(base + b + 1, 1 - slot)

            @pl.when(b >= 2)
            def _():
                wait_out(slot)

            xb = _unpack_rows(xbuf.at[slot])
            g = jnp.dot(xb, wg_sc[...], preferred_element_type=F32)
            u = jnp.dot(xb, wu_sc[...], preferred_element_type=F32)
            h = (_silu(g) * u).astype(BF16)
            _store_rows(ybuf.at[slot], jnp.dot(h, wd_sc[...], preferred_element_type=F32))
            store(base + b, slot)

        def pair(p, c):
            for slot in range(2):
                @pl.when(2 * p + slot < nb)
                def _():
                    step(2 * p + slot, slot)
            return c

        lax.fori_loop(0, (nb + 1) // 2, pair, 0)
        wait_out(0)

        @pl.when(nb >= 2)
        def _():
            wait_out(1)

        @pl.when(base + nb < nu_ref[0])
        def _():
            fetch(base + nb, 0)


def _experts(start_blk, n_blk_e, n_used, xs, w_gate, w_up, w_down):
    n_rows = xs.shape[0]
    E, D, H = w_gate.shape
    wsel = lambda e, sb, nb, nu: (e, 0, 0)
    grid_spec = pltpu.PrefetchScalarGridSpec(
        num_scalar_prefetch=3,
        grid=(E,),
        in_specs=[pl.BlockSpec((1, D, H), wsel), pl.BlockSpec((1, D, H), wsel), pl.BlockSpec((1, H, D), wsel),
                  pl.BlockSpec(memory_space=pl.ANY)],
        out_specs=pl.BlockSpec(memory_space=pl.ANY),
        scratch_shapes=[pltpu.VMEM((2, ROW_BLOCK, PACK_SUB, LANES), U32),
                        pltpu.VMEM((2, ROW_BLOCK * SUBLANES, LANES), F32),
                        pltpu.VMEM((D, H), BF16), pltpu.VMEM((D, H), BF16), pltpu.VMEM((H, D), BF16),
                        pltpu.SemaphoreType.DMA((2,)), pltpu.SemaphoreType.DMA((2,))],
    )
    return pl.pallas_call(
        _expert_kernel,
        grid_spec=grid_spec,
        out_shape=jax.ShapeDtypeStruct((n_rows * SUBLANES, LANES), F32),
        compiler_params=_cparams(("arbitrary",)),
        name="moe_experts",
    )(start_blk, n_blk_e, n_used, w_gate, w_up, w_down, xs)


def _combine_kernel(dcur_ref, dnxt_ref, x_ref, gate_ref, wsg_ref, wsu_ref, wsd_ref, g_ref, b_ref, y_ref,
                    o_ref, buf, sem, *, tt, alpha):
    i = pl.program_id(0)
    n = pl.num_programs(0)

    def fetch(d_ref, slot):
        def body(j, c):
            for kk in range(TOP_K):
                src = _tile_rows(y_ref, d_ref[kk, j])
                pltpu.make_async_copy(src, _tile_rows(buf.at[slot, kk], j), sem.at[slot]).start(priority=kk % 2)
            return c

        lax.fori_loop(0, tt, body, 0)

    def reduce(slot):
        for kk in range(TOP_K):
            pltpu.make_async_copy(y_ref.at[pl.ds(0, tt * SUBLANES), :], buf.at[slot, kk], sem.at[slot]).wait()
        x = _load_rows(x_ref, tt)
        xb = x.astype(BF16)
        hs = _silu(jnp.dot(xb, wsg_ref[...], preferred_element_type=F32)) * jnp.dot(
            xb, wsu_ref[...], preferred_element_type=F32)
        moe = jnp.dot(hs.astype(BF16), wsd_ref[...], preferred_element_type=F32)
        gates = gate_ref[...]
        for kk in range(TOP_K):
            moe = moe + gates[:, kk:kk + 1] * _load_rows(buf.at[slot, kk], tt)
        o_ref[...] = _layer_norm(alpha * x + moe, g_ref[...], b_ref[...])

    @pl.when(i == 0)
    def _():
        fetch(dcur_ref, 0)

    for slot in range(2):
        @pl.when((i % 2 == slot) & (i + 1 < n))
        def _():
            fetch(dnxt_ref, 1 - slot)

    for slot in range(2):
        @pl.when(i % 2 == slot)
        def _():
            reduce(slot)


def _combine(dest, x1t, gates, wsg, wsu, wsd, g, b, y, alpha, tt=128):
    T = x1t.shape[0] // SUBLANES
    D = SUBLANES * LANES
    n = T // tt
    row = lambda i: (i, 0)
    fixed = lambda i: (0, 0)
    return pl.pallas_call(
        functools.partial(_combine_kernel, tt=tt, alpha=alpha),
        grid=(n,),
        in_specs=[pl.BlockSpec((TOP_K, tt), lambda i: (0, i), memory_space=pltpu.SMEM),
                  pl.BlockSpec((TOP_K, tt), lambda i: (0, jnp.minimum(i + 1, n - 1)), memory_space=pltpu.SMEM),
                  pl.BlockSpec((tt * SUBLANES, LANES), row), pl.BlockSpec((tt, TOP_K), row),
                  pl.BlockSpec(wsg.shape, fixed), pl.BlockSpec(wsu.shape, fixed), pl.BlockSpec(wsd.shape, fixed),
                  pl.BlockSpec((1, D), fixed), pl.BlockSpec((1, D), fixed),
                  pl.BlockSpec(memory_space=pl.ANY)],
        out_specs=pl.BlockSpec((tt, D), row),
        out_shape=jax.ShapeDtypeStruct((T, D), F32),
        scratch_shapes=[pltpu.VMEM((2, TOP_K, tt * SUBLANES, LANES), F32), pltpu.SemaphoreType.DMA((2,))],
        compiler_params=_cparams(("arbitrary",)),
        name="moe_combine_ln2",
    )(dest, dest, x1t, gates, wsg, wsu, wsd, g, b, y)


def _layer(x2, mem, B, S, depth, w_in, b_f, w_pool, pool_scale, w_mem_kv, w_out, ln1_g, ln1_b,
           w_router, router_bias, w_gate, w_up, w_down, ws_gate, ws_up, ws_down, ln2_g, ln2_b):
    T, D = x2.shape
    n_win, grp = w_pool.shape[0], w_pool.shape[1]
    pw = n_win * grp
    fw = FOX_HEADS * HEAD_DIM
    mw = w_mem_kv.shape[1] // 2
    E = w_router.shape[1]
    alpha = (2 * depth) ** 0.25

    f_lo = pw + 3 * fw
    w_main = jnp.concatenate([w_in[:, :f_lo], w_in[:, f_lo + FOX_HEADS:]], axis=1).astype(BF16)
    w_f = jnp.pad(w_in[:, f_lo:f_lo + FOX_HEADS], ((0, 0), (0, LANES - FOX_HEADS))).astype(BF16)
    bf_pad = jnp.pad(b_f, (0, LANES - FOX_HEADS)).reshape(1, LANES)
    wbd = jnp.zeros((pw, pw), F32)
    for g in range(n_win):
        wbd = wbd.at[g * grp:(g + 1) * grp, g * grp:(g + 1) * grp].set(w_pool[g])
    wrt = w_router.T
    wrt_hi = wrt.astype(BF16)
    wrt_lo = (wrt - wrt_hi.astype(F32)).astype(BF16)

    u_pool, q, k, v, q_mem, f_logit = _inproj(x2, w_main, w_f, pw, fw, mw)
    y_pool = _pool(u_pool, wbd.astype(BF16), pool_scale.reshape(1, pw), B, S)
    fcum = _fgate(f_logit, bf_pad, B, S)
    y_fox = _fox(q, k, v, fcum, B, S)
    y_mem = _memattn(q_mem, mem, w_mem_kv.astype(BF16), B, S)
    x1, x1_packed = _outproj(y_pool, y_fox, y_mem, x2, w_out.astype(BF16), ln1_g.reshape(1, D),
                             ln1_b.reshape(1, D), alpha)

    eidx, gates, rank, counts = _router(x1, wrt_hi, wrt_lo, _per_expert_column(router_bias))

    cnt = counts[:, 0].astype(I32)
    padded = (cnt + ROW_BLOCK - 1) // ROW_BLOCK * ROW_BLOCK
    pad_end = jnp.cumsum(padded)
    start_pad = (pad_end - padded).astype(I32)
    n_rows = (T * TOP_K + E * (ROW_BLOCK - 1)) // ROW_BLOCK * ROW_BLOCK
    n_used = (pad_end[-1] // ROW_BLOCK).astype(I32)
    zero_blk = jnp.where(padded > 0, pad_end - ROW_BLOCK, -1).astype(I32)
    dest = _dest(eidx, rank, _per_expert_column(start_pad))

    xs = _dispatch(zero_blk, dest, x1_packed, n_rows)
    y = _experts(start_pad // ROW_BLOCK, (padded // ROW_BLOCK).astype(I32), n_used.reshape(1), xs,
                 w_gate, w_up, w_down)
    return _combine(dest, x1, gates.T, ws_gate.astype(BF16), ws_up.astype(BF16),
                    ws_down.astype(BF16), ln2_g.reshape(1, D), ln2_b.reshape(1, D), y, alpha)


def kernel(x, mem, w_in, b_f, w_pool, pool_scale, w_mem_kv, w_out, ln1_g, ln1_b, w_router, router_bias,
           w_gate, w_up, w_down, ws_gate, ws_up, ws_down, ln2_g, ln2_b):
    B, S, D = x.shape
    depth = w_in.shape[0]
    x2 = x.reshape(B * S, D)
    for l in range(depth):
        x2 = _layer(x2, mem, B, S, depth, w_in[l], b_f[l], w_pool[l], pool_scale[l], w_mem_kv[l], w_out[l],
                    ln1_g[l], ln1_b[l], w_router[l], router_bias[l], w_gate[l], w_up[l], w_down[l],
                    ws_gate[l], ws_up[l], ws_down[l], ln2_g[l], ln2_b[l])
    return x2.reshape(B, S, D)
```

```python
import functools

import jax
import jax.numpy as jnp
from jax import lax
from jax.experimental import pallas as pl
from jax.experimental.pallas import tpu as pltpu

F32 = jnp.float32
BF16 = jnp.bfloat16
I32 = jnp.int32
U32 = jnp.uint32

LANES = 128
SUBLANES = 8
POOL_WINDOWS = (2, 4, 8, 16)
POOL_HALO = 16
HEAD_DIM = 64
FOX_HEADS = 8
BIAS_TERMS = 3
MEM_HEADS = 4
TOP_K = 8
N_GROUPS = 8
TOPK_GROUPS = 4
ROUTED_SCALE = 2.5
LN_EPS = 1e-5
ROW_BLOCK = 256
VMEM_LIMIT = 48 * 1024 * 1024


def _cparams(sem):
    return pltpu.CompilerParams(dimension_semantics=sem, vmem_limit_bytes=VMEM_LIMIT)


def _layer_norm(z, g, b):
    mu = jnp.mean(z, axis=-1, keepdims=True)
    zc = z - mu
    var = jnp.mean(zc * zc, axis=-1, keepdims=True)
    return zc * lax.rsqrt(var + LN_EPS) * g + b


def _silu(x):
    return x * (1.0 / (1.0 + jnp.exp(-x)))


def _load_rows(ref, n):
    return jnp.concatenate([ref[pl.ds(j, n, stride=SUBLANES), :] for j in range(SUBLANES)], axis=1)


def _store_rows(ref, val):
    n = val.shape[0]
    for j in range(SUBLANES):
        ref[pl.ds(j, n, stride=SUBLANES), :] = val[:, j * LANES:(j + 1) * LANES]


def _tile_rows(ref, r):
    return ref.at[pl.ds(pl.multiple_of(r * SUBLANES, SUBLANES), SUBLANES), :]


def _inproj_kernel(x_ref, w_ref, wf_ref, up_ref, q_ref, k_ref, v_ref, qm_ref, f_ref, *, pw, fw, mw, scale):
    xb = x_ref[...].astype(BF16)

    def proj(lo, hi):
        return jnp.dot(xb, w_ref[:, lo:hi], preferred_element_type=F32)

    up_ref[...] = proj(0, pw)
    q_ref[...] = (proj(pw, pw + fw) * scale).astype(BF16)
    k_ref[...] = proj(pw + fw, pw + 2 * fw).astype(BF16)
    v_ref[...] = proj(pw + 2 * fw, pw + 3 * fw).astype(BF16)
    qm_ref[...] = (proj(pw + 3 * fw, pw + 3 * fw + mw) * scale).astype(BF16)
    f_ref[...] = jnp.dot(xb, wf_ref[...], preferred_element_type=F32)


def _inproj(x2, w_main, w_f, pw, fw, mw, tm=512):
    T, D = x2.shape
    kern = functools.partial(_inproj_kernel, pw=pw, fw=fw, mw=mw, scale=HEAD_DIM ** -0.5)
    row = lambda i: (i, 0)
    fixed = lambda i: (0, 0)
    return pl.pallas_call(
        kern,
        grid=(T // tm,),
        in_specs=[pl.BlockSpec((tm, D), row),
                  pl.BlockSpec(w_main.shape, fixed),
                  pl.BlockSpec(w_f.shape, fixed)],
        out_specs=[pl.BlockSpec((tm, pw), row), pl.BlockSpec((tm, fw), row), pl.BlockSpec((tm, fw), row),
                   pl.BlockSpec((tm, fw), row), pl.BlockSpec((tm, mw), row), pl.BlockSpec((tm, LANES), row)],
        out_shape=[jax.ShapeDtypeStruct((T, pw), F32), jax.ShapeDtypeStruct((T, fw), BF16),
                   jax.ShapeDtypeStruct((T, fw), BF16), jax.ShapeDtypeStruct((T, fw), BF16),
                   jax.ShapeDtypeStruct((T, mw), BF16), jax.ShapeDtypeStruct((T, LANES), F32)],
        compiler_params=_cparams(("parallel",)),
        name="inproj",
    )(x2, w_main, w_f)


def _pool_kernel(u_ref, wbd_ref, sc_ref, o_ref, ext_sc, *, chunk, group):
    S, W = u_ref.shape
    ext_sc[0:POOL_HALO, :] = jnp.zeros((POOL_HALO, W), F32)
    ext_sc[POOL_HALO:, :] = u_ref[...]
    rows = chunk + POOL_HALO
    lrow = lax.broadcasted_iota(I32, (rows, W), 0)
    lane = lax.broadcasted_iota(I32, (rows, W), 1)

    def body(c, carry):
        start = pl.multiple_of(c * chunk, chunk)
        e = ext_sc[pl.ds(start, rows), :]
        posf = (lrow + (start - POOL_HALO + 1)).astype(F32)
        acc = e
        d = jnp.zeros_like(e)
        shift = 1
        for g, w in enumerate(POOL_WINDOWS):
            while shift < w:
                acc = acc + pltpu.roll(acc, shift, axis=0)
                shift *= 2
            mean = acc / jnp.minimum(posf, float(w))
            d = jnp.where((lane >= g * group) & (lane < (g + 1) * group), mean, d)
        d = (d - e)[POOL_HALO:, :]
        y = jnp.dot(d.astype(BF16), wbd_ref[...], preferred_element_type=F32) * sc_ref[...]
        o_ref[pl.ds(start, chunk), :] = y.astype(BF16)
        return carry

    lax.fori_loop(0, S // chunk, body, 0)


def _pool(u, wbd, pscale, B, S, chunk=512):
    T, W = u.shape
    kern = functools.partial(_pool_kernel, chunk=chunk, group=W // len(POOL_WINDOWS))
    return pl.pallas_call(
        kern,
        grid=(B,),
        in_specs=[pl.BlockSpec((S, W), lambda b: (b, 0)),
                  pl.BlockSpec((W, W), lambda b: (0, 0)),
                  pl.BlockSpec((1, W), lambda b: (0, 0))],
        out_specs=pl.BlockSpec((S, W), lambda b: (b, 0)),
        out_shape=jax.ShapeDtypeStruct((T, W), BF16),
        scratch_shapes=[pltpu.VMEM((S + POOL_HALO, W), F32)],
        compiler_params=_cparams(("parallel",)),
        name="pool_mixer",
    )(u, wbd, pscale)


def _fgate_kernel(f_ref, bf_ref, o_ref):
    S = f_ref.shape[0]
    z = f_ref[...] + bf_ref[...]
    x = jnp.minimum(z, 0.0) - jnp.log(1.0 + jnp.exp(-jnp.abs(z)))
    row = lax.broadcasted_iota(I32, x.shape, 0)
    lane = lax.broadcasted_iota(I32, x.shape, 1)
    shift = 1
    while shift < S:
        x = x + jnp.where(row >= shift, pltpu.roll(x, shift, axis=0), 0.0)
        shift *= 2
    hi = x.astype(BF16).astype(F32)
    mid = (x - hi).astype(BF16).astype(F32)
    lo = (x - hi) - mid
    for p in range(FOX_HEADS // 2):
        out = jnp.zeros(x.shape, F32)
        for hh in range(2):
            c = 2 * p + hh
            for j, part in enumerate((hi, mid, lo)):
                out = jnp.where(lane == BIAS_TERMS * hh + j, -part[:, c:c + 1], out)
        o_ref[0, p] = out.astype(BF16)


def _fgate(f_logit, bf_pad, B, S):
    return pl.pallas_call(
        _fgate_kernel,
        grid=(B,),
        in_specs=[pl.BlockSpec((S, LANES), lambda b: (b, 0)),
                  pl.BlockSpec((1, LANES), lambda b: (0, 0))],
        out_specs=pl.BlockSpec((1, FOX_HEADS // 2, S, LANES), lambda b: (b, 0, 0, 0)),
        out_shape=jax.ShapeDtypeStruct((B, FOX_HEADS // 2, S, LANES), BF16),
        compiler_params=_cparams(("parallel",)),
        name="forget_cumsum",
    )(f_logit, bf_pad)


def _fox_kernel(q_ref, k_ref, a_ref, v_ref, o_ref, vt_sc, m_sc, l_sc, acc_sc, *, blk):
    qi = pl.program_id(2)
    nk = vt_sc.shape[0]

    @pl.when(qi == 0)
    def _():
        for c in range(nk):
            vt_sc[c] = v_ref[c * blk:(c + 1) * blk, :].astype(F32).T.astype(BF16)

    q = q_ref[...].astype(F32)
    lane = lax.broadcasted_iota(I32, q.shape, 1)
    qa = []
    for h in range(2):
        mine = (lane >= h * HEAD_DIM) & (lane < (h + 1) * HEAD_DIM)
        bias_rows = (lane >= h * BIAS_TERMS) & (lane < (h + 1) * BIAS_TERMS)
        qa.append(jnp.concatenate([jnp.where(mine, q, 0.0), jnp.where(bias_rows, 1.0, 0.0)], axis=1).astype(BF16))
    m_sc[...] = jnp.full(m_sc.shape, -jnp.inf, F32)
    l_sc[...] = jnp.zeros(l_sc.shape, F32)
    acc_sc[...] = jnp.zeros(acc_sc.shape, F32)

    def step(ki, causal):
        ks = pl.multiple_of(ki * blk, blk)
        kb = jnp.concatenate([k_ref[pl.ds(ks, blk), :], a_ref[0, 0, pl.ds(ks, blk), :]], axis=1)
        for h in range(2):
            st = lax.dot_general(kb, qa[h], (((1,), (1,)), ((), ())), preferred_element_type=F32)
            if causal:
                kpos = lax.broadcasted_iota(I32, st.shape, 0)
                qpos = lax.broadcasted_iota(I32, st.shape, 1)
                st = jnp.where(kpos <= qpos, st, -jnp.inf)
            m_prev = m_sc[h]
            m_new = jnp.maximum(m_prev, jnp.max(st, axis=0, keepdims=True))
            alpha = jnp.exp(m_prev - m_new)
            p = jnp.exp(st - m_new)
            l_sc[h] = alpha * l_sc[h] + jnp.sum(p, axis=0, keepdims=True)
            m_sc[h] = m_new
            vt = vt_sc[ki, h * HEAD_DIM:(h + 1) * HEAD_DIM, :]
            pv = jnp.dot(vt, p.astype(BF16), preferred_element_type=F32)
            rows = slice(h * HEAD_DIM, (h + 1) * HEAD_DIM)
            acc_sc[rows, :] = acc_sc[rows, :] * alpha + pv

    def body(ki, carry):
        step(ki, False)
        return carry

    lax.fori_loop(0, qi, body, 0)
    step(qi, True)
    out_t = jnp.concatenate([acc_sc[h * HEAD_DIM:(h + 1) * HEAD_DIM, :] / l_sc[h] for h in range(2)], axis=0)
    o_ref[...] = out_t.T.astype(BF16)


def _fox(q, k, v, fbias, B, S, blk=512):
    T, FW = q.shape
    nq = S // blk
    pairs = FW // LANES
    kern = functools.partial(_fox_kernel, blk=blk)
    return pl.pallas_call(
        kern,
        grid=(B, pairs, nq),
        in_specs=[pl.BlockSpec((blk, LANES), lambda b, p, i: (b * nq + i, p)),
                  pl.BlockSpec((S, LANES), lambda b, p, i: (b, p)),
                  pl.BlockSpec((1, 1, S, LANES), lambda b, p, i: (b, p, 0, 0)),
                  pl.BlockSpec((S, LANES), lambda b, p, i: (b, p))],
        out_specs=pl.BlockSpec((blk, LANES), lambda b, p, i: (b * nq + i, p)),
        out_shape=jax.ShapeDtypeStruct((T, FW), BF16),
        scratch_shapes=[pltpu.VMEM((nq, LANES, blk), BF16), pltpu.VMEM((2, 1, blk), F32),
                        pltpu.VMEM((2, 1, blk), F32), pltpu.VMEM((LANES, blk), F32)],
        compiler_params=_cparams(("parallel", "parallel", "arbitrary")),
        name="fox_attention",
    )(q, k, fbias, v)


def _memattn_kernel(qm_ref, mem_ref, wkv_ref, o_ref, k_sc, v_sc):
    MW = qm_ref.shape[1]

    @pl.when(pl.program_id(1) == 0)
    def _():
        kv = jnp.dot(mem_ref[0].astype(BF16), wkv_ref[...], preferred_element_type=F32)
        k_sc[...] = kv[:, :MW].astype(BF16)
        v_sc[...] = kv[:, MW:].astype(BF16)

    q = qm_ref[...]
    lane = lax.broadcasted_iota(I32, q.shape, 1)
    hd = MW // MEM_HEADS
    out = jnp.zeros(q.shape, F32)
    for h in range(MEM_HEADS):
        mine = (lane >= h * hd) & (lane < (h + 1) * hd)
        qh = jnp.where(mine, q, jnp.zeros_like(q))
        s = lax.dot_general(qh, k_sc[...], (((1,), (1,)), ((), ())), preferred_element_type=F32)
        p = jnp.exp(s - jnp.max(s, axis=1, keepdims=True))
        l = jnp.sum(p, axis=1, keepdims=True)
        o = jnp.dot(p.astype(BF16), v_sc[...], preferred_element_type=F32)
        out = jnp.where(mine, o / l, out)
    o_ref[...] = out.astype(BF16)


def _memattn(qm, mem, wkv, B, S, tq=512):
    T, MW = qm.shape
    M, D = mem.shape[1], mem.shape[2]
    nq = S // tq
    return pl.pallas_call(
        _memattn_kernel,
        grid=(B, nq),
        in_specs=[pl.BlockSpec((tq, MW), lambda b, i: (b * nq + i, 0)),
                  pl.BlockSpec((1, M, D), lambda b, i: (b, 0, 0)),
                  pl.BlockSpec(wkv.shape, lambda b, i: (0, 0))],
        out_specs=pl.BlockSpec((tq, MW), lambda b, i: (b * nq + i, 0)),
        out_shape=jax.ShapeDtypeStruct((T, MW), BF16),
        scratch_shapes=[pltpu.VMEM((M, MW), BF16), pltpu.VMEM((M, MW), BF16)],
        compiler_params=_cparams(("parallel", "arbitrary")),
        name="memory_attention",
    )(qm, mem, wkv)


PACK_SUB = 4


def _pack_rows(ref, val):
    n = val.shape[0]
    bits = pltpu.bitcast(val, U32)
    r = (bits + jnp.uint32(0x7FFF) + ((bits >> 16) & jnp.uint32(1))) >> 16
    half = val.shape[1] // 2
    words = (r[:, half:] << 16) | r[:, :half]
    for j in range(PACK_SUB):
        ref[pl.ds(j, n, stride=PACK_SUB), :] = words[:, j * LANES:(j + 1) * LANES]


def _unpack_rows(ref, n):
    slabs = [ref[pl.ds(j, n, stride=PACK_SUB), :] for j in range(PACK_SUB)]
    lo = [pltpu.bitcast(s << 16, F32) for s in slabs]
    hi = [pltpu.bitcast(s & jnp.uint32(0xFFFF0000), F32) for s in slabs]
    return jnp.concatenate(lo + hi, axis=1).astype(BF16)


def _packed_row(ref, r):
    return ref.at[pl.ds(pl.multiple_of(r * PACK_SUB, PACK_SUB), PACK_SUB), :]


def _outproj_kernel(yp_ref, yf_ref, ym_ref, x_ref, w_ref, g_ref, b_ref, o_ref, pk_ref, *, alpha):
    pw, fw = yp_ref.shape[1], yf_ref.shape[1]
    h = jnp.dot(yp_ref[...], w_ref[0:pw, :], preferred_element_type=F32)
    h = h + jnp.dot(yf_ref[...], w_ref[pw:pw + fw, :], preferred_element_type=F32)
    h = h + jnp.dot(ym_ref[...], w_ref[pw + fw:, :], preferred_element_type=F32)
    y = _layer_norm(alpha * x_ref[...] + h, g_ref[...], b_ref[...])
    _store_rows(o_ref, y)
    _pack_rows(pk_ref, y)


def _outproj(yp, yf, ym, x2, w_out, g, b, alpha, tm=512):
    T, D = x2.shape
    assert D == SUBLANES * LANES == 2 * PACK_SUB * LANES
    row = lambda i: (i, 0)
    fixed = lambda i: (0, 0)
    return pl.pallas_call(
        functools.partial(_outproj_kernel, alpha=alpha),
        grid=(T // tm,),
        in_specs=[pl.BlockSpec((tm, yp.shape[1]), row), pl.BlockSpec((tm, yf.shape[1]), row),
                  pl.BlockSpec((tm, ym.shape[1]), row), pl.BlockSpec((tm, D), row),
                  pl.BlockSpec(w_out.shape, fixed), pl.BlockSpec((1, D), fixed), pl.BlockSpec((1, D), fixed)],
        out_specs=[pl.BlockSpec((tm * SUBLANES, LANES), row),
                   pl.BlockSpec((tm * PACK_SUB, LANES), row)],
        out_shape=[jax.ShapeDtypeStruct((T * SUBLANES, LANES), F32),
                   jax.ShapeDtypeStruct((T * PACK_SUB, LANES), U32)],
        compiler_params=_cparams(("parallel",)),
        name="outproj_ln1",
    )(yp, yf, ym, x2, w_out, g, b)


ROUTE_TILE = 512


def _per_expert_column(v, dtype=F32):
    return jnp.broadcast_to(v.astype(dtype)[:, None], (v.shape[0], ROUTE_TILE))


def _router_kernel(x_ref, wh_ref, wl_ref, bias_ref, eidx_ref, gate_ref, rank_ref, cnt_ref, carry_sc, *, tt):
    E = wh_ref.shape[0]
    gsz = E // N_GROUPS
    ninf = -jnp.inf

    @pl.when(pl.program_id(0) == 0)
    def _():
        carry_sc[...] = jnp.zeros(carry_sc.shape, F32)

    x = _load_rows(x_ref, tt)
    xh = x.astype(BF16)
    xl = (x - xh.astype(F32)).astype(BF16)
    wh = wh_ref[...]
    nt = (((1,), (1,)), ((), ()))
    logits = lax.dot_general(wh, xh, nt, preferred_element_type=F32) + (
        lax.dot_general(wl_ref[...], xh, nt, preferred_element_type=F32)
        + lax.dot_general(wh, xl, nt, preferred_element_type=F32))
    scores = 1.0 / (1.0 + jnp.exp(-logits))
    biased = scores + bias_ref[...]
    eiota = lax.broadcasted_iota(I32, (E, tt), 0).astype(F32)
    giota = lax.broadcasted_iota(I32, (gsz, tt), 0).astype(F32)

    def cmax(a):
        return jnp.max(a, axis=0, keepdims=True)

    def first_at(a, m, iota, n):
        return jnp.min(jnp.where(a == m, iota, float(n)), axis=0, keepdims=True)

    groups = [biased[g * gsz:(g + 1) * gsz, :] for g in range(N_GROUPS)]
    gscore = []
    for blk in groups:
        m1 = cmax(blk)
        m2 = cmax(jnp.where(giota == first_at(blk, m1, giota, gsz), ninf, blk))
        gscore.append(m1 + m2)

    kept = []
    for g in range(N_GROUPS):
        ahead = jnp.zeros((1, tt), F32)
        for o in range(N_GROUPS):
            if o == g:
                continue
            beats = gscore[o] > gscore[g]
            if o < g:
                beats = beats | (gscore[o] == gscore[g])
            ahead = ahead + jnp.where(beats, 1.0, 0.0)
        kept.append(groups[g] + jnp.where(ahead < float(TOPK_GROUPS), 0.0, ninf))
    masked = jnp.concatenate(kept, axis=0)

    picks, sels = [], []
    chosen = jnp.zeros((E, tt), F32)
    for _ in range(TOP_K):
        ik = first_at(masked, cmax(masked), eiota, E)
        oh = eiota == ik
        picks.append(ik)
        sels.append(jnp.sum(jnp.where(oh, scores, 0.0), axis=0, keepdims=True))
        chosen = jnp.where(oh, 1.0, chosen)
        masked = jnp.where(oh, ninf, masked)
    denom = sels[0]
    for sk in sels[1:]:
        denom = denom + sk

    r = lax.broadcasted_iota(I32, (tt, tt), 0)
    c = lax.broadcasted_iota(I32, (tt, tt), 1)
    earlier = jnp.where(r < c, 1.0, 0.0).astype(BF16)
    chosen_b = chosen.astype(BF16)
    pos = carry_sc[...] + jnp.dot(chosen_b, earlier, preferred_element_type=F32)
    carry_sc[...] = carry_sc[...] + jnp.dot(chosen_b, jnp.ones((tt, tt), BF16), preferred_element_type=F32)
    cnt_ref[...] = carry_sc[...]

    ranks = [jnp.sum(jnp.where(eiota == ik, pos, 0.0), axis=0, keepdims=True) for ik in picks]
    eidx_ref[...] = jnp.concatenate(picks, axis=0).astype(I32)
    gate_ref[...] = jnp.concatenate([sk / denom * ROUTED_SCALE for sk in sels], axis=0)
    rank_ref[...] = jnp.concatenate(ranks, axis=0).astype(I32)


def _router(x1t, wrt_hi, wrt_lo, rbias):
    T = x1t.shape[0] // SUBLANES
    E, D = wrt_hi.shape
    tt = ROUTE_TILE
    tok = lambda i: (0, i)
    fixed = lambda i: (0, 0)
    return pl.pallas_call(
        functools.partial(_router_kernel, tt=tt),
        grid=(T // tt,),
        in_specs=[pl.BlockSpec((tt * SUBLANES, LANES), lambda i: (i, 0)), pl.BlockSpec((E, D), fixed),
                  pl.BlockSpec((E, D), fixed), pl.BlockSpec((E, tt), fixed)],
        out_specs=[pl.BlockSpec((TOP_K, tt), tok), pl.BlockSpec((TOP_K, tt), tok), pl.BlockSpec((TOP_K, tt), tok),
                   pl.BlockSpec((E, tt), fixed)],
        out_shape=[jax.ShapeDtypeStruct((TOP_K, T), I32), jax.ShapeDtypeStruct((TOP_K, T), F32),
                   jax.ShapeDtypeStruct((TOP_K, T), I32), jax.ShapeDtypeStruct((E, tt), F32)],
        scratch_shapes=[pltpu.VMEM((E, tt), F32)],
        compiler_params=_cparams(("arbitrary",)),
        name="router_topk",
    )(x1t, wrt_hi, wrt_lo, rbias)


def _dest_kernel(e_ref, r_ref, sp_ref, d_ref):
    E = sp_ref.shape[0]
    tt = e_ref.shape[1]
    eiota = lax.broadcasted_iota(I32, (E, tt), 0)
    sp = sp_ref[...]
    rows = []
    for kk in range(TOP_K):
        start = jnp.sum(jnp.where(eiota == e_ref[kk:kk + 1, :], sp, 0.0), axis=0, keepdims=True)
        rows.append(start.astype(I32) + r_ref[kk:kk + 1, :])
    d_ref[...] = jnp.concatenate(rows, axis=0)


def _dest(eidx, rank, start_pad_b):
    K, T = eidx.shape
    tt = ROUTE_TILE
    tok = lambda i: (0, i)
    return pl.pallas_call(
        _dest_kernel,
        grid=(T // tt,),
        in_specs=[pl.BlockSpec((K, tt), tok), pl.BlockSpec((K, tt), tok),
                  pl.BlockSpec(start_pad_b.shape, lambda i: (0, 0))],
        out_specs=pl.BlockSpec((K, tt), tok),
        out_shape=jax.ShapeDtypeStruct((K, T), I32),
        compiler_params=_cparams(("parallel",)),
        name="moe_dest",
    )(eidx, rank, start_pad_b)


def _dispatch_kernel(zb_ref, d_ref, x_ref, xs_ref, zero_sc, sem, *, tt, n_exp):
    blk_rows = ROW_BLOCK * PACK_SUB

    @pl.when(pl.program_id(0) == 0)
    def _():
        zero_sc[...] = jnp.zeros(zero_sc.shape, U32)

        def zstart(e, c):
            @pl.when(zb_ref[e] >= 0)
            def _():
                dst = xs_ref.at[pl.ds(pl.multiple_of(zb_ref[e] * PACK_SUB, blk_rows), blk_rows), :]
                pltpu.make_async_copy(zero_sc, dst, sem).start()
            return c

        def zwait(e, c):
            @pl.when(zb_ref[e] >= 0)
            def _():
                pltpu.make_async_copy(zero_sc, xs_ref.at[pl.ds(0, blk_rows), :], sem).wait()
            return c

        lax.fori_loop(0, n_exp, zstart, 0)
        lax.fori_loop(0, n_exp, zwait, 0)

    def start(j, c):
        src = _packed_row(x_ref, j)
        for kk in range(TOP_K):
            pltpu.make_async_copy(src, _packed_row(xs_ref, d_ref[kk, j]), sem).start(priority=kk % 2)
        return c

    lax.fori_loop(0, tt, start, 0)
    for kk in range(TOP_K):
        pltpu.make_async_copy(x_ref, xs_ref.at[pl.ds(0, tt * PACK_SUB), :], sem).wait()


def _dispatch(zero_blk, dest, xpk, n_rows, tt=256):
    T = xpk.shape[0] // PACK_SUB
    n_exp = zero_blk.shape[0]
    grid_spec = pltpu.PrefetchScalarGridSpec(
        num_scalar_prefetch=1,
        grid=(T // tt,),
        in_specs=[pl.BlockSpec((TOP_K, tt), lambda i, zb: (0, i), memory_space=pltpu.SMEM),
                  pl.BlockSpec((tt * PACK_SUB, LANES), lambda i, zb: (i, 0))],
        out_specs=pl.BlockSpec(memory_space=pl.ANY),
        scratch_shapes=[pltpu.VMEM((ROW_BLOCK * PACK_SUB, LANES), U32), pltpu.SemaphoreType.DMA],
    )
    return pl.pallas_call(
        functools.partial(_dispatch_kernel, tt=tt, n_exp=n_exp),
        grid_spec=grid_spec,
        out_shape=jax.ShapeDtypeStruct((n_rows * PACK_SUB, LANES), U32),
        compiler_params=_cparams(("arbitrary",)),
        name="moe_dispatch",
    )(zero_blk, dest, xpk)


def _expert_kernel(sb_ref, nb_ref, nu_ref, wg_ref, wu_ref, wd_ref, xs_ref, y_ref,
                   xbuf, ybuf, wg_sc, wu_sc, wd_sc, isem, osem):
    e = pl.program_id(0)
    nb = nb_ref[e]
    base = sb_ref[e]
    out_rows = ROW_BLOCK * SUBLANES
    in_rows = ROW_BLOCK * PACK_SUB

    def fetch(blk, slot):
        src = xs_ref.at[pl.ds(pl.multiple_of(blk * in_rows, in_rows), in_rows), :]
        pltpu.make_async_copy(src, xbuf.at[slot], isem.at[slot]).start()

    def wait_in(slot):
        pltpu.make_async_copy(xs_ref.at[pl.ds(0, in_rows), :], xbuf.at[slot], isem.at[slot]).wait()

    def store(blk, slot):
        dst = y_ref.at[pl.ds(pl.multiple_of(blk * out_rows, out_rows), out_rows), :]
        pltpu.make_async_copy(ybuf.at[slot], dst, osem.at[slot]).start()

    def wait_out(slot):
        pltpu.make_async_copy(ybuf.at[slot], y_ref.at[pl.ds(0, out_rows), :], osem.at[slot]).wait()

    @pl.when(nb > 0)
    def _():
        @pl.when(base == 0)
        def _():
            fetch(0, 0)

        wg_sc[...] = wg_ref[0].astype(BF16)
        wu_sc[...] = wu_ref[0].astype(BF16)
        wd_sc[...] = wd_ref[0].astype(BF16)

        def step(b, slot):
            wait_in(slot)

            @pl.when(b + 1 < nb)
            def _():
                fetch(base + b + 1, 1 - slot)

            @pl.when(b >= 2)
            def _():
                wait_out(slot)

            xb = _unpack_rows(xbuf.at[slot], ROW_BLOCK)
            g = jnp.dot(xb, wg_sc[...], preferred_element_type=F32)
            u = jnp.dot(xb, wu_sc[...], preferred_element_type=F32)
            h = (_silu(g) * u).astype(BF16)
            _store_rows(ybuf.at[slot], jnp.dot(h, wd_sc[...], preferred_element_type=F32))
            store(base + b, slot)

        def pair(p, c):
            for slot in range(2):
                @pl.when(2 * p + slot < nb)
                def _():
                    step(2 * p + slot, slot)
            return c

        lax.fori_loop(0, (nb + 1) // 2, pair, 0)
        wait_out(0)

        @pl.when(nb >= 2)
        def _():
            wait_out(1)

        @pl.when(base + nb < nu_ref[0])
        def _():
            fetch(base + nb, 0)


def _experts(start_blk, n_blk_e, n_used, xs, w_gate, w_up, w_down):
    n_rows = xs.shape[0] // PACK_SUB
    E, D, H = w_gate.shape
    wsel = lambda e, sb, nb, nu: (e, 0, 0)
    grid_spec = pltpu.PrefetchScalarGridSpec(
        num_scalar_prefetch=3,
        grid=(E,),
        in_specs=[pl.BlockSpec((1, D, H), wsel), pl.BlockSpec((1, D, H), wsel), pl.BlockSpec((1, H, D), wsel),
                  pl.BlockSpec(memory_space=pl.ANY)],
        out_specs=pl.BlockSpec(memory_space=pl.ANY),
        scratch_shapes=[pltpu.VMEM((2, ROW_BLOCK * PACK_SUB, LANES), U32),
                        pltpu.VMEM((2, ROW_BLOCK * SUBLANES, LANES), F32),
                        pltpu.VMEM((D, H), BF16), pltpu.VMEM((D, H), BF16), pltpu.VMEM((H, D), BF16),
                        pltpu.SemaphoreType.DMA((2,)), pltpu.SemaphoreType.DMA((2,))],
    )
    return pl.pallas_call(
        _expert_kernel,
        grid_spec=grid_spec,
        out_shape=jax.ShapeDtypeStruct((n_rows * SUBLANES, LANES), F32),
        compiler_params=_cparams(("arbitrary",)),
        name="moe_experts",
    )(start_blk, n_blk_e, n_used, w_gate, w_up, w_down, xs)


def _combine_kernel(dcur_ref, dnxt_ref, x_ref, gate_ref, wsg_ref, wsu_ref, wsd_ref, g_ref, b_ref, y_ref,
                    o_ref, buf, sem, *, tt, alpha):
    i = pl.program_id(0)
    n = pl.num_programs(0)

    def fetch(d_ref, slot):
        def body(j, c):
            for kk in range(TOP_K):
                src = _tile_rows(y_ref, d_ref[kk, j])
                pltpu.make_async_copy(src, _tile_rows(buf.at[slot, kk], j), sem.at[slot]).start(priority=kk % 2)
            return c

        lax.fori_loop(0, tt, body, 0)

    def reduce(slot):
        for kk in range(TOP_K):
            pltpu.make_async_copy(y_ref.at[pl.ds(0, tt * SUBLANES), :], buf.at[slot, kk], sem.at[slot]).wait()
        x = _load_rows(x_ref, tt)
        xb = x.astype(BF16)
        hs = _silu(jnp.dot(xb, wsg_ref[...], preferred_element_type=F32)) * jnp.dot(
            xb, wsu_ref[...], preferred_element_type=F32)
        moe = jnp.dot(hs.astype(BF16), wsd_ref[...], preferred_element_type=F32)
        gates = gate_ref[...]
        for kk in range(TOP_K):
            moe = moe + gates[:, kk:kk + 1] * _load_rows(buf.at[slot, kk], tt)
        o_ref[...] = _layer_norm(alpha * x + moe, g_ref[...], b_ref[...])

    @pl.when(i == 0)
    def _():
        fetch(dcur_ref, 0)

    for slot in range(2):
        @pl.when((i % 2 == slot) & (i + 1 < n))
        def _():
            fetch(dnxt_ref, 1 - slot)

    for slot in range(2):
        @pl.when(i % 2 == slot)
        def _():
            reduce(slot)


def _combine(dest, x1t, gates, wsg, wsu, wsd, g, b, y, alpha, tt=128):
    T = x1t.shape[0] // SUBLANES
    D = SUBLANES * LANES
    n = T // tt
    row = lambda i: (i, 0)
    fixed = lambda i: (0, 0)
    return pl.pallas_call(
        functools.partial(_combine_kernel, tt=tt, alpha=alpha),
        grid=(n,),
        in_specs=[pl.BlockSpec((TOP_K, tt), lambda i: (0, i), memory_space=pltpu.SMEM),
                  pl.BlockSpec((TOP_K, tt), lambda i: (0, jnp.minimum(i + 1, n - 1)), memory_space=pltpu.SMEM),
                  pl.BlockSpec((tt * SUBLANES, LANES), row), pl.BlockSpec((tt, TOP_K), row),
                  pl.BlockSpec(wsg.shape, fixed), pl.BlockSpec(wsu.shape, fixed), pl.BlockSpec(wsd.shape, fixed),
                  pl.BlockSpec((1, D), fixed), pl.BlockSpec((1, D), fixed),
                  pl.BlockSpec(memory_space=pl.ANY)],
        out_specs=pl.BlockSpec((tt, D), row),
        out_shape=jax.ShapeDtypeStruct((T, D), F32),
        scratch_shapes=[pltpu.VMEM((2, TOP_K, tt * SUBLANES, LANES), F32), pltpu.SemaphoreType.DMA((2,))],
        compiler_params=_cparams(("arbitrary",)),
        name="moe_combine_ln2",
    )(dest, dest, x1t, gates, wsg, wsu, wsd, g, b, y)


def _layer(x2, mem, B, S, depth, w_in, b_f, w_pool, pool_scale, w_mem_kv, w_out, ln1_g, ln1_b,
           w_router, router_bias, w_gate, w_up, w_down, ws_gate, ws_up, ws_down, ln2_g, ln2_b):
    T, D = x2.shape
    n_win, grp = w_pool.shape[0], w_pool.shape[1]
    pw = n_win * grp
    fw = FOX_HEADS * HEAD_DIM
    mw = w_mem_kv.shape[1] // 2
    E = w_router.shape[1]
    alpha = (2 * depth) ** 0.25

    f_lo = pw + 3 * fw
    w_main = jnp.concatenate([w_in[:, :f_lo], w_in[:, f_lo + FOX_HEADS:]], axis=1).astype(BF16)
    w_f = jnp.pad(w_in[:, f_lo:f_lo + FOX_HEADS], ((0, 0), (0, LANES - FOX_HEADS))).astype(BF16)
    bf_pad = jnp.pad(b_f, (0, LANES - FOX_HEADS)).reshape(1, LANES)
    wbd = jnp.zeros((pw, pw), F32)
    for g in range(n_win):
        wbd = wbd.at[g * grp:(g + 1) * grp, g * grp:(g + 1) * grp].set(w_pool[g])
    wrt = w_router.T
    wrt_hi = wrt.astype(BF16)
    wrt_lo = (wrt - wrt_hi.astype(F32)).astype(BF16)

    u_pool, q, k, v, q_mem, f_logit = _inproj(x2, w_main, w_f, pw, fw, mw)
    y_pool = _pool(u_pool, wbd.astype(BF16), pool_scale.reshape(1, pw), B, S)
    fcum = _fgate(f_logit, bf_pad, B, S)
    y_fox = _fox(q, k, v, fcum, B, S)
    y_mem = _memattn(q_mem, mem, w_mem_kv.astype(BF16), B, S)
    x1, x1_packed = _outproj(y_pool, y_fox, y_mem, x2, w_out.astype(BF16), ln1_g.reshape(1, D),
                             ln1_b.reshape(1, D), alpha)

    eidx, gates, rank, counts = _router(x1, wrt_hi, wrt_lo, _per_expert_column(router_bias))

    cnt = counts[:, 0].astype(I32)
    padded = (cnt + ROW_BLOCK - 1) // ROW_BLOCK * ROW_BLOCK
    pad_end = jnp.cumsum(padded)
    start_pad = (pad_end - padded).astype(I32)
    n_rows = (T * TOP_K + E * (ROW_BLOCK - 1)) // ROW_BLOCK * ROW_BLOCK
    n_used = (pad_end[-1] // ROW_BLOCK).astype(I32)
    zero_blk = jnp.where(padded > 0, pad_end - ROW_BLOCK, -1).astype(I32)
    dest = _dest(eidx, rank, _per_expert_column(start_pad))

    xs = _dispatch(zero_blk, dest, x1_packed, n_rows)
    y = _experts(start_pad // ROW_BLOCK, (padded // ROW_BLOCK).astype(I32), n_used.reshape(1), xs,
                 w_gate, w_up, w_down)
    return _combine(dest, x1, gates.T, ws_gate.astype(BF16), ws_up.astype(BF16),
                    ws_down.astype(BF16), ln2_g.reshape(1, D), ln2_b.reshape(1, D), y, alpha)


def kernel(x, mem, w_in, b_f, w_pool, pool_scale, w_mem_kv, w_out, ln1_g, ln1_b, w_router, router_bias,
           w_gate, w_up, w_down, ws_gate, ws_up, ws_down, ln2_g, ln2_b):
    B, S, D = x.shape
    depth = w_in.shape[0]
    x2 = x.reshape(B * S, D)
    for l in range(depth):
        x2 = _layer(x2, mem, B, S, depth, w_in[l], b_f[l], w_pool[l], pool_scale[l], w_mem_kv[l], w_out[l],
                    ln1_g[l], ln1_b[l], w_router[l], router_bias[l], w_gate[l], w_up[l], w_down[l],
                    ws_gate[l], ws_up[l], ws_down[l], ln2_g[l], ln2_b[l])
    return x2.reshape(B, S, D)
```

```python
import functools

import jax
import jax.numpy as jnp
from jax import lax
from jax.experimental import pallas as pl
from jax.experimental.pallas import tpu as pltpu

F32 = jnp.float32
BF16 = jnp.bfloat16
I32 = jnp.int32
U32 = jnp.uint32

LANES = 128
SUBLANES = 8
POOL_WINDOWS = (2, 4, 8, 16)
POOL_HALO = 16
HEAD_DIM = 64
FOX_HEADS = 8
BIAS_TERMS = 3
MEM_HEADS = 4
TOP_K = 8
N_GROUPS = 8
TOPK_GROUPS = 4
ROUTED_SCALE = 2.5
LN_EPS = 1e-5
ROW_BLOCK = 256
EXPERT_BUFS = 4
EXPERT_AHEAD = EXPERT_BUFS - 1
VMEM_LIMIT = 48 * 1024 * 1024


def _cparams(sem):
    return pltpu.CompilerParams(dimension_semantics=sem, vmem_limit_bytes=VMEM_LIMIT)


def _layer_norm(z, g, b):
    mu = jnp.mean(z, axis=-1, keepdims=True)
    zc = z - mu
    var = jnp.mean(zc * zc, axis=-1, keepdims=True)
    return zc * lax.rsqrt(var + LN_EPS) * g + b


def _silu(x):
    return x * (1.0 / (1.0 + jnp.exp(-x)))


def _load_rows(ref, n):
    return jnp.concatenate([ref[pl.ds(j, n, stride=SUBLANES), :] for j in range(SUBLANES)], axis=1)


def _store_rows(ref, val):
    n = val.shape[0]
    for j in range(SUBLANES):
        ref[pl.ds(j, n, stride=SUBLANES), :] = val[:, j * LANES:(j + 1) * LANES]


def _tile_rows(ref, r):
    return ref.at[pl.ds(pl.multiple_of(r * SUBLANES, SUBLANES), SUBLANES), :]


def _inproj_kernel(x_ref, w_ref, wf_ref, up_ref, q_ref, k_ref, v_ref, qm_ref, f_ref, *, pw, fw, mw, scale):
    xb = x_ref[...].astype(BF16)

    def proj(lo, hi):
        return jnp.dot(xb, w_ref[:, lo:hi], preferred_element_type=F32)

    up_ref[...] = proj(0, pw)
    q_ref[...] = (proj(pw, pw + fw) * scale).astype(BF16)
    k_ref[...] = proj(pw + fw, pw + 2 * fw).astype(BF16)
    v_ref[...] = proj(pw + 2 * fw, pw + 3 * fw).astype(BF16)
    qm_ref[...] = (proj(pw + 3 * fw, pw + 3 * fw + mw) * scale).astype(BF16)
    f_ref[...] = jnp.dot(xb, wf_ref[...], preferred_element_type=F32)


def _inproj(x2, w_main, w_f, pw, fw, mw, tm=512):
    T, D = x2.shape
    kern = functools.partial(_inproj_kernel, pw=pw, fw=fw, mw=mw, scale=HEAD_DIM ** -0.5)
    row = lambda i: (i, 0)
    fixed = lambda i: (0, 0)
    return pl.pallas_call(
        kern,
        grid=(T // tm,),
        in_specs=[pl.BlockSpec((tm, D), row),
                  pl.BlockSpec(w_main.shape, fixed),
                  pl.BlockSpec(w_f.shape, fixed)],
        out_specs=[pl.BlockSpec((tm, pw), row), pl.BlockSpec((tm, fw), row), pl.BlockSpec((tm, fw), row),
                   pl.BlockSpec((tm, fw), row), pl.BlockSpec((tm, mw), row), pl.BlockSpec((tm, LANES), row)],
        out_shape=[jax.ShapeDtypeStruct((T, pw), F32), jax.ShapeDtypeStruct((T, fw), BF16),
                   jax.ShapeDtypeStruct((T, fw), BF16), jax.ShapeDtypeStruct((T, fw), BF16),
                   jax.ShapeDtypeStruct((T, mw), BF16), jax.ShapeDtypeStruct((T, LANES), F32)],
        compiler_params=_cparams(("parallel",)),
        name="inproj",
    )(x2, w_main, w_f)


def _pool_kernel(u_ref, wbd_ref, sc_ref, o_ref, ext_sc, *, chunk, group):
    S, W = u_ref.shape
    ext_sc[0:POOL_HALO, :] = jnp.zeros((POOL_HALO, W), F32)
    ext_sc[POOL_HALO:, :] = u_ref[...]
    rows = chunk + POOL_HALO
    lrow = lax.broadcasted_iota(I32, (rows, W), 0)
    lane = lax.broadcasted_iota(I32, (rows, W), 1)

    def body(c, carry):
        start = pl.multiple_of(c * chunk, chunk)
        e = ext_sc[pl.ds(start, rows), :]
        posf = (lrow + (start - POOL_HALO + 1)).astype(F32)
        acc = e
        d = jnp.zeros_like(e)
        shift = 1
        for g, w in enumerate(POOL_WINDOWS):
            while shift < w:
                acc = acc + pltpu.roll(acc, shift, axis=0)
                shift *= 2
            mean = acc / jnp.minimum(posf, float(w))
            d = jnp.where((lane >= g * group) & (lane < (g + 1) * group), mean, d)
        d = (d - e)[POOL_HALO:, :]
        y = jnp.dot(d.astype(BF16), wbd_ref[...], preferred_element_type=F32) * sc_ref[...]
        o_ref[pl.ds(start, chunk), :] = y.astype(BF16)
        return carry

    lax.fori_loop(0, S // chunk, body, 0)


def _pool(u, wbd, pscale, B, S, chunk=512):
    T, W = u.shape
    kern = functools.partial(_pool_kernel, chunk=chunk, group=W // len(POOL_WINDOWS))
    return pl.pallas_call(
        kern,
        grid=(B,),
        in_specs=[pl.BlockSpec((S, W), lambda b: (b, 0)),
                  pl.BlockSpec((W, W), lambda b: (0, 0)),
                  pl.BlockSpec((1, W), lambda b: (0, 0))],
        out_specs=pl.BlockSpec((S, W), lambda b: (b, 0)),
        out_shape=jax.ShapeDtypeStruct((T, W), BF16),
        scratch_shapes=[pltpu.VMEM((S + POOL_HALO, W), F32)],
        compiler_params=_cparams(("parallel",)),
        name="pool_mixer",
    )(u, wbd, pscale)


def _fgate_kernel(f_ref, bf_ref, o_ref):
    S = f_ref.shape[0]
    z = f_ref[...] + bf_ref[...]
    x = jnp.minimum(z, 0.0) - jnp.log(1.0 + jnp.exp(-jnp.abs(z)))
    row = lax.broadcasted_iota(I32, x.shape, 0)
    lane = lax.broadcasted_iota(I32, x.shape, 1)
    shift = 1
    while shift < S:
        x = x + jnp.where(row >= shift, pltpu.roll(x, shift, axis=0), 0.0)
        shift *= 2
    hi = x.astype(BF16).astype(F32)
    mid = (x - hi).astype(BF16).astype(F32)
    lo = (x - hi) - mid
    for p in range(FOX_HEADS // 2):
        out = jnp.zeros(x.shape, F32)
        for hh in range(2):
            c = 2 * p + hh
            for j, part in enumerate((hi, mid, lo)):
                out = jnp.where(lane == BIAS_TERMS * hh + j, -part[:, c:c + 1], out)
        o_ref[0, p] = out.astype(BF16)


def _fgate(f_logit, bf_pad, B, S):
    return pl.pallas_call(
        _fgate_kernel,
        grid=(B,),
        in_specs=[pl.BlockSpec((S, LANES), lambda b: (b, 0)),
                  pl.BlockSpec((1, LANES), lambda b: (0, 0))],
        out_specs=pl.BlockSpec((1, FOX_HEADS // 2, S, LANES), lambda b: (b, 0, 0, 0)),
        out_shape=jax.ShapeDtypeStruct((B, FOX_HEADS // 2, S, LANES), BF16),
        compiler_params=_cparams(("parallel",)),
        name="forget_cumsum",
    )(f_logit, bf_pad)


def _fox_kernel(q_ref, k_ref, a_ref, v_ref, o_ref, vt_sc, m_sc, l_sc, acc_sc, *, blk):
    qi = pl.program_id(2)
    nk = vt_sc.shape[0]

    @pl.when(qi == 0)
    def _():
        for c in range(nk):
            vt_sc[c] = v_ref[c * blk:(c + 1) * blk, :].astype(F32).T.astype(BF16)

    q = q_ref[...].astype(F32)
    lane = lax.broadcasted_iota(I32, q.shape, 1)
    qa = []
    for h in range(2):
        mine = (lane >= h * HEAD_DIM) & (lane < (h + 1) * HEAD_DIM)
        bias_rows = (lane >= h * BIAS_TERMS) & (lane < (h + 1) * BIAS_TERMS)
        qa.append(jnp.concatenate([jnp.where(mine, q, 0.0), jnp.where(bias_rows, 1.0, 0.0)], axis=1).astype(BF16))
    m_sc[...] = jnp.full(m_sc.shape, -jnp.inf, F32)
    l_sc[...] = jnp.zeros(l_sc.shape, F32)
    acc_sc[...] = jnp.zeros(acc_sc.shape, F32)

    def step(ki, causal):
        ks = pl.multiple_of(ki * blk, blk)
        kb = jnp.concatenate([k_ref[pl.ds(ks, blk), :], a_ref[0, 0, pl.ds(ks, blk), :]], axis=1)
        for h in range(2):
            st = lax.dot_general(kb, qa[h], (((1,), (1,)), ((), ())), preferred_element_type=F32)
            if causal:
                kpos = lax.broadcasted_iota(I32, st.shape, 0)
                qpos = lax.broadcasted_iota(I32, st.shape, 1)
                st = jnp.where(kpos <= qpos, st, -jnp.inf)
            m_prev = m_sc[h]
            m_new = jnp.maximum(m_prev, jnp.max(st, axis=0, keepdims=True))
            alpha = jnp.exp(m_prev - m_new)
            p = jnp.exp(st - m_new)
            l_sc[h] = alpha * l_sc[h] + jnp.sum(p, axis=0, keepdims=True)
            m_sc[h] = m_new
            vt = vt_sc[ki, h * HEAD_DIM:(h + 1) * HEAD_DIM, :]
            pv = jnp.dot(vt, p.astype(BF16), preferred_element_type=F32)
            rows = slice(h * HEAD_DIM, (h + 1) * HEAD_DIM)
            acc_sc[rows, :] = acc_sc[rows, :] * alpha + pv

    def body(ki, carry):
        step(ki, False)
        return carry

    lax.fori_loop(0, qi, body, 0)
    step(qi, True)
    out_t = jnp.concatenate([acc_sc[h * HEAD_DIM:(h + 1) * HEAD_DIM, :] / l_sc[h] for h in range(2)], axis=0)
    o_ref[...] = out_t.T.astype(BF16)


def _fox(q, k, v, fbias, B, S, blk=512):
    T, FW = q.shape
    nq = S // blk
    pairs = FW // LANES
    kern = functools.partial(_fox_kernel, blk=blk)
    return pl.pallas_call(
        kern,
        grid=(B, pairs, nq),
        in_specs=[pl.BlockSpec((blk, LANES), lambda b, p, i: (b * nq + i, p)),
                  pl.BlockSpec((S, LANES), lambda b, p, i: (b, p)),
                  pl.BlockSpec((1, 1, S, LANES), lambda b, p, i: (b, p, 0, 0)),
                  pl.BlockSpec((S, LANES), lambda b, p, i: (b, p))],
        out_specs=pl.BlockSpec((blk, LANES), lambda b, p, i: (b * nq + i, p)),
        out_shape=jax.ShapeDtypeStruct((T, FW), BF16),
        scratch_shapes=[pltpu.VMEM((nq, LANES, blk), BF16), pltpu.VMEM((2, 1, blk), F32),
                        pltpu.VMEM((2, 1, blk), F32), pltpu.VMEM((LANES, blk), F32)],
        compiler_params=_cparams(("parallel", "parallel", "arbitrary")),
        name="fox_attention",
    )(q, k, fbias, v)


def _memattn_kernel(qm_ref, mem_ref, wkv_ref, o_ref, k_sc, v_sc):
    MW = qm_ref.shape[1]

    @pl.when(pl.program_id(1) == 0)
    def _():
        kv = jnp.dot(mem_ref[0].astype(BF16), wkv_ref[...], preferred_element_type=F32)
        k_sc[...] = kv[:, :MW].astype(BF16)
        v_sc[...] = kv[:, MW:].astype(BF16)

    q = qm_ref[...]
    lane = lax.broadcasted_iota(I32, q.shape, 1)
    hd = MW // MEM_HEADS
    out = jnp.zeros(q.shape, F32)
    for h in range(MEM_HEADS):
        mine = (lane >= h * hd) & (lane < (h + 1) * hd)
        qh = jnp.where(mine, q, jnp.zeros_like(q))
        s = lax.dot_general(qh, k_sc[...], (((1,), (1,)), ((), ())), preferred_element_type=F32)
        p = jnp.exp(s - jnp.max(s, axis=1, keepdims=True))
        l = jnp.sum(p, axis=1, keepdims=True)
        o = jnp.dot(p.astype(BF16), v_sc[...], preferred_element_type=F32)
        out = jnp.where(mine, o / l, out)
    o_ref[...] = out.astype(BF16)


def _memattn(qm, mem, wkv, B, S, tq=512):
    T, MW = qm.shape
    M, D = mem.shape[1], mem.shape[2]
    nq = S // tq
    return pl.pallas_call(
        _memattn_kernel,
        grid=(B, nq),
        in_specs=[pl.BlockSpec((tq, MW), lambda b, i: (b * nq + i, 0)),
                  pl.BlockSpec((1, M, D), lambda b, i: (b, 0, 0)),
                  pl.BlockSpec(wkv.shape, lambda b, i: (0, 0))],
        out_specs=pl.BlockSpec((tq, MW), lambda b, i: (b * nq + i, 0)),
        out_shape=jax.ShapeDtypeStruct((T, MW), BF16),
        scratch_shapes=[pltpu.VMEM((M, MW), BF16), pltpu.VMEM((M, MW), BF16)],
        compiler_params=_cparams(("parallel", "arbitrary")),
        name="memory_attention",
    )(qm, mem, wkv)


PACK_SUB = 4


def _pack_rows(ref, val):
    n = val.shape[0]
    bits = pltpu.bitcast(val, U32)
    r = (bits + jnp.uint32(0x7FFF) + ((bits >> 16) & jnp.uint32(1))) >> 16
    half = val.shape[1] // 2
    words = (r[:, half:] << 16) | r[:, :half]
    for j in range(PACK_SUB):
        ref[pl.ds(j, n, stride=PACK_SUB), :] = words[:, j * LANES:(j + 1) * LANES]


def _unpack_rows(ref, n):
    slabs = [ref[pl.ds(j, n, stride=PACK_SUB), :] for j in range(PACK_SUB)]
    lo = [pltpu.bitcast(s << 16, F32) for s in slabs]
    hi = [pltpu.bitcast(s & jnp.uint32(0xFFFF0000), F32) for s in slabs]
    return jnp.concatenate(lo + hi, axis=1).astype(BF16)


def _packed_row(ref, r):
    return ref.at[pl.ds(pl.multiple_of(r * PACK_SUB, PACK_SUB), PACK_SUB), :]


def _outproj_kernel(yp_ref, yf_ref, ym_ref, x_ref, w_ref, g_ref, b_ref, o_ref, pk_ref, *, alpha):
    pw, fw = yp_ref.shape[1], yf_ref.shape[1]
    h = jnp.dot(yp_ref[...], w_ref[0:pw, :], preferred_element_type=F32)
    h = h + jnp.dot(yf_ref[...], w_ref[pw:pw + fw, :], preferred_element_type=F32)
    h = h + jnp.dot(ym_ref[...], w_ref[pw + fw:, :], preferred_element_type=F32)
    y = _layer_norm(alpha * x_ref[...] + h, g_ref[...], b_ref[...])
    _store_rows(o_ref, y)
    _pack_rows(pk_ref, y)


def _outproj(yp, yf, ym, x2, w_out, g, b, alpha, tm=512):
    T, D = x2.shape
    assert D == SUBLANES * LANES == 2 * PACK_SUB * LANES
    row = lambda i: (i, 0)
    fixed = lambda i: (0, 0)
    return pl.pallas_call(
        functools.partial(_outproj_kernel, alpha=alpha),
        grid=(T // tm,),
        in_specs=[pl.BlockSpec((tm, yp.shape[1]), row), pl.BlockSpec((tm, yf.shape[1]), row),
                  pl.BlockSpec((tm, ym.shape[1]), row), pl.BlockSpec((tm, D), row),
                  pl.BlockSpec(w_out.shape, fixed), pl.BlockSpec((1, D), fixed), pl.BlockSpec((1, D), fixed)],
        out_specs=[pl.BlockSpec((tm * SUBLANES, LANES), row),
                   pl.BlockSpec((tm * PACK_SUB, LANES), row)],
        out_shape=[jax.ShapeDtypeStruct((T * SUBLANES, LANES), F32),
                   jax.ShapeDtypeStruct((T * PACK_SUB, LANES), U32)],
        compiler_params=_cparams(("parallel",)),
        name="outproj_ln1",
    )(yp, yf, ym, x2, w_out, g, b)


ROUTE_TILE = 512


def _per_expert_column(v, dtype=F32):
    return jnp.broadcast_to(v.astype(dtype)[:, None], (v.shape[0], ROUTE_TILE))


def _router_kernel(x_ref, wh_ref, wl_ref, bias_ref, eidx_ref, gate_ref, rank_ref, cnt_ref, carry_sc, *, tt):
    E = wh_ref.shape[0]
    gsz = E // N_GROUPS
    ninf = -jnp.inf

    @pl.when(pl.program_id(0) == 0)
    def _():
        carry_sc[...] = jnp.zeros(carry_sc.shape, F32)

    x = _load_rows(x_ref, tt)
    xh = x.astype(BF16)
    xl = (x - xh.astype(F32)).astype(BF16)
    wh = wh_ref[...]
    nt = (((1,), (1,)), ((), ()))
    logits = lax.dot_general(wh, xh, nt, preferred_element_type=F32) + (
        lax.dot_general(wl_ref[...], xh, nt, preferred_element_type=F32)
        + lax.dot_general(wh, xl, nt, preferred_element_type=F32))
    scores = 1.0 / (1.0 + jnp.exp(-logits))
    biased = scores + bias_ref[...]
    eiota = lax.broadcasted_iota(I32, (E, tt), 0).astype(F32)
    giota = lax.broadcasted_iota(I32, (gsz, tt), 0).astype(F32)

    def cmax(a):
        return jnp.max(a, axis=0, keepdims=True)

    def first_at(a, m, iota, n):
        return jnp.min(jnp.where(a == m, iota, float(n)), axis=0, keepdims=True)

    groups = [biased[g * gsz:(g + 1) * gsz, :] for g in range(N_GROUPS)]
    gscore = []
    for blk in groups:
        m1 = cmax(blk)
        m2 = cmax(jnp.where(giota == first_at(blk, m1, giota, gsz), ninf, blk))
        gscore.append(m1 + m2)

    kept = []
    for g in range(N_GROUPS):
        ahead = jnp.zeros((1, tt), F32)
        for o in range(N_GROUPS):
            if o == g:
                continue
            beats = gscore[o] > gscore[g]
            if o < g:
                beats = beats | (gscore[o] == gscore[g])
            ahead = ahead + jnp.where(beats, 1.0, 0.0)
        kept.append(groups[g] + jnp.where(ahead < float(TOPK_GROUPS), 0.0, ninf))
    masked = jnp.concatenate(kept, axis=0)

    picks, sels = [], []
    chosen = jnp.zeros((E, tt), F32)
    for _ in range(TOP_K):
        ik = first_at(masked, cmax(masked), eiota, E)
        oh = eiota == ik
        picks.append(ik)
        sels.append(jnp.sum(jnp.where(oh, scores, 0.0), axis=0, keepdims=True))
        chosen = jnp.where(oh, 1.0, chosen)
        masked = jnp.where(oh, ninf, masked)
    denom = sels[0]
    for sk in sels[1:]:
        denom = denom + sk

    r = lax.broadcasted_iota(I32, (tt, tt), 0)
    c = lax.broadcasted_iota(I32, (tt, tt), 1)
    earlier = jnp.where(r < c, 1.0, 0.0).astype(BF16)
    chosen_b = chosen.astype(BF16)
    pos = carry_sc[...] + jnp.dot(chosen_b, earlier, preferred_element_type=F32)
    carry_sc[...] = carry_sc[...] + jnp.dot(chosen_b, jnp.ones((tt, tt), BF16), preferred_element_type=F32)
    cnt_ref[...] = carry_sc[...]

    ranks = [jnp.sum(jnp.where(eiota == ik, pos, 0.0), axis=0, keepdims=True) for ik in picks]
    eidx_ref[...] = jnp.concatenate(picks, axis=0).astype(I32)
    gate_ref[...] = jnp.concatenate([sk / denom * ROUTED_SCALE for sk in sels], axis=0)
    rank_ref[...] = jnp.concatenate(ranks, axis=0).astype(I32)


def _router(x1t, wrt_hi, wrt_lo, rbias):
    T = x1t.shape[0] // SUBLANES
    E, D = wrt_hi.shape
    tt = ROUTE_TILE
    tok = lambda i: (0, i)
    fixed = lambda i: (0, 0)
    return pl.pallas_call(
        functools.partial(_router_kernel, tt=tt),
        grid=(T // tt,),
        in_specs=[pl.BlockSpec((tt * SUBLANES, LANES), lambda i: (i, 0)), pl.BlockSpec((E, D), fixed),
                  pl.BlockSpec((E, D), fixed), pl.BlockSpec((E, tt), fixed)],
        out_specs=[pl.BlockSpec((TOP_K, tt), tok), pl.BlockSpec((TOP_K, tt), tok), pl.BlockSpec((TOP_K, tt), tok),
                   pl.BlockSpec((E, tt), fixed)],
        out_shape=[jax.ShapeDtypeStruct((TOP_K, T), I32), jax.ShapeDtypeStruct((TOP_K, T), F32),
                   jax.ShapeDtypeStruct((TOP_K, T), I32), jax.ShapeDtypeStruct((E, tt), F32)],
        scratch_shapes=[pltpu.VMEM((E, tt), F32)],
        compiler_params=_cparams(("arbitrary",)),
        name="router_topk",
    )(x1t, wrt_hi, wrt_lo, rbias)


def _dest_kernel(e_ref, r_ref, sp_ref, d_ref):
    E = sp_ref.shape[0]
    tt = e_ref.shape[1]
    eiota = lax.broadcasted_iota(I32, (E, tt), 0)
    sp = sp_ref[...]
    rows = []
    for kk in range(TOP_K):
        start = jnp.sum(jnp.where(eiota == e_ref[kk:kk + 1, :], sp, 0.0), axis=0, keepdims=True)
        rows.append(start.astype(I32) + r_ref[kk:kk + 1, :])
    d_ref[...] = jnp.concatenate(rows, axis=0)


def _dest(eidx, rank, start_pad_b):
    K, T = eidx.shape
    tt = ROUTE_TILE
    tok = lambda i: (0, i)
    return pl.pallas_call(
        _dest_kernel,
        grid=(T // tt,),
        in_specs=[pl.BlockSpec((K, tt), tok), pl.BlockSpec((K, tt), tok),
                  pl.BlockSpec(start_pad_b.shape, lambda i: (0, 0))],
        out_specs=pl.BlockSpec((K, tt), tok),
        out_shape=jax.ShapeDtypeStruct((K, T), I32),
        compiler_params=_cparams(("parallel",)),
        name="moe_dest",
    )(eidx, rank, start_pad_b)


def _dispatch_kernel(zb_ref, d_ref, x_ref, xs_ref, zero_sc, sem, *, tt, n_exp):
    blk_rows = ROW_BLOCK * PACK_SUB

    @pl.when(pl.program_id(0) == 0)
    def _():
        zero_sc[...] = jnp.zeros(zero_sc.shape, U32)

        def zstart(e, c):
            @pl.when(zb_ref[e] >= 0)
            def _():
                dst = xs_ref.at[pl.ds(pl.multiple_of(zb_ref[e] * PACK_SUB, blk_rows), blk_rows), :]
                pltpu.make_async_copy(zero_sc, dst, sem).start()
            return c

        def zwait(e, c):
            @pl.when(zb_ref[e] >= 0)
            def _():
                pltpu.make_async_copy(zero_sc, xs_ref.at[pl.ds(0, blk_rows), :], sem).wait()
            return c

        lax.fori_loop(0, n_exp, zstart, 0)
        lax.fori_loop(0, n_exp, zwait, 0)

    def start(j, c):
        src = _packed_row(x_ref, j)
        for kk in range(TOP_K):
            pltpu.make_async_copy(src, _packed_row(xs_ref, d_ref[kk, j]), sem).start(priority=kk % 2)
        return c

    lax.fori_loop(0, tt, start, 0)
    for kk in range(TOP_K):
        pltpu.make_async_copy(x_ref, xs_ref.at[pl.ds(0, tt * PACK_SUB), :], sem).wait()


def _dispatch(zero_blk, dest, xpk, n_rows, tt=256):
    T = xpk.shape[0] // PACK_SUB
    n_exp = zero_blk.shape[0]
    grid_spec = pltpu.PrefetchScalarGridSpec(
        num_scalar_prefetch=1,
        grid=(T // tt,),
        in_specs=[pl.BlockSpec((TOP_K, tt), lambda i, zb: (0, i), memory_space=pltpu.SMEM),
                  pl.BlockSpec((tt * PACK_SUB, LANES), lambda i, zb: (i, 0))],
        out_specs=pl.BlockSpec(memory_space=pl.ANY),
        scratch_shapes=[pltpu.VMEM((ROW_BLOCK * PACK_SUB, LANES), U32), pltpu.SemaphoreType.DMA],
    )
    return pl.pallas_call(
        functools.partial(_dispatch_kernel, tt=tt, n_exp=n_exp),
        grid_spec=grid_spec,
        out_shape=jax.ShapeDtypeStruct((n_rows * PACK_SUB, LANES), U32),
        compiler_params=_cparams(("arbitrary",)),
        name="moe_dispatch",
    )(zero_blk, dest, xpk)


def _expert_kernel(sb_ref, nb_ref, nu_ref, wg_ref, wu_ref, wd_ref, xs_ref, y_ref,
                   xbuf, ybuf, wg_sc, wu_sc, wd_sc, isem, osem):
    e = pl.program_id(0)
    nb = nb_ref[e]
    base = sb_ref[e]
    n_used = nu_ref[0]
    out_rows = ROW_BLOCK * SUBLANES
    in_rows = ROW_BLOCK * PACK_SUB

    def fetch(blk, slot):
        src = xs_ref.at[pl.ds(pl.multiple_of(blk * in_rows, in_rows), in_rows), :]
        pltpu.make_async_copy(src, xbuf.at[slot], isem.at[slot]).start()

    def wait_in(slot):
        pltpu.make_async_copy(xs_ref.at[pl.ds(0, in_rows), :], xbuf.at[slot], isem.at[slot]).wait()

    def store(blk, slot):
        dst = y_ref.at[pl.ds(pl.multiple_of(blk * out_rows, out_rows), out_rows), :]
        pltpu.make_async_copy(ybuf.at[slot], dst, osem.at[slot]).start()

    def wait_out(slot):
        pltpu.make_async_copy(ybuf.at[slot], y_ref.at[pl.ds(0, out_rows), :], osem.at[slot]).wait()

    @pl.when(nb > 0)
    def _():
        @pl.when(base == 0)
        def _():
            for g0 in range(EXPERT_AHEAD):
                @pl.when(g0 < n_used)
                def _():
                    fetch(g0, g0 % EXPERT_BUFS)

        wg_sc[...] = wg_ref[0].astype(BF16)
        wu_sc[...] = wu_ref[0].astype(BF16)
        wd_sc[...] = wd_ref[0].astype(BF16)

        def step(g, slot):
            wait_in(slot)

            @pl.when(g + EXPERT_AHEAD < n_used)
            def _():
                fetch(g + EXPERT_AHEAD, (slot + EXPERT_AHEAD) % EXPERT_BUFS)

            @pl.when(g >= EXPERT_BUFS)
            def _():
                wait_out(slot)

            xb = _unpack_rows(xbuf.at[slot], ROW_BLOCK)
            a = jnp.dot(xb, wg_sc[...], preferred_element_type=F32)
            u = jnp.dot(xb, wu_sc[...], preferred_element_type=F32)
            h = (_silu(a) * u).astype(BF16)
            _store_rows(ybuf.at[slot], jnp.dot(h, wd_sc[...], preferred_element_type=F32))
            store(g, slot)

        first_group = base // EXPERT_BUFS

        def group(p, c):
            for slot in range(EXPERT_BUFS):
                g = (first_group + p) * EXPERT_BUFS + slot

                @pl.when((g >= base) & (g < base + nb))
                def _():
                    step(g, slot)
            return c

        lax.fori_loop(0, (base + nb + EXPERT_BUFS - 1) // EXPERT_BUFS - first_group, group, 0)

    @pl.when(e == pl.num_programs(0) - 1)
    def _():
        for slot in range(EXPERT_BUFS):
            @pl.when(slot < n_used)
            def _():
                wait_out(slot)


def _experts(start_blk, n_blk_e, n_used, xs, w_gate, w_up, w_down):
    n_rows = xs.shape[0] // PACK_SUB
    E, D, H = w_gate.shape
    wsel = lambda e, sb, nb, nu: (e, 0, 0)
    grid_spec = pltpu.PrefetchScalarGridSpec(
        num_scalar_prefetch=3,
        grid=(E,),
        in_specs=[pl.BlockSpec((1, D, H), wsel), pl.BlockSpec((1, D, H), wsel), pl.BlockSpec((1, H, D), wsel),
                  pl.BlockSpec(memory_space=pl.ANY)],
        out_specs=pl.BlockSpec(memory_space=pl.ANY),
        scratch_shapes=[pltpu.VMEM((EXPERT_BUFS, ROW_BLOCK * PACK_SUB, LANES), U32),
                        pltpu.VMEM((EXPERT_BUFS, ROW_BLOCK * SUBLANES, LANES), F32),
                        pltpu.VMEM((D, H), BF16), pltpu.VMEM((D, H), BF16), pltpu.VMEM((H, D), BF16),
                        pltpu.SemaphoreType.DMA((EXPERT_BUFS,)), pltpu.SemaphoreType.DMA((EXPERT_BUFS,))],
    )
    return pl.pallas_call(
        _expert_kernel,
        grid_spec=grid_spec,
        out_shape=jax.ShapeDtypeStruct((n_rows * SUBLANES, LANES), F32),
        compiler_params=_cparams(("arbitrary",)),
        name="moe_experts",
    )(start_blk, n_blk_e, n_used, w_gate, w_up, w_down, xs)


def _combine_kernel(dcur_ref, dnxt_ref, x_ref, gate_ref, wsg_ref, wsu_ref, wsd_ref, g_ref, b_ref, y_ref,
                    o_ref, buf, sem, *, tt, alpha):
    i = pl.program_id(0)
    n = pl.num_programs(0)

    def fetch(d_ref, slot):
        def body(j, c):
            for kk in range(TOP_K):
                src = _tile_rows(y_ref, d_ref[kk, j])
                pltpu.make_async_copy(src, _tile_rows(buf.at[slot, kk], j), sem.at[slot]).start(priority=kk % 2)
            return c

        lax.fori_loop(0, tt, body, 0)

    def reduce(slot):
        for kk in range(TOP_K):
            pltpu.make_async_copy(y_ref.at[pl.ds(0, tt * SUBLANES), :], buf.at[slot, kk], sem.at[slot]).wait()
        x = _load_rows(x_ref, tt)
        xb = x.astype(BF16)
        hs = _silu(jnp.dot(xb, wsg_ref[...], preferred_element_type=F32)) * jnp.dot(
            xb, wsu_ref[...], preferred_element_type=F32)
        moe = jnp.dot(hs.astype(BF16), wsd_ref[...], preferred_element_type=F32)
        gates = gate_ref[...]
        for kk in range(TOP_K):
            moe = moe + gates[:, kk:kk + 1] * _load_rows(buf.at[slot, kk], tt)
        o_ref[...] = _layer_norm(alpha * x + moe, g_ref[...], b_ref[...])

    @pl.when(i == 0)
    def _():
        fetch(dcur_ref, 0)

    for slot in range(2):
        @pl.when((i % 2 == slot) & (i + 1 < n))
        def _():
            fetch(dnxt_ref, 1 - slot)

    for slot in range(2):
        @pl.when(i % 2 == slot)
        def _():
            reduce(slot)


def _combine(dest, x1t, gates, wsg, wsu, wsd, g, b, y, alpha, tt=128):
    T = x1t.shape[0] // SUBLANES
    D = SUBLANES * LANES
    n = T // tt
    row = lambda i: (i, 0)
    fixed = lambda i: (0, 0)
    return pl.pallas_call(
        functools.partial(_combine_kernel, tt=tt, alpha=alpha),
        grid=(n,),
        in_specs=[pl.BlockSpec((TOP_K, tt), lambda i: (0, i), memory_space=pltpu.SMEM),
                  pl.BlockSpec((TOP_K, tt), lambda i: (0, jnp.minimum(i + 1, n - 1)), memory_space=pltpu.SMEM),
                  pl.BlockSpec((tt * SUBLANES, LANES), row), pl.BlockSpec((tt, TOP_K), row),
                  pl.BlockSpec(wsg.shape, fixed), pl.BlockSpec(wsu.shape, fixed), pl.BlockSpec(wsd.shape, fixed),
                  pl.BlockSpec((1, D), fixed), pl.BlockSpec((1, D), fixed),
                  pl.BlockSpec(memory_space=pl.ANY)],
        out_specs=pl.BlockSpec((tt, D), row),
        out_shape=jax.ShapeDtypeStruct((T, D), F32),
        scratch_shapes=[pltpu.VMEM((2, TOP_K, tt * SUBLANES, LANES), F32), pltpu.SemaphoreType.DMA((2,))],
        compiler_params=_cparams(("arbitrary",)),
        name="moe_combine_ln2",
    )(dest, dest, x1t, gates, wsg, wsu, wsd, g, b, y)


def _layer(x2, mem, B, S, depth, w_in, b_f, w_pool, pool_scale, w_mem_kv, w_out, ln1_g, ln1_b,
           w_router, router_bias, w_gate, w_up, w_down, ws_gate, ws_up, ws_down, ln2_g, ln2_b):
    T, D = x2.shape
    n_win, grp = w_pool.shape[0], w_pool.shape[1]
    pw = n_win * grp
    fw = FOX_HEADS * HEAD_DIM
    mw = w_mem_kv.shape[1] // 2
    E = w_router.shape[1]
    alpha = (2 * depth) ** 0.25

    f_lo = pw + 3 * fw
    w_main = jnp.concatenate([w_in[:, :f_lo], w_in[:, f_lo + FOX_HEADS:]], axis=1).astype(BF16)
    w_f = jnp.pad(w_in[:, f_lo:f_lo + FOX_HEADS], ((0, 0), (0, LANES - FOX_HEADS))).astype(BF16)
    bf_pad = jnp.pad(b_f, (0, LANES - FOX_HEADS)).reshape(1, LANES)
    wbd = jnp.zeros((pw, pw), F32)
    for g in range(n_win):
        wbd = wbd.at[g * grp:(g + 1) * grp, g * grp:(g + 1) * grp].set(w_pool[g])
    wrt = w_router.T
    wrt_hi = wrt.astype(BF16)
    wrt_lo = (wrt - wrt_hi.astype(F32)).astype(BF16)

    u_pool, q, k, v, q_mem, f_logit = _inproj(x2, w_main, w_f, pw, fw, mw)
    y_pool = _pool(u_pool, wbd.astype(BF16), pool_scale.reshape(1, pw), B, S)
    fcum = _fgate(f_logit, bf_pad, B, S)
    y_fox = _fox(q, k, v, fcum, B, S)
    y_mem = _memattn(q_mem, mem, w_mem_kv.astype(BF16), B, S)
    x1, x1_packed = _outproj(y_pool, y_fox, y_mem, x2, w_out.astype(BF16), ln1_g.reshape(1, D),
                             ln1_b.reshape(1, D), alpha)

    eidx, gates, rank, counts = _router(x1, wrt_hi, wrt_lo, _per_expert_column(router_bias))

    cnt = counts[:, 0].astype(I32)
    padded = (cnt + ROW_BLOCK - 1) // ROW_BLOCK * ROW_BLOCK
    pad_end = jnp.cumsum(padded)
    start_pad = (pad_end - padded).astype(I32)
    n_rows = (T * TOP_K + E * (ROW_BLOCK - 1)) // ROW_BLOCK * ROW_BLOCK
    n_used = (pad_end[-1] // ROW_BLOCK).astype(I32)
    zero_blk = jnp.where(padded > 0, pad_end - ROW_BLOCK, -1).astype(I32)
    dest = _dest(eidx, rank, _per_expert_column(start_pad))

    xs = _dispatch(zero_blk, dest, x1_packed, n_rows)
    y = _experts(start_pad // ROW_BLOCK, (padded // ROW_BLOCK).astype(I32), n_used.reshape(1), xs,
                 w_gate, w_up, w_down)
    return _combine(dest, x1, gates.T, ws_gate.astype(BF16), ws_up.astype(BF16),
                    ws_down.astype(BF16), ln2_g.reshape(1, D), ln2_b.reshape(1, D), y, alpha)


def kernel(x, mem, w_in, b_f, w_pool, pool_scale, w_mem_kv, w_out, ln1_g, ln1_b, w_router, router_bias,
           w_gate, w_up, w_down, ws_gate, ws_up, ws_down, ln2_g, ln2_b):
    B, S, D = x.shape
    depth = w_in.shape[0]
    x2 = x.reshape(B * S, D)
    for l in range(depth):
        x2 = _layer(x2, mem, B, S, depth, w_in[l], b_f[l], w_pool[l], pool_scale[l], w_mem_kv[l], w_out[l],
                    ln1_g[l], ln1_b[l], w_router[l], router_bias[l], w_gate[l], w_up[l], w_down[l],
                    ws_gate[l], ws_up[l], ws_down[l], ln2_g[l], ln2_b[l])
    return x2.reshape(B, S, D)
```

```python
import functools

import jax
import jax.numpy as jnp
from jax import lax
from jax.experimental import pallas as pl
from jax.experimental.pallas import tpu as pltpu

F32 = jnp.float32
BF16 = jnp.bfloat16
I32 = jnp.int32
U32 = jnp.uint32

LANES = 128
SUBLANES = 8
POOL_WINDOWS = (2, 4, 8, 16)
POOL_HALO = 16
HEAD_DIM = 64
FOX_HEADS = 8
BIAS_TERMS = 3
MEM_HEADS = 4
TOP_K = 8
N_GROUPS = 8
TOPK_GROUPS = 4
ROUTED_SCALE = 2.5
LN_EPS = 1e-5
ROW_BLOCK = 256
EXPERT_BUFS = 4
EXPERT_AHEAD = EXPERT_BUFS - 1
VMEM_LIMIT = 48 * 1024 * 1024


def _cparams(sem):
    return pltpu.CompilerParams(dimension_semantics=sem, vmem_limit_bytes=VMEM_LIMIT)


def _layer_norm(z, g, b):
    mu = jnp.mean(z, axis=-1, keepdims=True)
    zc = z - mu
    var = jnp.mean(zc * zc, axis=-1, keepdims=True)
    return zc * lax.rsqrt(var + LN_EPS) * g + b


def _silu(x):
    return x * (1.0 / (1.0 + jnp.exp(-x)))


def _load_rows(ref, n):
    return jnp.concatenate([ref[pl.ds(j, n, stride=SUBLANES), :] for j in range(SUBLANES)], axis=1)


def _store_rows(ref, val):
    n = val.shape[0]
    for j in range(SUBLANES):
        ref[pl.ds(j, n, stride=SUBLANES), :] = val[:, j * LANES:(j + 1) * LANES]


def _tile_rows(ref, r):
    return ref.at[pl.ds(pl.multiple_of(r * SUBLANES, SUBLANES), SUBLANES), :]


def _inproj_kernel(x_ref, w_ref, wf_ref, up_ref, q_ref, k_ref, v_ref, qm_ref, f_ref, *, pw, fw, mw, scale):
    xb = x_ref[...].astype(BF16)

    def proj(lo, hi):
        return jnp.dot(xb, w_ref[:, lo:hi], preferred_element_type=F32)

    up_ref[...] = proj(0, pw)
    q_ref[...] = (proj(pw, pw + fw) * scale).astype(BF16)
    k_ref[...] = proj(pw + fw, pw + 2 * fw).astype(BF16)
    v_ref[...] = proj(pw + 2 * fw, pw + 3 * fw).astype(BF16)
    qm_ref[...] = (proj(pw + 3 * fw, pw + 3 * fw + mw) * scale).astype(BF16)
    f_ref[...] = jnp.dot(xb, wf_ref[...], preferred_element_type=F32)


def _inproj(x2, w_main, w_f, pw, fw, mw, tm=512):
    T, D = x2.shape
    kern = functools.partial(_inproj_kernel, pw=pw, fw=fw, mw=mw, scale=HEAD_DIM ** -0.5)
    row = lambda i: (i, 0)
    fixed = lambda i: (0, 0)
    return pl.pallas_call(
        kern,
        grid=(T // tm,),
        in_specs=[pl.BlockSpec((tm, D), row),
                  pl.BlockSpec(w_main.shape, fixed),
                  pl.BlockSpec(w_f.shape, fixed)],
        out_specs=[pl.BlockSpec((tm, pw), row), pl.BlockSpec((tm, fw), row), pl.BlockSpec((tm, fw), row),
                   pl.BlockSpec((tm, fw), row), pl.BlockSpec((tm, mw), row), pl.BlockSpec((tm, LANES), row)],
        out_shape=[jax.ShapeDtypeStruct((T, pw), F32), jax.ShapeDtypeStruct((T, fw), BF16),
                   jax.ShapeDtypeStruct((T, fw), BF16), jax.ShapeDtypeStruct((T, fw), BF16),
                   jax.ShapeDtypeStruct((T, mw), BF16), jax.ShapeDtypeStruct((T, LANES), F32)],
        compiler_params=_cparams(("parallel",)),
        name="inproj",
    )(x2, w_main, w_f)


def _pool_kernel(u_ref, wbd_ref, sc_ref, o_ref, ext_sc, *, chunk, group):
    S, W = u_ref.shape
    ext_sc[0:POOL_HALO, :] = jnp.zeros((POOL_HALO, W), F32)
    ext_sc[POOL_HALO:, :] = u_ref[...]
    rows = chunk + POOL_HALO
    lrow = lax.broadcasted_iota(I32, (rows, W), 0)
    lane = lax.broadcasted_iota(I32, (rows, W), 1)

    def body(c, carry):
        start = pl.multiple_of(c * chunk, chunk)
        e = ext_sc[pl.ds(start, rows), :]
        posf = (lrow + (start - POOL_HALO + 1)).astype(F32)
        acc = e
        d = jnp.zeros_like(e)
        shift = 1
        for g, w in enumerate(POOL_WINDOWS):
            while shift < w:
                acc = acc + pltpu.roll(acc, shift, axis=0)
                shift *= 2
            mean = acc / jnp.minimum(posf, float(w))
            d = jnp.where((lane >= g * group) & (lane < (g + 1) * group), mean, d)
        d = (d - e)[POOL_HALO:, :]
        y = jnp.dot(d.astype(BF16), wbd_ref[...], preferred_element_type=F32) * sc_ref[...]
        o_ref[pl.ds(start, chunk), :] = y.astype(BF16)
        return carry

    lax.fori_loop(0, S // chunk, body, 0)


def _pool(u, wbd, pscale, B, S, chunk=512):
    T, W = u.shape
    kern = functools.partial(_pool_kernel, chunk=chunk, group=W // len(POOL_WINDOWS))
    return pl.pallas_call(
        kern,
        grid=(B,),
        in_specs=[pl.BlockSpec((S, W), lambda b: (b, 0)),
                  pl.BlockSpec((W, W), lambda b: (0, 0)),
                  pl.BlockSpec((1, W), lambda b: (0, 0))],
        out_specs=pl.BlockSpec((S, W), lambda b: (b, 0)),
        out_shape=jax.ShapeDtypeStruct((T, W), BF16),
        scratch_shapes=[pltpu.VMEM((S + POOL_HALO, W), F32)],
        compiler_params=_cparams(("parallel",)),
        name="pool_mixer",
    )(u, wbd, pscale)


def _fgate_kernel(f_ref, bf_ref, o_ref):
    S = f_ref.shape[0]
    z = f_ref[...] + bf_ref[...]
    x = jnp.minimum(z, 0.0) - jnp.log(1.0 + jnp.exp(-jnp.abs(z)))
    row = lax.broadcasted_iota(I32, x.shape, 0)
    lane = lax.broadcasted_iota(I32, x.shape, 1)
    shift = 1
    while shift < S:
        x = x + jnp.where(row >= shift, pltpu.roll(x, shift, axis=0), 0.0)
        shift *= 2
    hi = x.astype(BF16).astype(F32)
    mid = (x - hi).astype(BF16).astype(F32)
    lo = (x - hi) - mid
    for p in range(FOX_HEADS // 2):
        out = jnp.zeros(x.shape, F32)
        for hh in range(2):
            c = 2 * p + hh
            for j, part in enumerate((hi, mid, lo)):
                out = jnp.where(lane == BIAS_TERMS * hh + j, -part[:, c:c + 1], out)
        o_ref[0, p] = out.astype(BF16)


def _fgate(f_logit, bf_pad, B, S):
    return pl.pallas_call(
        _fgate_kernel,
        grid=(B,),
        in_specs=[pl.BlockSpec((S, LANES), lambda b: (b, 0)),
                  pl.BlockSpec((1, LANES), lambda b: (0, 0))],
        out_specs=pl.BlockSpec((1, FOX_HEADS // 2, S, LANES), lambda b: (b, 0, 0, 0)),
        out_shape=jax.ShapeDtypeStruct((B, FOX_HEADS // 2, S, LANES), BF16),
        compiler_params=_cparams(("parallel",)),
        name="forget_cumsum",
    )(f_logit, bf_pad)


def _fox_kernel(q_ref, k_ref, a_ref, v_ref, o_ref, vt_sc, qa_sc, sa_sc, sb_sc, m_sc, l_sc, acc_sc, *, blk):
    qi = pl.program_id(2)
    nk = vt_sc.shape[0]
    nt = (((1,), (1,)), ((), ()))

    @pl.when(qi == 0)
    def _():
        for c in range(nk):
            vt_sc[c] = v_ref[c * blk:(c + 1) * blk, :].astype(F32).T.astype(BF16)

    q = q_ref[...].astype(F32)
    lane = lax.broadcasted_iota(I32, q.shape, 1)
    for h in range(2):
        mine = (lane >= h * HEAD_DIM) & (lane < (h + 1) * HEAD_DIM)
        bias_rows = (lane >= h * BIAS_TERMS) & (lane < (h + 1) * BIAS_TERMS)
        qa_sc[h] = jnp.concatenate([jnp.where(mine, q, 0.0), jnp.where(bias_rows, 1.0, 0.0)], axis=1).astype(BF16)
    m_sc[...] = jnp.full(m_sc.shape, -jnp.inf, F32)
    l_sc[...] = jnp.zeros(l_sc.shape, F32)
    acc_sc[...] = jnp.zeros(acc_sc.shape, F32)

    def scores(ki, dst):
        ks = pl.multiple_of(ki * blk, blk)
        kb = jnp.concatenate([k_ref[pl.ds(ks, blk), :], a_ref[0, 0, pl.ds(ks, blk), :]], axis=1)
        for h in range(2):
            dst[h] = lax.dot_general(kb, qa_sc[h], nt, preferred_element_type=F32)

    def absorb(ki, src, causal):
        for h in range(2):
            st = src[h]
            if causal:
                kpos = lax.broadcasted_iota(I32, st.shape, 0)
                qpos = lax.broadcasted_iota(I32, st.shape, 1)
                st = jnp.where(kpos <= qpos, st, -jnp.inf)
            m_prev = m_sc[h]
            m_new = jnp.maximum(m_prev, jnp.max(st, axis=0, keepdims=True))
            alpha = jnp.exp(m_prev - m_new)
            p = jnp.exp(st - m_new)
            l_sc[h] = alpha * l_sc[h] + jnp.sum(p, axis=0, keepdims=True)
            m_sc[h] = m_new
            vt = vt_sc[ki, h * HEAD_DIM:(h + 1) * HEAD_DIM, :]
            pv = jnp.dot(vt, p.astype(BF16), preferred_element_type=F32)
            rows = slice(h * HEAD_DIM, (h + 1) * HEAD_DIM)
            acc_sc[rows, :] = acc_sc[rows, :] * alpha + pv

    scores(0, sa_sc)

    def body(p, carry):
        scores(2 * p + 1, sb_sc)
        absorb(2 * p, sa_sc, False)
        scores(2 * p + 2, sa_sc)
        absorb(2 * p + 1, sb_sc, False)
        return carry

    lax.fori_loop(0, qi // 2, body, 0)

    @pl.when(qi % 2 == 1)
    def _():
        scores(qi, sb_sc)
        absorb(qi - 1, sa_sc, False)
        absorb(qi, sb_sc, True)

    @pl.when(qi % 2 == 0)
    def _():
        absorb(qi, sa_sc, True)

    out_t = jnp.concatenate([acc_sc[h * HEAD_DIM:(h + 1) * HEAD_DIM, :] / l_sc[h] for h in range(2)], axis=0)
    o_ref[...] = out_t.T.astype(BF16)


def _fox(q, k, v, fbias, B, S, blk=512):
    T, FW = q.shape
    nq = S // blk
    pairs = FW // LANES
    kern = functools.partial(_fox_kernel, blk=blk)
    return pl.pallas_call(
        kern,
        grid=(B, pairs, nq),
        in_specs=[pl.BlockSpec((blk, LANES), lambda b, p, i: (b * nq + i, p)),
                  pl.BlockSpec((S, LANES), lambda b, p, i: (b, p)),
                  pl.BlockSpec((1, 1, S, LANES), lambda b, p, i: (b, p, 0, 0)),
                  pl.BlockSpec((S, LANES), lambda b, p, i: (b, p))],
        out_specs=pl.BlockSpec((blk, LANES), lambda b, p, i: (b * nq + i, p)),
        out_shape=jax.ShapeDtypeStruct((T, FW), BF16),
        scratch_shapes=[pltpu.VMEM((nq, LANES, blk), BF16), pltpu.VMEM((2, blk, 2 * LANES), BF16),
                        pltpu.VMEM((2, blk, blk), F32), pltpu.VMEM((2, blk, blk), F32),
                        pltpu.VMEM((2, 1, blk), F32), pltpu.VMEM((2, 1, blk), F32), pltpu.VMEM((LANES, blk), F32)],
        compiler_params=_cparams(("parallel", "parallel", "arbitrary")),
        name="fox_attention",
    )(q, k, fbias, v)


def _memattn_kernel(qm_ref, mem_ref, wkv_ref, o_ref, k_sc, v_sc):
    MW = qm_ref.shape[1]

    @pl.when(pl.program_id(1) == 0)
    def _():
        kv = jnp.dot(mem_ref[0].astype(BF16), wkv_ref[...], preferred_element_type=F32)
        k_sc[...] = kv[:, :MW].astype(BF16)
        v_sc[...] = kv[:, MW:].astype(BF16)

    q = qm_ref[...]
    lane = lax.broadcasted_iota(I32, q.shape, 1)
    hd = MW // MEM_HEADS
    out = jnp.zeros(q.shape, F32)
    for h in range(MEM_HEADS):
        mine = (lane >= h * hd) & (lane < (h + 1) * hd)
        qh = jnp.where(mine, q, jnp.zeros_like(q))
        s = lax.dot_general(qh, k_sc[...], (((1,), (1,)), ((), ())), preferred_element_type=F32)
        p = jnp.exp(s - jnp.max(s, axis=1, keepdims=True))
        l = jnp.sum(p, axis=1, keepdims=True)
        o = jnp.dot(p.astype(BF16), v_sc[...], preferred_element_type=F32)
        out = jnp.where(mine, o / l, out)
    o_ref[...] = out.astype(BF16)


def _memattn(qm, mem, wkv, B, S, tq=512):
    T, MW = qm.shape
    M, D = mem.shape[1], mem.shape[2]
    nq = S // tq
    return pl.pallas_call(
        _memattn_kernel,
        grid=(B, nq),
        in_specs=[pl.BlockSpec((tq, MW), lambda b, i: (b * nq + i, 0)),
                  pl.BlockSpec((1, M, D), lambda b, i: (b, 0, 0)),
                  pl.BlockSpec(wkv.shape, lambda b, i: (0, 0))],
        out_specs=pl.BlockSpec((tq, MW), lambda b, i: (b * nq + i, 0)),
        out_shape=jax.ShapeDtypeStruct((T, MW), BF16),
        scratch_shapes=[pltpu.VMEM((M, MW), BF16), pltpu.VMEM((M, MW), BF16)],
        compiler_params=_cparams(("parallel", "arbitrary")),
        name="memory_attention",
    )(qm, mem, wkv)


PACK_SUB = 4


def _pack_rows(ref, val):
    n = val.shape[0]
    bits = pltpu.bitcast(val, U32)
    r = (bits + jnp.uint32(0x7FFF) + ((bits >> 16) & jnp.uint32(1))) >> 16
    half = val.shape[1] // 2
    words = (r[:, half:] << 16) | r[:, :half]
    for j in range(PACK_SUB):
        ref[pl.ds(j, n, stride=PACK_SUB), :] = words[:, j * LANES:(j + 1) * LANES]


def _unpack_rows(ref, n):
    slabs = [ref[pl.ds(j, n, stride=PACK_SUB), :] for j in range(PACK_SUB)]
    lo = [pltpu.bitcast(s << 16, F32) for s in slabs]
    hi = [pltpu.bitcast(s & jnp.uint32(0xFFFF0000), F32) for s in slabs]
    return jnp.concatenate(lo + hi, axis=1).astype(BF16)


def _packed_row(ref, r):
    return ref.at[pl.ds(pl.multiple_of(r * PACK_SUB, PACK_SUB), PACK_SUB), :]


def _outproj_kernel(yp_ref, yf_ref, ym_ref, x_ref, w_ref, g_ref, b_ref, o_ref, pk_ref, *, alpha):
    pw, fw = yp_ref.shape[1], yf_ref.shape[1]
    h = jnp.dot(yp_ref[...], w_ref[0:pw, :], preferred_element_type=F32)
    h = h + jnp.dot(yf_ref[...], w_ref[pw:pw + fw, :], preferred_element_type=F32)
    h = h + jnp.dot(ym_ref[...], w_ref[pw + fw:, :], preferred_element_type=F32)
    y = _layer_norm(alpha * x_ref[...] + h, g_ref[...], b_ref[...])
    _store_rows(o_ref, y)
    _pack_rows(pk_ref, y)


def _outproj(yp, yf, ym, x2, w_out, g, b, alpha, tm=512):
    T, D = x2.shape
    assert D == SUBLANES * LANES == 2 * PACK_SUB * LANES
    row = lambda i: (i, 0)
    fixed = lambda i: (0, 0)
    return pl.pallas_call(
        functools.partial(_outproj_kernel, alpha=alpha),
        grid=(T // tm,),
        in_specs=[pl.BlockSpec((tm, yp.shape[1]), row), pl.BlockSpec((tm, yf.shape[1]), row),
                  pl.BlockSpec((tm, ym.shape[1]), row), pl.BlockSpec((tm, D), row),
                  pl.BlockSpec(w_out.shape, fixed), pl.BlockSpec((1, D), fixed), pl.BlockSpec((1, D), fixed)],
        out_specs=[pl.BlockSpec((tm * SUBLANES, LANES), row),
                   pl.BlockSpec((tm * PACK_SUB, LANES), row)],
        out_shape=[jax.ShapeDtypeStruct((T * SUBLANES, LANES), F32),
                   jax.ShapeDtypeStruct((T * PACK_SUB, LANES), U32)],
        compiler_params=_cparams(("parallel",)),
        name="outproj_ln1",
    )(yp, yf, ym, x2, w_out, g, b)


ROUTE_TILE = 512


def _per_expert_column(v, dtype=F32):
    return jnp.broadcast_to(v.astype(dtype)[:, None], (v.shape[0], ROUTE_TILE))


def _router_kernel(x_ref, wh_ref, wl_ref, bias_ref, eidx_ref, gate_ref, rank_ref, cnt_ref, carry_sc, *, tt):
    E = wh_ref.shape[0]
    gsz = E // N_GROUPS
    ninf = -jnp.inf

    @pl.when(pl.program_id(0) == 0)
    def _():
        carry_sc[...] = jnp.zeros(carry_sc.shape, F32)

    x = _load_rows(x_ref, tt)
    xh = x.astype(BF16)
    xl = (x - xh.astype(F32)).astype(BF16)
    wh = wh_ref[...]
    nt = (((1,), (1,)), ((), ()))
    logits = lax.dot_general(wh, xh, nt, preferred_element_type=F32) + (
        lax.dot_general(wl_ref[...], xh, nt, preferred_element_type=F32)
        + lax.dot_general(wh, xl, nt, preferred_element_type=F32))
    scores = 1.0 / (1.0 + jnp.exp(-logits))
    biased = scores + bias_ref[...]
    eiota = lax.broadcasted_iota(I32, (E, tt), 0).astype(F32)
    giota = lax.broadcasted_iota(I32, (gsz, tt), 0).astype(F32)

    def cmax(a):
        return jnp.max(a, axis=0, keepdims=True)

    def first_at(a, m, iota, n):
        return jnp.min(jnp.where(a == m, iota, float(n)), axis=0, keepdims=True)

    groups = [biased[g * gsz:(g + 1) * gsz, :] for g in range(N_GROUPS)]
    gscore = []
    for blk in groups:
        m1 = cmax(blk)
        m2 = cmax(jnp.where(giota == first_at(blk, m1, giota, gsz), ninf, blk))
        gscore.append(m1 + m2)

    kept = []
    for g in range(N_GROUPS):
        ahead = jnp.zeros((1, tt), F32)
        for o in range(N_GROUPS):
            if o == g:
                continue
            beats = gscore[o] > gscore[g]
            if o < g:
                beats = beats | (gscore[o] == gscore[g])
            ahead = ahead + jnp.where(beats, 1.0, 0.0)
        kept.append(groups[g] + jnp.where(ahead < float(TOPK_GROUPS), 0.0, ninf))
    masked = jnp.concatenate(kept, axis=0)

    picks, sels = [], []
    chosen = jnp.zeros((E, tt), F32)
    for _ in range(TOP_K):
        ik = first_at(masked, cmax(masked), eiota, E)
        oh = eiota == ik
        picks.append(ik)
        sels.append(jnp.sum(jnp.where(oh, scores, 0.0), axis=0, keepdims=True))
        chosen = jnp.where(oh, 1.0, chosen)
        masked = jnp.where(oh, ninf, masked)
    denom = sels[0]
    for sk in sels[1:]:
        denom = denom + sk

    r = lax.broadcasted_iota(I32, (tt, tt), 0)
    c = lax.broadcasted_iota(I32, (tt, tt), 1)
    earlier = jnp.where(r < c, 1.0, 0.0).astype(BF16)
    chosen_b = chosen.astype(BF16)
    pos = carry_sc[...] + jnp.dot(chosen_b, earlier, preferred_element_type=F32)
    carry_sc[...] = carry_sc[...] + jnp.dot(chosen_b, jnp.ones((tt, tt), BF16), preferred_element_type=F32)
    cnt_ref[...] = carry_sc[...]

    ranks = [jnp.sum(jnp.where(eiota == ik, pos, 0.0), axis=0, keepdims=True) for ik in picks]
    eidx_ref[...] = jnp.concatenate(picks, axis=0).astype(I32)
    gate_ref[...] = jnp.concatenate([sk / denom * ROUTED_SCALE for sk in sels], axis=0)
    rank_ref[...] = jnp.concatenate(ranks, axis=0).astype(I32)


def _router(x1t, wrt_hi, wrt_lo, rbias):
    T = x1t.shape[0] // SUBLANES
    E, D = wrt_hi.shape
    tt = ROUTE_TILE
    tok = lambda i: (0, i)
    fixed = lambda i: (0, 0)
    return pl.pallas_call(
        functools.partial(_router_kernel, tt=tt),
        grid=(T // tt,),
        in_specs=[pl.BlockSpec((tt * SUBLANES, LANES), lambda i: (i, 0)), pl.BlockSpec((E, D), fixed),
                  pl.BlockSpec((E, D), fixed), pl.BlockSpec((E, tt), fixed)],
        out_specs=[pl.BlockSpec((TOP_K, tt), tok), pl.BlockSpec((TOP_K, tt), tok), pl.BlockSpec((TOP_K, tt), tok),
                   pl.BlockSpec((E, tt), fixed)],
        out_shape=[jax.ShapeDtypeStruct((TOP_K, T), I32), jax.ShapeDtypeStruct((TOP_K, T), F32),
                   jax.ShapeDtypeStruct((TOP_K, T), I32), jax.ShapeDtypeStruct((E, tt), F32)],
        scratch_shapes=[pltpu.VMEM((E, tt), F32)],
        compiler_params=_cparams(("arbitrary",)),
        name="router_topk",
    )(x1t, wrt_hi, wrt_lo, rbias)


def _dest_kernel(e_ref, r_ref, sp_ref, d_ref):
    E = sp_ref.shape[0]
    tt = e_ref.shape[1]
    eiota = lax.broadcasted_iota(I32, (E, tt), 0)
    sp = sp_ref[...]
    rows = []
    for kk in range(TOP_K):
        start = jnp.sum(jnp.where(eiota == e_ref[kk:kk + 1, :], sp, 0.0), axis=0, keepdims=True)
        rows.append(start.astype(I32) + r_ref[kk:kk + 1, :])
    d_ref[...] = jnp.concatenate(rows, axis=0)


def _dest(eidx, rank, start_pad_b):
    K, T = eidx.shape
    tt = ROUTE_TILE
    tok = lambda i: (0, i)
    return pl.pallas_call(
        _dest_kernel,
        grid=(T // tt,),
        in_specs=[pl.BlockSpec((K, tt), tok), pl.BlockSpec((K, tt), tok),
                  pl.BlockSpec(start_pad_b.shape, lambda i: (0, 0))],
        out_specs=pl.BlockSpec((K, tt), tok),
        out_shape=jax.ShapeDtypeStruct((K, T), I32),
        compiler_params=_cparams(("parallel",)),
        name="moe_dest",
    )(eidx, rank, start_pad_b)


def _dispatch_kernel(zb_ref, d_ref, x_ref, xs_ref, zero_sc, sem, *, tt, n_exp):
    blk_rows = ROW_BLOCK * PACK_SUB

    @pl.when(pl.program_id(0) == 0)
    def _():
        zero_sc[...] = jnp.zeros(zero_sc.shape, U32)

        def zstart(e, c):
            @pl.when(zb_ref[e] >= 0)
            def _():
                dst = xs_ref.at[pl.ds(pl.multiple_of(zb_ref[e] * PACK_SUB, blk_rows), blk_rows), :]
                pltpu.make_async_copy(zero_sc, dst, sem).start()
            return c

        def zwait(e, c):
            @pl.when(zb_ref[e] >= 0)
            def _():
                pltpu.make_async_copy(zero_sc, xs_ref.at[pl.ds(0, blk_rows), :], sem).wait()
            return c

        lax.fori_loop(0, n_exp, zstart, 0)
        lax.fori_loop(0, n_exp, zwait, 0)

    def start(j, c):
        src = _packed_row(x_ref, j)
        for kk in range(TOP_K):
            pltpu.make_async_copy(src, _packed_row(xs_ref, d_ref[kk, j]), sem).start(priority=kk % 2)
        return c

    lax.fori_loop(0, tt, start, 0)
    for kk in range(TOP_K):
        pltpu.make_async_copy(x_ref, xs_ref.at[pl.ds(0, tt * PACK_SUB), :], sem).wait()


def _dispatch(zero_blk, dest, xpk, n_rows, tt=256):
    T = xpk.shape[0] // PACK_SUB
    n_exp = zero_blk.shape[0]
    grid_spec = pltpu.PrefetchScalarGridSpec(
        num_scalar_prefetch=1,
        grid=(T // tt,),
        in_specs=[pl.BlockSpec((TOP_K, tt), lambda i, zb: (0, i), memory_space=pltpu.SMEM),
                  pl.BlockSpec((tt * PACK_SUB, LANES), lambda i, zb: (i, 0))],
        out_specs=pl.BlockSpec(memory_space=pl.ANY),
        scratch_shapes=[pltpu.VMEM((ROW_BLOCK * PACK_SUB, LANES), U32), pltpu.SemaphoreType.DMA],
    )
    return pl.pallas_call(
        functools.partial(_dispatch_kernel, tt=tt, n_exp=n_exp),
        grid_spec=grid_spec,
        out_shape=jax.ShapeDtypeStruct((n_rows * PACK_SUB, LANES), U32),
        compiler_params=_cparams(("arbitrary",)),
        name="moe_dispatch",
    )(zero_blk, dest, xpk)


def _expert_kernel(sb_ref, nb_ref, nu_ref, wg_ref, wu_ref, wd_ref, xs_ref, y_ref,
                   xbuf, ybuf, wg_sc, wu_sc, wd_sc, isem, osem):
    e = pl.program_id(0)
    nb = nb_ref[e]
    base = sb_ref[e]
    n_used = nu_ref[0]
    out_rows = ROW_BLOCK * SUBLANES
    in_rows = ROW_BLOCK * PACK_SUB

    def fetch(blk, slot):
        src = xs_ref.at[pl.ds(pl.multiple_of(blk * in_rows, in_rows), in_rows), :]
        pltpu.make_async_copy(src, xbuf.at[slot], isem.at[slot]).start()

    def wait_in(slot):
        pltpu.make_async_copy(xs_ref.at[pl.ds(0, in_rows), :], xbuf.at[slot], isem.at[slot]).wait()

    def store(blk, slot):
        dst = y_ref.at[pl.ds(pl.multiple_of(blk * out_rows, out_rows), out_rows), :]
        pltpu.make_async_copy(ybuf.at[slot], dst, osem.at[slot]).start()

    def wait_out(slot):
        pltpu.make_async_copy(ybuf.at[slot], y_ref.at[pl.ds(0, out_rows), :], osem.at[slot]).wait()

    @pl.when(nb > 0)
    def _():
        @pl.when(base == 0)
        def _():
            for g0 in range(EXPERT_AHEAD):
                @pl.when(g0 < n_used)
                def _():
                    fetch(g0, g0 % EXPERT_BUFS)

        wg_sc[...] = wg_ref[0].astype(BF16)
        wu_sc[...] = wu_ref[0].astype(BF16)
        wd_sc[...] = wd_ref[0].astype(BF16)

        def step(g, slot):
            wait_in(slot)

            @pl.when(g + EXPERT_AHEAD < n_used)
            def _():
                fetch(g + EXPERT_AHEAD, (slot + EXPERT_AHEAD) % EXPERT_BUFS)

            @pl.when(g >= EXPERT_BUFS)
            def _():
                wait_out(slot)

            xb = _unpack_rows(xbuf.at[slot], ROW_BLOCK)
            a = jnp.dot(xb, wg_sc[...], preferred_element_type=F32)
            u = jnp.dot(xb, wu_sc[...], preferred_element_type=F32)
            h = (_silu(a) * u).astype(BF16)
            _store_rows(ybuf.at[slot], jnp.dot(h, wd_sc[...], preferred_element_type=F32))
            store(g, slot)

        first_group = base // EXPERT_BUFS

        def group(p, c):
            for slot in range(EXPERT_BUFS):
                g = (first_group + p) * EXPERT_BUFS + slot

                @pl.when((g >= base) & (g < base + nb))
                def _():
                    step(g, slot)
            return c

        lax.fori_loop(0, (base + nb + EXPERT_BUFS - 1) // EXPERT_BUFS - first_group, group, 0)

    @pl.when(e == pl.num_programs(0) - 1)
    def _():
        for slot in range(EXPERT_BUFS):
            @pl.when(slot < n_used)
            def _():
                wait_out(slot)


def _experts(start_blk, n_blk_e, n_used, xs, w_gate, w_up, w_down):
    n_rows = xs.shape[0] // PACK_SUB
    E, D, H = w_gate.shape
    wsel = lambda e, sb, nb, nu: (e, 0, 0)
    grid_spec = pltpu.PrefetchScalarGridSpec(
        num_scalar_prefetch=3,
        grid=(E,),
        in_specs=[pl.BlockSpec((1, D, H), wsel), pl.BlockSpec((1, D, H), wsel), pl.BlockSpec((1, H, D), wsel),
                  pl.BlockSpec(memory_space=pl.ANY)],
        out_specs=pl.BlockSpec(memory_space=pl.ANY),
        scratch_shapes=[pltpu.VMEM((EXPERT_BUFS, ROW_BLOCK * PACK_SUB, LANES), U32),
                        pltpu.VMEM((EXPERT_BUFS, ROW_BLOCK * SUBLANES, LANES), F32),
                        pltpu.VMEM((D, H), BF16), pltpu.VMEM((D, H), BF16), pltpu.VMEM((H, D), BF16),
                        pltpu.SemaphoreType.DMA((EXPERT_BUFS,)), pltpu.SemaphoreType.DMA((EXPERT_BUFS,))],
    )
    return pl.pallas_call(
        _expert_kernel,
        grid_spec=grid_spec,
        out_shape=jax.ShapeDtypeStruct((n_rows * SUBLANES, LANES), F32),
        compiler_params=_cparams(("arbitrary",)),
        name="moe_experts",
    )(start_blk, n_blk_e, n_used, w_gate, w_up, w_down, xs)


def _combine_kernel(dcur_ref, dnxt_ref, x_ref, gate_ref, wsg_ref, wsu_ref, wsd_ref, g_ref, b_ref, y_ref,
                    o_ref, buf, sem, *, tt, alpha):
    i = pl.program_id(0)
    n = pl.num_programs(0)

    def fetch(d_ref, slot):
        def body(j, c):
            for kk in range(TOP_K):
                src = _tile_rows(y_ref, d_ref[kk, j])
                pltpu.make_async_copy(src, _tile_rows(buf.at[slot, kk], j), sem.at[slot]).start(priority=kk % 2)
            return c

        lax.fori_loop(0, tt, body, 0)

    def reduce(slot):
        for kk in range(TOP_K):
            pltpu.make_async_copy(y_ref.at[pl.ds(0, tt * SUBLANES), :], buf.at[slot, kk], sem.at[slot]).wait()
        x = _load_rows(x_ref, tt)
        xb = x.astype(BF16)
        hs = _silu(jnp.dot(xb, wsg_ref[...], preferred_element_type=F32)) * jnp.dot(
            xb, wsu_ref[...], preferred_element_type=F32)
        moe = jnp.dot(hs.astype(BF16), wsd_ref[...], preferred_element_type=F32)
        gates = gate_ref[...]
        for kk in range(TOP_K):
            moe = moe + gates[:, kk:kk + 1] * _load_rows(buf.at[slot, kk], tt)
        o_ref[...] = _layer_norm(alpha * x + moe, g_ref[...], b_ref[...])

    @pl.when(i == 0)
    def _():
        fetch(dcur_ref, 0)

    for slot in range(2):
        @pl.when((i % 2 == slot) & (i + 1 < n))
        def _():
            fetch(dnxt_ref, 1 - slot)

    for slot in range(2):
        @pl.when(i % 2 == slot)
        def _():
            reduce(slot)


def _combine(dest, x1t, gates, wsg, wsu, wsd, g, b, y, alpha, tt=128):
    T = x1t.shape[0] // SUBLANES
    D = SUBLANES * LANES
    n = T // tt
    row = lambda i: (i, 0)
    fixed = lambda i: (0, 0)
    return pl.pallas_call(
        functools.partial(_combine_kernel, tt=tt, alpha=alpha),
        grid=(n,),
        in_specs=[pl.BlockSpec((TOP_K, tt), lambda i: (0, i), memory_space=pltpu.SMEM),
                  pl.BlockSpec((TOP_K, tt), lambda i: (0, jnp.minimum(i + 1, n - 1)), memory_space=pltpu.SMEM),
                  pl.BlockSpec((tt * SUBLANES, LANES), row), pl.BlockSpec((tt, TOP_K), row),
                  pl.BlockSpec(wsg.shape, fixed), pl.BlockSpec(wsu.shape, fixed), pl.BlockSpec(wsd.shape, fixed),
                  pl.BlockSpec((1, D), fixed), pl.BlockSpec((1, D), fixed),
                  pl.BlockSpec(memory_space=pl.ANY)],
        out_specs=pl.BlockSpec((tt, D), row),
        out_shape=jax.ShapeDtypeStruct((T, D), F32),
        scratch_shapes=[pltpu.VMEM((2, TOP_K, tt * SUBLANES, LANES), F32), pltpu.SemaphoreType.DMA((2,))],
        compiler_params=_cparams(("arbitrary",)),
        name="moe_combine_ln2",
    )(dest, dest, x1t, gates, wsg, wsu, wsd, g, b, y)


def _layer(x2, mem, B, S, depth, w_in, b_f, w_pool, pool_scale, w_mem_kv, w_out, ln1_g, ln1_b,
           w_router, router_bias, w_gate, w_up, w_down, ws_gate, ws_up, ws_down, ln2_g, ln2_b):
    T, D = x2.shape
    n_win, grp = w_pool.shape[0], w_pool.shape[1]
    pw = n_win * grp
    fw = FOX_HEADS * HEAD_DIM
    mw = w_mem_kv.shape[1] // 2
    E = w_router.shape[1]
    alpha = (2 * depth) ** 0.25

    f_lo = pw + 3 * fw
    w_main = jnp.concatenate([w_in[:, :f_lo], w_in[:, f_lo + FOX_HEADS:]], axis=1).astype(BF16)
    w_f = jnp.pad(w_in[:, f_lo:f_lo + FOX_HEADS], ((0, 0), (0, LANES - FOX_HEADS))).astype(BF16)
    bf_pad = jnp.pad(b_f, (0, LANES - FOX_HEADS)).reshape(1, LANES)
    wbd = jnp.zeros((pw, pw), F32)
    for g in range(n_win):
        wbd = wbd.at[g * grp:(g + 1) * grp, g * grp:(g + 1) * grp].set(w_pool[g])
    wrt = w_router.T
    wrt_hi = wrt.astype(BF16)
    wrt_lo = (wrt - wrt_hi.astype(F32)).astype(BF16)

    u_pool, q, k, v, q_mem, f_logit = _inproj(x2, w_main, w_f, pw, fw, mw)
    y_pool = _pool(u_pool, wbd.astype(BF16), pool_scale.reshape(1, pw), B, S)
    fcum = _fgate(f_logit, bf_pad, B, S)
    y_fox = _fox(q, k, v, fcum, B, S)
    y_mem = _memattn(q_mem, mem, w_mem_kv.astype(BF16), B, S)
    x1, x1_packed = _outproj(y_pool, y_fox, y_mem, x2, w_out.astype(BF16), ln1_g.reshape(1, D),
                             ln1_b.reshape(1, D), alpha)

    eidx, gates, rank, counts = _router(x1, wrt_hi, wrt_lo, _per_expert_column(router_bias))

    cnt = counts[:, 0].astype(I32)
    padded = (cnt + ROW_BLOCK - 1) // ROW_BLOCK * ROW_BLOCK
    pad_end = jnp.cumsum(padded)
    start_pad = (pad_end - padded).astype(I32)
    n_rows = (T * TOP_K + E * (ROW_BLOCK - 1)) // ROW_BLOCK * ROW_BLOCK
    n_used = (pad_end[-1] // ROW_BLOCK).astype(I32)
    zero_blk = jnp.where(padded > 0, pad_end - ROW_BLOCK, -1).astype(I32)
    dest = _dest(eidx, rank, _per_expert_column(start_pad))

    xs = _dispatch(zero_blk, dest, x1_packed, n_rows)
    y = _experts(start_pad // ROW_BLOCK, (padded // ROW_BLOCK).astype(I32), n_used.reshape(1), xs,
                 w_gate, w_up, w_down)
    return _combine(dest, x1, gates.T, ws_gate.astype(BF16), ws_up.astype(BF16),
                    ws_down.astype(BF16), ln2_g.reshape(1, D), ln2_b.reshape(1, D), y, alpha)


def kernel(x, mem, w_in, b_f, w_pool, pool_scale, w_mem_kv, w_out, ln1_g, ln1_b, w_router, router_bias,
           w_gate, w_up, w_down, ws_gate, ws_up, ws_down, ln2_g, ln2_b):
    B, S, D = x.shape
    depth = w_in.shape[0]
    x2 = x.reshape(B * S, D)
    for l in range(depth):
        x2 = _layer(x2, mem, B, S, depth, w_in[l], b_f[l], w_pool[l], pool_scale[l], w_mem_kv[l], w_out[l],
                    ln1_g[l], ln1_b[l], w_router[l], router_bias[l], w_gate[l], w_up[l], w_down[l],
                    ws_gate[l], ws_up[l], ws_down[l], ln2_g[l], ln2_b[l])
    return x2.reshape(B, S, D)
```

```python
import functools

import jax
import jax.numpy as jnp
from jax import lax
from jax.experimental import pallas as pl
from jax.experimental.pallas import tpu as pltpu

F32 = jnp.float32
BF16 = jnp.bfloat16
I32 = jnp.int32
U32 = jnp.uint32

LANES = 128
SUBLANES = 8
POOL_WINDOWS = (2, 4, 8, 16)
POOL_HALO = 16
HEAD_DIM = 64
FOX_HEADS = 8
BIAS_TERMS = 3
MEM_HEADS = 4
TOP_K = 8
N_GROUPS = 8
TOPK_GROUPS = 4
ROUTED_SCALE = 2.5
LN_EPS = 1e-5
ROW_BLOCK = 256
EXPERT_BUFS = 4
EXPERT_AHEAD = EXPERT_BUFS - 1
VMEM_LIMIT = 48 * 1024 * 1024


def _cparams(sem):
    return pltpu.CompilerParams(dimension_semantics=sem, vmem_limit_bytes=VMEM_LIMIT)


def _layer_norm(z, g, b):
    mu = jnp.mean(z, axis=-1, keepdims=True)
    zc = z - mu
    var = jnp.mean(zc * zc, axis=-1, keepdims=True)
    return zc * lax.rsqrt(var + LN_EPS) * g + b


def _silu(x):
    return x * (1.0 / (1.0 + jnp.exp(-x)))


def _load_rows(ref, n):
    return jnp.concatenate([ref[pl.ds(j, n, stride=SUBLANES), :] for j in range(SUBLANES)], axis=1)


def _store_rows(ref, val):
    n = val.shape[0]
    for j in range(SUBLANES):
        ref[pl.ds(j, n, stride=SUBLANES), :] = val[:, j * LANES:(j + 1) * LANES]


def _tile_rows(ref, r):
    return ref.at[pl.ds(pl.multiple_of(r * SUBLANES, SUBLANES), SUBLANES), :]


def _inproj_kernel(x_ref, w_ref, wf_ref, up_ref, q_ref, k_ref, v_ref, qm_ref, f_ref, *, pw, fw, mw, scale):
    xb = x_ref[...].astype(BF16)

    def proj(lo, hi):
        return jnp.dot(xb, w_ref[:, lo:hi], preferred_element_type=F32)

    up_ref[...] = proj(0, pw)
    q_ref[...] = (proj(pw, pw + fw) * scale).astype(BF16)
    k_ref[...] = proj(pw + fw, pw + 2 * fw).astype(BF16)
    v_ref[...] = proj(pw + 2 * fw, pw + 3 * fw).astype(BF16)
    qm_ref[...] = (proj(pw + 3 * fw, pw + 3 * fw + mw) * scale).astype(BF16)
    f_ref[...] = jnp.dot(xb, wf_ref[...], preferred_element_type=F32)


def _inproj(x2, w_main, w_f, pw, fw, mw, tm=512):
    T, D = x2.shape
    kern = functools.partial(_inproj_kernel, pw=pw, fw=fw, mw=mw, scale=HEAD_DIM ** -0.5)
    row = lambda i: (i, 0)
    fixed = lambda i: (0, 0)
    return pl.pallas_call(
        kern,
        grid=(T // tm,),
        in_specs=[pl.BlockSpec((tm, D), row),
                  pl.BlockSpec(w_main.shape, fixed),
                  pl.BlockSpec(w_f.shape, fixed)],
        out_specs=[pl.BlockSpec((tm, pw), row), pl.BlockSpec((tm, fw), row), pl.BlockSpec((tm, fw), row),
                   pl.BlockSpec((tm, fw), row), pl.BlockSpec((tm, mw), row), pl.BlockSpec((tm, LANES), row)],
        out_shape=[jax.ShapeDtypeStruct((T, pw), F32), jax.ShapeDtypeStruct((T, fw), BF16),
                   jax.ShapeDtypeStruct((T, fw), BF16), jax.ShapeDtypeStruct((T, fw), BF16),
                   jax.ShapeDtypeStruct((T, mw), BF16), jax.ShapeDtypeStruct((T, LANES), F32)],
        compiler_params=_cparams(("parallel",)),
        name="inproj",
    )(x2, w_main, w_f)


def _pool_kernel(u_ref, wbd_ref, sc_ref, o_ref, ext_sc, *, chunk, group):
    S, W = u_ref.shape
    ext_sc[0:POOL_HALO, :] = jnp.zeros((POOL_HALO, W), F32)
    ext_sc[POOL_HALO:, :] = u_ref[...]
    rows = chunk + POOL_HALO
    lrow = lax.broadcasted_iota(I32, (rows, W), 0)
    lane = lax.broadcasted_iota(I32, (rows, W), 1)

    def body(c, carry):
        start = pl.multiple_of(c * chunk, chunk)
        e = ext_sc[pl.ds(start, rows), :]
        posf = (lrow + (start - POOL_HALO + 1)).astype(F32)
        acc = e
        d = jnp.zeros_like(e)
        shift = 1
        for g, w in enumerate(POOL_WINDOWS):
            while shift < w:
                acc = acc + pltpu.roll(acc, shift, axis=0)
                shift *= 2
            mean = acc / jnp.minimum(posf, float(w))
            d = jnp.where((lane >= g * group) & (lane < (g + 1) * group), mean, d)
        d = (d - e)[POOL_HALO:, :]
        y = jnp.dot(d.astype(BF16), wbd_ref[...], preferred_element_type=F32) * sc_ref[...]
        o_ref[pl.ds(start, chunk), :] = y.astype(BF16)
        return carry

    lax.fori_loop(0, S // chunk, body, 0)


def _pool(u, wbd, pscale, B, S, chunk=512):
    T, W = u.shape
    kern = functools.partial(_pool_kernel, chunk=chunk, group=W // len(POOL_WINDOWS))
    return pl.pallas_call(
        kern,
        grid=(B,),
        in_specs=[pl.BlockSpec((S, W), lambda b: (b, 0)),
                  pl.BlockSpec((W, W), lambda b: (0, 0)),
                  pl.BlockSpec((1, W), lambda b: (0, 0))],
        out_specs=pl.BlockSpec((S, W), lambda b: (b, 0)),
        out_shape=jax.ShapeDtypeStruct((T, W), BF16),
        scratch_shapes=[pltpu.VMEM((S + POOL_HALO, W), F32)],
        compiler_params=_cparams(("parallel",)),
        name="pool_mixer",
    )(u, wbd, pscale)


def _fgate_kernel(f_ref, bf_ref, o_ref):
    S = f_ref.shape[0]
    z = f_ref[...] + bf_ref[...]
    x = jnp.minimum(z, 0.0) - jnp.log(1.0 + jnp.exp(-jnp.abs(z)))
    row = lax.broadcasted_iota(I32, x.shape, 0)
    lane = lax.broadcasted_iota(I32, x.shape, 1)
    shift = 1
    while shift < S:
        x = x + jnp.where(row >= shift, pltpu.roll(x, shift, axis=0), 0.0)
        shift *= 2
    hi = x.astype(BF16).astype(F32)
    mid = (x - hi).astype(BF16).astype(F32)
    lo = (x - hi) - mid
    for p in range(FOX_HEADS // 2):
        out = jnp.zeros(x.shape, F32)
        for hh in range(2):
            c = 2 * p + hh
            for j, part in enumerate((hi, mid, lo)):
                out = jnp.where(lane == BIAS_TERMS * hh + j, -part[:, c:c + 1], out)
        o_ref[0, p] = out.astype(BF16)


def _fgate(f_logit, bf_pad, B, S):
    return pl.pallas_call(
        _fgate_kernel,
        grid=(B,),
        in_specs=[pl.BlockSpec((S, LANES), lambda b: (b, 0)),
                  pl.BlockSpec((1, LANES), lambda b: (0, 0))],
        out_specs=pl.BlockSpec((1, FOX_HEADS // 2, S, LANES), lambda b: (b, 0, 0, 0)),
        out_shape=jax.ShapeDtypeStruct((B, FOX_HEADS // 2, S, LANES), BF16),
        compiler_params=_cparams(("parallel",)),
        name="forget_cumsum",
    )(f_logit, bf_pad)


def _fox_kernel(q_ref, k_ref, a_ref, v_ref, o_ref, vt_sc, qa_sc, sa_sc, sb_sc, m_sc, l_sc, acc_sc, *, blk):
    qi = pl.program_id(2)
    nk = vt_sc.shape[0]
    nt = (((1,), (1,)), ((), ()))

    @pl.when(qi == 0)
    def _():
        for c in range(nk):
            vt_sc[c] = v_ref[c * blk:(c + 1) * blk, :].astype(F32).T.astype(BF16)

    q = q_ref[...].astype(F32)
    lane = lax.broadcasted_iota(I32, q.shape, 1)
    for h in range(2):
        mine = (lane >= h * HEAD_DIM) & (lane < (h + 1) * HEAD_DIM)
        bias_rows = (lane >= h * BIAS_TERMS) & (lane < (h + 1) * BIAS_TERMS)
        qa_sc[h] = jnp.concatenate([jnp.where(mine, q, 0.0), jnp.where(bias_rows, 1.0, 0.0)], axis=1).astype(BF16)
    m_sc[...] = jnp.full(m_sc.shape, -jnp.inf, F32)
    l_sc[...] = jnp.zeros(l_sc.shape, F32)
    acc_sc[...] = jnp.zeros(acc_sc.shape, F32)

    def scores(ki, dst):
        ks = pl.multiple_of(ki * blk, blk)
        kb = jnp.concatenate([k_ref[pl.ds(ks, blk), :], a_ref[0, 0, pl.ds(ks, blk), :]], axis=1)
        for h in range(2):
            dst[h] = lax.dot_general(kb, qa_sc[h], nt, preferred_element_type=F32)

    def absorb(ki, src, causal):
        for h in range(2):
            st = src[h]
            if causal:
                kpos = lax.broadcasted_iota(I32, st.shape, 0)
                qpos = lax.broadcasted_iota(I32, st.shape, 1)
                st = jnp.where(kpos <= qpos, st, -jnp.inf)
            m_prev = m_sc[h]
            m_new = jnp.maximum(m_prev, jnp.max(st, axis=0, keepdims=True))
            alpha = jnp.exp(m_prev - m_new)
            p = jnp.exp(st - m_new)
            l_sc[h] = alpha * l_sc[h] + jnp.sum(p, axis=0, keepdims=True)
            m_sc[h] = m_new
            vt = vt_sc[ki, h * HEAD_DIM:(h + 1) * HEAD_DIM, :]
            pv = jnp.dot(vt, p.astype(BF16), preferred_element_type=F32)
            rows = slice(h * HEAD_DIM, (h + 1) * HEAD_DIM)
            acc_sc[rows, :] = acc_sc[rows, :] * alpha + pv

    scores(0, sa_sc)

    def body(p, carry):
        scores(2 * p + 1, sb_sc)
        absorb(2 * p, sa_sc, False)
        scores(2 * p + 2, sa_sc)
        absorb(2 * p + 1, sb_sc, False)
        return carry

    lax.fori_loop(0, qi // 2, body, 0)

    @pl.when(qi % 2 == 1)
    def _():
        scores(qi, sb_sc)
        absorb(qi - 1, sa_sc, False)
        absorb(qi, sb_sc, True)

    @pl.when(qi % 2 == 0)
    def _():
        absorb(qi, sa_sc, True)

    out_t = jnp.concatenate([acc_sc[h * HEAD_DIM:(h + 1) * HEAD_DIM, :] / l_sc[h] for h in range(2)], axis=0)
    o_ref[...] = out_t.T.astype(BF16)


def _fox(q, k, v, fbias, B, S, blk=512):
    T, FW = q.shape
    nq = S // blk
    pairs = FW // LANES
    kern = functools.partial(_fox_kernel, blk=blk)
    return pl.pallas_call(
        kern,
        grid=(B, pairs, nq),
        in_specs=[pl.BlockSpec((blk, LANES), lambda b, p, i: (b * nq + i, p)),
                  pl.BlockSpec((S, LANES), lambda b, p, i: (b, p)),
                  pl.BlockSpec((1, 1, S, LANES), lambda b, p, i: (b, p, 0, 0)),
                  pl.BlockSpec((S, LANES), lambda b, p, i: (b, p))],
        out_specs=pl.BlockSpec((blk, LANES), lambda b, p, i: (b * nq + i, p)),
        out_shape=jax.ShapeDtypeStruct((T, FW), BF16),
        scratch_shapes=[pltpu.VMEM((nq, LANES, blk), BF16), pltpu.VMEM((2, blk, 2 * LANES), BF16),
                        pltpu.VMEM((2, blk, blk), F32), pltpu.VMEM((2, blk, blk), F32),
                        pltpu.VMEM((2, 1, blk), F32), pltpu.VMEM((2, 1, blk), F32), pltpu.VMEM((LANES, blk), F32)],
        compiler_params=_cparams(("parallel", "parallel", "arbitrary")),
        name="fox_attention",
    )(q, k, fbias, v)


def _memattn_kernel(qm_ref, mem_ref, wkv_ref, o_ref, k_sc, v_sc):
    MW = qm_ref.shape[1]

    @pl.when(pl.program_id(1) == 0)
    def _():
        kv = jnp.dot(mem_ref[0].astype(BF16), wkv_ref[...], preferred_element_type=F32)
        k_sc[...] = kv[:, :MW].astype(BF16)
        v_sc[...] = kv[:, MW:].astype(BF16)

    q = qm_ref[...]
    lane = lax.broadcasted_iota(I32, q.shape, 1)
    hd = MW // MEM_HEADS
    out = jnp.zeros(q.shape, F32)
    for h in range(MEM_HEADS):
        mine = (lane >= h * hd) & (lane < (h + 1) * hd)
        qh = jnp.where(mine, q, jnp.zeros_like(q))
        s = lax.dot_general(qh, k_sc[...], (((1,), (1,)), ((), ())), preferred_element_type=F32)
        p = jnp.exp(s - jnp.max(s, axis=1, keepdims=True))
        l = jnp.sum(p, axis=1, keepdims=True)
        o = jnp.dot(p.astype(BF16), v_sc[...], preferred_element_type=F32)
        out = jnp.where(mine, o / l, out)
    o_ref[...] = out.astype(BF16)


def _memattn(qm, mem, wkv, B, S, tq=512):
    T, MW = qm.shape
    M, D = mem.shape[1], mem.shape[2]
    nq = S // tq
    return pl.pallas_call(
        _memattn_kernel,
        grid=(B, nq),
        in_specs=[pl.BlockSpec((tq, MW), lambda b, i: (b * nq + i, 0)),
                  pl.BlockSpec((1, M, D), lambda b, i: (b, 0, 0)),
                  pl.BlockSpec(wkv.shape, lambda b, i: (0, 0))],
        out_specs=pl.BlockSpec((tq, MW), lambda b, i: (b * nq + i, 0)),
        out_shape=jax.ShapeDtypeStruct((T, MW), BF16),
        scratch_shapes=[pltpu.VMEM((M, MW), BF16), pltpu.VMEM((M, MW), BF16)],
        compiler_params=_cparams(("parallel", "arbitrary")),
        name="memory_attention",
    )(qm, mem, wkv)


PACK_SUB = 4


def _pack_rows(ref, val):
    n = val.shape[0]
    bits = pltpu.bitcast(val, U32)
    r = (bits + jnp.uint32(0x7FFF) + ((bits >> 16) & jnp.uint32(1))) >> 16
    half = val.shape[1] // 2
    words = (r[:, half:] << 16) | r[:, :half]
    for j in range(PACK_SUB):
        ref[pl.ds(j, n, stride=PACK_SUB), :] = words[:, j * LANES:(j + 1) * LANES]


def _unpack_rows(ref, n, dtype=BF16):
    slabs = [ref[pl.ds(j, n, stride=PACK_SUB), :] for j in range(PACK_SUB)]
    lo = [pltpu.bitcast(s << 16, F32) for s in slabs]
    hi = [pltpu.bitcast(s & jnp.uint32(0xFFFF0000), F32) for s in slabs]
    return jnp.concatenate(lo + hi, axis=1).astype(dtype)


def _packed_row(ref, r):
    return ref.at[pl.ds(pl.multiple_of(r * PACK_SUB, PACK_SUB), PACK_SUB), :]


def _outproj_kernel(yp_ref, yf_ref, ym_ref, x_ref, w_ref, g_ref, b_ref, o_ref, pk_ref, *, alpha):
    pw, fw = yp_ref.shape[1], yf_ref.shape[1]
    h = jnp.dot(yp_ref[...], w_ref[0:pw, :], preferred_element_type=F32)
    h = h + jnp.dot(yf_ref[...], w_ref[pw:pw + fw, :], preferred_element_type=F32)
    h = h + jnp.dot(ym_ref[...], w_ref[pw + fw:, :], preferred_element_type=F32)
    y = _layer_norm(alpha * x_ref[...] + h, g_ref[...], b_ref[...])
    _store_rows(o_ref, y)
    _pack_rows(pk_ref, y)


def _outproj(yp, yf, ym, x2, w_out, g, b, alpha, tm=512):
    T, D = x2.shape
    assert D == SUBLANES * LANES == 2 * PACK_SUB * LANES
    row = lambda i: (i, 0)
    fixed = lambda i: (0, 0)
    return pl.pallas_call(
        functools.partial(_outproj_kernel, alpha=alpha),
        grid=(T // tm,),
        in_specs=[pl.BlockSpec((tm, yp.shape[1]), row), pl.BlockSpec((tm, yf.shape[1]), row),
                  pl.BlockSpec((tm, ym.shape[1]), row), pl.BlockSpec((tm, D), row),
                  pl.BlockSpec(w_out.shape, fixed), pl.BlockSpec((1, D), fixed), pl.BlockSpec((1, D), fixed)],
        out_specs=[pl.BlockSpec((tm * SUBLANES, LANES), row),
                   pl.BlockSpec((tm * PACK_SUB, LANES), row)],
        out_shape=[jax.ShapeDtypeStruct((T * SUBLANES, LANES), F32),
                   jax.ShapeDtypeStruct((T * PACK_SUB, LANES), U32)],
        compiler_params=_cparams(("parallel",)),
        name="outproj_ln1",
    )(yp, yf, ym, x2, w_out, g, b)


ROUTE_TILE = 512


def _per_expert_column(v, dtype=F32):
    return jnp.broadcast_to(v.astype(dtype)[:, None], (v.shape[0], ROUTE_TILE))


def _router_kernel(x_ref, wh_ref, wl_ref, bias_ref, eidx_ref, gate_ref, rank_ref, cnt_ref, carry_sc, *, tt):
    E = wh_ref.shape[0]
    gsz = E // N_GROUPS
    ninf = -jnp.inf

    @pl.when(pl.program_id(0) == 0)
    def _():
        carry_sc[...] = jnp.zeros(carry_sc.shape, F32)

    x = _load_rows(x_ref, tt)
    xh = x.astype(BF16)
    xl = (x - xh.astype(F32)).astype(BF16)
    wh = wh_ref[...]
    nt = (((1,), (1,)), ((), ()))
    logits = lax.dot_general(wh, xh, nt, preferred_element_type=F32) + (
        lax.dot_general(wl_ref[...], xh, nt, preferred_element_type=F32)
        + lax.dot_general(wh, xl, nt, preferred_element_type=F32))
    scores = 1.0 / (1.0 + jnp.exp(-logits))
    biased = scores + bias_ref[...]
    eiota = lax.broadcasted_iota(I32, (E, tt), 0).astype(F32)
    giota = lax.broadcasted_iota(I32, (gsz, tt), 0).astype(F32)

    def cmax(a):
        return jnp.max(a, axis=0, keepdims=True)

    def first_at(a, m, iota, n):
        return jnp.min(jnp.where(a == m, iota, float(n)), axis=0, keepdims=True)

    groups = [biased[g * gsz:(g + 1) * gsz, :] for g in range(N_GROUPS)]
    gscore = []
    for blk in groups:
        m1 = cmax(blk)
        m2 = cmax(jnp.where(giota == first_at(blk, m1, giota, gsz), ninf, blk))
        gscore.append(m1 + m2)

    kept = []
    for g in range(N_GROUPS):
        ahead = jnp.zeros((1, tt), F32)
        for o in range(N_GROUPS):
            if o == g:
                continue
            beats = gscore[o] > gscore[g]
            if o < g:
                beats = beats | (gscore[o] == gscore[g])
            ahead = ahead + jnp.where(beats, 1.0, 0.0)
        kept.append(groups[g] + jnp.where(ahead < float(TOPK_GROUPS), 0.0, ninf))
    masked = jnp.concatenate(kept, axis=0)

    picks, sels = [], []
    chosen = jnp.zeros((E, tt), F32)
    for _ in range(TOP_K):
        ik = first_at(masked, cmax(masked), eiota, E)
        oh = eiota == ik
        picks.append(ik)
        sels.append(jnp.sum(jnp.where(oh, scores, 0.0), axis=0, keepdims=True))
        chosen = jnp.where(oh, 1.0, chosen)
        masked = jnp.where(oh, ninf, masked)
    denom = sels[0]
    for sk in sels[1:]:
        denom = denom + sk

    r = lax.broadcasted_iota(I32, (tt, tt), 0)
    c = lax.broadcasted_iota(I32, (tt, tt), 1)
    earlier = jnp.where(r < c, 1.0, 0.0).astype(BF16)
    chosen_b = chosen.astype(BF16)
    pos = carry_sc[...] + jnp.dot(chosen_b, earlier, preferred_element_type=F32)
    carry_sc[...] = carry_sc[...] + jnp.dot(chosen_b, jnp.ones((tt, tt), BF16), preferred_element_type=F32)
    cnt_ref[...] = carry_sc[...]

    ranks = [jnp.sum(jnp.where(eiota == ik, pos, 0.0), axis=0, keepdims=True) for ik in picks]
    eidx_ref[...] = jnp.concatenate(picks, axis=0).astype(I32)
    gate_ref[...] = jnp.concatenate([sk / denom * ROUTED_SCALE for sk in sels], axis=0)
    rank_ref[...] = jnp.concatenate(ranks, axis=0).astype(I32)


def _router(x1t, wrt_hi, wrt_lo, rbias):
    T = x1t.shape[0] // SUBLANES
    E, D = wrt_hi.shape
    tt = ROUTE_TILE
    tok = lambda i: (0, i)
    fixed = lambda i: (0, 0)
    return pl.pallas_call(
        functools.partial(_router_kernel, tt=tt),
        grid=(T // tt,),
        in_specs=[pl.BlockSpec((tt * SUBLANES, LANES), lambda i: (i, 0)), pl.BlockSpec((E, D), fixed),
                  pl.BlockSpec((E, D), fixed), pl.BlockSpec((E, tt), fixed)],
        out_specs=[pl.BlockSpec((TOP_K, tt), tok), pl.BlockSpec((TOP_K, tt), tok), pl.BlockSpec((TOP_K, tt), tok),
                   pl.BlockSpec((E, tt), fixed)],
        out_shape=[jax.ShapeDtypeStruct((TOP_K, T), I32), jax.ShapeDtypeStruct((TOP_K, T), F32),
                   jax.ShapeDtypeStruct((TOP_K, T), I32), jax.ShapeDtypeStruct((E, tt), F32)],
        scratch_shapes=[pltpu.VMEM((E, tt), F32)],
        compiler_params=_cparams(("arbitrary",)),
        name="router_topk",
    )(x1t, wrt_hi, wrt_lo, rbias)


def _dest_kernel(e_ref, r_ref, sp_ref, d_ref):
    E = sp_ref.shape[0]
    tt = e_ref.shape[1]
    eiota = lax.broadcasted_iota(I32, (E, tt), 0)
    sp = sp_ref[...]
    rows = []
    for kk in range(TOP_K):
        start = jnp.sum(jnp.where(eiota == e_ref[kk:kk + 1, :], sp, 0.0), axis=0, keepdims=True)
        rows.append(start.astype(I32) + r_ref[kk:kk + 1, :])
    d_ref[...] = jnp.concatenate(rows, axis=0)


def _dest(eidx, rank, start_pad_b):
    K, T = eidx.shape
    tt = ROUTE_TILE
    tok = lambda i: (0, i)
    return pl.pallas_call(
        _dest_kernel,
        grid=(T // tt,),
        in_specs=[pl.BlockSpec((K, tt), tok), pl.BlockSpec((K, tt), tok),
                  pl.BlockSpec(start_pad_b.shape, lambda i: (0, 0))],
        out_specs=pl.BlockSpec((K, tt), tok),
        out_shape=jax.ShapeDtypeStruct((K, T), I32),
        compiler_params=_cparams(("parallel",)),
        name="moe_dest",
    )(eidx, rank, start_pad_b)


def _dispatch_kernel(zb_ref, d_ref, x_ref, xs_ref, zero_sc, sem, *, tt, n_exp):
    blk_rows = ROW_BLOCK * PACK_SUB

    @pl.when(pl.program_id(0) == 0)
    def _():
        zero_sc[...] = jnp.zeros(zero_sc.shape, U32)

        def zstart(e, c):
            @pl.when(zb_ref[e] >= 0)
            def _():
                dst = xs_ref.at[pl.ds(pl.multiple_of(zb_ref[e] * PACK_SUB, blk_rows), blk_rows), :]
                pltpu.make_async_copy(zero_sc, dst, sem).start()
            return c

        def zwait(e, c):
            @pl.when(zb_ref[e] >= 0)
            def _():
                pltpu.make_async_copy(zero_sc, xs_ref.at[pl.ds(0, blk_rows), :], sem).wait()
            return c

        lax.fori_loop(0, n_exp, zstart, 0)
        lax.fori_loop(0, n_exp, zwait, 0)

    def start(j, c):
        src = _packed_row(x_ref, j)
        for kk in range(TOP_K):
            pltpu.make_async_copy(src, _packed_row(xs_ref, d_ref[kk, j]), sem).start(priority=kk % 2)
        return c

    lax.fori_loop(0, tt, start, 0)
    for kk in range(TOP_K):
        pltpu.make_async_copy(x_ref, xs_ref.at[pl.ds(0, tt * PACK_SUB), :], sem).wait()


def _dispatch(zero_blk, dest, xpk, n_rows, tt=256):
    T = xpk.shape[0] // PACK_SUB
    n_exp = zero_blk.shape[0]
    grid_spec = pltpu.PrefetchScalarGridSpec(
        num_scalar_prefetch=1,
        grid=(T // tt,),
        in_specs=[pl.BlockSpec((TOP_K, tt), lambda i, zb: (0, i), memory_space=pltpu.SMEM),
                  pl.BlockSpec((tt * PACK_SUB, LANES), lambda i, zb: (i, 0))],
        out_specs=pl.BlockSpec(memory_space=pl.ANY),
        scratch_shapes=[pltpu.VMEM((ROW_BLOCK * PACK_SUB, LANES), U32), pltpu.SemaphoreType.DMA],
    )
    return pl.pallas_call(
        functools.partial(_dispatch_kernel, tt=tt, n_exp=n_exp),
        grid_spec=grid_spec,
        out_shape=jax.ShapeDtypeStruct((n_rows * PACK_SUB, LANES), U32),
        compiler_params=_cparams(("arbitrary",)),
        name="moe_dispatch",
    )(zero_blk, dest, xpk)


def _expert_kernel(sb_ref, nb_ref, nu_ref, wg_ref, wu_ref, wd_ref, xs_ref, y_ref,
                   xbuf, ybuf, wg_sc, wu_sc, wd_sc, isem, osem):
    e = pl.program_id(0)
    nb = nb_ref[e]
    base = sb_ref[e]
    n_used = nu_ref[0]
    in_rows = out_rows = ROW_BLOCK * PACK_SUB

    def fetch(blk, slot):
        src = xs_ref.at[pl.ds(pl.multiple_of(blk * in_rows, in_rows), in_rows), :]
        pltpu.make_async_copy(src, xbuf.at[slot], isem.at[slot]).start()

    def wait_in(slot):
        pltpu.make_async_copy(xs_ref.at[pl.ds(0, in_rows), :], xbuf.at[slot], isem.at[slot]).wait()

    def store(blk, slot):
        dst = y_ref.at[pl.ds(pl.multiple_of(blk * out_rows, out_rows), out_rows), :]
        pltpu.make_async_copy(ybuf.at[slot], dst, osem.at[slot]).start()

    def wait_out(slot):
        pltpu.make_async_copy(ybuf.at[slot], y_ref.at[pl.ds(0, out_rows), :], osem.at[slot]).wait()

    @pl.when(nb > 0)
    def _():
        @pl.when(base == 0)
        def _():
            for g0 in range(EXPERT_AHEAD):
                @pl.when(g0 < n_used)
                def _():
                    fetch(g0, g0 % EXPERT_BUFS)

        wg_sc[...] = wg_ref[0].astype(BF16)
        wu_sc[...] = wu_ref[0].astype(BF16)
        wd_sc[...] = wd_ref[0].astype(BF16)

        def step(g, slot):
            wait_in(slot)

            @pl.when(g + EXPERT_AHEAD < n_used)
            def _():
                fetch(g + EXPERT_AHEAD, (slot + EXPERT_AHEAD) % EXPERT_BUFS)

            @pl.when(g >= EXPERT_BUFS)
            def _():
                wait_out(slot)

            xb = _unpack_rows(xbuf.at[slot], ROW_BLOCK)
            a = jnp.dot(xb, wg_sc[...], preferred_element_type=F32)
            u = jnp.dot(xb, wu_sc[...], preferred_element_type=F32)
            h = (_silu(a) * u).astype(BF16)
            _pack_rows(ybuf.at[slot], jnp.dot(h, wd_sc[...], preferred_element_type=F32))
            store(g, slot)

        first_group = base // EXPERT_BUFS

        def group(p, c):
            for slot in range(EXPERT_BUFS):
                g = (first_group + p) * EXPERT_BUFS + slot

                @pl.when((g >= base) & (g < base + nb))
                def _():
                    step(g, slot)
            return c

        lax.fori_loop(0, (base + nb + EXPERT_BUFS - 1) // EXPERT_BUFS - first_group, group, 0)

    @pl.when(e == pl.num_programs(0) - 1)
    def _():
        for slot in range(EXPERT_BUFS):
            @pl.when(slot < n_used)
            def _():
                wait_out(slot)


def _experts(start_blk, n_blk_e, n_used, xs, w_gate, w_up, w_down):
    n_rows = xs.shape[0] // PACK_SUB
    E, D, H = w_gate.shape
    wsel = lambda e, sb, nb, nu: (e, 0, 0)
    grid_spec = pltpu.PrefetchScalarGridSpec(
        num_scalar_prefetch=3,
        grid=(E,),
        in_specs=[pl.BlockSpec((1, D, H), wsel), pl.BlockSpec((1, D, H), wsel), pl.BlockSpec((1, H, D), wsel),
                  pl.BlockSpec(memory_space=pl.ANY)],
        out_specs=pl.BlockSpec(memory_space=pl.ANY),
        scratch_shapes=[pltpu.VMEM((EXPERT_BUFS, ROW_BLOCK * PACK_SUB, LANES), U32),
                        pltpu.VMEM((EXPERT_BUFS, ROW_BLOCK * PACK_SUB, LANES), U32),
                        pltpu.VMEM((D, H), BF16), pltpu.VMEM((D, H), BF16), pltpu.VMEM((H, D), BF16),
                        pltpu.SemaphoreType.DMA((EXPERT_BUFS,)), pltpu.SemaphoreType.DMA((EXPERT_BUFS,))],
    )
    return pl.pallas_call(
        _expert_kernel,
        grid_spec=grid_spec,
        out_shape=jax.ShapeDtypeStruct((n_rows * PACK_SUB, LANES), U32),
        compiler_params=_cparams(("arbitrary",)),
        name="moe_experts",
    )(start_blk, n_blk_e, n_used, w_gate, w_up, w_down, xs)


def _combine_kernel(dcur_ref, dnxt_ref, x_ref, gate_ref, wsg_ref, wsu_ref, wsd_ref, g_ref, b_ref, y_ref,
                    o_ref, buf, sem, *, tt, alpha):
    i = pl.program_id(0)
    n = pl.num_programs(0)

    def fetch(d_ref, slot):
        def body(j, c):
            for kk in range(TOP_K):
                src = _packed_row(y_ref, d_ref[kk, j])
                pltpu.make_async_copy(src, _packed_row(buf.at[slot, kk], j), sem.at[slot]).start(priority=kk % 2)
            return c

        lax.fori_loop(0, tt, body, 0)

    def reduce(slot):
        for kk in range(TOP_K):
            pltpu.make_async_copy(y_ref.at[pl.ds(0, tt * PACK_SUB), :], buf.at[slot, kk], sem.at[slot]).wait()
        x = _load_rows(x_ref, tt)
        xb = x.astype(BF16)
        hs = _silu(jnp.dot(xb, wsg_ref[...], preferred_element_type=F32)) * jnp.dot(
            xb, wsu_ref[...], preferred_element_type=F32)
        moe = jnp.dot(hs.astype(BF16), wsd_ref[...], preferred_element_type=F32)
        gates = gate_ref[...]
        for kk in range(TOP_K):
            moe = moe + gates[:, kk:kk + 1] * _unpack_rows(buf.at[slot, kk], tt, F32)
        o_ref[...] = _layer_norm(alpha * x + moe, g_ref[...], b_ref[...])

    @pl.when(i == 0)
    def _():
        fetch(dcur_ref, 0)

    for slot in range(2):
        @pl.when((i % 2 == slot) & (i + 1 < n))
        def _():
            fetch(dnxt_ref, 1 - slot)

    for slot in range(2):
        @pl.when(i % 2 == slot)
        def _():
            reduce(slot)


def _combine(dest, x1t, gates, wsg, wsu, wsd, g, b, y, alpha, tt=128):
    T = x1t.shape[0] // SUBLANES
    D = SUBLANES * LANES
    n = T // tt
    row = lambda i: (i, 0)
    fixed = lambda i: (0, 0)
    return pl.pallas_call(
        functools.partial(_combine_kernel, tt=tt, alpha=alpha),
        grid=(n,),
        in_specs=[pl.BlockSpec((TOP_K, tt), lambda i: (0, i), memory_space=pltpu.SMEM),
                  pl.BlockSpec((TOP_K, tt), lambda i: (0, jnp.minimum(i + 1, n - 1)), memory_space=pltpu.SMEM),
                  pl.BlockSpec((tt * SUBLANES, LANES), row), pl.BlockSpec((tt, TOP_K), row),
                  pl.BlockSpec(wsg.shape, fixed), pl.BlockSpec(wsu.shape, fixed), pl.BlockSpec(wsd.shape, fixed),
                  pl.BlockSpec((1, D), fixed), pl.BlockSpec((1, D), fixed),
                  pl.BlockSpec(memory_space=pl.ANY)],
        out_specs=pl.BlockSpec((tt, D), row),
        out_shape=jax.ShapeDtypeStruct((T, D), F32),
        scratch_shapes=[pltpu.VMEM((2, TOP_K, tt * PACK_SUB, LANES), U32), pltpu.SemaphoreType.DMA((2,))],
        compiler_params=_cparams(("arbitrary",)),
        name="moe_combine_ln2",
    )(dest, dest, x1t, gates, wsg, wsu, wsd, g, b, y)


def _layer(x2, mem, B, S, depth, w_in, b_f, w_pool, pool_scale, w_mem_kv, w_out, ln1_g, ln1_b,
           w_router, router_bias, w_gate, w_up, w_down, ws_gate, ws_up, ws_down, ln2_g, ln2_b):
    T, D = x2.shape
    n_win, grp = w_pool.shape[0], w_pool.shape[1]
    pw = n_win * grp
    fw = FOX_HEADS * HEAD_DIM
    mw = w_mem_kv.shape[1] // 2
    E = w_router.shape[1]
    alpha = (2 * depth) ** 0.25

    f_lo = pw + 3 * fw
    w_main = jnp.concatenate([w_in[:, :f_lo], w_in[:, f_lo + FOX_HEADS:]], axis=1).astype(BF16)
    w_f = jnp.pad(w_in[:, f_lo:f_lo + FOX_HEADS], ((0, 0), (0, LANES - FOX_HEADS))).astype(BF16)
    bf_pad = jnp.pad(b_f, (0, LANES - FOX_HEADS)).reshape(1, LANES)
    wbd = jnp.zeros((pw, pw), F32)
    for g in range(n_win):
        wbd = wbd.at[g * grp:(g + 1) * grp, g * grp:(g + 1) * grp].set(w_pool[g])
    wrt = w_router.T
    wrt_hi = wrt.astype(BF16)
    wrt_lo = (wrt - wrt_hi.astype(F32)).astype(BF16)

    u_pool, q, k, v, q_mem, f_logit = _inproj(x2, w_main, w_f, pw, fw, mw)
    y_pool = _pool(u_pool, wbd.astype(BF16), pool_scale.reshape(1, pw), B, S)
    fcum = _fgate(f_logit, bf_pad, B, S)
    y_fox = _fox(q, k, v, fcum, B, S)
    y_mem = _memattn(q_mem, mem, w_mem_kv.astype(BF16), B, S)
    x1, x1_packed = _outproj(y_pool, y_fox, y_mem, x2, w_out.astype(BF16), ln1_g.reshape(1, D),
                             ln1_b.reshape(1, D), alpha)

    eidx, gates, rank, counts = _router(x1, wrt_hi, wrt_lo, _per_expert_column(router_bias))

    cnt = counts[:, 0].astype(I32)
    padded = (cnt + ROW_BLOCK - 1) // ROW_BLOCK * ROW_BLOCK
    pad_end = jnp.cumsum(padded)
    start_pad = (pad_end - padded).astype(I32)
    n_rows = (T * TOP_K + E * (ROW_BLOCK - 1)) // ROW_BLOCK * ROW_BLOCK
    n_used = (pad_end[-1] // ROW_BLOCK).astype(I32)
    zero_blk = jnp.where(padded > 0, pad_end - ROW_BLOCK, -1).astype(I32)
    dest = _dest(eidx, rank, _per_expert_column(start_pad))

    xs = _dispatch(zero_blk, dest, x1_packed, n_rows)
    y = _experts(start_pad // ROW_BLOCK, (padded // ROW_BLOCK).astype(I32), n_used.reshape(1), xs,
                 w_gate, w_up, w_down)
    return _combine(dest, x1, gates.T, ws_gate.astype(BF16), ws_up.astype(BF16),
                    ws_down.astype(BF16), ln2_g.reshape(1, D), ln2_b.reshape(1, D), y, alpha)


def kernel(x, mem, w_in, b_f, w_pool, pool_scale, w_mem_kv, w_out, ln1_g, ln1_b, w_router, router_bias,
           w_gate, w_up, w_down, ws_gate, ws_up, ws_down, ln2_g, ln2_b):
    B, S, D = x.shape
    depth = w_in.shape[0]
    x2 = x.reshape(B * S, D)
    for l in range(depth):
        x2 = _layer(x2, mem, B, S, depth, w_in[l], b_f[l], w_pool[l], pool_scale[l], w_mem_kv[l], w_out[l],
                    ln1_g[l], ln1_b[l], w_router[l], router_bias[l], w_gate[l], w_up[l], w_down[l],
                    ws_gate[l], ws_up[l], ws_down[l], ln2_g[l], ln2_b[l])
    return x2.reshape(B, S, D)
```

```python
import functools

import jax
import jax.numpy as jnp
from jax import lax
from jax.experimental import pallas as pl
from jax.experimental.pallas import tpu as pltpu

F32 = jnp.float32
BF16 = jnp.bfloat16
I32 = jnp.int32
PACKED = jnp.int32

LANES = 128
SUBLANES = 8
POOL_WINDOWS = (2, 4, 8, 16)
POOL_HALO = 16
HEAD_DIM = 64
FOX_HEADS = 8
BIAS_TERMS = 3
MEM_HEADS = 4
TOP_K = 8
N_GROUPS = 8
TOPK_GROUPS = 4
ROUTED_SCALE = 2.5
LN_EPS = 1e-5
ROW_BLOCK = 256
EXPERT_BUFS = 4
EXPERT_AHEAD = EXPERT_BUFS - 1
VMEM_LIMIT = 48 * 1024 * 1024


def _cparams(sem):
    return pltpu.CompilerParams(dimension_semantics=sem, vmem_limit_bytes=VMEM_LIMIT)


def _layer_norm(z, g, b):
    mu = jnp.mean(z, axis=-1, keepdims=True)
    zc = z - mu
    var = jnp.mean(zc * zc, axis=-1, keepdims=True)
    return zc * lax.rsqrt(var + LN_EPS) * g + b


def _silu(x):
    return x * (1.0 / (1.0 + jnp.exp(-x)))


def _load_rows(ref, n):
    return jnp.concatenate([ref[pl.ds(j, n, stride=SUBLANES), :] for j in range(SUBLANES)], axis=1)


def _store_rows(ref, val):
    n = val.shape[0]
    for j in range(SUBLANES):
        ref[pl.ds(j, n, stride=SUBLANES), :] = val[:, j * LANES:(j + 1) * LANES]


def _tile_rows(ref, r):
    return ref.at[pl.ds(pl.multiple_of(r * SUBLANES, SUBLANES), SUBLANES), :]


def _inproj_kernel(x_ref, w_ref, wf_ref, up_ref, q_ref, k_ref, v_ref, qm_ref, f_ref, *, pw, fw, mw, scale):
    xb = x_ref[...].astype(BF16)

    def proj(lo, hi):
        return jnp.dot(xb, w_ref[:, lo:hi], preferred_element_type=F32)

    up_ref[...] = proj(0, pw)
    q_ref[...] = (proj(pw, pw + fw) * scale).astype(BF16)
    k_ref[...] = proj(pw + fw, pw + 2 * fw).astype(BF16)
    v_ref[...] = proj(pw + 2 * fw, pw + 3 * fw).astype(BF16)
    qm_ref[...] = (proj(pw + 3 * fw, pw + 3 * fw + mw) * scale).astype(BF16)
    f_ref[...] = jnp.dot(xb, wf_ref[...], preferred_element_type=F32)


def _inproj(x2, w_main, w_f, pw, fw, mw, tm=512):
    T, D = x2.shape
    kern = functools.partial(_inproj_kernel, pw=pw, fw=fw, mw=mw, scale=HEAD_DIM ** -0.5)
    row = lambda i: (i, 0)
    fixed = lambda i: (0, 0)
    return pl.pallas_call(
        kern,
        grid=(T // tm,),
        in_specs=[pl.BlockSpec((tm, D), row),
                  pl.BlockSpec(w_main.shape, fixed),
                  pl.BlockSpec(w_f.shape, fixed)],
        out_specs=[pl.BlockSpec((tm, pw), row), pl.BlockSpec((tm, fw), row), pl.BlockSpec((tm, fw), row),
                   pl.BlockSpec((tm, fw), row), pl.BlockSpec((tm, mw), row), pl.BlockSpec((tm, LANES), row)],
        out_shape=[jax.ShapeDtypeStruct((T, pw), F32), jax.ShapeDtypeStruct((T, fw), BF16),
                   jax.ShapeDtypeStruct((T, fw), BF16), jax.ShapeDtypeStruct((T, fw), BF16),
                   jax.ShapeDtypeStruct((T, mw), BF16), jax.ShapeDtypeStruct((T, LANES), F32)],
        compiler_params=_cparams(("parallel",)),
        name="inproj",
    )(x2, w_main, w_f)


def _pool_kernel(u_ref, wbd_ref, sc_ref, o_ref, ext_sc, *, chunk, group):
    S, W = u_ref.shape
    ext_sc[0:POOL_HALO, :] = jnp.zeros((POOL_HALO, W), F32)
    ext_sc[POOL_HALO:, :] = u_ref[...]
    rows = chunk + POOL_HALO
    lrow = lax.broadcasted_iota(I32, (rows, W), 0)
    lane = lax.broadcasted_iota(I32, (rows, W), 1)

    def body(c, carry):
        start = pl.multiple_of(c * chunk, chunk)
        e = ext_sc[pl.ds(start, rows), :]
        posf = (lrow + (start - POOL_HALO + 1)).astype(F32)
        acc = e
        d = jnp.zeros_like(e)
        shift = 1
        for g, w in enumerate(POOL_WINDOWS):
            while shift < w:
                acc = acc + pltpu.roll(acc, shift, axis=0)
                shift *= 2
            mean = acc / jnp.minimum(posf, float(w))
            d = jnp.where((lane >= g * group) & (lane < (g + 1) * group), mean, d)
        d = (d - e)[POOL_HALO:, :]
        y = jnp.dot(d.astype(BF16), wbd_ref[...], preferred_element_type=F32) * sc_ref[...]
        o_ref[pl.ds(start, chunk), :] = y.astype(BF16)
        return carry

    lax.fori_loop(0, S // chunk, body, 0)


def _pool(u, wbd, pscale, B, S, chunk=512):
    T, W = u.shape
    kern = functools.partial(_pool_kernel, chunk=chunk, group=W // len(POOL_WINDOWS))
    return pl.pallas_call(
        kern,
        grid=(B,),
        in_specs=[pl.BlockSpec((S, W), lambda b: (b, 0)),
                  pl.BlockSpec((W, W), lambda b: (0, 0)),
                  pl.BlockSpec((1, W), lambda b: (0, 0))],
        out_specs=pl.BlockSpec((S, W), lambda b: (b, 0)),
        out_shape=jax.ShapeDtypeStruct((T, W), BF16),
        scratch_shapes=[pltpu.VMEM((S + POOL_HALO, W), F32)],
        compiler_params=_cparams(("parallel",)),
        name="pool_mixer",
    )(u, wbd, pscale)


def _fgate_kernel(f_ref, bf_ref, o_ref):
    S = f_ref.shape[0]
    z = f_ref[...] + bf_ref[...]
    x = jnp.minimum(z, 0.0) - jnp.log(1.0 + jnp.exp(-jnp.abs(z)))
    row = lax.broadcasted_iota(I32, x.shape, 0)
    lane = lax.broadcasted_iota(I32, x.shape, 1)
    shift = 1
    while shift < S:
        x = x + jnp.where(row >= shift, pltpu.roll(x, shift, axis=0), 0.0)
        shift *= 2
    hi = x.astype(BF16).astype(F32)
    mid = (x - hi).astype(BF16).astype(F32)
    lo = (x - hi) - mid
    for p in range(FOX_HEADS // 2):
        out = jnp.zeros(x.shape, F32)
        for hh in range(2):
            c = 2 * p + hh
            for j, part in enumerate((hi, mid, lo)):
                out = jnp.where(lane == BIAS_TERMS * hh + j, -part[:, c:c + 1], out)
        o_ref[0, p] = out.astype(BF16)


def _fgate(f_logit, bf_pad, B, S):
    return pl.pallas_call(
        _fgate_kernel,
        grid=(B,),
        in_specs=[pl.BlockSpec((S, LANES), lambda b: (b, 0)),
                  pl.BlockSpec((1, LANES), lambda b: (0, 0))],
        out_specs=pl.BlockSpec((1, FOX_HEADS // 2, S, LANES), lambda b: (b, 0, 0, 0)),
        out_shape=jax.ShapeDtypeStruct((B, FOX_HEADS // 2, S, LANES), BF16),
        compiler_params=_cparams(("parallel",)),
        name="forget_cumsum",
    )(f_logit, bf_pad)


def _fox_kernel(q_ref, k_ref, a_ref, v_ref, o_ref, vt_sc, qa_sc, sa_sc, sb_sc, m_sc, l_sc, acc_sc, *, blk):
    qi = pl.program_id(2)
    nk = vt_sc.shape[0]
    nt = (((1,), (1,)), ((), ()))

    @pl.when(qi == 0)
    def _():
        for c in range(nk):
            vt_sc[c] = v_ref[c * blk:(c + 1) * blk, :].astype(F32).T.astype(BF16)

    q = q_ref[...].astype(F32)
    lane = lax.broadcasted_iota(I32, q.shape, 1)
    for h in range(2):
        mine = (lane >= h * HEAD_DIM) & (lane < (h + 1) * HEAD_DIM)
        bias_rows = (lane >= h * BIAS_TERMS) & (lane < (h + 1) * BIAS_TERMS)
        qa_sc[h] = jnp.concatenate([jnp.where(mine, q, 0.0), jnp.where(bias_rows, 1.0, 0.0)], axis=1).astype(BF16)
    m_sc[...] = jnp.full(m_sc.shape, -jnp.inf, F32)
    l_sc[...] = jnp.zeros(l_sc.shape, F32)
    acc_sc[...] = jnp.zeros(acc_sc.shape, F32)

    def scores(ki, dst):
        ks = pl.multiple_of(ki * blk, blk)
        kb = jnp.concatenate([k_ref[pl.ds(ks, blk), :], a_ref[0, 0, pl.ds(ks, blk), :]], axis=1)
        for h in range(2):
            dst[h] = lax.dot_general(kb, qa_sc[h], nt, preferred_element_type=F32)

    def absorb(ki, src, causal):
        for h in range(2):
            st = src[h]
            if causal:
                kpos = lax.broadcasted_iota(I32, st.shape, 0)
                qpos = lax.broadcasted_iota(I32, st.shape, 1)
                st = jnp.where(kpos <= qpos, st, -jnp.inf)
            m_prev = m_sc[h]
            m_new = jnp.maximum(m_prev, jnp.max(st, axis=0, keepdims=True))
            alpha = jnp.exp(m_prev - m_new)
            p = jnp.exp(st - m_new)
            l_sc[h] = alpha * l_sc[h] + jnp.sum(p, axis=0, keepdims=True)
            m_sc[h] = m_new
            vt = vt_sc[ki, h * HEAD_DIM:(h + 1) * HEAD_DIM, :]
            pv = jnp.dot(vt, p.astype(BF16), preferred_element_type=F32)
            rows = slice(h * HEAD_DIM, (h + 1) * HEAD_DIM)
            acc_sc[rows, :] = acc_sc[rows, :] * alpha + pv

    scores(0, sa_sc)

    def body(p, carry):
        scores(2 * p + 1, sb_sc)
        absorb(2 * p, sa_sc, False)
        scores(2 * p + 2, sa_sc)
        absorb(2 * p + 1, sb_sc, False)
        return carry

    lax.fori_loop(0, qi // 2, body, 0)

    @pl.when(qi % 2 == 1)
    def _():
        scores(qi, sb_sc)
        absorb(qi - 1, sa_sc, False)
        absorb(qi, sb_sc, True)

    @pl.when(qi % 2 == 0)
    def _():
        absorb(qi, sa_sc, True)

    out_t = jnp.concatenate([acc_sc[h * HEAD_DIM:(h + 1) * HEAD_DIM, :] / l_sc[h] for h in range(2)], axis=0)
    o_ref[...] = out_t.T.astype(BF16)


def _fox(q, k, v, fbias, B, S, blk=512):
    T, FW = q.shape
    nq = S // blk
    pairs = FW // LANES
    kern = functools.partial(_fox_kernel, blk=blk)
    return pl.pallas_call(
        kern,
        grid=(B, pairs, nq),
        in_specs=[pl.BlockSpec((blk, LANES), lambda b, p, i: (b * nq + i, p)),
                  pl.BlockSpec((S, LANES), lambda b, p, i: (b, p)),
                  pl.BlockSpec((1, 1, S, LANES), lambda b, p, i: (b, p, 0, 0)),
                  pl.BlockSpec((S, LANES), lambda b, p, i: (b, p))],
        out_specs=pl.BlockSpec((blk, LANES), lambda b, p, i: (b * nq + i, p)),
        out_shape=jax.ShapeDtypeStruct((T, FW), BF16),
        scratch_shapes=[pltpu.VMEM((nq, LANES, blk), BF16), pltpu.VMEM((2, blk, 2 * LANES), BF16),
                        pltpu.VMEM((2, blk, blk), F32), pltpu.VMEM((2, blk, blk), F32),
                        pltpu.VMEM((2, 1, blk), F32), pltpu.VMEM((2, 1, blk), F32), pltpu.VMEM((LANES, blk), F32)],
        compiler_params=_cparams(("parallel", "parallel", "arbitrary")),
        name="fox_attention",
    )(q, k, fbias, v)


def _memattn_kernel(qm_ref, mem_ref, wkv_ref, o_ref, k_sc, v_sc):
    MW = qm_ref.shape[1]

    @pl.when(pl.program_id(1) == 0)
    def _():
        kv = jnp.dot(mem_ref[0].astype(BF16), wkv_ref[...], preferred_element_type=F32)
        k_sc[...] = kv[:, :MW].astype(BF16)
        v_sc[...] = kv[:, MW:].astype(BF16)

    q = qm_ref[...]
    lane = lax.broadcasted_iota(I32, q.shape, 1)
    hd = MW // MEM_HEADS
    out = jnp.zeros(q.shape, F32)
    for h in range(MEM_HEADS):
        mine = (lane >= h * hd) & (lane < (h + 1) * hd)
        qh = jnp.where(mine, q, jnp.zeros_like(q))
        s = lax.dot_general(qh, k_sc[...], (((1,), (1,)), ((), ())), preferred_element_type=F32)
        p = jnp.exp(s - jnp.max(s, axis=1, keepdims=True))
        l = jnp.sum(p, axis=1, keepdims=True)
        o = jnp.dot(p.astype(BF16), v_sc[...], preferred_element_type=F32)
        out = jnp.where(mine, o / l, out)
    o_ref[...] = out.astype(BF16)


def _memattn(qm, mem, wkv, B, S, tq=512):
    T, MW = qm.shape
    M, D = mem.shape[1], mem.shape[2]
    nq = S // tq
    return pl.pallas_call(
        _memattn_kernel,
        grid=(B, nq),
        in_specs=[pl.BlockSpec((tq, MW), lambda b, i: (b * nq + i, 0)),
                  pl.BlockSpec((1, M, D), lambda b, i: (b, 0, 0)),
                  pl.BlockSpec(wkv.shape, lambda b, i: (0, 0))],
        out_specs=pl.BlockSpec((tq, MW), lambda b, i: (b * nq + i, 0)),
        out_shape=jax.ShapeDtypeStruct((T, MW), BF16),
        scratch_shapes=[pltpu.VMEM((M, MW), BF16), pltpu.VMEM((M, MW), BF16)],
        compiler_params=_cparams(("parallel", "arbitrary")),
        name="memory_attention",
    )(qm, mem, wkv)


PACK_SUB = 4


def _pack_rows(ref, val):
    n = val.shape[0]
    half = val.shape[1] // 2
    words = pltpu.pack_elementwise([val[:, :half], val[:, half:]], packed_dtype=BF16)
    for j in range(PACK_SUB):
        ref[pl.ds(j, n, stride=PACK_SUB), :] = words[:, j * LANES:(j + 1) * LANES]


def _unpack_rows(ref, n, dtype=BF16):
    slabs = [ref[pl.ds(j, n, stride=PACK_SUB), :] for j in range(PACK_SUB)]
    lo = [pltpu.unpack_elementwise(s, index=0, packed_dtype=BF16, unpacked_dtype=F32) for s in slabs]
    hi = [pltpu.unpack_elementwise(s, index=1, packed_dtype=BF16, unpacked_dtype=F32) for s in slabs]
    return jnp.concatenate(lo + hi, axis=1).astype(dtype)


def _packed_row(ref, r):
    return ref.at[pl.ds(pl.multiple_of(r * PACK_SUB, PACK_SUB), PACK_SUB), :]


def _outproj_kernel(yp_ref, yf_ref, ym_ref, x_ref, w_ref, g_ref, b_ref, o_ref, pk_ref, *, alpha):
    pw, fw = yp_ref.shape[1], yf_ref.shape[1]
    h = jnp.dot(yp_ref[...], w_ref[0:pw, :], preferred_element_type=F32)
    h = h + jnp.dot(yf_ref[...], w_ref[pw:pw + fw, :], preferred_element_type=F32)
    h = h + jnp.dot(ym_ref[...], w_ref[pw + fw:, :], preferred_element_type=F32)
    y = _layer_norm(alpha * x_ref[...] + h, g_ref[...], b_ref[...])
    _store_rows(o_ref, y)
    _pack_rows(pk_ref, y)


def _outproj(yp, yf, ym, x2, w_out, g, b, alpha, tm=512):
    T, D = x2.shape
    assert D == SUBLANES * LANES == 2 * PACK_SUB * LANES
    row = lambda i: (i, 0)
    fixed = lambda i: (0, 0)
    return pl.pallas_call(
        functools.partial(_outproj_kernel, alpha=alpha),
        grid=(T // tm,),
        in_specs=[pl.BlockSpec((tm, yp.shape[1]), row), pl.BlockSpec((tm, yf.shape[1]), row),
                  pl.BlockSpec((tm, ym.shape[1]), row), pl.BlockSpec((tm, D), row),
                  pl.BlockSpec(w_out.shape, fixed), pl.BlockSpec((1, D), fixed), pl.BlockSpec((1, D), fixed)],
        out_specs=[pl.BlockSpec((tm * SUBLANES, LANES), row),
                   pl.BlockSpec((tm * PACK_SUB, LANES), row)],
        out_shape=[jax.ShapeDtypeStruct((T * SUBLANES, LANES), F32),
                   jax.ShapeDtypeStruct((T * PACK_SUB, LANES), PACKED)],
        compiler_params=_cparams(("parallel",)),
        name="outproj_ln1",
    )(yp, yf, ym, x2, w_out, g, b)


ROUTE_TILE = 512


def _per_expert_column(v, dtype=F32):
    return jnp.broadcast_to(v.astype(dtype)[:, None], (v.shape[0], ROUTE_TILE))


def _router_kernel(x_ref, wh_ref, wl_ref, bias_ref, eidx_ref, gate_ref, rank_ref, cnt_ref, carry_sc, *, tt):
    E = wh_ref.shape[0]
    gsz = E // N_GROUPS
    ninf = -jnp.inf

    @pl.when(pl.program_id(0) == 0)
    def _():
        carry_sc[...] = jnp.zeros(carry_sc.shape, F32)

    x = _load_rows(x_ref, tt)
    xh = x.astype(BF16)
    xl = (x - xh.astype(F32)).astype(BF16)
    wh = wh_ref[...]
    nt = (((1,), (1,)), ((), ()))
    logits = lax.dot_general(wh, xh, nt, preferred_element_type=F32) + (
        lax.dot_general(wl_ref[...], xh, nt, preferred_element_type=F32)
        + lax.dot_general(wh, xl, nt, preferred_element_type=F32))
    scores = 1.0 / (1.0 + jnp.exp(-logits))
    biased = scores + bias_ref[...]
    eiota = lax.broadcasted_iota(I32, (E, tt), 0).astype(F32)
    giota = lax.broadcasted_iota(I32, (gsz, tt), 0).astype(F32)

    def cmax(a):
        return jnp.max(a, axis=0, keepdims=True)

    def first_at(a, m, iota, n):
        return jnp.min(jnp.where(a == m, iota, float(n)), axis=0, keepdims=True)

    groups = [biased[g * gsz:(g + 1) * gsz, :] for g in range(N_GROUPS)]
    gscore = []
    for blk in groups:
        m1 = cmax(blk)
        m2 = cmax(jnp.where(giota == first_at(blk, m1, giota, gsz), ninf, blk))
        gscore.append(m1 + m2)

    kept = []
    for g in range(N_GROUPS):
        ahead = jnp.zeros((1, tt), F32)
        for o in range(N_GROUPS):
            if o == g:
                continue
            beats = gscore[o] > gscore[g]
            if o < g:
                beats = beats | (gscore[o] == gscore[g])
            ahead = ahead + jnp.where(beats, 1.0, 0.0)
        kept.append(groups[g] + jnp.where(ahead < float(TOPK_GROUPS), 0.0, ninf))
    masked = jnp.concatenate(kept, axis=0)

    picks, sels = [], []
    chosen = jnp.zeros((E, tt), F32)
    for _ in range(TOP_K):
        ik = first_at(masked, cmax(masked), eiota, E)
        oh = eiota == ik
        picks.append(ik)
        sels.append(jnp.sum(jnp.where(oh, scores, 0.0), axis=0, keepdims=True))
        chosen = jnp.where(oh, 1.0, chosen)
        masked = jnp.where(oh, ninf, masked)
    denom = sels[0]
    for sk in sels[1:]:
        denom = denom + sk

    r = lax.broadcasted_iota(I32, (tt, tt), 0)
    c = lax.broadcasted_iota(I32, (tt, tt), 1)
    earlier = jnp.where(r < c, 1.0, 0.0).astype(BF16)
    chosen_b = chosen.astype(BF16)
    pos = carry_sc[...] + jnp.dot(chosen_b, earlier, preferred_element_type=F32)
    carry_sc[...] = carry_sc[...] + jnp.dot(chosen_b, jnp.ones((tt, tt), BF16), preferred_element_type=F32)
    cnt_ref[...] = carry_sc[...]

    ranks = [jnp.sum(jnp.where(eiota == ik, pos, 0.0), axis=0, keepdims=True) for ik in picks]
    eidx_ref[...] = jnp.concatenate(picks, axis=0).astype(I32)
    gate_ref[...] = jnp.concatenate([sk / denom * ROUTED_SCALE for sk in sels], axis=0)
    rank_ref[...] = jnp.concatenate(ranks, axis=0).astype(I32)


def _router(x1t, wrt_hi, wrt_lo, rbias):
    T = x1t.shape[0] // SUBLANES
    E, D = wrt_hi.shape
    tt = ROUTE_TILE
    tok = lambda i: (0, i)
    fixed = lambda i: (0, 0)
    return pl.pallas_call(
        functools.partial(_router_kernel, tt=tt),
        grid=(T // tt,),
        in_specs=[pl.BlockSpec((tt * SUBLANES, LANES), lambda i: (i, 0)), pl.BlockSpec((E, D), fixed),
                  pl.BlockSpec((E, D), fixed), pl.BlockSpec((E, tt), fixed)],
        out_specs=[pl.BlockSpec((TOP_K, tt), tok), pl.BlockSpec((TOP_K, tt), tok), pl.BlockSpec((TOP_K, tt), tok),
                   pl.BlockSpec((E, tt), fixed)],
        out_shape=[jax.ShapeDtypeStruct((TOP_K, T), I32), jax.ShapeDtypeStruct((TOP_K, T), F32),
                   jax.ShapeDtypeStruct((TOP_K, T), I32), jax.ShapeDtypeStruct((E, tt), F32)],
        scratch_shapes=[pltpu.VMEM((E, tt), F32)],
        compiler_params=_cparams(("arbitrary",)),
        name="router_topk",
    )(x1t, wrt_hi, wrt_lo, rbias)


def _dest_kernel(e_ref, r_ref, sp_ref, d_ref):
    E = sp_ref.shape[0]
    tt = e_ref.shape[1]
    eiota = lax.broadcasted_iota(I32, (E, tt), 0)
    sp = sp_ref[...]
    rows = []
    for kk in range(TOP_K):
        start = jnp.sum(jnp.where(eiota == e_ref[kk:kk + 1, :], sp, 0.0), axis=0, keepdims=True)
        rows.append(start.astype(I32) + r_ref[kk:kk + 1, :])
    d_ref[...] = jnp.concatenate(rows, axis=0)


def _dest(eidx, rank, start_pad_b):
    K, T = eidx.shape
    tt = ROUTE_TILE
    tok = lambda i: (0, i)
    return pl.pallas_call(
        _dest_kernel,
        grid=(T // tt,),
        in_specs=[pl.BlockSpec((K, tt), tok), pl.BlockSpec((K, tt), tok),
                  pl.BlockSpec(start_pad_b.shape, lambda i: (0, 0))],
        out_specs=pl.BlockSpec((K, tt), tok),
        out_shape=jax.ShapeDtypeStruct((K, T), I32),
        compiler_params=_cparams(("parallel",)),
        name="moe_dest",
    )(eidx, rank, start_pad_b)


def _dispatch_kernel(zb_ref, d_ref, x_ref, xs_ref, zero_sc, sem, *, tt, n_exp):
    blk_rows = ROW_BLOCK * PACK_SUB

    @pl.when(pl.program_id(0) == 0)
    def _():
        zero_sc[...] = jnp.zeros(zero_sc.shape, PACKED)

        def zstart(e, c):
            @pl.when(zb_ref[e] >= 0)
            def _():
                dst = xs_ref.at[pl.ds(pl.multiple_of(zb_ref[e] * PACK_SUB, blk_rows), blk_rows), :]
                pltpu.make_async_copy(zero_sc, dst, sem).start()
            return c

        def zwait(e, c):
            @pl.when(zb_ref[e] >= 0)
            def _():
                pltpu.make_async_copy(zero_sc, xs_ref.at[pl.ds(0, blk_rows), :], sem).wait()
            return c

        lax.fori_loop(0, n_exp, zstart, 0)
        lax.fori_loop(0, n_exp, zwait, 0)

    def start(j, c):
        src = _packed_row(x_ref, j)
        for kk in range(TOP_K):
            pltpu.make_async_copy(src, _packed_row(xs_ref, d_ref[kk, j]), sem).start(priority=kk % 2)
        return c

    lax.fori_loop(0, tt, start, 0)
    for kk in range(TOP_K):
        pltpu.make_async_copy(x_ref, xs_ref.at[pl.ds(0, tt * PACK_SUB), :], sem).wait()


def _dispatch(zero_blk, dest, xpk, n_rows, tt=256):
    T = xpk.shape[0] // PACK_SUB
    n_exp = zero_blk.shape[0]
    grid_spec = pltpu.PrefetchScalarGridSpec(
        num_scalar_prefetch=1,
        grid=(T // tt,),
        in_specs=[pl.BlockSpec((TOP_K, tt), lambda i, zb: (0, i), memory_space=pltpu.SMEM),
                  pl.BlockSpec((tt * PACK_SUB, LANES), lambda i, zb: (i, 0))],
        out_specs=pl.BlockSpec(memory_space=pl.ANY),
        scratch_shapes=[pltpu.VMEM((ROW_BLOCK * PACK_SUB, LANES), PACKED), pltpu.SemaphoreType.DMA],
    )
    return pl.pallas_call(
        functools.partial(_dispatch_kernel, tt=tt, n_exp=n_exp),
        grid_spec=grid_spec,
        out_shape=jax.ShapeDtypeStruct((n_rows * PACK_SUB, LANES), PACKED),
        compiler_params=_cparams(("arbitrary",)),
        name="moe_dispatch",
    )(zero_blk, dest, xpk)


def _expert_kernel(sb_ref, nb_ref, nu_ref, wg_ref, wu_ref, wd_ref, xs_ref, y_ref,
                   xbuf, ybuf, wg_sc, wu_sc, wd_sc, isem, osem):
    e = pl.program_id(0)
    nb = nb_ref[e]
    base = sb_ref[e]
    n_used = nu_ref[0]
    in_rows = out_rows = ROW_BLOCK * PACK_SUB

    def fetch(blk, slot):
        src = xs_ref.at[pl.ds(pl.multiple_of(blk * in_rows, in_rows), in_rows), :]
        pltpu.make_async_copy(src, xbuf.at[slot], isem.at[slot]).start()

    def wait_in(slot):
        pltpu.make_async_copy(xs_ref.at[pl.ds(0, in_rows), :], xbuf.at[slot], isem.at[slot]).wait()

    def store(blk, slot):
        dst = y_ref.at[pl.ds(pl.multiple_of(blk * out_rows, out_rows), out_rows), :]
        pltpu.make_async_copy(ybuf.at[slot], dst, osem.at[slot]).start()

    def wait_out(slot):
        pltpu.make_async_copy(ybuf.at[slot], y_ref.at[pl.ds(0, out_rows), :], osem.at[slot]).wait()

    @pl.when(nb > 0)
    def _():
        @pl.when(base == 0)
        def _():
            for g0 in range(EXPERT_AHEAD):
                @pl.when(g0 < n_used)
                def _():
                    fetch(g0, g0 % EXPERT_BUFS)

        wg_sc[...] = wg_ref[0].astype(BF16)
        wu_sc[...] = wu_ref[0].astype(BF16)
        wd_sc[...] = wd_ref[0].astype(BF16)

        def step(g, slot):
            wait_in(slot)

            @pl.when(g + EXPERT_AHEAD < n_used)
            def _():
                fetch(g + EXPERT_AHEAD, (slot + EXPERT_AHEAD) % EXPERT_BUFS)

            @pl.when(g >= EXPERT_BUFS)
            def _():
                wait_out(slot)

            xb = _unpack_rows(xbuf.at[slot], ROW_BLOCK)
            a = jnp.dot(xb, wg_sc[...], preferred_element_type=F32)
            u = jnp.dot(xb, wu_sc[...], preferred_element_type=F32)
            h = (_silu(a) * u).astype(BF16)
            _pack_rows(ybuf.at[slot], jnp.dot(h, wd_sc[...], preferred_element_type=F32))
            store(g, slot)

        first_group = base // EXPERT_BUFS

        def group(p, c):
            for slot in range(EXPERT_BUFS):
                g = (first_group + p) * EXPERT_BUFS + slot

                @pl.when((g >= base) & (g < base + nb))
                def _():
                    step(g, slot)
            return c

        lax.fori_loop(0, (base + nb + EXPERT_BUFS - 1) // EXPERT_BUFS - first_group, group, 0)

    @pl.when(e == pl.num_programs(0) - 1)
    def _():
        for slot in range(EXPERT_BUFS):
            @pl.when(slot < n_used)
            def _():
                wait_out(slot)


def _experts(start_blk, n_blk_e, n_used, xs, w_gate, w_up, w_down):
    n_rows = xs.shape[0] // PACK_SUB
    E, D, H = w_gate.shape
    wsel = lambda e, sb, nb, nu: (e, 0, 0)
    grid_spec = pltpu.PrefetchScalarGridSpec(
        num_scalar_prefetch=3,
        grid=(E,),
        in_specs=[pl.BlockSpec((1, D, H), wsel), pl.BlockSpec((1, D, H), wsel), pl.BlockSpec((1, H, D), wsel),
                  pl.BlockSpec(memory_space=pl.ANY)],
        out_specs=pl.BlockSpec(memory_space=pl.ANY),
        scratch_shapes=[pltpu.VMEM((EXPERT_BUFS, ROW_BLOCK * PACK_SUB, LANES), PACKED),
                        pltpu.VMEM((EXPERT_BUFS, ROW_BLOCK * PACK_SUB, LANES), PACKED),
                        pltpu.VMEM((D, H), BF16), pltpu.VMEM((D, H), BF16), pltpu.VMEM((H, D), BF16),
                        pltpu.SemaphoreType.DMA((EXPERT_BUFS,)), pltpu.SemaphoreType.DMA((EXPERT_BUFS,))],
    )
    return pl.pallas_call(
        _expert_kernel,
        grid_spec=grid_spec,
        out_shape=jax.ShapeDtypeStruct((n_rows * PACK_SUB, LANES), PACKED),
        compiler_params=_cparams(("arbitrary",)),
        name="moe_experts",
    )(start_blk, n_blk_e, n_used, w_gate, w_up, w_down, xs)


def _combine_kernel(dcur_ref, dnxt_ref, x_ref, gate_ref, wsg_ref, wsu_ref, wsd_ref, g_ref, b_ref, y_ref,
                    o_ref, buf, sem, *, tt, alpha):
    i = pl.program_id(0)
    n = pl.num_programs(0)

    def fetch(d_ref, slot):
        def body(j, c):
            for kk in range(TOP_K):
                src = _packed_row(y_ref, d_ref[kk, j])
                pltpu.make_async_copy(src, _packed_row(buf.at[slot, kk], j), sem.at[slot]).start(priority=kk % 2)
            return c

        lax.fori_loop(0, tt, body, 0)

    def reduce(slot):
        for kk in range(TOP_K):
            pltpu.make_async_copy(y_ref.at[pl.ds(0, tt * PACK_SUB), :], buf.at[slot, kk], sem.at[slot]).wait()
        x = _load_rows(x_ref, tt)
        xb = x.astype(BF16)
        hs = _silu(jnp.dot(xb, wsg_ref[...], preferred_element_type=F32)) * jnp.dot(
            xb, wsu_ref[...], preferred_element_type=F32)
        moe = jnp.dot(hs.astype(BF16), wsd_ref[...], preferred_element_type=F32)
        gates = gate_ref[...]
        for kk in range(TOP_K):
            moe = moe + gates[:, kk:kk + 1] * _unpack_rows(buf.at[slot, kk], tt, F32)
        o_ref[...] = _layer_norm(alpha * x + moe, g_ref[...], b_ref[...])

    @pl.when(i == 0)
    def _():
        fetch(dcur_ref, 0)

    for slot in range(2):
        @pl.when((i % 2 == slot) & (i + 1 < n))
        def _():
            fetch(dnxt_ref, 1 - slot)

    for slot in range(2):
        @pl.when(i % 2 == slot)
        def _():
            reduce(slot)


def _combine(dest, x1t, gates, wsg, wsu, wsd, g, b, y, alpha, tt=128):
    T = x1t.shape[0] // SUBLANES
    D = SUBLANES * LANES
    n = T // tt
    row = lambda i: (i, 0)
    fixed = lambda i: (0, 0)
    return pl.pallas_call(
        functools.partial(_combine_kernel, tt=tt, alpha=alpha),
        grid=(n,),
        in_specs=[pl.BlockSpec((TOP_K, tt), lambda i: (0, i), memory_space=pltpu.SMEM),
                  pl.BlockSpec((TOP_K, tt), lambda i: (0, jnp.minimum(i + 1, n - 1)), memory_space=pltpu.SMEM),
                  pl.BlockSpec((tt * SUBLANES, LANES), row), pl.BlockSpec((tt, TOP_K), row),
                  pl.BlockSpec(wsg.shape, fixed), pl.BlockSpec(wsu.shape, fixed), pl.BlockSpec(wsd.shape, fixed),
                  pl.BlockSpec((1, D), fixed), pl.BlockSpec((1, D), fixed),
                  pl.BlockSpec(memory_space=pl.ANY)],
        out_specs=pl.BlockSpec((tt, D), row),
        out_shape=jax.ShapeDtypeStruct((T, D), F32),
        scratch_shapes=[pltpu.VMEM((2, TOP_K, tt * PACK_SUB, LANES), PACKED), pltpu.SemaphoreType.DMA((2,))],
        compiler_params=_cparams(("arbitrary",)),
        name="moe_combine_ln2",
    )(dest, dest, x1t, gates, wsg, wsu, wsd, g, b, y)


def _layer(x2, mem, B, S, depth, w_in, b_f, w_pool, pool_scale, w_mem_kv, w_out, ln1_g, ln1_b,
           w_router, router_bias, w_gate, w_up, w_down, ws_gate, ws_up, ws_down, ln2_g, ln2_b):
    T, D = x2.shape
    n_win, grp = w_pool.shape[0], w_pool.shape[1]
    pw = n_win * grp
    fw = FOX_HEADS * HEAD_DIM
    mw = w_mem_kv.shape[1] // 2
    E = w_router.shape[1]
    alpha = (2 * depth) ** 0.25

    f_lo = pw + 3 * fw
    w_main = jnp.concatenate([w_in[:, :f_lo], w_in[:, f_lo + FOX_HEADS:]], axis=1).astype(BF16)
    w_f = jnp.pad(w_in[:, f_lo:f_lo + FOX_HEADS], ((0, 0), (0, LANES - FOX_HEADS))).astype(BF16)
    bf_pad = jnp.pad(b_f, (0, LANES - FOX_HEADS)).reshape(1, LANES)
    wbd = jnp.zeros((pw, pw), F32)
    for g in range(n_win):
        wbd = wbd.at[g * grp:(g + 1) * grp, g * grp:(g + 1) * grp].set(w_pool[g])
    wrt = w_router.T
    wrt_hi = wrt.astype(BF16)
    wrt_lo = (wrt - wrt_hi.astype(F32)).astype(BF16)

    u_pool, q, k, v, q_mem, f_logit = _inproj(x2, w_main, w_f, pw, fw, mw)
    y_pool = _pool(u_pool, wbd.astype(BF16), pool_scale.reshape(1, pw), B, S)
    fcum = _fgate(f_logit, bf_pad, B, S)
    y_fox = _fox(q, k, v, fcum, B, S)
    y_mem = _memattn(q_mem, mem, w_mem_kv.astype(BF16), B, S)
    x1, x1_packed = _outproj(y_pool, y_fox, y_mem, x2, w_out.astype(BF16), ln1_g.reshape(1, D),
                             ln1_b.reshape(1, D), alpha)

    eidx, gates, rank, counts = _router(x1, wrt_hi, wrt_lo, _per_expert_column(router_bias))

    cnt = counts[:, 0].astype(I32)
    padded = (cnt + ROW_BLOCK - 1) // ROW_BLOCK * ROW_BLOCK
    pad_end = jnp.cumsum(padded)
    start_pad = (pad_end - padded).astype(I32)
    n_rows = (T * TOP_K + E * (ROW_BLOCK - 1)) // ROW_BLOCK * ROW_BLOCK
    n_used = (pad_end[-1] // ROW_BLOCK).astype(I32)
    zero_blk = jnp.where(padded > 0, pad_end - ROW_BLOCK, -1).astype(I32)
    dest = _dest(eidx, rank, _per_expert_column(start_pad))

    xs = _dispatch(zero_blk, dest, x1_packed, n_rows)
    y = _experts(start_pad // ROW_BLOCK, (padded // ROW_BLOCK).astype(I32), n_used.reshape(1), xs,
                 w_gate, w_up, w_down)
    return _combine(dest, x1, gates.T, ws_gate.astype(BF16), ws_up.astype(BF16),
                    ws_down.astype(BF16), ln2_g.reshape(1, D), ln2_b.reshape(1, D), y, alpha)


def kernel(x, mem, w_in, b_f, w_pool, pool_scale, w_mem_kv, w_out, ln1_g, ln1_b, w_router, router_bias,
           w_gate, w_up, w_down, ws_gate, ws_up, ws_down, ln2_g, ln2_b):
    B, S, D = x.shape
    depth = w_in.shape[0]
    x2 = x.reshape(B * S, D)
    for l in range(depth):
        x2 = _layer(x2, mem, B, S, depth, w_in[l], b_f[l], w_pool[l], pool_scale[l], w_mem_kv[l], w_out[l],
                    ln1_g[l], ln1_b[l], w_router[l], router_bias[l], w_gate[l], w_up[l], w_down[l],
                    ws_gate[l], ws_up[l], ws_down[l], ln2_g[l], ln2_b[l])
    return x2.reshape(B, S, D)
```

```python
import functools

import jax
import jax.numpy as jnp
from jax import lax
from jax.experimental import pallas as pl
from jax.experimental.pallas import tpu as pltpu

F32 = jnp.float32
BF16 = jnp.bfloat16
I32 = jnp.int32
PACKED = jnp.int32

LANES = 128
SUBLANES = 8
POOL_WINDOWS = (2, 4, 8, 16)
POOL_HALO = 16
HEAD_DIM = 64
FOX_HEADS = 8
BIAS_TERMS = 3
MEM_HEADS = 4
TOP_K = 8
N_GROUPS = 8
TOPK_GROUPS = 4
ROUTED_SCALE = 2.5
LN_EPS = 1e-5
LOG2E = 1.4426950408889634
ROW_BLOCK = 256
EXPERT_BUFS = 4
EXPERT_AHEAD = EXPERT_BUFS - 1
VMEM_LIMIT = 48 * 1024 * 1024


def _cparams(sem):
    return pltpu.CompilerParams(dimension_semantics=sem, vmem_limit_bytes=VMEM_LIMIT)


def _layer_norm(z, g, b):
    mu = jnp.mean(z, axis=-1, keepdims=True)
    zc = z - mu
    var = jnp.mean(zc * zc, axis=-1, keepdims=True)
    return zc * lax.rsqrt(var + LN_EPS) * g + b


def _silu(x):
    return x * (1.0 / (1.0 + jnp.exp(-x)))


def _load_rows(ref, n):
    return jnp.concatenate([ref[pl.ds(j, n, stride=SUBLANES), :] for j in range(SUBLANES)], axis=1)


def _store_rows(ref, val):
    n = val.shape[0]
    for j in range(SUBLANES):
        ref[pl.ds(j, n, stride=SUBLANES), :] = val[:, j * LANES:(j + 1) * LANES]


def _tile_rows(ref, r):
    return ref.at[pl.ds(pl.multiple_of(r * SUBLANES, SUBLANES), SUBLANES), :]


def _inproj_kernel(x_ref, w_ref, wf_ref, up_ref, q_ref, k_ref, v_ref, qm_ref, f_ref, *, pw, fw, mw, scale):
    xb = x_ref[...].astype(BF16)

    def proj(lo, hi):
        return jnp.dot(xb, w_ref[:, lo:hi], preferred_element_type=F32)

    up_ref[...] = proj(0, pw)
    q_ref[...] = (proj(pw, pw + fw) * (scale * LOG2E)).astype(BF16)
    k_ref[...] = proj(pw + fw, pw + 2 * fw).astype(BF16)
    v_ref[...] = proj(pw + 2 * fw, pw + 3 * fw).astype(BF16)
    qm_ref[...] = (proj(pw + 3 * fw, pw + 3 * fw + mw) * scale).astype(BF16)
    f_ref[...] = jnp.dot(xb, wf_ref[...], preferred_element_type=F32)


def _inproj(x2, w_main, w_f, pw, fw, mw, tm=512):
    T, D = x2.shape
    kern = functools.partial(_inproj_kernel, pw=pw, fw=fw, mw=mw, scale=HEAD_DIM ** -0.5)
    row = lambda i: (i, 0)
    fixed = lambda i: (0, 0)
    return pl.pallas_call(
        kern,
        grid=(T // tm,),
        in_specs=[pl.BlockSpec((tm, D), row),
                  pl.BlockSpec(w_main.shape, fixed),
                  pl.BlockSpec(w_f.shape, fixed)],
        out_specs=[pl.BlockSpec((tm, pw), row), pl.BlockSpec((tm, fw), row), pl.BlockSpec((tm, fw), row),
                   pl.BlockSpec((tm, fw), row), pl.BlockSpec((tm, mw), row), pl.BlockSpec((tm, LANES), row)],
        out_shape=[jax.ShapeDtypeStruct((T, pw), F32), jax.ShapeDtypeStruct((T, fw), BF16),
                   jax.ShapeDtypeStruct((T, fw), BF16), jax.ShapeDtypeStruct((T, fw), BF16),
                   jax.ShapeDtypeStruct((T, mw), BF16), jax.ShapeDtypeStruct((T, LANES), F32)],
        compiler_params=_cparams(("parallel",)),
        name="inproj",
    )(x2, w_main, w_f)


def _pool_kernel(u_ref, wbd_ref, sc_ref, o_ref, ext_sc, *, chunk, group):
    S, W = u_ref.shape
    ext_sc[0:POOL_HALO, :] = jnp.zeros((POOL_HALO, W), F32)
    ext_sc[POOL_HALO:, :] = u_ref[...]
    rows = chunk + POOL_HALO
    lrow = lax.broadcasted_iota(I32, (rows, W), 0)
    lane = lax.broadcasted_iota(I32, (rows, W), 1)

    def body(c, carry):
        start = pl.multiple_of(c * chunk, chunk)
        e = ext_sc[pl.ds(start, rows), :]
        posf = (lrow + (start - POOL_HALO + 1)).astype(F32)
        acc = e
        d = jnp.zeros_like(e)
        shift = 1
        for g, w in enumerate(POOL_WINDOWS):
            while shift < w:
                acc = acc + pltpu.roll(acc, shift, axis=0)
                shift *= 2
            mean = acc / jnp.minimum(posf, float(w))
            d = jnp.where((lane >= g * group) & (lane < (g + 1) * group), mean, d)
        d = (d - e)[POOL_HALO:, :]
        y = jnp.dot(d.astype(BF16), wbd_ref[...], preferred_element_type=F32) * sc_ref[...]
        o_ref[pl.ds(start, chunk), :] = y.astype(BF16)
        return carry

    lax.fori_loop(0, S // chunk, body, 0)


def _pool(u, wbd, pscale, B, S, chunk=512):
    T, W = u.shape
    kern = functools.partial(_pool_kernel, chunk=chunk, group=W // len(POOL_WINDOWS))
    return pl.pallas_call(
        kern,
        grid=(B,),
        in_specs=[pl.BlockSpec((S, W), lambda b: (b, 0)),
                  pl.BlockSpec((W, W), lambda b: (0, 0)),
                  pl.BlockSpec((1, W), lambda b: (0, 0))],
        out_specs=pl.BlockSpec((S, W), lambda b: (b, 0)),
        out_shape=jax.ShapeDtypeStruct((T, W), BF16),
        scratch_shapes=[pltpu.VMEM((S + POOL_HALO, W), F32)],
        compiler_params=_cparams(("parallel",)),
        name="pool_mixer",
    )(u, wbd, pscale)


def _fgate_kernel(f_ref, bf_ref, o_ref):
    S = f_ref.shape[0]
    z = f_ref[...] + bf_ref[...]
    x = jnp.minimum(z, 0.0) - jnp.log(1.0 + jnp.exp(-jnp.abs(z)))
    row = lax.broadcasted_iota(I32, x.shape, 0)
    lane = lax.broadcasted_iota(I32, x.shape, 1)
    shift = 1
    while shift < S:
        x = x + jnp.where(row >= shift, pltpu.roll(x, shift, axis=0), 0.0)
        shift *= 2
    x = x * LOG2E
    hi = x.astype(BF16).astype(F32)
    mid = (x - hi).astype(BF16).astype(F32)
    lo = (x - hi) - mid
    for p in range(FOX_HEADS // 2):
        out = jnp.zeros(x.shape, F32)
        for hh in range(2):
            c = 2 * p + hh
            for j, part in enumerate((hi, mid, lo)):
                out = jnp.where(lane == BIAS_TERMS * hh + j, -part[:, c:c + 1], out)
        o_ref[0, p] = out.astype(BF16)


def _fgate(f_logit, bf_pad, B, S):
    return pl.pallas_call(
        _fgate_kernel,
        grid=(B,),
        in_specs=[pl.BlockSpec((S, LANES), lambda b: (b, 0)),
                  pl.BlockSpec((1, LANES), lambda b: (0, 0))],
        out_specs=pl.BlockSpec((1, FOX_HEADS // 2, S, LANES), lambda b: (b, 0, 0, 0)),
        out_shape=jax.ShapeDtypeStruct((B, FOX_HEADS // 2, S, LANES), BF16),
        compiler_params=_cparams(("parallel",)),
        name="forget_cumsum",
    )(f_logit, bf_pad)


def _fox_kernel(q_ref, k_ref, a_ref, v_ref, o_ref, vt_sc, qa_sc, sa_sc, sb_sc, m_sc, l_sc, acc_sc, *, blk):
    qi = pl.program_id(2)
    nk = vt_sc.shape[0]
    nt = (((1,), (1,)), ((), ()))

    @pl.when(qi == 0)
    def _():
        for c in range(nk):
            vt_sc[c] = v_ref[c * blk:(c + 1) * blk, :].astype(F32).T.astype(BF16)

    q = q_ref[...].astype(F32)
    lane = lax.broadcasted_iota(I32, q.shape, 1)
    for h in range(2):
        mine = (lane >= h * HEAD_DIM) & (lane < (h + 1) * HEAD_DIM)
        bias_rows = (lane >= h * BIAS_TERMS) & (lane < (h + 1) * BIAS_TERMS)
        qa_sc[h] = jnp.concatenate([jnp.where(mine, q, 0.0), jnp.where(bias_rows, 1.0, 0.0)], axis=1).astype(BF16)
    m_sc[...] = jnp.full(m_sc.shape, -jnp.inf, F32)
    l_sc[...] = jnp.zeros(l_sc.shape, F32)
    acc_sc[...] = jnp.zeros(acc_sc.shape, F32)

    def scores(ki, dst):
        ks = pl.multiple_of(ki * blk, blk)
        kb = jnp.concatenate([k_ref[pl.ds(ks, blk), :], a_ref[0, 0, pl.ds(ks, blk), :]], axis=1)
        for h in range(2):
            dst[h] = lax.dot_general(kb, qa_sc[h], nt, preferred_element_type=F32)

    def absorb(ki, src, causal):
        for h in range(2):
            st = src[h]
            if causal:
                kpos = lax.broadcasted_iota(I32, st.shape, 0)
                qpos = lax.broadcasted_iota(I32, st.shape, 1)
                st = jnp.where(kpos <= qpos, st, -jnp.inf)
            m_prev = m_sc[h]
            m_new = jnp.maximum(m_prev, jnp.max(st, axis=0, keepdims=True))
            alpha = jnp.exp2(m_prev - m_new)
            p = jnp.exp2(st - m_new)
            l_sc[h] = alpha * l_sc[h] + jnp.sum(p, axis=0, keepdims=True)
            m_sc[h] = m_new
            vt = vt_sc[ki, h * HEAD_DIM:(h + 1) * HEAD_DIM, :]
            pv = jnp.dot(vt, p.astype(BF16), preferred_element_type=F32)
            rows = slice(h * HEAD_DIM, (h + 1) * HEAD_DIM)
            acc_sc[rows, :] = acc_sc[rows, :] * alpha + pv

    scores(0, sa_sc)

    def body(p, carry):
        scores(2 * p + 1, sb_sc)
        absorb(2 * p, sa_sc, False)
        scores(2 * p + 2, sa_sc)
        absorb(2 * p + 1, sb_sc, False)
        return carry

    lax.fori_loop(0, qi // 2, body, 0)

    @pl.when(qi % 2 == 1)
    def _():
        scores(qi, sb_sc)
        absorb(qi - 1, sa_sc, False)
        absorb(qi, sb_sc, True)

    @pl.when(qi % 2 == 0)
    def _():
        absorb(qi, sa_sc, True)

    out_t = jnp.concatenate([acc_sc[h * HEAD_DIM:(h + 1) * HEAD_DIM, :] / l_sc[h] for h in range(2)], axis=0)
    o_ref[...] = out_t.T.astype(BF16)


def _fox(q, k, v, fbias, B, S, blk=512):
    T, FW = q.shape
    nq = S // blk
    pairs = FW // LANES
    kern = functools.partial(_fox_kernel, blk=blk)
    return pl.pallas_call(
        kern,
        grid=(B, pairs, nq),
        in_specs=[pl.BlockSpec((blk, LANES), lambda b, p, i: (b * nq + i, p)),
                  pl.BlockSpec((S, LANES), lambda b, p, i: (b, p)),
                  pl.BlockSpec((1, 1, S, LANES), lambda b, p, i: (b, p, 0, 0)),
                  pl.BlockSpec((S, LANES), lambda b, p, i: (b, p))],
        out_specs=pl.BlockSpec((blk, LANES), lambda b, p, i: (b * nq + i, p)),
        out_shape=jax.ShapeDtypeStruct((T, FW), BF16),
        scratch_shapes=[pltpu.VMEM((nq, LANES, blk), BF16), pltpu.VMEM((2, blk, 2 * LANES), BF16),
                        pltpu.VMEM((2, blk, blk), F32), pltpu.VMEM((2, blk, blk), F32),
                        pltpu.VMEM((2, 1, blk), F32), pltpu.VMEM((2, 1, blk), F32), pltpu.VMEM((LANES, blk), F32)],
        compiler_params=_cparams(("parallel", "parallel", "arbitrary")),
        name="fox_attention",
    )(q, k, fbias, v)


def _memattn_kernel(qm_ref, mem_ref, wkv_ref, o_ref, k_sc, v_sc):
    MW = qm_ref.shape[1]

    @pl.when(pl.program_id(1) == 0)
    def _():
        kv = jnp.dot(mem_ref[0].astype(BF16), wkv_ref[...], preferred_element_type=F32)
        k_sc[...] = kv[:, :MW].astype(BF16)
        v_sc[...] = kv[:, MW:].astype(BF16)

    q = qm_ref[...]
    lane = lax.broadcasted_iota(I32, q.shape, 1)
    hd = MW // MEM_HEADS
    out = jnp.zeros(q.shape, F32)
    for h in range(MEM_HEADS):
        mine = (lane >= h * hd) & (lane < (h + 1) * hd)
        qh = jnp.where(mine, q, jnp.zeros_like(q))
        s = lax.dot_general(qh, k_sc[...], (((1,), (1,)), ((), ())), preferred_element_type=F32)
        p = jnp.exp(s - jnp.max(s, axis=1, keepdims=True))
        l = jnp.sum(p, axis=1, keepdims=True)
        o = jnp.dot(p.astype(BF16), v_sc[...], preferred_element_type=F32)
        out = jnp.where(mine, o / l, out)
    o_ref[...] = out.astype(BF16)


def _memattn(qm, mem, wkv, B, S, tq=512):
    T, MW = qm.shape
    M, D = mem.shape[1], mem.shape[2]
    nq = S // tq
    return pl.pallas_call(
        _memattn_kernel,
        grid=(B, nq),
        in_specs=[pl.BlockSpec((tq, MW), lambda b, i: (b * nq + i, 0)),
                  pl.BlockSpec((1, M, D), lambda b, i: (b, 0, 0)),
                  pl.BlockSpec(wkv.shape, lambda b, i: (0, 0))],
        out_specs=pl.BlockSpec((tq, MW), lambda b, i: (b * nq + i, 0)),
        out_shape=jax.ShapeDtypeStruct((T, MW), BF16),
        scratch_shapes=[pltpu.VMEM((M, MW), BF16), pltpu.VMEM((M, MW), BF16)],
        compiler_params=_cparams(("parallel", "arbitrary")),
        name="memory_attention",
    )(qm, mem, wkv)


PACK_SUB = 4


def _pack_rows(ref, val):
    n = val.shape[0]
    half = val.shape[1] // 2
    words = pltpu.pack_elementwise([val[:, :half], val[:, half:]], packed_dtype=BF16)
    for j in range(PACK_SUB):
        ref[pl.ds(j, n, stride=PACK_SUB), :] = words[:, j * LANES:(j + 1) * LANES]


def _unpack_rows(ref, n, dtype=BF16):
    slabs = [ref[pl.ds(j, n, stride=PACK_SUB), :] for j in range(PACK_SUB)]
    lo = [pltpu.unpack_elementwise(s, index=0, packed_dtype=BF16, unpacked_dtype=F32) for s in slabs]
    hi = [pltpu.unpack_elementwise(s, index=1, packed_dtype=BF16, unpacked_dtype=F32) for s in slabs]
    return jnp.concatenate(lo + hi, axis=1).astype(dtype)


def _packed_row(ref, r):
    return ref.at[pl.ds(pl.multiple_of(r * PACK_SUB, PACK_SUB), PACK_SUB), :]


def _outproj_kernel(yp_ref, yf_ref, ym_ref, x_ref, w_ref, g_ref, b_ref, o_ref, pk_ref, *, alpha):
    pw, fw = yp_ref.shape[1], yf_ref.shape[1]
    h = jnp.dot(yp_ref[...], w_ref[0:pw, :], preferred_element_type=F32)
    h = h + jnp.dot(yf_ref[...], w_ref[pw:pw + fw, :], preferred_element_type=F32)
    h = h + jnp.dot(ym_ref[...], w_ref[pw + fw:, :], preferred_element_type=F32)
    y = _layer_norm(alpha * x_ref[...] + h, g_ref[...], b_ref[...])
    _store_rows(o_ref, y)
    _pack_rows(pk_ref, y)


def _outproj(yp, yf, ym, x2, w_out, g, b, alpha, tm=512):
    T, D = x2.shape
    assert D == SUBLANES * LANES == 2 * PACK_SUB * LANES
    row = lambda i: (i, 0)
    fixed = lambda i: (0, 0)
    return pl.pallas_call(
        functools.partial(_outproj_kernel, alpha=alpha),
        grid=(T // tm,),
        in_specs=[pl.BlockSpec((tm, yp.shape[1]), row), pl.BlockSpec((tm, yf.shape[1]), row),
                  pl.BlockSpec((tm, ym.shape[1]), row), pl.BlockSpec((tm, D), row),
                  pl.BlockSpec(w_out.shape, fixed), pl.BlockSpec((1, D), fixed), pl.BlockSpec((1, D), fixed)],
        out_specs=[pl.BlockSpec((tm * SUBLANES, LANES), row),
                   pl.BlockSpec((tm * PACK_SUB, LANES), row)],
        out_shape=[jax.ShapeDtypeStruct((T * SUBLANES, LANES), F32),
                   jax.ShapeDtypeStruct((T * PACK_SUB, LANES), PACKED)],
        compiler_params=_cparams(("parallel",)),
        name="outproj_ln1",
    )(yp, yf, ym, x2, w_out, g, b)


ROUTE_TILE = 512


def _per_expert_column(v, dtype=F32):
    return jnp.broadcast_to(v.astype(dtype)[:, None], (v.shape[0], ROUTE_TILE))


def _router_kernel(x_ref, wh_ref, wl_ref, bias_ref, eidx_ref, gate_ref, rank_ref, cnt_ref, carry_sc, *, tt):
    E = wh_ref.shape[0]
    gsz = E // N_GROUPS
    ninf = -jnp.inf

    @pl.when(pl.program_id(0) == 0)
    def _():
        carry_sc[...] = jnp.zeros(carry_sc.shape, F32)

    x = _load_rows(x_ref, tt)
    xh = x.astype(BF16)
    xl = (x - xh.astype(F32)).astype(BF16)
    wh = wh_ref[...]
    nt = (((1,), (1,)), ((), ()))
    logits = lax.dot_general(wh, xh, nt, preferred_element_type=F32) + (
        lax.dot_general(wl_ref[...], xh, nt, preferred_element_type=F32)
        + lax.dot_general(wh, xl, nt, preferred_element_type=F32))
    scores = 1.0 / (1.0 + jnp.exp(-logits))
    biased = scores + bias_ref[...]
    eiota = lax.broadcasted_iota(I32, (E, tt), 0).astype(F32)
    giota = lax.broadcasted_iota(I32, (gsz, tt), 0).astype(F32)

    def cmax(a):
        return jnp.max(a, axis=0, keepdims=True)

    def first_at(a, m, iota, n):
        return jnp.min(jnp.where(a == m, iota, float(n)), axis=0, keepdims=True)

    groups = [biased[g * gsz:(g + 1) * gsz, :] for g in range(N_GROUPS)]
    gscore = []
    for blk in groups:
        m1 = cmax(blk)
        m2 = cmax(jnp.where(giota == first_at(blk, m1, giota, gsz), ninf, blk))
        gscore.append(m1 + m2)

    kept = []
    for g in range(N_GROUPS):
        ahead = jnp.zeros((1, tt), F32)
        for o in range(N_GROUPS):
            if o == g:
                continue
            beats = gscore[o] > gscore[g]
            if o < g:
                beats = beats | (gscore[o] == gscore[g])
            ahead = ahead + jnp.where(beats, 1.0, 0.0)
        kept.append(groups[g] + jnp.where(ahead < float(TOPK_GROUPS), 0.0, ninf))
    masked = jnp.concatenate(kept, axis=0)

    picks, sels = [], []
    chosen = jnp.zeros((E, tt), F32)
    for _ in range(TOP_K):
        ik = first_at(masked, cmax(masked), eiota, E)
        oh = eiota == ik
        picks.append(ik)
        sels.append(jnp.sum(jnp.where(oh, scores, 0.0), axis=0, keepdims=True))
        chosen = jnp.where(oh, 1.0, chosen)
        masked = jnp.where(oh, ninf, masked)
    denom = sels[0]
    for sk in sels[1:]:
        denom = denom + sk

    r = lax.broadcasted_iota(I32, (tt, tt), 0)
    c = lax.broadcasted_iota(I32, (tt, tt), 1)
    earlier = jnp.where(r < c, 1.0, 0.0).astype(BF16)
    chosen_b = chosen.astype(BF16)
    pos = carry_sc[...] + jnp.dot(chosen_b, earlier, preferred_element_type=F32)
    carry_sc[...] = carry_sc[...] + jnp.dot(chosen_b, jnp.ones((tt, tt), BF16), preferred_element_type=F32)
    cnt_ref[...] = carry_sc[...]

    ranks = [jnp.sum(jnp.where(eiota == ik, pos, 0.0), axis=0, keepdims=True) for ik in picks]
    eidx_ref[...] = jnp.concatenate(picks, axis=0).astype(I32)
    gate_ref[...] = jnp.concatenate([sk / denom * ROUTED_SCALE for sk in sels], axis=0)
    rank_ref[...] = jnp.concatenate(ranks, axis=0).astype(I32)


def _router(x1t, wrt_hi, wrt_lo, rbias):
    T = x1t.shape[0] // SUBLANES
    E, D = wrt_hi.shape
    tt = ROUTE_TILE
    tok = lambda i: (0, i)
    fixed = lambda i: (0, 0)
    return pl.pallas_call(
        functools.partial(_router_kernel, tt=tt),
        grid=(T // tt,),
        in_specs=[pl.BlockSpec((tt * SUBLANES, LANES), lambda i: (i, 0)), pl.BlockSpec((E, D), fixed),
                  pl.BlockSpec((E, D), fixed), pl.BlockSpec((E, tt), fixed)],
        out_specs=[pl.BlockSpec((TOP_K, tt), tok), pl.BlockSpec((TOP_K, tt), tok), pl.BlockSpec((TOP_K, tt), tok),
                   pl.BlockSpec((E, tt), fixed)],
        out_shape=[jax.ShapeDtypeStruct((TOP_K, T), I32), jax.ShapeDtypeStruct((TOP_K, T), F32),
                   jax.ShapeDtypeStruct((TOP_K, T), I32), jax.ShapeDtypeStruct((E, tt), F32)],
        scratch_shapes=[pltpu.VMEM((E, tt), F32)],
        compiler_params=_cparams(("arbitrary",)),
        name="router_topk",
    )(x1t, wrt_hi, wrt_lo, rbias)


def _dest_kernel(e_ref, r_ref, sp_ref, d_ref):
    E = sp_ref.shape[0]
    tt = e_ref.shape[1]
    eiota = lax.broadcasted_iota(I32, (E, tt), 0)
    sp = sp_ref[...]
    rows = []
    for kk in range(TOP_K):
        start = jnp.sum(jnp.where(eiota == e_ref[kk:kk + 1, :], sp, 0.0), axis=0, keepdims=True)
        rows.append(start.astype(I32) + r_ref[kk:kk + 1, :])
    d_ref[...] = jnp.concatenate(rows, axis=0)


def _dest(eidx, rank, start_pad_b):
    K, T = eidx.shape
    tt = ROUTE_TILE
    tok = lambda i: (0, i)
    return pl.pallas_call(
        _dest_kernel,
        grid=(T // tt,),
        in_specs=[pl.BlockSpec((K, tt), tok), pl.BlockSpec((K, tt), tok),
                  pl.BlockSpec(start_pad_b.shape, lambda i: (0, 0))],
        out_specs=pl.BlockSpec((K, tt), tok),
        out_shape=jax.ShapeDtypeStruct((K, T), I32),
        compiler_params=_cparams(("parallel",)),
        name="moe_dest",
    )(eidx, rank, start_pad_b)


def _dispatch_kernel(zb_ref, d_ref, x_ref, wsg_ref, wsu_ref, wsd_ref, xs_ref, sh_ref, zero_sc, sem, *, tt, n_exp):
    blk_rows = ROW_BLOCK * PACK_SUB

    @pl.when(pl.program_id(0) == 0)
    def _():
        zero_sc[...] = jnp.zeros(zero_sc.shape, PACKED)

        def zstart(e, c):
            @pl.when(zb_ref[e] >= 0)
            def _():
                dst = xs_ref.at[pl.ds(pl.multiple_of(zb_ref[e] * PACK_SUB, blk_rows), blk_rows), :]
                pltpu.make_async_copy(zero_sc, dst, sem).start()
            return c

        def zwait(e, c):
            @pl.when(zb_ref[e] >= 0)
            def _():
                pltpu.make_async_copy(zero_sc, xs_ref.at[pl.ds(0, blk_rows), :], sem).wait()
            return c

        lax.fori_loop(0, n_exp, zstart, 0)
        lax.fori_loop(0, n_exp, zwait, 0)

    def start(j, c):
        src = _packed_row(x_ref, j)
        for kk in range(TOP_K):
            pltpu.make_async_copy(src, _packed_row(xs_ref, d_ref[kk, j]), sem).start(priority=kk % 2)
        return c

    lax.fori_loop(0, tt, start, 0)
    xb = _unpack_rows(x_ref, tt)
    hs = _silu(jnp.dot(xb, wsg_ref[...], preferred_element_type=F32)) * jnp.dot(
        xb, wsu_ref[...], preferred_element_type=F32)
    sh_ref[...] = jnp.dot(hs.astype(BF16), wsd_ref[...], preferred_element_type=F32)
    for kk in range(TOP_K):
        pltpu.make_async_copy(x_ref, xs_ref.at[pl.ds(0, tt * PACK_SUB), :], sem).wait()


def _dispatch(zero_blk, dest, xpk, wsg, wsu, wsd, n_rows, tt=256):
    T = xpk.shape[0] // PACK_SUB
    D = wsg.shape[0]
    n_exp = zero_blk.shape[0]
    fixed = lambda i, zb: (0, 0)
    grid_spec = pltpu.PrefetchScalarGridSpec(
        num_scalar_prefetch=1,
        grid=(T // tt,),
        in_specs=[pl.BlockSpec((TOP_K, tt), lambda i, zb: (0, i), memory_space=pltpu.SMEM),
                  pl.BlockSpec((tt * PACK_SUB, LANES), lambda i, zb: (i, 0)),
                  pl.BlockSpec(wsg.shape, fixed), pl.BlockSpec(wsu.shape, fixed), pl.BlockSpec(wsd.shape, fixed)],
        out_specs=[pl.BlockSpec(memory_space=pl.ANY), pl.BlockSpec((tt, D), lambda i, zb: (i, 0))],
        scratch_shapes=[pltpu.VMEM((ROW_BLOCK * PACK_SUB, LANES), PACKED), pltpu.SemaphoreType.DMA],
    )
    return pl.pallas_call(
        functools.partial(_dispatch_kernel, tt=tt, n_exp=n_exp),
        grid_spec=grid_spec,
        out_shape=[jax.ShapeDtypeStruct((n_rows * PACK_SUB, LANES), PACKED), jax.ShapeDtypeStruct((T, D), F32)],
        compiler_params=_cparams(("arbitrary",)),
        name="moe_dispatch",
    )(zero_blk, dest, xpk, wsg, wsu, wsd)


def _expert_kernel(sb_ref, nb_ref, nu_ref, wg_ref, wu_ref, wd_ref, xs_ref, y_ref,
                   xbuf, ybuf, wg_sc, wu_sc, wd_sc, isem, osem):
    e = pl.program_id(0)
    nb = nb_ref[e]
    base = sb_ref[e]
    n_used = nu_ref[0]
    in_rows = out_rows = ROW_BLOCK * PACK_SUB

    def fetch(blk, slot):
        src = xs_ref.at[pl.ds(pl.multiple_of(blk * in_rows, in_rows), in_rows), :]
        pltpu.make_async_copy(src, xbuf.at[slot], isem.at[slot]).start()

    def wait_in(slot):
        pltpu.make_async_copy(xs_ref.at[pl.ds(0, in_rows), :], xbuf.at[slot], isem.at[slot]).wait()

    def store(blk, slot):
        dst = y_ref.at[pl.ds(pl.multiple_of(blk * out_rows, out_rows), out_rows), :]
        pltpu.make_async_copy(ybuf.at[slot], dst, osem.at[slot]).start()

    def wait_out(slot):
        pltpu.make_async_copy(ybuf.at[slot], y_ref.at[pl.ds(0, out_rows), :], osem.at[slot]).wait()

    @pl.when(nb > 0)
    def _():
        @pl.when(base == 0)
        def _():
            for g0 in range(EXPERT_AHEAD):
                @pl.when(g0 < n_used)
                def _():
                    fetch(g0, g0 % EXPERT_BUFS)

        wg_sc[...] = wg_ref[0].astype(BF16)
        wu_sc[...] = wu_ref[0].astype(BF16)
        wd_sc[...] = wd_ref[0].astype(BF16)

        def step(g, slot):
            wait_in(slot)

            @pl.when(g + EXPERT_AHEAD < n_used)
            def _():
                fetch(g + EXPERT_AHEAD, (slot + EXPERT_AHEAD) % EXPERT_BUFS)

            @pl.when(g >= EXPERT_BUFS)
            def _():
                wait_out(slot)

            xb = _unpack_rows(xbuf.at[slot], ROW_BLOCK)
            a = jnp.dot(xb, wg_sc[...], preferred_element_type=F32)
            u = jnp.dot(xb, wu_sc[...], preferred_element_type=F32)
            h = (_silu(a) * u).astype(BF16)
            _pack_rows(ybuf.at[slot], jnp.dot(h, wd_sc[...], preferred_element_type=F32))
            store(g, slot)

        first_group = base // EXPERT_BUFS

        def group(p, c):
            for slot in range(EXPERT_BUFS):
                g = (first_group + p) * EXPERT_BUFS + slot

                @pl.when((g >= base) & (g < base + nb))
                def _():
                    step(g, slot)
            return c

        lax.fori_loop(0, (base + nb + EXPERT_BUFS - 1) // EXPERT_BUFS - first_group, group, 0)

    @pl.when(e == pl.num_programs(0) - 1)
    def _():
        for slot in range(EXPERT_BUFS):
            @pl.when(slot < n_used)
            def _():
                wait_out(slot)


def _experts(start_blk, n_blk_e, n_used, xs, w_gate, w_up, w_down):
    n_rows = xs.shape[0] // PACK_SUB
    E, D, H = w_gate.shape
    wsel = lambda e, sb, nb, nu: (e, 0, 0)
    grid_spec = pltpu.PrefetchScalarGridSpec(
        num_scalar_prefetch=3,
        grid=(E,),
        in_specs=[pl.BlockSpec((1, D, H), wsel), pl.BlockSpec((1, D, H), wsel), pl.BlockSpec((1, H, D), wsel),
                  pl.BlockSpec(memory_space=pl.ANY)],
        out_specs=pl.BlockSpec(memory_space=pl.ANY),
        scratch_shapes=[pltpu.VMEM((EXPERT_BUFS, ROW_BLOCK * PACK_SUB, LANES), PACKED),
                        pltpu.VMEM((EXPERT_BUFS, ROW_BLOCK * PACK_SUB, LANES), PACKED),
                        pltpu.VMEM((D, H), BF16), pltpu.VMEM((D, H), BF16), pltpu.VMEM((H, D), BF16),
                        pltpu.SemaphoreType.DMA((EXPERT_BUFS,)), pltpu.SemaphoreType.DMA((EXPERT_BUFS,))],
    )
    return pl.pallas_call(
        _expert_kernel,
        grid_spec=grid_spec,
        out_shape=jax.ShapeDtypeStruct((n_rows * PACK_SUB, LANES), PACKED),
        compiler_params=_cparams(("arbitrary",)),
        name="moe_experts",
    )(start_blk, n_blk_e, n_used, w_gate, w_up, w_down, xs)


def _combine_kernel(dcur_ref, dnxt_ref, x_ref, gate_ref, sh_ref, g_ref, b_ref, y_ref, o_ref, buf, sem, *, tt, alpha):
    i = pl.program_id(0)
    n = pl.num_programs(0)

    def fetch(d_ref, slot):
        def body(j, c):
            for kk in range(TOP_K):
                src = _packed_row(y_ref, d_ref[kk * tt + j])
                pltpu.make_async_copy(src, _packed_row(buf.at[slot, kk], j), sem.at[slot]).start(priority=kk % 2)
            return c

        lax.fori_loop(0, tt, body, 0)

    def reduce(slot):
        for kk in range(TOP_K):
            pltpu.make_async_copy(y_ref.at[pl.ds(0, tt * PACK_SUB), :], buf.at[slot, kk], sem.at[slot]).wait()
        x = _load_rows(x_ref, tt)
        moe = sh_ref[...]
        gates = gate_ref[...]
        for kk in range(TOP_K):
            moe = moe + gates[:, kk:kk + 1] * _unpack_rows(buf.at[slot, kk], tt, F32)
        o_ref[...] = _layer_norm(alpha * x + moe, g_ref[...], b_ref[...])

    @pl.when(i == 0)
    def _():
        fetch(dcur_ref, 0)

    for slot in range(2):
        @pl.when((i % 2 == slot) & (i + 1 < n))
        def _():
            fetch(dnxt_ref, 1 - slot)

    for slot in range(2):
        @pl.when(i % 2 == slot)
        def _():
            reduce(slot)


def _combine(dest, x1t, gates, shared, g, b, y, alpha, tt=128):
    T = x1t.shape[0] // SUBLANES
    D = SUBLANES * LANES
    n = T // tt
    dest = dest.reshape(TOP_K, n, tt).transpose(1, 0, 2).reshape(n * TOP_K * tt)
    row = lambda i: (i, 0)
    fixed = lambda i: (0, 0)
    return pl.pallas_call(
        functools.partial(_combine_kernel, tt=tt, alpha=alpha),
        grid=(n,),
        in_specs=[pl.BlockSpec((TOP_K * tt,), lambda i: (i,), memory_space=pltpu.SMEM),
                  pl.BlockSpec((TOP_K * tt,), lambda i: (jnp.minimum(i + 1, n - 1),), memory_space=pltpu.SMEM),
                  pl.BlockSpec((tt * SUBLANES, LANES), row), pl.BlockSpec((tt, TOP_K), row),
                  pl.BlockSpec((tt, D), row),
                  pl.BlockSpec((1, D), fixed), pl.BlockSpec((1, D), fixed),
                  pl.BlockSpec(memory_space=pl.ANY)],
        out_specs=pl.BlockSpec((tt, D), row),
        out_shape=jax.ShapeDtypeStruct((T, D), F32),
        scratch_shapes=[pltpu.VMEM((2, TOP_K, tt * PACK_SUB, LANES), PACKED), pltpu.SemaphoreType.DMA((2,))],
        compiler_params=_cparams(("arbitrary",)),
        name="moe_combine_ln2",
    )(dest, dest, x1t, gates, shared, g, b, y)


def _layer(x2, mem, B, S, depth, w_in, b_f, w_pool, pool_scale, w_mem_kv, w_out, ln1_g, ln1_b,
           w_router, router_bias, w_gate, w_up, w_down, ws_gate, ws_up, ws_down, ln2_g, ln2_b):
    T, D = x2.shape
    n_win, grp = w_pool.shape[0], w_pool.shape[1]
    pw = n_win * grp
    fw = FOX_HEADS * HEAD_DIM
    mw = w_mem_kv.shape[1] // 2
    E = w_router.shape[1]
    alpha = (2 * depth) ** 0.25

    f_lo = pw + 3 * fw
    w_main = jnp.concatenate([w_in[:, :f_lo], w_in[:, f_lo + FOX_HEADS:]], axis=1).astype(BF16)
    w_f = jnp.pad(w_in[:, f_lo:f_lo + FOX_HEADS], ((0, 0), (0, LANES - FOX_HEADS))).astype(BF16)
    bf_pad = jnp.pad(b_f, (0, LANES - FOX_HEADS)).reshape(1, LANES)
    wbd = jnp.zeros((pw, pw), F32)
    for g in range(n_win):
        wbd = wbd.at[g * grp:(g + 1) * grp, g * grp:(g + 1) * grp].set(w_pool[g])
    wrt = w_router.T
    wrt_hi = wrt.astype(BF16)
    wrt_lo = (wrt - wrt_hi.astype(F32)).astype(BF16)

    u_pool, q, k, v, q_mem, f_logit = _inproj(x2, w_main, w_f, pw, fw, mw)
    y_pool = _pool(u_pool, wbd.astype(BF16), pool_scale.reshape(1, pw), B, S)
    fcum = _fgate(f_logit, bf_pad, B, S)
    y_fox = _fox(q, k, v, fcum, B, S)
    y_mem = _memattn(q_mem, mem, w_mem_kv.astype(BF16), B, S)
    x1, x1_packed = _outproj(y_pool, y_fox, y_mem, x2, w_out.astype(BF16), ln1_g.reshape(1, D),
                             ln1_b.reshape(1, D), alpha)

    eidx, gates, rank, counts = _router(x1, wrt_hi, wrt_lo, _per_expert_column(router_bias))

    cnt = counts[:, 0].astype(I32)
    padded = (cnt + ROW_BLOCK - 1) // ROW_BLOCK * ROW_BLOCK
    pad_end = jnp.cumsum(padded)
    start_pad = (pad_end - padded).astype(I32)
    n_rows = (T * TOP_K + E * (ROW_BLOCK - 1)) // ROW_BLOCK * ROW_BLOCK
    n_used = (pad_end[-1] // ROW_BLOCK).astype(I32)
    zero_blk = jnp.where(padded > 0, pad_end - ROW_BLOCK, -1).astype(I32)
    dest = _dest(eidx, rank, _per_expert_column(start_pad))

    xs, shared = _dispatch(zero_blk, dest, x1_packed, ws_gate.astype(BF16), ws_up.astype(BF16),
                           ws_down.astype(BF16), n_rows)
    y = _experts(start_pad // ROW_BLOCK, (padded // ROW_BLOCK).astype(I32), n_used.reshape(1), xs,
                 w_gate, w_up, w_down)
    return _combine(dest, x1, gates.T, shared, ln2_g.reshape(1, D), ln2_b.reshape(1, D), y, alpha)


def kernel(x, mem, w_in, b_f, w_pool, pool_scale, w_mem_kv, w_out, ln1_g, ln1_b, w_router, router_bias,
           w_gate, w_up, w_down, ws_gate, ws_up, ws_down, ln2_g, ln2_b):
    B, S, D = x.shape
    depth = w_in.shape[0]
    x2 = x.reshape(B * S, D)
    for l in range(depth):
        x2 = _layer(x2, mem, B, S, depth, w_in[l], b_f[l], w_pool[l], pool_scale[l], w_mem_kv[l], w_out[l],
                    ln1_g[l], ln1_b[l], w_router[l], router_bias[l], w_gate[l], w_up[l], w_down[l],
                    ws_gate[l], ws_up[l], ws_down[l], ln2_g[l], ln2_b[l])
    return x2.reshape(B, S, D)
```

```python
import functools

import jax
import jax.numpy as jnp
from jax import lax
from jax.experimental import pallas as pl
from jax.experimental.pallas import tpu as pltpu

F32 = jnp.float32
BF16 = jnp.bfloat16
I32 = jnp.int32
PACKED = jnp.int32

LANES = 128
SUBLANES = 8
POOL_WINDOWS = (2, 4, 8, 16)
POOL_HALO = 16
HEAD_DIM = 64
FOX_HEADS = 8
BIAS_TERMS = 3
MEM_HEADS = 4
TOP_K = 8
N_GROUPS = 8
TOPK_GROUPS = 4
ROUTED_SCALE = 2.5
LN_EPS = 1e-5
LOG2E = 1.4426950408889634
ROW_BLOCK = 256
EXPERT_BUFS = 6
EXPERT_AHEAD = EXPERT_BUFS - 1
VMEM_LIMIT = 48 * 1024 * 1024


def _cparams(sem):
    return pltpu.CompilerParams(dimension_semantics=sem, vmem_limit_bytes=VMEM_LIMIT)


def _layer_norm(z, g, b):
    mu = jnp.mean(z, axis=-1, keepdims=True)
    zc = z - mu
    var = jnp.mean(zc * zc, axis=-1, keepdims=True)
    return zc * lax.rsqrt(var + LN_EPS) * g + b


def _silu(x):
    return x * (1.0 / (1.0 + jnp.exp(-x)))


def _load_rows(ref, n):
    return jnp.concatenate([ref[pl.ds(j, n, stride=SUBLANES), :] for j in range(SUBLANES)], axis=1)


def _store_rows(ref, val):
    n = val.shape[0]
    for j in range(SUBLANES):
        ref[pl.ds(j, n, stride=SUBLANES), :] = val[:, j * LANES:(j + 1) * LANES]


def _tile_rows(ref, r):
    return ref.at[pl.ds(pl.multiple_of(r * SUBLANES, SUBLANES), SUBLANES), :]


def _inproj_kernel(x_ref, w_ref, wf_ref, up_ref, q_ref, k_ref, v_ref, qm_ref, f_ref, *, pw, fw, mw, scale):
    xb = x_ref[...].astype(BF16)

    def proj(lo, hi):
        return jnp.dot(xb, w_ref[:, lo:hi], preferred_element_type=F32)

    up_ref[...] = proj(0, pw)
    q_ref[...] = (proj(pw, pw + fw) * (scale * LOG2E)).astype(BF16)
    k_ref[...] = proj(pw + fw, pw + 2 * fw).astype(BF16)
    v_ref[...] = proj(pw + 2 * fw, pw + 3 * fw).astype(BF16)
    qm_ref[...] = (proj(pw + 3 * fw, pw + 3 * fw + mw) * scale).astype(BF16)
    f_ref[...] = jnp.dot(xb, wf_ref[...], preferred_element_type=F32)


def _inproj(x2, w_main, w_f, pw, fw, mw, tm=512):
    T, D = x2.shape
    kern = functools.partial(_inproj_kernel, pw=pw, fw=fw, mw=mw, scale=HEAD_DIM ** -0.5)
    row = lambda i: (i, 0)
    fixed = lambda i: (0, 0)
    return pl.pallas_call(
        kern,
        grid=(T // tm,),
        in_specs=[pl.BlockSpec((tm, D), row),
                  pl.BlockSpec(w_main.shape, fixed),
                  pl.BlockSpec(w_f.shape, fixed)],
        out_specs=[pl.BlockSpec((tm, pw), row), pl.BlockSpec((tm, fw), row), pl.BlockSpec((tm, fw), row),
                   pl.BlockSpec((tm, fw), row), pl.BlockSpec((tm, mw), row), pl.BlockSpec((tm, LANES), row)],
        out_shape=[jax.ShapeDtypeStruct((T, pw), F32), jax.ShapeDtypeStruct((T, fw), BF16),
                   jax.ShapeDtypeStruct((T, fw), BF16), jax.ShapeDtypeStruct((T, fw), BF16),
                   jax.ShapeDtypeStruct((T, mw), BF16), jax.ShapeDtypeStruct((T, LANES), F32)],
        compiler_params=_cparams(("parallel",)),
        name="inproj",
    )(x2, w_main, w_f)


def _pool_kernel(u_ref, wbd_ref, sc_ref, o_ref, ext_sc, *, chunk, group):
    S, W = u_ref.shape
    ext_sc[0:POOL_HALO, :] = jnp.zeros((POOL_HALO, W), F32)
    ext_sc[POOL_HALO:, :] = u_ref[...]
    rows = chunk + POOL_HALO
    lrow = lax.broadcasted_iota(I32, (rows, W), 0)
    lane = lax.broadcasted_iota(I32, (rows, W), 1)

    def body(c, carry):
        start = pl.multiple_of(c * chunk, chunk)
        e = ext_sc[pl.ds(start, rows), :]
        posf = (lrow + (start - POOL_HALO + 1)).astype(F32)
        acc = e
        d = jnp.zeros_like(e)
        shift = 1
        for g, w in enumerate(POOL_WINDOWS):
            while shift < w:
                acc = acc + pltpu.roll(acc, shift, axis=0)
                shift *= 2
            mean = acc / jnp.minimum(posf, float(w))
            d = jnp.where((lane >= g * group) & (lane < (g + 1) * group), mean, d)
        d = (d - e)[POOL_HALO:, :]
        y = jnp.dot(d.astype(BF16), wbd_ref[...], preferred_element_type=F32) * sc_ref[...]
        o_ref[pl.ds(start, chunk), :] = y.astype(BF16)
        return carry

    lax.fori_loop(0, S // chunk, body, 0)


def _pool(u, wbd, pscale, B, S, chunk=512):
    T, W = u.shape
    kern = functools.partial(_pool_kernel, chunk=chunk, group=W // len(POOL_WINDOWS))
    return pl.pallas_call(
        kern,
        grid=(B,),
        in_specs=[pl.BlockSpec((S, W), lambda b: (b, 0)),
                  pl.BlockSpec((W, W), lambda b: (0, 0)),
                  pl.BlockSpec((1, W), lambda b: (0, 0))],
        out_specs=pl.BlockSpec((S, W), lambda b: (b, 0)),
        out_shape=jax.ShapeDtypeStruct((T, W), BF16),
        scratch_shapes=[pltpu.VMEM((S + POOL_HALO, W), F32)],
        compiler_params=_cparams(("parallel",)),
        name="pool_mixer",
    )(u, wbd, pscale)


def _fgate_kernel(f_ref, bf_ref, o_ref):
    S = f_ref.shape[0]
    z = f_ref[...] + bf_ref[...]
    x = jnp.minimum(z, 0.0) - jnp.log(1.0 + jnp.exp(-jnp.abs(z)))
    row = lax.broadcasted_iota(I32, x.shape, 0)
    lane = lax.broadcasted_iota(I32, x.shape, 1)
    shift = 1
    while shift < S:
        x = x + jnp.where(row >= shift, pltpu.roll(x, shift, axis=0), 0.0)
        shift *= 2
    x = x * LOG2E
    hi = x.astype(BF16).astype(F32)
    mid = (x - hi).astype(BF16).astype(F32)
    lo = (x - hi) - mid
    for p in range(FOX_HEADS // 2):
        out = jnp.zeros(x.shape, F32)
        for hh in range(2):
            c = 2 * p + hh
            for j, part in enumerate((hi, mid, lo)):
                out = jnp.where(lane == BIAS_TERMS * hh + j, -part[:, c:c + 1], out)
        o_ref[0, p] = out.astype(BF16)


def _fgate(f_logit, bf_pad, B, S):
    return pl.pallas_call(
        _fgate_kernel,
        grid=(B,),
        in_specs=[pl.BlockSpec((S, LANES), lambda b: (b, 0)),
                  pl.BlockSpec((1, LANES), lambda b: (0, 0))],
        out_specs=pl.BlockSpec((1, FOX_HEADS // 2, S, LANES), lambda b: (b, 0, 0, 0)),
        out_shape=jax.ShapeDtypeStruct((B, FOX_HEADS // 2, S, LANES), BF16),
        compiler_params=_cparams(("parallel",)),
        name="forget_cumsum",
    )(f_logit, bf_pad)


def _fox_kernel(q_ref, k_ref, a_ref, v_ref, o_ref, vt_sc, qa_sc, sa_sc, sb_sc, m_sc, l_sc, acc_sc, *, blk):
    qi = pl.program_id(2)
    nk = vt_sc.shape[0]
    nt = (((1,), (1,)), ((), ()))

    @pl.when(qi == 0)
    def _():
        for c in range(nk):
            vt_sc[c] = v_ref[c * blk:(c + 1) * blk, :].astype(F32).T.astype(BF16)

    q = q_ref[...].astype(F32)
    lane = lax.broadcasted_iota(I32, q.shape, 1)
    for h in range(2):
        mine = (lane >= h * HEAD_DIM) & (lane < (h + 1) * HEAD_DIM)
        bias_rows = (lane >= h * BIAS_TERMS) & (lane < (h + 1) * BIAS_TERMS)
        qa_sc[h] = jnp.concatenate([jnp.where(mine, q, 0.0), jnp.where(bias_rows, 1.0, 0.0)], axis=1).astype(BF16)
    m_sc[...] = jnp.full(m_sc.shape, -jnp.inf, F32)
    l_sc[...] = jnp.zeros(l_sc.shape, F32)
    acc_sc[...] = jnp.zeros(acc_sc.shape, F32)

    def scores(ki, dst):
        ks = pl.multiple_of(ki * blk, blk)
        kb = jnp.concatenate([k_ref[pl.ds(ks, blk), :], a_ref[0, 0, pl.ds(ks, blk), :]], axis=1)
        for h in range(2):
            dst[h] = lax.dot_general(kb, qa_sc[h], nt, preferred_element_type=F32)

    def absorb(ki, src, causal):
        for h in range(2):
            st = src[h]
            if causal:
                kpos = lax.broadcasted_iota(I32, st.shape, 0)
                qpos = lax.broadcasted_iota(I32, st.shape, 1)
                st = jnp.where(kpos <= qpos, st, -jnp.inf)
            m_prev = m_sc[h]
            m_new = jnp.maximum(m_prev, jnp.max(st, axis=0, keepdims=True))
            alpha = jnp.exp2(m_prev - m_new)
            p = jnp.exp2(st - m_new)
            l_sc[h] = alpha * l_sc[h] + jnp.sum(p, axis=0, keepdims=True)
            m_sc[h] = m_new
            vt = vt_sc[ki, h * HEAD_DIM:(h + 1) * HEAD_DIM, :]
            pv = jnp.dot(vt, p.astype(BF16), preferred_element_type=F32)
            rows = slice(h * HEAD_DIM, (h + 1) * HEAD_DIM)
            acc_sc[rows, :] = acc_sc[rows, :] * alpha + pv

    scores(0, sa_sc)

    def body(p, carry):
        scores(2 * p + 1, sb_sc)
        absorb(2 * p, sa_sc, False)
        scores(2 * p + 2, sa_sc)
        absorb(2 * p + 1, sb_sc, False)
        return carry

    lax.fori_loop(0, qi // 2, body, 0)

    @pl.when(qi % 2 == 1)
    def _():
        scores(qi, sb_sc)
        absorb(qi - 1, sa_sc, False)
        absorb(qi, sb_sc, True)

    @pl.when(qi % 2 == 0)
    def _():
        absorb(qi, sa_sc, True)

    out_t = jnp.concatenate([acc_sc[h * HEAD_DIM:(h + 1) * HEAD_DIM, :] / l_sc[h] for h in range(2)], axis=0)
    o_ref[...] = out_t.T.astype(BF16)


def _fox(q, k, v, fbias, B, S, blk=512):
    T, FW = q.shape
    nq = S // blk
    pairs = FW // LANES
    kern = functools.partial(_fox_kernel, blk=blk)
    return pl.pallas_call(
        kern,
        grid=(B, pairs, nq),
        in_specs=[pl.BlockSpec((blk, LANES), lambda b, p, i: (b * nq + i, p)),
                  pl.BlockSpec((S, LANES), lambda b, p, i: (b, p)),
                  pl.BlockSpec((1, 1, S, LANES), lambda b, p, i: (b, p, 0, 0)),
                  pl.BlockSpec((S, LANES), lambda b, p, i: (b, p))],
        out_specs=pl.BlockSpec((blk, LANES), lambda b, p, i: (b * nq + i, p)),
        out_shape=jax.ShapeDtypeStruct((T, FW), BF16),
        scratch_shapes=[pltpu.VMEM((nq, LANES, blk), BF16), pltpu.VMEM((2, blk, 2 * LANES), BF16),
                        pltpu.VMEM((2, blk, blk), F32), pltpu.VMEM((2, blk, blk), F32),
                        pltpu.VMEM((2, 1, blk), F32), pltpu.VMEM((2, 1, blk), F32), pltpu.VMEM((LANES, blk), F32)],
        compiler_params=_cparams(("parallel", "parallel", "arbitrary")),
        name="fox_attention",
    )(q, k, fbias, v)


def _memattn_kernel(qm_ref, mem_ref, wkv_ref, o_ref, k_sc, v_sc):
    MW = qm_ref.shape[1]

    @pl.when(pl.program_id(1) == 0)
    def _():
        kv = jnp.dot(mem_ref[0].astype(BF16), wkv_ref[...], preferred_element_type=F32)
        k_sc[...] = kv[:, :MW].astype(BF16)
        v_sc[...] = kv[:, MW:].astype(BF16)

    q = qm_ref[...]
    lane = lax.broadcasted_iota(I32, q.shape, 1)
    hd = MW // MEM_HEADS
    out = jnp.zeros(q.shape, F32)
    for h in range(MEM_HEADS):
        mine = (lane >= h * hd) & (lane < (h + 1) * hd)
        qh = jnp.where(mine, q, jnp.zeros_like(q))
        s = lax.dot_general(qh, k_sc[...], (((1,), (1,)), ((), ())), preferred_element_type=F32)
        p = jnp.exp(s - jnp.max(s, axis=1, keepdims=True))
        l = jnp.sum(p, axis=1, keepdims=True)
        o = jnp.dot(p.astype(BF16), v_sc[...], preferred_element_type=F32)
        out = jnp.where(mine, o / l, out)
    o_ref[...] = out.astype(BF16)


def _memattn(qm, mem, wkv, B, S, tq=512):
    T, MW = qm.shape
    M, D = mem.shape[1], mem.shape[2]
    nq = S // tq
    return pl.pallas_call(
        _memattn_kernel,
        grid=(B, nq),
        in_specs=[pl.BlockSpec((tq, MW), lambda b, i: (b * nq + i, 0)),
                  pl.BlockSpec((1, M, D), lambda b, i: (b, 0, 0)),
                  pl.BlockSpec(wkv.shape, lambda b, i: (0, 0))],
        out_specs=pl.BlockSpec((tq, MW), lambda b, i: (b * nq + i, 0)),
        out_shape=jax.ShapeDtypeStruct((T, MW), BF16),
        scratch_shapes=[pltpu.VMEM((M, MW), BF16), pltpu.VMEM((M, MW), BF16)],
        compiler_params=_cparams(("parallel", "arbitrary")),
        name="memory_attention",
    )(qm, mem, wkv)


PACK_SUB = 4


def _pack_rows(ref, val):
    n = val.shape[0]
    half = val.shape[1] // 2
    words = pltpu.pack_elementwise([val[:, :half], val[:, half:]], packed_dtype=BF16)
    for j in range(PACK_SUB):
        ref[pl.ds(j, n, stride=PACK_SUB), :] = words[:, j * LANES:(j + 1) * LANES]


def _unpack_rows(ref, n, dtype=BF16):
    slabs = [ref[pl.ds(j, n, stride=PACK_SUB), :] for j in range(PACK_SUB)]
    lo = [pltpu.unpack_elementwise(s, index=0, packed_dtype=BF16, unpacked_dtype=F32) for s in slabs]
    hi = [pltpu.unpack_elementwise(s, index=1, packed_dtype=BF16, unpacked_dtype=F32) for s in slabs]
    return jnp.concatenate(lo + hi, axis=1).astype(dtype)


def _packed_row(ref, r):
    return ref.at[pl.ds(pl.multiple_of(r * PACK_SUB, PACK_SUB), PACK_SUB), :]


def _outproj_kernel(yp_ref, yf_ref, ym_ref, x_ref, w_ref, g_ref, b_ref, o_ref, pk_ref, *, alpha):
    pw, fw = yp_ref.shape[1], yf_ref.shape[1]
    h = jnp.dot(yp_ref[...], w_ref[0:pw, :], preferred_element_type=F32)
    h = h + jnp.dot(yf_ref[...], w_ref[pw:pw + fw, :], preferred_element_type=F32)
    h = h + jnp.dot(ym_ref[...], w_ref[pw + fw:, :], preferred_element_type=F32)
    y = _layer_norm(alpha * x_ref[...] + h, g_ref[...], b_ref[...])
    _store_rows(o_ref, y)
    _pack_rows(pk_ref, y)


def _outproj(yp, yf, ym, x2, w_out, g, b, alpha, tm=512):
    T, D = x2.shape
    assert D == SUBLANES * LANES == 2 * PACK_SUB * LANES
    row = lambda i: (i, 0)
    fixed = lambda i: (0, 0)
    return pl.pallas_call(
        functools.partial(_outproj_kernel, alpha=alpha),
        grid=(T // tm,),
        in_specs=[pl.BlockSpec((tm, yp.shape[1]), row), pl.BlockSpec((tm, yf.shape[1]), row),
                  pl.BlockSpec((tm, ym.shape[1]), row), pl.BlockSpec((tm, D), row),
                  pl.BlockSpec(w_out.shape, fixed), pl.BlockSpec((1, D), fixed), pl.BlockSpec((1, D), fixed)],
        out_specs=[pl.BlockSpec((tm * SUBLANES, LANES), row),
                   pl.BlockSpec((tm * PACK_SUB, LANES), row)],
        out_shape=[jax.ShapeDtypeStruct((T * SUBLANES, LANES), F32),
                   jax.ShapeDtypeStruct((T * PACK_SUB, LANES), PACKED)],
        compiler_params=_cparams(("parallel",)),
        name="outproj_ln1",
    )(yp, yf, ym, x2, w_out, g, b)


ROUTE_TILE = 512


def _per_expert_column(v, dtype=F32):
    return jnp.broadcast_to(v.astype(dtype)[:, None], (v.shape[0], ROUTE_TILE))


def _router_kernel(x_ref, wh_ref, wl_ref, bias_ref, eidx_ref, gate_ref, rank_ref, cnt_ref, carry_sc, *, tt):
    E = wh_ref.shape[0]
    gsz = E // N_GROUPS
    ninf = -jnp.inf

    @pl.when(pl.program_id(0) == 0)
    def _():
        carry_sc[...] = jnp.zeros(carry_sc.shape, F32)

    x = _load_rows(x_ref, tt)
    xh = x.astype(BF16)
    xl = (x - xh.astype(F32)).astype(BF16)
    wh = wh_ref[...]
    nt = (((1,), (1,)), ((), ()))
    logits = lax.dot_general(wh, xh, nt, preferred_element_type=F32) + (
        lax.dot_general(wl_ref[...], xh, nt, preferred_element_type=F32)
        + lax.dot_general(wh, xl, nt, preferred_element_type=F32))
    scores = 1.0 / (1.0 + jnp.exp(-logits))
    biased = scores + bias_ref[...]
    eiota = lax.broadcasted_iota(I32, (E, tt), 0).astype(F32)
    giota = lax.broadcasted_iota(I32, (gsz, tt), 0).astype(F32)

    def cmax(a):
        return jnp.max(a, axis=0, keepdims=True)

    def first_at(a, m, iota, n):
        return jnp.min(jnp.where(a == m, iota, float(n)), axis=0, keepdims=True)

    groups = [biased[g * gsz:(g + 1) * gsz, :] for g in range(N_GROUPS)]
    gscore = []
    for blk in groups:
        m1 = cmax(blk)
        m2 = cmax(jnp.where(giota == first_at(blk, m1, giota, gsz), ninf, blk))
        gscore.append(m1 + m2)

    kept = []
    for g in range(N_GROUPS):
        ahead = jnp.zeros((1, tt), F32)
        for o in range(N_GROUPS):
            if o == g:
                continue
            beats = gscore[o] > gscore[g]
            if o < g:
                beats = beats | (gscore[o] == gscore[g])
            ahead = ahead + jnp.where(beats, 1.0, 0.0)
        kept.append(groups[g] + jnp.where(ahead < float(TOPK_GROUPS), 0.0, ninf))
    masked = jnp.concatenate(kept, axis=0)

    picks, sels = [], []
    chosen = jnp.zeros((E, tt), F32)
    for _ in range(TOP_K):
        ik = first_at(masked, cmax(masked), eiota, E)
        oh = eiota == ik
        picks.append(ik)
        sels.append(jnp.sum(jnp.where(oh, scores, 0.0), axis=0, keepdims=True))
        chosen = jnp.where(oh, 1.0, chosen)
        masked = jnp.where(oh, ninf, masked)
    denom = sels[0]
    for sk in sels[1:]:
        denom = denom + sk

    r = lax.broadcasted_iota(I32, (tt, tt), 0)
    c = lax.broadcasted_iota(I32, (tt, tt), 1)
    earlier = jnp.where(r < c, 1.0, 0.0).astype(BF16)
    chosen_b = chosen.astype(BF16)
    pos = carry_sc[...] + jnp.dot(chosen_b, earlier, preferred_element_type=F32)
    carry_sc[...] = carry_sc[...] + jnp.dot(chosen_b, jnp.ones((tt, tt), BF16), preferred_element_type=F32)
    cnt_ref[...] = carry_sc[...]

    ranks = [jnp.sum(jnp.where(eiota == ik, pos, 0.0), axis=0, keepdims=True) for ik in picks]
    eidx_ref[...] = jnp.concatenate(picks, axis=0).astype(I32)
    gate_ref[...] = jnp.concatenate([sk / denom * ROUTED_SCALE for sk in sels], axis=0)
    rank_ref[...] = jnp.concatenate(ranks, axis=0).astype(I32)


def _router(x1t, wrt_hi, wrt_lo, rbias):
    T = x1t.shape[0] // SUBLANES
    E, D = wrt_hi.shape
    tt = ROUTE_TILE
    tok = lambda i: (0, i)
    fixed = lambda i: (0, 0)
    return pl.pallas_call(
        functools.partial(_router_kernel, tt=tt),
        grid=(T // tt,),
        in_specs=[pl.BlockSpec((tt * SUBLANES, LANES), lambda i: (i, 0)), pl.BlockSpec((E, D), fixed),
                  pl.BlockSpec((E, D), fixed), pl.BlockSpec((E, tt), fixed)],
        out_specs=[pl.BlockSpec((TOP_K, tt), tok), pl.BlockSpec((TOP_K, tt), tok), pl.BlockSpec((TOP_K, tt), tok),
                   pl.BlockSpec((E, tt), fixed)],
        out_shape=[jax.ShapeDtypeStruct((TOP_K, T), I32), jax.ShapeDtypeStruct((TOP_K, T), F32),
                   jax.ShapeDtypeStruct((TOP_K, T), I32), jax.ShapeDtypeStruct((E, tt), F32)],
        scratch_shapes=[pltpu.VMEM((E, tt), F32)],
        compiler_params=_cparams(("arbitrary",)),
        name="router_topk",
    )(x1t, wrt_hi, wrt_lo, rbias)


def _dest_kernel(e_ref, r_ref, sp_ref, d_ref):
    E = sp_ref.shape[0]
    tt = e_ref.shape[1]
    eiota = lax.broadcasted_iota(I32, (E, tt), 0)
    sp = sp_ref[...]
    rows = []
    for kk in range(TOP_K):
        start = jnp.sum(jnp.where(eiota == e_ref[kk:kk + 1, :], sp, 0.0), axis=0, keepdims=True)
        rows.append(start.astype(I32) + r_ref[kk:kk + 1, :])
    d_ref[...] = jnp.concatenate(rows, axis=0)


def _dest(eidx, rank, start_pad_b):
    K, T = eidx.shape
    tt = ROUTE_TILE
    tok = lambda i: (0, i)
    return pl.pallas_call(
        _dest_kernel,
        grid=(T // tt,),
        in_specs=[pl.BlockSpec((K, tt), tok), pl.BlockSpec((K, tt), tok),
                  pl.BlockSpec(start_pad_b.shape, lambda i: (0, 0))],
        out_specs=pl.BlockSpec((K, tt), tok),
        out_shape=jax.ShapeDtypeStruct((K, T), I32),
        compiler_params=_cparams(("parallel",)),
        name="moe_dest",
    )(eidx, rank, start_pad_b)


def _dispatch_kernel(zb_ref, d_ref, x_ref, wsg_ref, wsu_ref, wsd_ref, xs_ref, sh_ref, zero_sc, sem, *, tt, n_exp):
    blk_rows = ROW_BLOCK * PACK_SUB

    @pl.when(pl.program_id(0) == 0)
    def _():
        zero_sc[...] = jnp.zeros(zero_sc.shape, PACKED)

        def zstart(e, c):
            @pl.when(zb_ref[e] >= 0)
            def _():
                dst = xs_ref.at[pl.ds(pl.multiple_of(zb_ref[e] * PACK_SUB, blk_rows), blk_rows), :]
                pltpu.make_async_copy(zero_sc, dst, sem).start()
            return c

        def zwait(e, c):
            @pl.when(zb_ref[e] >= 0)
            def _():
                pltpu.make_async_copy(zero_sc, xs_ref.at[pl.ds(0, blk_rows), :], sem).wait()
            return c

        lax.fori_loop(0, n_exp, zstart, 0)
        lax.fori_loop(0, n_exp, zwait, 0)

    def start(j, c):
        src = _packed_row(x_ref, j)
        for kk in range(TOP_K):
            pltpu.make_async_copy(src, _packed_row(xs_ref, d_ref[kk, j]), sem).start(priority=kk % 2)
        return c

    lax.fori_loop(0, tt, start, 0)
    xb = _unpack_rows(x_ref, tt)
    hs = _silu(jnp.dot(xb, wsg_ref[...], preferred_element_type=F32)) * jnp.dot(
        xb, wsu_ref[...], preferred_element_type=F32)
    sh_ref[...] = jnp.dot(hs.astype(BF16), wsd_ref[...], preferred_element_type=F32)
    for kk in range(TOP_K):
        pltpu.make_async_copy(x_ref, xs_ref.at[pl.ds(0, tt * PACK_SUB), :], sem).wait()


def _dispatch(zero_blk, dest, xpk, wsg, wsu, wsd, n_rows, tt=1024):
    T = xpk.shape[0] // PACK_SUB
    D = wsg.shape[0]
    n_exp = zero_blk.shape[0]
    fixed = lambda i, zb: (0, 0)
    grid_spec = pltpu.PrefetchScalarGridSpec(
        num_scalar_prefetch=1,
        grid=(T // tt,),
        in_specs=[pl.BlockSpec((TOP_K, tt), lambda i, zb: (0, i), memory_space=pltpu.SMEM),
                  pl.BlockSpec((tt * PACK_SUB, LANES), lambda i, zb: (i, 0)),
                  pl.BlockSpec(wsg.shape, fixed), pl.BlockSpec(wsu.shape, fixed), pl.BlockSpec(wsd.shape, fixed)],
        out_specs=[pl.BlockSpec(memory_space=pl.ANY), pl.BlockSpec((tt, D), lambda i, zb: (i, 0))],
        scratch_shapes=[pltpu.VMEM((ROW_BLOCK * PACK_SUB, LANES), PACKED), pltpu.SemaphoreType.DMA],
    )
    return pl.pallas_call(
        functools.partial(_dispatch_kernel, tt=tt, n_exp=n_exp),
        grid_spec=grid_spec,
        out_shape=[jax.ShapeDtypeStruct((n_rows * PACK_SUB, LANES), PACKED), jax.ShapeDtypeStruct((T, D), F32)],
        compiler_params=_cparams(("arbitrary",)),
        name="moe_dispatch",
    )(zero_blk, dest, xpk, wsg, wsu, wsd)


def _expert_kernel(sb_ref, nb_ref, nu_ref, wg_ref, wu_ref, wd_ref, xs_ref, y_ref,
                   xbuf, ybuf, wg_sc, wu_sc, wd_sc, isem, osem):
    e = pl.program_id(0)
    nb = nb_ref[e]
    base = sb_ref[e]
    n_used = nu_ref[0]
    in_rows = out_rows = ROW_BLOCK * PACK_SUB

    def fetch(blk, slot):
        src = xs_ref.at[pl.ds(pl.multiple_of(blk * in_rows, in_rows), in_rows), :]
        pltpu.make_async_copy(src, xbuf.at[slot], isem.at[slot]).start()

    def wait_in(slot):
        pltpu.make_async_copy(xs_ref.at[pl.ds(0, in_rows), :], xbuf.at[slot], isem.at[slot]).wait()

    def store(blk, slot):
        dst = y_ref.at[pl.ds(pl.multiple_of(blk * out_rows, out_rows), out_rows), :]
        pltpu.make_async_copy(ybuf.at[slot], dst, osem.at[slot]).start()

    def wait_out(slot):
        pltpu.make_async_copy(ybuf.at[slot], y_ref.at[pl.ds(0, out_rows), :], osem.at[slot]).wait()

    @pl.when(nb > 0)
    def _():
        @pl.when(base == 0)
        def _():
            for g0 in range(EXPERT_AHEAD):
                @pl.when(g0 < n_used)
                def _():
                    fetch(g0, g0 % EXPERT_BUFS)

        wg_sc[...] = wg_ref[0].astype(BF16)
        wu_sc[...] = wu_ref[0].astype(BF16)
        wd_sc[...] = wd_ref[0].astype(BF16)

        def step(g, slot):
            wait_in(slot)

            @pl.when(g + EXPERT_AHEAD < n_used)
            def _():
                fetch(g + EXPERT_AHEAD, (slot + EXPERT_AHEAD) % EXPERT_BUFS)

            @pl.when(g >= EXPERT_BUFS)
            def _():
                wait_out(slot)

            xb = _unpack_rows(xbuf.at[slot], ROW_BLOCK)
            a = jnp.dot(xb, wg_sc[...], preferred_element_type=F32)
            u = jnp.dot(xb, wu_sc[...], preferred_element_type=F32)
            h = (_silu(a) * u).astype(BF16)
            _pack_rows(ybuf.at[slot], jnp.dot(h, wd_sc[...], preferred_element_type=F32))
            store(g, slot)

        first_group = base // EXPERT_BUFS

        def group(p, c):
            for slot in range(EXPERT_BUFS):
                g = (first_group + p) * EXPERT_BUFS + slot

                @pl.when((g >= base) & (g < base + nb))
                def _():
                    step(g, slot)
            return c

        lax.fori_loop(0, (base + nb + EXPERT_BUFS - 1) // EXPERT_BUFS - first_group, group, 0)

    @pl.when(e == pl.num_programs(0) - 1)
    def _():
        for slot in range(EXPERT_BUFS):
            @pl.when(slot < n_used)
            def _():
                wait_out(slot)


def _experts(start_blk, n_blk_e, n_used, xs, w_gate, w_up, w_down):
    n_rows = xs.shape[0] // PACK_SUB
    E, D, H = w_gate.shape
    wsel = lambda e, sb, nb, nu: (e, 0, 0)
    grid_spec = pltpu.PrefetchScalarGridSpec(
        num_scalar_prefetch=3,
        grid=(E,),
        in_specs=[pl.BlockSpec((1, D, H), wsel), pl.BlockSpec((1, D, H), wsel), pl.BlockSpec((1, H, D), wsel),
                  pl.BlockSpec(memory_space=pl.ANY)],
        out_specs=pl.BlockSpec(memory_space=pl.ANY),
        scratch_shapes=[pltpu.VMEM((EXPERT_BUFS, ROW_BLOCK * PACK_SUB, LANES), PACKED),
                        pltpu.VMEM((EXPERT_BUFS, ROW_BLOCK * PACK_SUB, LANES), PACKED),
                        pltpu.VMEM((D, H), BF16), pltpu.VMEM((D, H), BF16), pltpu.VMEM((H, D), BF16),
                        pltpu.SemaphoreType.DMA((EXPERT_BUFS,)), pltpu.SemaphoreType.DMA((EXPERT_BUFS,))],
    )
    return pl.pallas_call(
        _expert_kernel,
        grid_spec=grid_spec,
        out_shape=jax.ShapeDtypeStruct((n_rows * PACK_SUB, LANES), PACKED),
        compiler_params=_cparams(("arbitrary",)),
        name="moe_experts",
    )(start_blk, n_blk_e, n_used, w_gate, w_up, w_down, xs)


def _combine_kernel(dcur_ref, dnxt_ref, x_ref, gate_ref, sh_ref, g_ref, b_ref, y_ref, o_ref, buf, sem, *, tt, alpha):
    i = pl.program_id(0)
    n = pl.num_programs(0)

    def fetch(d_ref, slot):
        def body(j, c):
            for kk in range(TOP_K):
                src = _packed_row(y_ref, d_ref[kk * tt + j])
                pltpu.make_async_copy(src, _packed_row(buf.at[slot, kk], j), sem.at[slot]).start(priority=kk % 2)
            return c

        lax.fori_loop(0, tt, body, 0)

    def reduce(slot):
        for kk in range(TOP_K):
            pltpu.make_async_copy(y_ref.at[pl.ds(0, tt * PACK_SUB), :], buf.at[slot, kk], sem.at[slot]).wait()
        x = _load_rows(x_ref, tt)
        moe = sh_ref[...]
        gates = gate_ref[...]
        for kk in range(TOP_K):
            moe = moe + gates[:, kk:kk + 1] * _unpack_rows(buf.at[slot, kk], tt, F32)
        o_ref[...] = _layer_norm(alpha * x + moe, g_ref[...], b_ref[...])

    @pl.when(i == 0)
    def _():
        fetch(dcur_ref, 0)

    for slot in range(2):
        @pl.when((i % 2 == slot) & (i + 1 < n))
        def _():
            fetch(dnxt_ref, 1 - slot)

    for slot in range(2):
        @pl.when(i % 2 == slot)
        def _():
            reduce(slot)


def _combine(dest, x1t, gates, shared, g, b, y, alpha, tt=256):
    T = x1t.shape[0] // SUBLANES
    D = SUBLANES * LANES
    n = T // tt
    dest = dest.reshape(TOP_K, n, tt).transpose(1, 0, 2).reshape(n * TOP_K * tt)
    row = lambda i: (i, 0)
    fixed = lambda i: (0, 0)
    return pl.pallas_call(
        functools.partial(_combine_kernel, tt=tt, alpha=alpha),
        grid=(n,),
        in_specs=[pl.BlockSpec((TOP_K * tt,), lambda i: (i,), memory_space=pltpu.SMEM),
                  pl.BlockSpec((TOP_K * tt,), lambda i: (jnp.minimum(i + 1, n - 1),), memory_space=pltpu.SMEM),
                  pl.BlockSpec((tt * SUBLANES, LANES), row), pl.BlockSpec((tt, TOP_K), row),
                  pl.BlockSpec((tt, D), row),
                  pl.BlockSpec((1, D), fixed), pl.BlockSpec((1, D), fixed),
                  pl.BlockSpec(memory_space=pl.ANY)],
        out_specs=pl.BlockSpec((tt, D), row),
        out_shape=jax.ShapeDtypeStruct((T, D), F32),
        scratch_shapes=[pltpu.VMEM((2, TOP_K, tt * PACK_SUB, LANES), PACKED), pltpu.SemaphoreType.DMA((2,))],
        compiler_params=_cparams(("arbitrary",)),
        name="moe_combine_ln2",
    )(dest, dest, x1t, gates, shared, g, b, y)


def _layer(x2, mem, B, S, depth, w_in, b_f, w_pool, pool_scale, w_mem_kv, w_out, ln1_g, ln1_b,
           w_router, router_bias, w_gate, w_up, w_down, ws_gate, ws_up, ws_down, ln2_g, ln2_b):
    T, D = x2.shape
    n_win, grp = w_pool.shape[0], w_pool.shape[1]
    pw = n_win * grp
    fw = FOX_HEADS * HEAD_DIM
    mw = w_mem_kv.shape[1] // 2
    E = w_router.shape[1]
    alpha = (2 * depth) ** 0.25

    f_lo = pw + 3 * fw
    w_main = jnp.concatenate([w_in[:, :f_lo], w_in[:, f_lo + FOX_HEADS:]], axis=1).astype(BF16)
    w_f = jnp.pad(w_in[:, f_lo:f_lo + FOX_HEADS], ((0, 0), (0, LANES - FOX_HEADS))).astype(BF16)
    bf_pad = jnp.pad(b_f, (0, LANES - FOX_HEADS)).reshape(1, LANES)
    wbd = jnp.zeros((pw, pw), F32)
    for g in range(n_win):
        wbd = wbd.at[g * grp:(g + 1) * grp, g * grp:(g + 1) * grp].set(w_pool[g])
    wrt = w_router.T
    wrt_hi = wrt.astype(BF16)
    wrt_lo = (wrt - wrt_hi.astype(F32)).astype(BF16)

    u_pool, q, k, v, q_mem, f_logit = _inproj(x2, w_main, w_f, pw, fw, mw)
    y_pool = _pool(u_pool, wbd.astype(BF16), pool_scale.reshape(1, pw), B, S)
    fcum = _fgate(f_logit, bf_pad, B, S)
    y_fox = _fox(q, k, v, fcum, B, S)
    y_mem = _memattn(q_mem, mem, w_mem_kv.astype(BF16), B, S)
    x1, x1_packed = _outproj(y_pool, y_fox, y_mem, x2, w_out.astype(BF16), ln1_g.reshape(1, D),
                             ln1_b.reshape(1, D), alpha)

    eidx, gates, rank, counts = _router(x1, wrt_hi, wrt_lo, _per_expert_column(router_bias))

    cnt = counts[:, 0].astype(I32)
    padded = (cnt + ROW_BLOCK - 1) // ROW_BLOCK * ROW_BLOCK
    pad_end = jnp.cumsum(padded)
    start_pad = (pad_end - padded).astype(I32)
    n_rows = (T * TOP_K + E * (ROW_BLOCK - 1)) // ROW_BLOCK * ROW_BLOCK
    n_used = (pad_end[-1] // ROW_BLOCK).astype(I32)
    zero_blk = jnp.where(padded > 0, pad_end - ROW_BLOCK, -1).astype(I32)
    dest = _dest(eidx, rank, _per_expert_column(start_pad))

    xs, shared = _dispatch(zero_blk, dest, x1_packed, ws_gate.astype(BF16), ws_up.astype(BF16),
                           ws_down.astype(BF16), n_rows)
    y = _experts(start_pad // ROW_BLOCK, (padded // ROW_BLOCK).astype(I32), n_used.reshape(1), xs,
                 w_gate, w_up, w_down)
    return _combine(dest, x1, gates.T, shared, ln2_g.reshape(1, D), ln2_b.reshape(1, D), y, alpha)


def kernel(x, mem, w_in, b_f, w_pool, pool_scale, w_mem_kv, w_out, ln1_g, ln1_b, w_router, router_bias,
           w_gate, w_up, w_down, ws_gate, ws_up, ws_down, ln2_g, ln2_b):
    B, S, D = x.shape
    depth = w_in.shape[0]
    x2 = x.reshape(B * S, D)
    for l in range(depth):
        x2 = _layer(x2, mem, B, S, depth, w_in[l], b_f[l], w_pool[l], pool_scale[l], w_mem_kv[l], w_out[l],
                    ln1_g[l], ln1_b[l], w_router[l], router_bias[l], w_gate[l], w_up[l], w_down[l],
                    ws_gate[l], ws_up[l], ws_down[l], ln2_g[l], ln2_b[l])
    return x2.reshape(B, S, D)
```

```python
import functools

import jax
import jax.numpy as jnp
from jax import lax
from jax.experimental import pallas as pl
from jax.experimental.pallas import tpu as pltpu

F32 = jnp.float32
BF16 = jnp.bfloat16
I32 = jnp.int32
PACKED = jnp.int32

LANES = 128
SUBLANES = 8
POOL_WINDOWS = (2, 4, 8, 16)
POOL_HALO = 16
HEAD_DIM = 64
FOX_HEADS = 8
BIAS_TERMS = 3
MEM_HEADS = 4
TOP_K = 8
N_GROUPS = 8
TOPK_GROUPS = 4
ROUTED_SCALE = 2.5
LN_EPS = 1e-5
LOG2E = 1.4426950408889634
ROW_BLOCK = 256
EXPERT_BUFS = 8
EXPERT_AHEAD = EXPERT_BUFS - 2
VMEM_LIMIT = 48 * 1024 * 1024


def _cparams(sem):
    return pltpu.CompilerParams(dimension_semantics=sem, vmem_limit_bytes=VMEM_LIMIT)


def _layer_norm(z, g, b):
    mu = jnp.mean(z, axis=-1, keepdims=True)
    zc = z - mu
    var = jnp.mean(zc * zc, axis=-1, keepdims=True)
    return zc * lax.rsqrt(var + LN_EPS) * g + b


def _silu(x):
    return x * (1.0 / (1.0 + jnp.exp(-x)))


def _load_rows(ref, n):
    return jnp.concatenate([ref[pl.ds(j, n, stride=SUBLANES), :] for j in range(SUBLANES)], axis=1)


def _store_rows(ref, val):
    n = val.shape[0]
    for j in range(SUBLANES):
        ref[pl.ds(j, n, stride=SUBLANES), :] = val[:, j * LANES:(j + 1) * LANES]


def _tile_rows(ref, r):
    return ref.at[pl.ds(pl.multiple_of(r * SUBLANES, SUBLANES), SUBLANES), :]


def _inproj_kernel(x_ref, w_ref, wf_ref, up_ref, q_ref, k_ref, v_ref, qm_ref, f_ref, *, pw, fw, mw, scale):
    xb = x_ref[...].astype(BF16)

    def proj(lo, hi):
        return jnp.dot(xb, w_ref[:, lo:hi], preferred_element_type=F32)

    up_ref[...] = proj(0, pw)
    q_ref[...] = (proj(pw, pw + fw) * (scale * LOG2E)).astype(BF16)
    k_ref[...] = proj(pw + fw, pw + 2 * fw).astype(BF16)
    v_ref[...] = proj(pw + 2 * fw, pw + 3 * fw).astype(BF16)
    qm_ref[...] = (proj(pw + 3 * fw, pw + 3 * fw + mw) * scale).astype(BF16)
    f_ref[...] = jnp.dot(xb, wf_ref[...], preferred_element_type=F32)


def _inproj(x2, w_main, w_f, pw, fw, mw, tm=512):
    T, D = x2.shape
    kern = functools.partial(_inproj_kernel, pw=pw, fw=fw, mw=mw, scale=HEAD_DIM ** -0.5)
    row = lambda i: (i, 0)
    fixed = lambda i: (0, 0)
    return pl.pallas_call(
        kern,
        grid=(T // tm,),
        in_specs=[pl.BlockSpec((tm, D), row),
                  pl.BlockSpec(w_main.shape, fixed),
                  pl.BlockSpec(w_f.shape, fixed)],
        out_specs=[pl.BlockSpec((tm, pw), row), pl.BlockSpec((tm, fw), row), pl.BlockSpec((tm, fw), row),
                   pl.BlockSpec((tm, fw), row), pl.BlockSpec((tm, mw), row), pl.BlockSpec((tm, LANES), row)],
        out_shape=[jax.ShapeDtypeStruct((T, pw), F32), jax.ShapeDtypeStruct((T, fw), BF16),
                   jax.ShapeDtypeStruct((T, fw), BF16), jax.ShapeDtypeStruct((T, fw), BF16),
                   jax.ShapeDtypeStruct((T, mw), BF16), jax.ShapeDtypeStruct((T, LANES), F32)],
        compiler_params=_cparams(("parallel",)),
        name="inproj",
    )(x2, w_main, w_f)


def _pool_kernel(u_ref, wbd_ref, sc_ref, o_ref, ext_sc, *, chunk, group):
    S, W = u_ref.shape
    ext_sc[0:POOL_HALO, :] = jnp.zeros((POOL_HALO, W), F32)
    ext_sc[POOL_HALO:, :] = u_ref[...]
    rows = chunk + POOL_HALO
    lrow = lax.broadcasted_iota(I32, (rows, W), 0)
    lane = lax.broadcasted_iota(I32, (rows, W), 1)

    def body(c, carry):
        start = pl.multiple_of(c * chunk, chunk)
        e = ext_sc[pl.ds(start, rows), :]
        posf = (lrow + (start - POOL_HALO + 1)).astype(F32)
        acc = e
        d = jnp.zeros_like(e)
        shift = 1
        for g, w in enumerate(POOL_WINDOWS):
            while shift < w:
                acc = acc + pltpu.roll(acc, shift, axis=0)
                shift *= 2
            mean = acc / jnp.minimum(posf, float(w))
            d = jnp.where((lane >= g * group) & (lane < (g + 1) * group), mean, d)
        d = (d - e)[POOL_HALO:, :]
        y = jnp.dot(d.astype(BF16), wbd_ref[...], preferred_element_type=F32) * sc_ref[...]
        o_ref[pl.ds(start, chunk), :] = y.astype(BF16)
        return carry

    lax.fori_loop(0, S // chunk, body, 0)


def _pool(u, wbd, pscale, B, S, chunk=512):
    T, W = u.shape
    kern = functools.partial(_pool_kernel, chunk=chunk, group=W // len(POOL_WINDOWS))
    return pl.pallas_call(
        kern,
        grid=(B,),
        in_specs=[pl.BlockSpec((S, W), lambda b: (b, 0)),
                  pl.BlockSpec((W, W), lambda b: (0, 0)),
                  pl.BlockSpec((1, W), lambda b: (0, 0))],
        out_specs=pl.BlockSpec((S, W), lambda b: (b, 0)),
        out_shape=jax.ShapeDtypeStruct((T, W), BF16),
        scratch_shapes=[pltpu.VMEM((S + POOL_HALO, W), F32)],
        compiler_params=_cparams(("parallel",)),
        name="pool_mixer",
    )(u, wbd, pscale)


def _fgate_kernel(f_ref, bf_ref, o_ref):
    S = f_ref.shape[0]
    z = f_ref[...] + bf_ref[...]
    x = jnp.minimum(z, 0.0) - jnp.log(1.0 + jnp.exp(-jnp.abs(z)))
    row = lax.broadcasted_iota(I32, x.shape, 0)
    lane = lax.broadcasted_iota(I32, x.shape, 1)
    shift = 1
    while shift < S:
        x = x + jnp.where(row >= shift, pltpu.roll(x, shift, axis=0), 0.0)
        shift *= 2
    x = x * LOG2E
    hi = x.astype(BF16).astype(F32)
    mid = (x - hi).astype(BF16).astype(F32)
    lo = (x - hi) - mid
    for p in range(FOX_HEADS // 2):
        out = jnp.zeros(x.shape, F32)
        for hh in range(2):
            c = 2 * p + hh
            for j, part in enumerate((hi, mid, lo)):
                out = jnp.where(lane == BIAS_TERMS * hh + j, -part[:, c:c + 1], out)
        o_ref[0, p] = out.astype(BF16)


def _fgate(f_logit, bf_pad, B, S):
    return pl.pallas_call(
        _fgate_kernel,
        grid=(B,),
        in_specs=[pl.BlockSpec((S, LANES), lambda b: (b, 0)),
                  pl.BlockSpec((1, LANES), lambda b: (0, 0))],
        out_specs=pl.BlockSpec((1, FOX_HEADS // 2, S, LANES), lambda b: (b, 0, 0, 0)),
        out_shape=jax.ShapeDtypeStruct((B, FOX_HEADS // 2, S, LANES), BF16),
        compiler_params=_cparams(("parallel",)),
        name="forget_cumsum",
    )(f_logit, bf_pad)


def _fox_kernel(q_ref, k_ref, a_ref, v_ref, o_ref, vt_sc, qa_sc, sa_sc, sb_sc, m_sc, l_sc, acc_sc, *, blk):
    qi = pl.program_id(2)
    nk = vt_sc.shape[0]
    nt = (((1,), (1,)), ((), ()))

    @pl.when(qi == 0)
    def _():
        for c in range(nk):
            vt_sc[c] = v_ref[c * blk:(c + 1) * blk, :].astype(F32).T.astype(BF16)

    q = q_ref[...].astype(F32)
    lane = lax.broadcasted_iota(I32, q.shape, 1)
    for h in range(2):
        mine = (lane >= h * HEAD_DIM) & (lane < (h + 1) * HEAD_DIM)
        bias_rows = (lane >= h * BIAS_TERMS) & (lane < (h + 1) * BIAS_TERMS)
        qa_sc[h] = jnp.concatenate([jnp.where(mine, q, 0.0), jnp.where(bias_rows, 1.0, 0.0)], axis=1).astype(BF16)
    m_sc[...] = jnp.full(m_sc.shape, -jnp.inf, F32)
    l_sc[...] = jnp.zeros(l_sc.shape, F32)
    acc_sc[...] = jnp.zeros(acc_sc.shape, F32)

    def scores(ki, dst):
        ks = pl.multiple_of(ki * blk, blk)
        kb = jnp.concatenate([k_ref[pl.ds(ks, blk), :], a_ref[0, 0, pl.ds(ks, blk), :]], axis=1)
        for h in range(2):
            dst[h] = lax.dot_general(kb, qa_sc[h], nt, preferred_element_type=F32)

    def absorb(ki, src, causal):
        for h in range(2):
            st = src[h]
            if causal:
                kpos = lax.broadcasted_iota(I32, st.shape, 0)
                qpos = lax.broadcasted_iota(I32, st.shape, 1)
                st = jnp.where(kpos <= qpos, st, -jnp.inf)
            m_prev = m_sc[h]
            m_new = jnp.maximum(m_prev, jnp.max(st, axis=0, keepdims=True))
            alpha = jnp.exp2(m_prev - m_new)
            p = jnp.exp2(st - m_new)
            l_sc[h] = alpha * l_sc[h] + jnp.sum(p, axis=0, keepdims=True)
            m_sc[h] = m_new
            vt = vt_sc[ki, h * HEAD_DIM:(h + 1) * HEAD_DIM, :]
            pv = jnp.dot(vt, p.astype(BF16), preferred_element_type=F32)
            rows = slice(h * HEAD_DIM, (h + 1) * HEAD_DIM)
            acc_sc[rows, :] = acc_sc[rows, :] * alpha + pv

    scores(0, sa_sc)

    def body(p, carry):
        scores(2 * p + 1, sb_sc)
        absorb(2 * p, sa_sc, False)
        scores(2 * p + 2, sa_sc)
        absorb(2 * p + 1, sb_sc, False)
        return carry

    lax.fori_loop(0, qi // 2, body, 0)

    @pl.when(qi % 2 == 1)
    def _():
        scores(qi, sb_sc)
        absorb(qi - 1, sa_sc, False)
        absorb(qi, sb_sc, True)

    @pl.when(qi % 2 == 0)
    def _():
        absorb(qi, sa_sc, True)

    out_t = jnp.concatenate([acc_sc[h * HEAD_DIM:(h + 1) * HEAD_DIM, :] / l_sc[h] for h in range(2)], axis=0)
    o_ref[...] = out_t.T.astype(BF16)


def _fox(q, k, v, fbias, B, S, blk=512):
    T, FW = q.shape
    nq = S // blk
    pairs = FW // LANES
    kern = functools.partial(_fox_kernel, blk=blk)
    return pl.pallas_call(
        kern,
        grid=(B, pairs, nq),
        in_specs=[pl.BlockSpec((blk, LANES), lambda b, p, i: (b * nq + i, p)),
                  pl.BlockSpec((S, LANES), lambda b, p, i: (b, p)),
                  pl.BlockSpec((1, 1, S, LANES), lambda b, p, i: (b, p, 0, 0)),
                  pl.BlockSpec((S, LANES), lambda b, p, i: (b, p))],
        out_specs=pl.BlockSpec((blk, LANES), lambda b, p, i: (b * nq + i, p)),
        out_shape=jax.ShapeDtypeStruct((T, FW), BF16),
        scratch_shapes=[pltpu.VMEM((nq, LANES, blk), BF16), pltpu.VMEM((2, blk, 2 * LANES), BF16),
                        pltpu.VMEM((2, blk, blk), F32), pltpu.VMEM((2, blk, blk), F32),
                        pltpu.VMEM((2, 1, blk), F32), pltpu.VMEM((2, 1, blk), F32), pltpu.VMEM((LANES, blk), F32)],
        compiler_params=_cparams(("parallel", "parallel", "arbitrary")),
        name="fox_attention",
    )(q, k, fbias, v)


def _memattn_kernel(qm_ref, mem_ref, wkv_ref, o_ref, k_sc, v_sc):
    MW = qm_ref.shape[1]

    @pl.when(pl.program_id(1) == 0)
    def _():
        kv = jnp.dot(mem_ref[0].astype(BF16), wkv_ref[...], preferred_element_type=F32)
        k_sc[...] = kv[:, :MW].astype(BF16)
        v_sc[...] = kv[:, MW:].astype(BF16)

    q = qm_ref[...]
    lane = lax.broadcasted_iota(I32, q.shape, 1)
    hd = MW // MEM_HEADS
    out = jnp.zeros(q.shape, F32)
    for h in range(MEM_HEADS):
        mine = (lane >= h * hd) & (lane < (h + 1) * hd)
        qh = jnp.where(mine, q, jnp.zeros_like(q))
        s = lax.dot_general(qh, k_sc[...], (((1,), (1,)), ((), ())), preferred_element_type=F32)
        p = jnp.exp(s - jnp.max(s, axis=1, keepdims=True))
        l = jnp.sum(p, axis=1, keepdims=True)
        o = jnp.dot(p.astype(BF16), v_sc[...], preferred_element_type=F32)
        out = jnp.where(mine, o / l, out)
    o_ref[...] = out.astype(BF16)


def _memattn(qm, mem, wkv, B, S, tq=512):
    T, MW = qm.shape
    M, D = mem.shape[1], mem.shape[2]
    nq = S // tq
    return pl.pallas_call(
        _memattn_kernel,
        grid=(B, nq),
        in_specs=[pl.BlockSpec((tq, MW), lambda b, i: (b * nq + i, 0)),
                  pl.BlockSpec((1, M, D), lambda b, i: (b, 0, 0)),
                  pl.BlockSpec(wkv.shape, lambda b, i: (0, 0))],
        out_specs=pl.BlockSpec((tq, MW), lambda b, i: (b * nq + i, 0)),
        out_shape=jax.ShapeDtypeStruct((T, MW), BF16),
        scratch_shapes=[pltpu.VMEM((M, MW), BF16), pltpu.VMEM((M, MW), BF16)],
        compiler_params=_cparams(("parallel", "arbitrary")),
        name="memory_attention",
    )(qm, mem, wkv)


PACK_SUB = 4


def _pack_rows(ref, val):
    n = val.shape[0]
    half = val.shape[1] // 2
    words = pltpu.pack_elementwise([val[:, :half], val[:, half:]], packed_dtype=BF16)
    for j in range(PACK_SUB):
        ref[pl.ds(j, n, stride=PACK_SUB), :] = words[:, j * LANES:(j + 1) * LANES]


def _unpack_rows(ref, n, dtype=BF16):
    slabs = [ref[pl.ds(j, n, stride=PACK_SUB), :] for j in range(PACK_SUB)]
    lo = [pltpu.unpack_elementwise(s, index=0, packed_dtype=BF16, unpacked_dtype=F32) for s in slabs]
    hi = [pltpu.unpack_elementwise(s, index=1, packed_dtype=BF16, unpacked_dtype=F32) for s in slabs]
    return jnp.concatenate(lo + hi, axis=1).astype(dtype)


def _packed_row(ref, r):
    return ref.at[pl.ds(pl.multiple_of(r * PACK_SUB, PACK_SUB), PACK_SUB), :]


def _outproj_kernel(yp_ref, yf_ref, ym_ref, x_ref, w_ref, g_ref, b_ref, o_ref, pk_ref, *, alpha):
    pw, fw = yp_ref.shape[1], yf_ref.shape[1]
    h = jnp.dot(yp_ref[...], w_ref[0:pw, :], preferred_element_type=F32)
    h = h + jnp.dot(yf_ref[...], w_ref[pw:pw + fw, :], preferred_element_type=F32)
    h = h + jnp.dot(ym_ref[...], w_ref[pw + fw:, :], preferred_element_type=F32)
    y = _layer_norm(alpha * x_ref[...] + h, g_ref[...], b_ref[...])
    _store_rows(o_ref, y)
    _pack_rows(pk_ref, y)


def _outproj(yp, yf, ym, x2, w_out, g, b, alpha, tm=512):
    T, D = x2.shape
    assert D == SUBLANES * LANES == 2 * PACK_SUB * LANES
    row = lambda i: (i, 0)
    fixed = lambda i: (0, 0)
    return pl.pallas_call(
        functools.partial(_outproj_kernel, alpha=alpha),
        grid=(T // tm,),
        in_specs=[pl.BlockSpec((tm, yp.shape[1]), row), pl.BlockSpec((tm, yf.shape[1]), row),
                  pl.BlockSpec((tm, ym.shape[1]), row), pl.BlockSpec((tm, D), row),
                  pl.BlockSpec(w_out.shape, fixed), pl.BlockSpec((1, D), fixed), pl.BlockSpec((1, D), fixed)],
        out_specs=[pl.BlockSpec((tm * SUBLANES, LANES), row),
                   pl.BlockSpec((tm * PACK_SUB, LANES), row)],
        out_shape=[jax.ShapeDtypeStruct((T * SUBLANES, LANES), F32),
                   jax.ShapeDtypeStruct((T * PACK_SUB, LANES), PACKED)],
        compiler_params=_cparams(("parallel",)),
        name="outproj_ln1",
    )(yp, yf, ym, x2, w_out, g, b)


ROUTE_TILE = 512


def _per_expert_column(v, dtype=F32):
    return jnp.broadcast_to(v.astype(dtype)[:, None], (v.shape[0], ROUTE_TILE))


def _router_kernel(x_ref, wh_ref, wl_ref, bias_ref, eidx_ref, gate_ref, rank_ref, cnt_ref, carry_sc, *, tt):
    E = wh_ref.shape[0]
    gsz = E // N_GROUPS
    ninf = -jnp.inf

    @pl.when(pl.program_id(0) == 0)
    def _():
        carry_sc[...] = jnp.zeros(carry_sc.shape, F32)

    x = _load_rows(x_ref, tt)
    xh = x.astype(BF16)
    xl = (x - xh.astype(F32)).astype(BF16)
    wh = wh_ref[...]
    nt = (((1,), (1,)), ((), ()))
    logits = lax.dot_general(wh, xh, nt, preferred_element_type=F32) + (
        lax.dot_general(wl_ref[...], xh, nt, preferred_element_type=F32)
        + lax.dot_general(wh, xl, nt, preferred_element_type=F32))
    scores = 1.0 / (1.0 + jnp.exp(-logits))
    biased = scores + bias_ref[...]
    eiota = lax.broadcasted_iota(I32, (E, tt), 0).astype(F32)
    giota = lax.broadcasted_iota(I32, (gsz, tt), 0).astype(F32)

    def cmax(a):
        return jnp.max(a, axis=0, keepdims=True)

    def first_at(a, m, iota, n):
        return jnp.min(jnp.where(a == m, iota, float(n)), axis=0, keepdims=True)

    groups = [biased[g * gsz:(g + 1) * gsz, :] for g in range(N_GROUPS)]
    gscore = []
    for blk in groups:
        m1 = cmax(blk)
        m2 = cmax(jnp.where(giota == first_at(blk, m1, giota, gsz), ninf, blk))
        gscore.append(m1 + m2)

    kept = []
    for g in range(N_GROUPS):
        ahead = jnp.zeros((1, tt), F32)
        for o in range(N_GROUPS):
            if o == g:
                continue
            beats = gscore[o] > gscore[g]
            if o < g:
                beats = beats | (gscore[o] == gscore[g])
            ahead = ahead + jnp.where(beats, 1.0, 0.0)
        kept.append(groups[g] + jnp.where(ahead < float(TOPK_GROUPS), 0.0, ninf))
    masked = jnp.concatenate(kept, axis=0)

    picks, sels = [], []
    chosen = jnp.zeros((E, tt), F32)
    for _ in range(TOP_K):
        ik = first_at(masked, cmax(masked), eiota, E)
        oh = eiota == ik
        picks.append(ik)
        sels.append(jnp.sum(jnp.where(oh, scores, 0.0), axis=0, keepdims=True))
        chosen = jnp.where(oh, 1.0, chosen)
        masked = jnp.where(oh, ninf, masked)
    denom = sels[0]
    for sk in sels[1:]:
        denom = denom + sk

    r = lax.broadcasted_iota(I32, (tt, tt), 0)
    c = lax.broadcasted_iota(I32, (tt, tt), 1)
    earlier = jnp.where(r < c, 1.0, 0.0).astype(BF16)
    chosen_b = chosen.astype(BF16)
    pos = carry_sc[...] + jnp.dot(chosen_b, earlier, preferred_element_type=F32)
    carry_sc[...] = carry_sc[...] + jnp.dot(chosen_b, jnp.ones((tt, tt), BF16), preferred_element_type=F32)
    cnt_ref[...] = carry_sc[...]

    ranks = [jnp.sum(jnp.where(eiota == ik, pos, 0.0), axis=0, keepdims=True) for ik in picks]
    eidx_ref[...] = jnp.concatenate(picks, axis=0).astype(I32)
    gate_ref[...] = jnp.concatenate([sk / denom * ROUTED_SCALE for sk in sels], axis=0)
    rank_ref[...] = jnp.concatenate(ranks, axis=0).astype(I32)


def _router(x1t, wrt_hi, wrt_lo, rbias):
    T = x1t.shape[0] // SUBLANES
    E, D = wrt_hi.shape
    tt = ROUTE_TILE
    tok = lambda i: (0, i)
    fixed = lambda i: (0, 0)
    return pl.pallas_call(
        functools.partial(_router_kernel, tt=tt),
        grid=(T // tt,),
        in_specs=[pl.BlockSpec((tt * SUBLANES, LANES), lambda i: (i, 0)), pl.BlockSpec((E, D), fixed),
                  pl.BlockSpec((E, D), fixed), pl.BlockSpec((E, tt), fixed)],
        out_specs=[pl.BlockSpec((TOP_K, tt), tok), pl.BlockSpec((TOP_K, tt), tok), pl.BlockSpec((TOP_K, tt), tok),
                   pl.BlockSpec((E, tt), fixed)],
        out_shape=[jax.ShapeDtypeStruct((TOP_K, T), I32), jax.ShapeDtypeStruct((TOP_K, T), F32),
                   jax.ShapeDtypeStruct((TOP_K, T), I32), jax.ShapeDtypeStruct((E, tt), F32)],
        scratch_shapes=[pltpu.VMEM((E, tt), F32)],
        compiler_params=_cparams(("arbitrary",)),
        name="router_topk",
    )(x1t, wrt_hi, wrt_lo, rbias)


def _dest_kernel(e_ref, r_ref, sp_ref, d_ref):
    E = sp_ref.shape[0]
    tt = e_ref.shape[1]
    eiota = lax.broadcasted_iota(I32, (E, tt), 0)
    sp = sp_ref[...]
    rows = []
    for kk in range(TOP_K):
        start = jnp.sum(jnp.where(eiota == e_ref[kk:kk + 1, :], sp, 0.0), axis=0, keepdims=True)
        rows.append(start.astype(I32) + r_ref[kk:kk + 1, :])
    d_ref[...] = jnp.concatenate(rows, axis=0)


def _dest(eidx, rank, start_pad_b):
    K, T = eidx.shape
    tt = ROUTE_TILE
    tok = lambda i: (0, i)
    return pl.pallas_call(
        _dest_kernel,
        grid=(T // tt,),
        in_specs=[pl.BlockSpec((K, tt), tok), pl.BlockSpec((K, tt), tok),
                  pl.BlockSpec(start_pad_b.shape, lambda i: (0, 0))],
        out_specs=pl.BlockSpec((K, tt), tok),
        out_shape=jax.ShapeDtypeStruct((K, T), I32),
        compiler_params=_cparams(("parallel",)),
        name="moe_dest",
    )(eidx, rank, start_pad_b)


def _dispatch_kernel(zb_ref, d_ref, x_ref, wsg_ref, wsu_ref, wsd_ref, xs_ref, sh_ref, zero_sc, sem, *, tt, n_exp):
    blk_rows = ROW_BLOCK * PACK_SUB

    @pl.when(pl.program_id(0) == 0)
    def _():
        zero_sc[...] = jnp.zeros(zero_sc.shape, PACKED)

        def zstart(e, c):
            @pl.when(zb_ref[e] >= 0)
            def _():
                dst = xs_ref.at[pl.ds(pl.multiple_of(zb_ref[e] * PACK_SUB, blk_rows), blk_rows), :]
                pltpu.make_async_copy(zero_sc, dst, sem).start()
            return c

        def zwait(e, c):
            @pl.when(zb_ref[e] >= 0)
            def _():
                pltpu.make_async_copy(zero_sc, xs_ref.at[pl.ds(0, blk_rows), :], sem).wait()
            return c

        lax.fori_loop(0, n_exp, zstart, 0)
        lax.fori_loop(0, n_exp, zwait, 0)

    def start(j, c):
        src = _packed_row(x_ref, j)
        for kk in range(TOP_K):
            pltpu.make_async_copy(src, _packed_row(xs_ref, d_ref[kk, j]), sem).start(priority=kk % 2)
        return c

    lax.fori_loop(0, tt, start, 0)
    xb = _unpack_rows(x_ref, tt)
    hs = _silu(jnp.dot(xb, wsg_ref[...], preferred_element_type=F32)) * jnp.dot(
        xb, wsu_ref[...], preferred_element_type=F32)
    sh_ref[...] = jnp.dot(hs.astype(BF16), wsd_ref[...], preferred_element_type=F32)
    for kk in range(TOP_K):
        pltpu.make_async_copy(x_ref, xs_ref.at[pl.ds(0, tt * PACK_SUB), :], sem).wait()


def _dispatch(zero_blk, dest, xpk, wsg, wsu, wsd, n_rows, tt=1024):
    T = xpk.shape[0] // PACK_SUB
    D = wsg.shape[0]
    n_exp = zero_blk.shape[0]
    fixed = lambda i, zb: (0, 0)
    grid_spec = pltpu.PrefetchScalarGridSpec(
        num_scalar_prefetch=1,
        grid=(T // tt,),
        in_specs=[pl.BlockSpec((TOP_K, tt), lambda i, zb: (0, i), memory_space=pltpu.SMEM),
                  pl.BlockSpec((tt * PACK_SUB, LANES), lambda i, zb: (i, 0)),
                  pl.BlockSpec(wsg.shape, fixed), pl.BlockSpec(wsu.shape, fixed), pl.BlockSpec(wsd.shape, fixed)],
        out_specs=[pl.BlockSpec(memory_space=pl.ANY), pl.BlockSpec((tt, D), lambda i, zb: (i, 0))],
        scratch_shapes=[pltpu.VMEM((ROW_BLOCK * PACK_SUB, LANES), PACKED), pltpu.SemaphoreType.DMA],
    )
    return pl.pallas_call(
        functools.partial(_dispatch_kernel, tt=tt, n_exp=n_exp),
        grid_spec=grid_spec,
        out_shape=[jax.ShapeDtypeStruct((n_rows * PACK_SUB, LANES), PACKED), jax.ShapeDtypeStruct((T, D), F32)],
        compiler_params=_cparams(("arbitrary",)),
        name="moe_dispatch",
    )(zero_blk, dest, xpk, wsg, wsu, wsd)


def _expert_kernel(sb_ref, nb_ref, nu_ref, wg_ref, wu_ref, wd_ref, xs_ref, y_ref,
                   xbuf, ybuf, wg_sc, wu_sc, wd_sc, isem, osem):
    e = pl.program_id(0)
    nb = nb_ref[e]
    base = sb_ref[e]
    n_used = nu_ref[0]
    in_rows = out_rows = ROW_BLOCK * PACK_SUB

    def fetch(blk, slot):
        src = xs_ref.at[pl.ds(pl.multiple_of(blk * in_rows, in_rows), in_rows), :]
        pltpu.make_async_copy(src, xbuf.at[slot], isem.at[slot]).start()

    def wait_in(slot):
        pltpu.make_async_copy(xs_ref.at[pl.ds(0, in_rows), :], xbuf.at[slot], isem.at[slot]).wait()

    def store(blk, slot):
        dst = y_ref.at[pl.ds(pl.multiple_of(blk * out_rows, out_rows), out_rows), :]
        pltpu.make_async_copy(ybuf.at[slot], dst, osem.at[slot]).start()

    def wait_out(slot):
        pltpu.make_async_copy(ybuf.at[slot], y_ref.at[pl.ds(0, out_rows), :], osem.at[slot]).wait()

    @pl.when(nb > 0)
    def _():
        @pl.when(base == 0)
        def _():
            for g0 in range(EXPERT_AHEAD):
                @pl.when(g0 < n_used)
                def _():
                    fetch(g0, g0 % EXPERT_BUFS)

        wg_sc[...] = wg_ref[0].astype(BF16)
        wu_sc[...] = wu_ref[0].astype(BF16)
        wd_sc[...] = wd_ref[0].astype(BF16)

        def run(blocks):
            slots = [g % EXPERT_BUFS for g in blocks]
            for g, slot in zip(blocks, slots):
                wait_in(slot)

                @pl.when(g + EXPERT_AHEAD < n_used)
                def _():
                    fetch(g + EXPERT_AHEAD, (g + EXPERT_AHEAD) % EXPERT_BUFS)

                @pl.when(g >= EXPERT_BUFS)
                def _():
                    wait_out(slot)

            for g, slot in zip(blocks, slots):
                xb = _unpack_rows(xbuf.at[slot], ROW_BLOCK)
                a = jnp.dot(xb, wg_sc[...], preferred_element_type=F32)
                u = jnp.dot(xb, wu_sc[...], preferred_element_type=F32)
                h = (_silu(a) * u).astype(BF16)
                _pack_rows(ybuf.at[slot], jnp.dot(h, wd_sc[...], preferred_element_type=F32))
            for g, slot in zip(blocks, slots):
                store(g, slot)

        def pair(p, c):
            run([base + 2 * p, base + 2 * p + 1])
            return c

        lax.fori_loop(0, nb // 2, pair, 0)

        @pl.when(nb % 2 == 1)
        def _():
            run([base + nb - 1])

    @pl.when(e == pl.num_programs(0) - 1)
    def _():
        for slot in range(EXPERT_BUFS):
            @pl.when(slot < n_used)
            def _():
                wait_out(slot)


def _experts(start_blk, n_blk_e, n_used, xs, w_gate, w_up, w_down):
    n_rows = xs.shape[0] // PACK_SUB
    E, D, H = w_gate.shape
    wsel = lambda e, sb, nb, nu: (e, 0, 0)
    grid_spec = pltpu.PrefetchScalarGridSpec(
        num_scalar_prefetch=3,
        grid=(E,),
        in_specs=[pl.BlockSpec((1, D, H), wsel), pl.BlockSpec((1, D, H), wsel), pl.BlockSpec((1, H, D), wsel),
                  pl.BlockSpec(memory_space=pl.ANY)],
        out_specs=pl.BlockSpec(memory_space=pl.ANY),
        scratch_shapes=[pltpu.VMEM((EXPERT_BUFS, ROW_BLOCK * PACK_SUB, LANES), PACKED),
                        pltpu.VMEM((EXPERT_BUFS, ROW_BLOCK * PACK_SUB, LANES), PACKED),
                        pltpu.VMEM((D, H), BF16), pltpu.VMEM((D, H), BF16), pltpu.VMEM((H, D), BF16),
                        pltpu.SemaphoreType.DMA((EXPERT_BUFS,)), pltpu.SemaphoreType.DMA((EXPERT_BUFS,))],
    )
    return pl.pallas_call(
        _expert_kernel,
        grid_spec=grid_spec,
        out_shape=jax.ShapeDtypeStruct((n_rows * PACK_SUB, LANES), PACKED),
        compiler_params=_cparams(("arbitrary",)),
        name="moe_experts",
    )(start_blk, n_blk_e, n_used, w_gate, w_up, w_down, xs)


def _combine_kernel(dcur_ref, dnxt_ref, x_ref, gate_ref, sh_ref, g_ref, b_ref, y_ref, o_ref, buf, sem, *, tt, alpha):
    i = pl.program_id(0)
    n = pl.num_programs(0)

    def fetch(d_ref, slot):
        def body(j, c):
            for kk in range(TOP_K):
                src = _packed_row(y_ref, d_ref[kk * tt + j])
                pltpu.make_async_copy(src, _packed_row(buf.at[slot, kk], j), sem.at[slot]).start(priority=kk % 2)
            return c

        lax.fori_loop(0, tt, body, 0)

    def reduce(slot):
        for kk in range(TOP_K):
            pltpu.make_async_copy(y_ref.at[pl.ds(0, tt * PACK_SUB), :], buf.at[slot, kk], sem.at[slot]).wait()
        x = _load_rows(x_ref, tt)
        moe = sh_ref[...]
        gates = gate_ref[...]
        for kk in range(TOP_K):
            moe = moe + gates[:, kk:kk + 1] * _unpack_rows(buf.at[slot, kk], tt, F32)
        o_ref[...] = _layer_norm(alpha * x + moe, g_ref[...], b_ref[...])

    @pl.when(i == 0)
    def _():
        fetch(dcur_ref, 0)

    for slot in range(2):
        @pl.when((i % 2 == slot) & (i + 1 < n))
        def _():
            fetch(dnxt_ref, 1 - slot)

    for slot in range(2):
        @pl.when(i % 2 == slot)
        def _():
            reduce(slot)


def _combine(dest, x1t, gates, shared, g, b, y, alpha, tt=256):
    T = x1t.shape[0] // SUBLANES
    D = SUBLANES * LANES
    n = T // tt
    dest = dest.reshape(TOP_K, n, tt).transpose(1, 0, 2).reshape(n * TOP_K * tt)
    row = lambda i: (i, 0)
    fixed = lambda i: (0, 0)
    return pl.pallas_call(
        functools.partial(_combine_kernel, tt=tt, alpha=alpha),
        grid=(n,),
        in_specs=[pl.BlockSpec((TOP_K * tt,), lambda i: (i,), memory_space=pltpu.SMEM),
                  pl.BlockSpec((TOP_K * tt,), lambda i: (jnp.minimum(i + 1, n - 1),), memory_space=pltpu.SMEM),
                  pl.BlockSpec((tt * SUBLANES, LANES), row), pl.BlockSpec((tt, TOP_K), row),
                  pl.BlockSpec((tt, D), row),
                  pl.BlockSpec((1, D), fixed), pl.BlockSpec((1, D), fixed),
                  pl.BlockSpec(memory_space=pl.ANY)],
        out_specs=pl.BlockSpec((tt, D), row),
        out_shape=jax.ShapeDtypeStruct((T, D), F32),
        scratch_shapes=[pltpu.VMEM((2, TOP_K, tt * PACK_SUB, LANES), PACKED), pltpu.SemaphoreType.DMA((2,))],
        compiler_params=_cparams(("arbitrary",)),
        name="moe_combine_ln2",
    )(dest, dest, x1t, gates, shared, g, b, y)


def _layer(x2, mem, B, S, depth, w_in, b_f, w_pool, pool_scale, w_mem_kv, w_out, ln1_g, ln1_b,
           w_router, router_bias, w_gate, w_up, w_down, ws_gate, ws_up, ws_down, ln2_g, ln2_b):
    T, D = x2.shape
    n_win, grp = w_pool.shape[0], w_pool.shape[1]
    pw = n_win * grp
    fw = FOX_HEADS * HEAD_DIM
    mw = w_mem_kv.shape[1] // 2
    E = w_router.shape[1]
    alpha = (2 * depth) ** 0.25

    f_lo = pw + 3 * fw
    w_main = jnp.concatenate([w_in[:, :f_lo], w_in[:, f_lo + FOX_HEADS:]], axis=1).astype(BF16)
    w_f = jnp.pad(w_in[:, f_lo:f_lo + FOX_HEADS], ((0, 0), (0, LANES - FOX_HEADS))).astype(BF16)
    bf_pad = jnp.pad(b_f, (0, LANES - FOX_HEADS)).reshape(1, LANES)
    wbd = jnp.zeros((pw, pw), F32)
    for g in range(n_win):
        wbd = wbd.at[g * grp:(g + 1) * grp, g * grp:(g + 1) * grp].set(w_pool[g])
    wrt = w_router.T
    wrt_hi = wrt.astype(BF16)
    wrt_lo = (wrt - wrt_hi.astype(F32)).astype(BF16)

    u_pool, q, k, v, q_mem, f_logit = _inproj(x2, w_main, w_f, pw, fw, mw)
    y_pool = _pool(u_pool, wbd.astype(BF16), pool_scale.reshape(1, pw), B, S)
    fcum = _fgate(f_logit, bf_pad, B, S)
    y_fox = _fox(q, k, v, fcum, B, S)
    y_mem = _memattn(q_mem, mem, w_mem_kv.astype(BF16), B, S)
    x1, x1_packed = _outproj(y_pool, y_fox, y_mem, x2, w_out.astype(BF16), ln1_g.reshape(1, D),
                             ln1_b.reshape(1, D), alpha)

    eidx, gates, rank, counts = _router(x1, wrt_hi, wrt_lo, _per_expert_column(router_bias))

    cnt = counts[:, 0].astype(I32)
    padded = (cnt + ROW_BLOCK - 1) // ROW_BLOCK * ROW_BLOCK
    pad_end = jnp.cumsum(padded)
    start_pad = (pad_end - padded).astype(I32)
    n_rows = (T * TOP_K + E * (ROW_BLOCK - 1)) // ROW_BLOCK * ROW_BLOCK
    n_used = (pad_end[-1] // ROW_BLOCK).astype(I32)
    zero_blk = jnp.where(padded > 0, pad_end - ROW_BLOCK, -1).astype(I32)
    dest = _dest(eidx, rank, _per_expert_column(start_pad))

    xs, shared = _dispatch(zero_blk, dest, x1_packed, ws_gate.astype(BF16), ws_up.astype(BF16),
                           ws_down.astype(BF16), n_rows)
    y = _experts(start_pad // ROW_BLOCK, (padded // ROW_BLOCK).astype(I32), n_used.reshape(1), xs,
                 w_gate, w_up, w_down)
    return _combine(dest, x1, gates.T, shared, ln2_g.reshape(1, D), ln2_b.reshape(1, D), y, alpha)


def kernel(x, mem, w_in, b_f, w_pool, pool_scale, w_mem_kv, w_out, ln1_g, ln1_b, w_router, router_bias,
           w_gate, w_up, w_down, ws_gate, ws_up, ws_down, ln2_g, ln2_b):
    B, S, D = x.shape
    depth = w_in.shape[0]
    x2 = x.reshape(B * S, D)
    for l in range(depth):
        x2 = _layer(x2, mem, B, S, depth, w_in[l], b_f[l], w_pool[l], pool_scale[l], w_mem_kv[l], w_out[l],
                    ln1_g[l], ln1_b[l], w_router[l], router_bias[l], w_gate[l], w_up[l], w_down[l],
                    ws_gate[l], ws_up[l], ws_down[l], ln2_g[l], ln2_b[l])
    return x2.reshape(B, S, D)
```

```python
import functools

import jax
import jax.numpy as jnp
from jax import lax
from jax.experimental import pallas as pl
from jax.experimental.pallas import tpu as pltpu

F32 = jnp.float32
BF16 = jnp.bfloat16
I32 = jnp.int32
PACKED = jnp.int32

LANES = 128
SUBLANES = 8
POOL_WINDOWS = (2, 4, 8, 16)
POOL_HALO = 16
HEAD_DIM = 64
FOX_HEADS = 8
BIAS_TERMS = 3
MEM_HEADS = 4
TOP_K = 8
N_GROUPS = 8
TOPK_GROUPS = 4
ROUTED_SCALE = 2.5
LN_EPS = 1e-5
LOG2E = 1.4426950408889634
ROW_BLOCK = 256
EXPERT_BUFS = 8
EXPERT_REGION = 4
EXPERT_AHEAD = EXPERT_BUFS - EXPERT_REGION
VMEM_LIMIT = 48 * 1024 * 1024


def _cparams(sem):
    return pltpu.CompilerParams(dimension_semantics=sem, vmem_limit_bytes=VMEM_LIMIT)


def _layer_norm(z, g, b):
    mu = jnp.mean(z, axis=-1, keepdims=True)
    zc = z - mu
    var = jnp.mean(zc * zc, axis=-1, keepdims=True)
    return zc * lax.rsqrt(var + LN_EPS) * g + b


def _silu(x):
    return x * (1.0 / (1.0 + jnp.exp(-x)))


def _load_rows(ref, n):
    return jnp.concatenate([ref[pl.ds(j, n, stride=SUBLANES), :] for j in range(SUBLANES)], axis=1)


def _store_rows(ref, val):
    n = val.shape[0]
    for j in range(SUBLANES):
        ref[pl.ds(j, n, stride=SUBLANES), :] = val[:, j * LANES:(j + 1) * LANES]


def _tile_rows(ref, r):
    return ref.at[pl.ds(pl.multiple_of(r * SUBLANES, SUBLANES), SUBLANES), :]


def _inproj_kernel(x_ref, w_ref, wf_ref, up_ref, q_ref, k_ref, v_ref, qm_ref, f_ref, *, pw, fw, mw, scale):
    xb = x_ref[...].astype(BF16)

    def proj(lo, hi):
        return jnp.dot(xb, w_ref[:, lo:hi], preferred_element_type=F32)

    up_ref[...] = proj(0, pw)
    q_ref[...] = (proj(pw, pw + fw) * (scale * LOG2E)).astype(BF16)
    k_ref[...] = proj(pw + fw, pw + 2 * fw).astype(BF16)
    v_ref[...] = proj(pw + 2 * fw, pw + 3 * fw).astype(BF16)
    qm_ref[...] = (proj(pw + 3 * fw, pw + 3 * fw + mw) * scale).astype(BF16)
    f_ref[...] = jnp.dot(xb, wf_ref[...], preferred_element_type=F32)


def _inproj(x2, w_main, w_f, pw, fw, mw, tm=512):
    T, D = x2.shape
    kern = functools.partial(_inproj_kernel, pw=pw, fw=fw, mw=mw, scale=HEAD_DIM ** -0.5)
    row = lambda i: (i, 0)
    fixed = lambda i: (0, 0)
    return pl.pallas_call(
        kern,
        grid=(T // tm,),
        in_specs=[pl.BlockSpec((tm, D), row),
                  pl.BlockSpec(w_main.shape, fixed),
                  pl.BlockSpec(w_f.shape, fixed)],
        out_specs=[pl.BlockSpec((tm, pw), row), pl.BlockSpec((tm, fw), row), pl.BlockSpec((tm, fw), row),
                   pl.BlockSpec((tm, fw), row), pl.BlockSpec((tm, mw), row), pl.BlockSpec((tm, LANES), row)],
        out_shape=[jax.ShapeDtypeStruct((T, pw), F32), jax.ShapeDtypeStruct((T, fw), BF16),
                   jax.ShapeDtypeStruct((T, fw), BF16), jax.ShapeDtypeStruct((T, fw), BF16),
                   jax.ShapeDtypeStruct((T, mw), BF16), jax.ShapeDtypeStruct((T, LANES), F32)],
        compiler_params=_cparams(("parallel",)),
        name="inproj",
    )(x2, w_main, w_f)


def _pool_kernel(u_ref, wbd_ref, sc_ref, o_ref, ext_sc, *, chunk, group):
    S, W = u_ref.shape
    ext_sc[0:POOL_HALO, :] = jnp.zeros((POOL_HALO, W), F32)
    ext_sc[POOL_HALO:, :] = u_ref[...]
    rows = chunk + POOL_HALO
    lrow = lax.broadcasted_iota(I32, (rows, W), 0)
    lane = lax.broadcasted_iota(I32, (rows, W), 1)

    def body(c, carry):
        start = pl.multiple_of(c * chunk, chunk)
        e = ext_sc[pl.ds(start, rows), :]
        posf = (lrow + (start - POOL_HALO + 1)).astype(F32)
        acc = e
        d = jnp.zeros_like(e)
        shift = 1
        for g, w in enumerate(POOL_WINDOWS):
            while shift < w:
                acc = acc + pltpu.roll(acc, shift, axis=0)
                shift *= 2
            mean = acc / jnp.minimum(posf, float(w))
            d = jnp.where((lane >= g * group) & (lane < (g + 1) * group), mean, d)
        d = (d - e)[POOL_HALO:, :]
        y = jnp.dot(d.astype(BF16), wbd_ref[...], preferred_element_type=F32) * sc_ref[...]
        o_ref[pl.ds(start, chunk), :] = y.astype(BF16)
        return carry

    lax.fori_loop(0, S // chunk, body, 0)


def _pool(u, wbd, pscale, B, S, chunk=512):
    T, W = u.shape
    kern = functools.partial(_pool_kernel, chunk=chunk, group=W // len(POOL_WINDOWS))
    return pl.pallas_call(
        kern,
        grid=(B,),
        in_specs=[pl.BlockSpec((S, W), lambda b: (b, 0)),
                  pl.BlockSpec((W, W), lambda b: (0, 0)),
                  pl.BlockSpec((1, W), lambda b: (0, 0))],
        out_specs=pl.BlockSpec((S, W), lambda b: (b, 0)),
        out_shape=jax.ShapeDtypeStruct((T, W), BF16),
        scratch_shapes=[pltpu.VMEM((S + POOL_HALO, W), F32)],
        compiler_params=_cparams(("parallel",)),
        name="pool_mixer",
    )(u, wbd, pscale)


def _fgate_kernel(f_ref, bf_ref, o_ref):
    S = f_ref.shape[0]
    z = f_ref[...] + bf_ref[...]
    x = jnp.minimum(z, 0.0) - jnp.log(1.0 + jnp.exp(-jnp.abs(z)))
    row = lax.broadcasted_iota(I32, x.shape, 0)
    lane = lax.broadcasted_iota(I32, x.shape, 1)
    shift = 1
    while shift < S:
        x = x + jnp.where(row >= shift, pltpu.roll(x, shift, axis=0), 0.0)
        shift *= 2
    x = x * LOG2E
    hi = x.astype(BF16).astype(F32)
    mid = (x - hi).astype(BF16).astype(F32)
    lo = (x - hi) - mid
    for p in range(FOX_HEADS // 2):
        out = jnp.zeros(x.shape, F32)
        for hh in range(2):
            c = 2 * p + hh
            for j, part in enumerate((hi, mid, lo)):
                out = jnp.where(lane == BIAS_TERMS * hh + j, -part[:, c:c + 1], out)
        o_ref[0, p] = out.astype(BF16)


def _fgate(f_logit, bf_pad, B, S):
    return pl.pallas_call(
        _fgate_kernel,
        grid=(B,),
        in_specs=[pl.BlockSpec((S, LANES), lambda b: (b, 0)),
                  pl.BlockSpec((1, LANES), lambda b: (0, 0))],
        out_specs=pl.BlockSpec((1, FOX_HEADS // 2, S, LANES), lambda b: (b, 0, 0, 0)),
        out_shape=jax.ShapeDtypeStruct((B, FOX_HEADS // 2, S, LANES), BF16),
        compiler_params=_cparams(("parallel",)),
        name="forget_cumsum",
    )(f_logit, bf_pad)


def _fox_kernel(q_ref, k_ref, a_ref, v_ref, o_ref, vt_sc, qa_sc, sa_sc, sb_sc, m_sc, l_sc, acc_sc, *, blk):
    qi = pl.program_id(2)
    nk = vt_sc.shape[0]
    nt = (((1,), (1,)), ((), ()))

    @pl.when(qi == 0)
    def _():
        for c in range(nk):
            vt_sc[c] = v_ref[c * blk:(c + 1) * blk, :].astype(F32).T.astype(BF16)

    q = q_ref[...].astype(F32)
    lane = lax.broadcasted_iota(I32, q.shape, 1)
    for h in range(2):
        mine = (lane >= h * HEAD_DIM) & (lane < (h + 1) * HEAD_DIM)
        bias_rows = (lane >= h * BIAS_TERMS) & (lane < (h + 1) * BIAS_TERMS)
        qa_sc[h] = jnp.concatenate([jnp.where(mine, q, 0.0), jnp.where(bias_rows, 1.0, 0.0)], axis=1).astype(BF16)
    m_sc[...] = jnp.full(m_sc.shape, -jnp.inf, F32)
    l_sc[...] = jnp.zeros(l_sc.shape, F32)
    acc_sc[...] = jnp.zeros(acc_sc.shape, F32)

    def scores(ki, dst):
        ks = pl.multiple_of(ki * blk, blk)
        kb = jnp.concatenate([k_ref[pl.ds(ks, blk), :], a_ref[0, 0, pl.ds(ks, blk), :]], axis=1)
        for h in range(2):
            dst[h] = lax.dot_general(kb, qa_sc[h], nt, preferred_element_type=F32)

    def absorb(ki, src, causal):
        for h in range(2):
            st = src[h]
            if causal:
                kpos = lax.broadcasted_iota(I32, st.shape, 0)
                qpos = lax.broadcasted_iota(I32, st.shape, 1)
                st = jnp.where(kpos <= qpos, st, -jnp.inf)
            m_prev = m_sc[h]
            m_new = jnp.maximum(m_prev, jnp.max(st, axis=0, keepdims=True))
            alpha = jnp.exp2(m_prev - m_new)
            p = jnp.exp2(st - m_new)
            l_sc[h] = alpha * l_sc[h] + jnp.sum(p, axis=0, keepdims=True)
            m_sc[h] = m_new
            vt = vt_sc[ki, h * HEAD_DIM:(h + 1) * HEAD_DIM, :]
            pv = jnp.dot(vt, p.astype(BF16), preferred_element_type=F32)
            rows = slice(h * HEAD_DIM, (h + 1) * HEAD_DIM)
            acc_sc[rows, :] = acc_sc[rows, :] * alpha + pv

    scores(0, sa_sc)

    def body(p, carry):
        scores(2 * p + 1, sb_sc)
        absorb(2 * p, sa_sc, False)
        scores(2 * p + 2, sa_sc)
        absorb(2 * p + 1, sb_sc, False)
        return carry

    lax.fori_loop(0, qi // 2, body, 0)

    @pl.when(qi % 2 == 1)
    def _():
        scores(qi, sb_sc)
        absorb(qi - 1, sa_sc, False)
        absorb(qi, sb_sc, True)

    @pl.when(qi % 2 == 0)
    def _():
        absorb(qi, sa_sc, True)

    out_t = jnp.concatenate([acc_sc[h * HEAD_DIM:(h + 1) * HEAD_DIM, :] / l_sc[h] for h in range(2)], axis=0)
    o_ref[...] = out_t.T.astype(BF16)


def _fox(q, k, v, fbias, B, S, blk=512):
    T, FW = q.shape
    nq = S // blk
    pairs = FW // LANES
    kern = functools.partial(_fox_kernel, blk=blk)
    return pl.pallas_call(
        kern,
        grid=(B, pairs, nq),
        in_specs=[pl.BlockSpec((blk, LANES), lambda b, p, i: (b * nq + i, p)),
                  pl.BlockSpec((S, LANES), lambda b, p, i: (b, p)),
                  pl.BlockSpec((1, 1, S, LANES), lambda b, p, i: (b, p, 0, 0)),
                  pl.BlockSpec((S, LANES), lambda b, p, i: (b, p))],
        out_specs=pl.BlockSpec((blk, LANES), lambda b, p, i: (b * nq + i, p)),
        out_shape=jax.ShapeDtypeStruct((T, FW), BF16),
        scratch_shapes=[pltpu.VMEM((nq, LANES, blk), BF16), pltpu.VMEM((2, blk, 2 * LANES), BF16),
                        pltpu.VMEM((2, blk, blk), F32), pltpu.VMEM((2, blk, blk), F32),
                        pltpu.VMEM((2, 1, blk), F32), pltpu.VMEM((2, 1, blk), F32), pltpu.VMEM((LANES, blk), F32)],
        compiler_params=_cparams(("parallel", "parallel", "arbitrary")),
        name="fox_attention",
    )(q, k, fbias, v)


def _memattn_kernel(qm_ref, mem_ref, wkv_ref, o_ref, k_sc, v_sc):
    MW = qm_ref.shape[1]

    @pl.when(pl.program_id(1) == 0)
    def _():
        kv = jnp.dot(mem_ref[0].astype(BF16), wkv_ref[...], preferred_element_type=F32)
        k_sc[...] = kv[:, :MW].astype(BF16)
        v_sc[...] = kv[:, MW:].astype(BF16)

    q = qm_ref[...]
    lane = lax.broadcasted_iota(I32, q.shape, 1)
    hd = MW // MEM_HEADS
    out = jnp.zeros(q.shape, F32)
    for h in range(MEM_HEADS):
        mine = (lane >= h * hd) & (lane < (h + 1) * hd)
        qh = jnp.where(mine, q, jnp.zeros_like(q))
        s = lax.dot_general(qh, k_sc[...], (((1,), (1,)), ((), ())), preferred_element_type=F32)
        p = jnp.exp(s - jnp.max(s, axis=1, keepdims=True))
        l = jnp.sum(p, axis=1, keepdims=True)
        o = jnp.dot(p.astype(BF16), v_sc[...], preferred_element_type=F32)
        out = jnp.where(mine, o / l, out)
    o_ref[...] = out.astype(BF16)


def _memattn(qm, mem, wkv, B, S, tq=512):
    T, MW = qm.shape
    M, D = mem.shape[1], mem.shape[2]
    nq = S // tq
    return pl.pallas_call(
        _memattn_kernel,
        grid=(B, nq),
        in_specs=[pl.BlockSpec((tq, MW), lambda b, i: (b * nq + i, 0)),
                  pl.BlockSpec((1, M, D), lambda b, i: (b, 0, 0)),
                  pl.BlockSpec(wkv.shape, lambda b, i: (0, 0))],
        out_specs=pl.BlockSpec((tq, MW), lambda b, i: (b * nq + i, 0)),
        out_shape=jax.ShapeDtypeStruct((T, MW), BF16),
        scratch_shapes=[pltpu.VMEM((M, MW), BF16), pltpu.VMEM((M, MW), BF16)],
        compiler_params=_cparams(("parallel", "arbitrary")),
        name="memory_attention",
    )(qm, mem, wkv)


PACK_SUB = 4


def _pack_rows(ref, val):
    n = val.shape[0]
    half = val.shape[1] // 2
    words = pltpu.pack_elementwise([val[:, :half], val[:, half:]], packed_dtype=BF16)
    for j in range(PACK_SUB):
        ref[pl.ds(j, n, stride=PACK_SUB), :] = words[:, j * LANES:(j + 1) * LANES]


def _unpack_rows(ref, n, dtype=BF16):
    slabs = [ref[pl.ds(j, n, stride=PACK_SUB), :] for j in range(PACK_SUB)]
    lo = [pltpu.unpack_elementwise(s, index=0, packed_dtype=BF16, unpacked_dtype=F32) for s in slabs]
    hi = [pltpu.unpack_elementwise(s, index=1, packed_dtype=BF16, unpacked_dtype=F32) for s in slabs]
    return jnp.concatenate(lo + hi, axis=1).astype(dtype)


def _packed_row(ref, r):
    return ref.at[pl.ds(pl.multiple_of(r * PACK_SUB, PACK_SUB), PACK_SUB), :]


def _outproj_kernel(yp_ref, yf_ref, ym_ref, x_ref, w_ref, g_ref, b_ref, o_ref, pk_ref, *, alpha):
    pw, fw = yp_ref.shape[1], yf_ref.shape[1]
    h = jnp.dot(yp_ref[...], w_ref[0:pw, :], preferred_element_type=F32)
    h = h + jnp.dot(yf_ref[...], w_ref[pw:pw + fw, :], preferred_element_type=F32)
    h = h + jnp.dot(ym_ref[...], w_ref[pw + fw:, :], preferred_element_type=F32)
    y = _layer_norm(alpha * x_ref[...] + h, g_ref[...], b_ref[...])
    _store_rows(o_ref, y)
    _pack_rows(pk_ref, y)


def _outproj(yp, yf, ym, x2, w_out, g, b, alpha, tm=512):
    T, D = x2.shape
    assert D == SUBLANES * LANES == 2 * PACK_SUB * LANES
    row = lambda i: (i, 0)
    fixed = lambda i: (0, 0)
    return pl.pallas_call(
        functools.partial(_outproj_kernel, alpha=alpha),
        grid=(T // tm,),
        in_specs=[pl.BlockSpec((tm, yp.shape[1]), row), pl.BlockSpec((tm, yf.shape[1]), row),
                  pl.BlockSpec((tm, ym.shape[1]), row), pl.BlockSpec((tm, D), row),
                  pl.BlockSpec(w_out.shape, fixed), pl.BlockSpec((1, D), fixed), pl.BlockSpec((1, D), fixed)],
        out_specs=[pl.BlockSpec((tm * SUBLANES, LANES), row),
                   pl.BlockSpec((tm * PACK_SUB, LANES), row)],
        out_shape=[jax.ShapeDtypeStruct((T * SUBLANES, LANES), F32),
                   jax.ShapeDtypeStruct((T * PACK_SUB, LANES), PACKED)],
        compiler_params=_cparams(("parallel",)),
        name="outproj_ln1",
    )(yp, yf, ym, x2, w_out, g, b)


ROUTE_TILE = 512


def _per_expert_column(v, dtype=F32):
    return jnp.broadcast_to(v.astype(dtype)[:, None], (v.shape[0], ROUTE_TILE))


def _router_kernel(x_ref, wh_ref, wl_ref, bias_ref, eidx_ref, gate_ref, rank_ref, cnt_ref, carry_sc, *, tt):
    E = wh_ref.shape[0]
    gsz = E // N_GROUPS
    ninf = -jnp.inf

    @pl.when(pl.program_id(0) == 0)
    def _():
        carry_sc[...] = jnp.zeros(carry_sc.shape, F32)

    x = _load_rows(x_ref, tt)
    xh = x.astype(BF16)
    xl = (x - xh.astype(F32)).astype(BF16)
    wh = wh_ref[...]
    nt = (((1,), (1,)), ((), ()))
    logits = lax.dot_general(wh, xh, nt, preferred_element_type=F32) + (
        lax.dot_general(wl_ref[...], xh, nt, preferred_element_type=F32)
        + lax.dot_general(wh, xl, nt, preferred_element_type=F32))
    scores = 1.0 / (1.0 + jnp.exp(-logits))
    biased = scores + bias_ref[...]
    eiota = lax.broadcasted_iota(I32, (E, tt), 0).astype(F32)
    giota = lax.broadcasted_iota(I32, (gsz, tt), 0).astype(F32)

    def cmax(a):
        return jnp.max(a, axis=0, keepdims=True)

    def first_at(a, m, iota, n):
        return jnp.min(jnp.where(a == m, iota, float(n)), axis=0, keepdims=True)

    groups = [biased[g * gsz:(g + 1) * gsz, :] for g in range(N_GROUPS)]
    gscore = []
    for blk in groups:
        m1 = cmax(blk)
        m2 = cmax(jnp.where(giota == first_at(blk, m1, giota, gsz), ninf, blk))
        gscore.append(m1 + m2)

    kept = []
    for g in range(N_GROUPS):
        ahead = jnp.zeros((1, tt), F32)
        for o in range(N_GROUPS):
            if o == g:
                continue
            beats = gscore[o] > gscore[g]
            if o < g:
                beats = beats | (gscore[o] == gscore[g])
            ahead = ahead + jnp.where(beats, 1.0, 0.0)
        kept.append(groups[g] + jnp.where(ahead < float(TOPK_GROUPS), 0.0, ninf))
    masked = jnp.concatenate(kept, axis=0)

    picks, sels = [], []
    chosen = jnp.zeros((E, tt), F32)
    for _ in range(TOP_K):
        ik = first_at(masked, cmax(masked), eiota, E)
        oh = eiota == ik
        picks.append(ik)
        sels.append(jnp.sum(jnp.where(oh, scores, 0.0), axis=0, keepdims=True))
        chosen = jnp.where(oh, 1.0, chosen)
        masked = jnp.where(oh, ninf, masked)
    denom = sels[0]
    for sk in sels[1:]:
        denom = denom + sk

    r = lax.broadcasted_iota(I32, (tt, tt), 0)
    c = lax.broadcasted_iota(I32, (tt, tt), 1)
    earlier = jnp.where(r < c, 1.0, 0.0).astype(BF16)
    chosen_b = chosen.astype(BF16)
    pos = carry_sc[...] + jnp.dot(chosen_b, earlier, preferred_element_type=F32)
    carry_sc[...] = carry_sc[...] + jnp.dot(chosen_b, jnp.ones((tt, tt), BF16), preferred_element_type=F32)
    cnt_ref[...] = carry_sc[...]

    ranks = [jnp.sum(jnp.where(eiota == ik, pos, 0.0), axis=0, keepdims=True) for ik in picks]
    eidx_ref[...] = jnp.concatenate(picks, axis=0).astype(I32)
    gate_ref[...] = jnp.concatenate([sk / denom * ROUTED_SCALE for sk in sels], axis=0)
    rank_ref[...] = jnp.concatenate(ranks, axis=0).astype(I32)


def _router(x1t, wrt_hi, wrt_lo, rbias):
    T = x1t.shape[0] // SUBLANES
    E, D = wrt_hi.shape
    tt = ROUTE_TILE
    tok = lambda i: (0, i)
    fixed = lambda i: (0, 0)
    return pl.pallas_call(
        functools.partial(_router_kernel, tt=tt),
        grid=(T // tt,),
        in_specs=[pl.BlockSpec((tt * SUBLANES, LANES), lambda i: (i, 0)), pl.BlockSpec((E, D), fixed),
                  pl.BlockSpec((E, D), fixed), pl.BlockSpec((E, tt), fixed)],
        out_specs=[pl.BlockSpec((TOP_K, tt), tok), pl.BlockSpec((TOP_K, tt), tok), pl.BlockSpec((TOP_K, tt), tok),
                   pl.BlockSpec((E, tt), fixed)],
        out_shape=[jax.ShapeDtypeStruct((TOP_K, T), I32), jax.ShapeDtypeStruct((TOP_K, T), F32),
                   jax.ShapeDtypeStruct((TOP_K, T), I32), jax.ShapeDtypeStruct((E, tt), F32)],
        scratch_shapes=[pltpu.VMEM((E, tt), F32)],
        compiler_params=_cparams(("arbitrary",)),
        name="router_topk",
    )(x1t, wrt_hi, wrt_lo, rbias)


def _dest_kernel(e_ref, r_ref, sp_ref, d_ref):
    E = sp_ref.shape[0]
    tt = e_ref.shape[1]
    eiota = lax.broadcasted_iota(I32, (E, tt), 0)
    sp = sp_ref[...]
    rows = []
    for kk in range(TOP_K):
        start = jnp.sum(jnp.where(eiota == e_ref[kk:kk + 1, :], sp, 0.0), axis=0, keepdims=True)
        rows.append(start.astype(I32) + r_ref[kk:kk + 1, :])
    d_ref[...] = jnp.concatenate(rows, axis=0)


def _dest(eidx, rank, start_pad_b):
    K, T = eidx.shape
    tt = ROUTE_TILE
    tok = lambda i: (0, i)
    return pl.pallas_call(
        _dest_kernel,
        grid=(T // tt,),
        in_specs=[pl.BlockSpec((K, tt), tok), pl.BlockSpec((K, tt), tok),
                  pl.BlockSpec(start_pad_b.shape, lambda i: (0, 0))],
        out_specs=pl.BlockSpec((K, tt), tok),
        out_shape=jax.ShapeDtypeStruct((K, T), I32),
        compiler_params=_cparams(("parallel",)),
        name="moe_dest",
    )(eidx, rank, start_pad_b)


def _dispatch_kernel(zb_ref, d_ref, x_ref, wsg_ref, wsu_ref, wsd_ref, xs_ref, sh_ref, zero_sc, sem, *, tt, n_exp):
    blk_rows = ROW_BLOCK * PACK_SUB

    @pl.when(pl.program_id(0) == 0)
    def _():
        zero_sc[...] = jnp.zeros(zero_sc.shape, PACKED)

        def zstart(e, c):
            @pl.when(zb_ref[e] >= 0)
            def _():
                dst = xs_ref.at[pl.ds(pl.multiple_of(zb_ref[e] * PACK_SUB, blk_rows), blk_rows), :]
                pltpu.make_async_copy(zero_sc, dst, sem).start()
            return c

        def zwait(e, c):
            @pl.when(zb_ref[e] >= 0)
            def _():
                pltpu.make_async_copy(zero_sc, xs_ref.at[pl.ds(0, blk_rows), :], sem).wait()
            return c

        lax.fori_loop(0, n_exp, zstart, 0)
        lax.fori_loop(0, n_exp, zwait, 0)

    def start(j, c):
        src = _packed_row(x_ref, j)
        for kk in range(TOP_K):
            pltpu.make_async_copy(src, _packed_row(xs_ref, d_ref[kk, j]), sem).start(priority=kk % 2)
        return c

    lax.fori_loop(0, tt, start, 0)
    xb = _unpack_rows(x_ref, tt)
    hs = _silu(jnp.dot(xb, wsg_ref[...], preferred_element_type=F32)) * jnp.dot(
        xb, wsu_ref[...], preferred_element_type=F32)
    sh_ref[...] = jnp.dot(hs.astype(BF16), wsd_ref[...], preferred_element_type=F32)
    for kk in range(TOP_K):
        pltpu.make_async_copy(x_ref, xs_ref.at[pl.ds(0, tt * PACK_SUB), :], sem).wait()


def _dispatch(zero_blk, dest, xpk, wsg, wsu, wsd, n_rows, tt=1024):
    T = xpk.shape[0] // PACK_SUB
    D = wsg.shape[0]
    n_exp = zero_blk.shape[0]
    fixed = lambda i, zb: (0, 0)
    grid_spec = pltpu.PrefetchScalarGridSpec(
        num_scalar_prefetch=1,
        grid=(T // tt,),
        in_specs=[pl.BlockSpec((TOP_K, tt), lambda i, zb: (0, i), memory_space=pltpu.SMEM),
                  pl.BlockSpec((tt * PACK_SUB, LANES), lambda i, zb: (i, 0)),
                  pl.BlockSpec(wsg.shape, fixed), pl.BlockSpec(wsu.shape, fixed), pl.BlockSpec(wsd.shape, fixed)],
        out_specs=[pl.BlockSpec(memory_space=pl.ANY), pl.BlockSpec((tt, D), lambda i, zb: (i, 0))],
        scratch_shapes=[pltpu.VMEM((ROW_BLOCK * PACK_SUB, LANES), PACKED), pltpu.SemaphoreType.DMA],
    )
    return pl.pallas_call(
        functools.partial(_dispatch_kernel, tt=tt, n_exp=n_exp),
        grid_spec=grid_spec,
        out_shape=[jax.ShapeDtypeStruct((n_rows * PACK_SUB, LANES), PACKED), jax.ShapeDtypeStruct((T, D), F32)],
        compiler_params=_cparams(("arbitrary",)),
        name="moe_dispatch",
    )(zero_blk, dest, xpk, wsg, wsu, wsd)


def _expert_kernel(sb_ref, nb_ref, nu_ref, wg_ref, wu_ref, wd_ref, xs_ref, y_ref,
                   xbuf, ybuf, wg_sc, wu_sc, wd_sc, isem, osem):
    e = pl.program_id(0)
    nb = nb_ref[e]
    base = sb_ref[e]
    n_used = nu_ref[0]
    in_rows = out_rows = ROW_BLOCK * PACK_SUB

    def fetch(blk, slot):
        src = xs_ref.at[pl.ds(pl.multiple_of(blk * in_rows, in_rows), in_rows), :]
        pltpu.make_async_copy(src, xbuf.at[slot], isem.at[slot]).start()

    def wait_in(slot):
        pltpu.make_async_copy(xs_ref.at[pl.ds(0, in_rows), :], xbuf.at[slot], isem.at[slot]).wait()

    def store(blk, slot):
        dst = y_ref.at[pl.ds(pl.multiple_of(blk * out_rows, out_rows), out_rows), :]
        pltpu.make_async_copy(ybuf.at[slot], dst, osem.at[slot]).start()

    def wait_out(slot):
        pltpu.make_async_copy(ybuf.at[slot], y_ref.at[pl.ds(0, out_rows), :], osem.at[slot]).wait()

    @pl.when(nb > 0)
    def _():
        @pl.when(base == 0)
        def _():
            for g0 in range(EXPERT_AHEAD):
                @pl.when(g0 < n_used)
                def _():
                    fetch(g0, g0 % EXPERT_BUFS)

        wg_sc[...] = wg_ref[0].astype(BF16)
        wu_sc[...] = wu_ref[0].astype(BF16)
        wd_sc[...] = wd_ref[0].astype(BF16)

        def run(blocks):
            slots = [g % EXPERT_BUFS for g in blocks]
            for g, slot in zip(blocks, slots):
                wait_in(slot)

                @pl.when(g + EXPERT_AHEAD < n_used)
                def _():
                    fetch(g + EXPERT_AHEAD, (g + EXPERT_AHEAD) % EXPERT_BUFS)

                @pl.when(g >= EXPERT_BUFS)
                def _():
                    wait_out(slot)

            for g, slot in zip(blocks, slots):
                xb = _unpack_rows(xbuf.at[slot], ROW_BLOCK)
                a = jnp.dot(xb, wg_sc[...], preferred_element_type=F32)
                u = jnp.dot(xb, wu_sc[...], preferred_element_type=F32)
                h = (_silu(a) * u).astype(BF16)
                _pack_rows(ybuf.at[slot], jnp.dot(h, wd_sc[...], preferred_element_type=F32))
            for g, slot in zip(blocks, slots):
                store(g, slot)

        def region(p, c):
            run([base + EXPERT_REGION * p + i for i in range(EXPERT_REGION)])
            return c

        lax.fori_loop(0, nb // EXPERT_REGION, region, 0)
        done = nb // EXPERT_REGION * EXPERT_REGION

        @pl.when(nb % EXPERT_REGION >= 2)
        def _():
            run([base + done, base + done + 1])

        @pl.when(nb % 2 == 1)
        def _():
            run([base + nb - 1])

    @pl.when(e == pl.num_programs(0) - 1)
    def _():
        for slot in range(EXPERT_BUFS):
            @pl.when(slot < n_used)
            def _():
                wait_out(slot)


def _experts(start_blk, n_blk_e, n_used, xs, w_gate, w_up, w_down):
    n_rows = xs.shape[0] // PACK_SUB
    E, D, H = w_gate.shape
    wsel = lambda e, sb, nb, nu: (e, 0, 0)
    grid_spec = pltpu.PrefetchScalarGridSpec(
        num_scalar_prefetch=3,
        grid=(E,),
        in_specs=[pl.BlockSpec((1, D, H), wsel), pl.BlockSpec((1, D, H), wsel), pl.BlockSpec((1, H, D), wsel),
                  pl.BlockSpec(memory_space=pl.ANY)],
        out_specs=pl.BlockSpec(memory_space=pl.ANY),
        scratch_shapes=[pltpu.VMEM((EXPERT_BUFS, ROW_BLOCK * PACK_SUB, LANES), PACKED),
                        pltpu.VMEM((EXPERT_BUFS, ROW_BLOCK * PACK_SUB, LANES), PACKED),
                        pltpu.VMEM((D, H), BF16), pltpu.VMEM((D, H), BF16), pltpu.VMEM((H, D), BF16),
                        pltpu.SemaphoreType.DMA((EXPERT_BUFS,)), pltpu.SemaphoreType.DMA((EXPERT_BUFS,))],
    )
    return pl.pallas_call(
        _expert_kernel,
        grid_spec=grid_spec,
        out_shape=jax.ShapeDtypeStruct((n_rows * PACK_SUB, LANES), PACKED),
        compiler_params=_cparams(("arbitrary",)),
        name="moe_experts",
    )(start_blk, n_blk_e, n_used, w_gate, w_up, w_down, xs)


def _combine_kernel(dcur_ref, dnxt_ref, x_ref, gate_ref, sh_ref, g_ref, b_ref, y_ref, o_ref, buf, sem, *, tt, alpha):
    i = pl.program_id(0)
    n = pl.num_programs(0)

    def fetch(d_ref, slot):
        def body(j, c):
            for kk in range(TOP_K):
                src = _packed_row(y_ref, d_ref[kk * tt + j])
                pltpu.make_async_copy(src, _packed_row(buf.at[slot, kk], j), sem.at[slot]).start(priority=kk % 2)
            return c

        lax.fori_loop(0, tt, body, 0)

    def reduce(slot):
        for kk in range(TOP_K):
            pltpu.make_async_copy(y_ref.at[pl.ds(0, tt * PACK_SUB), :], buf.at[slot, kk], sem.at[slot]).wait()
        x = _load_rows(x_ref, tt)
        moe = sh_ref[...]
        gates = gate_ref[...]
        for kk in range(TOP_K):
            moe = moe + gates[:, kk:kk + 1] * _unpack_rows(buf.at[slot, kk], tt, F32)
        o_ref[...] = _layer_norm(alpha * x + moe, g_ref[...], b_ref[...])

    @pl.when(i == 0)
    def _():
        fetch(dcur_ref, 0)

    for slot in range(2):
        @pl.when((i % 2 == slot) & (i + 1 < n))
        def _():
            fetch(dnxt_ref, 1 - slot)

    for slot in range(2):
        @pl.when(i % 2 == slot)
        def _():
            reduce(slot)


def _combine(dest, x1t, gates, shared, g, b, y, alpha, tt=512):
    T = x1t.shape[0] // SUBLANES
    D = SUBLANES * LANES
    n = T // tt
    dest = dest.reshape(TOP_K, n, tt).transpose(1, 0, 2).reshape(n * TOP_K * tt)
    row = lambda i: (i, 0)
    fixed = lambda i: (0, 0)
    return pl.pallas_call(
        functools.partial(_combine_kernel, tt=tt, alpha=alpha),
        grid=(n,),
        in_specs=[pl.BlockSpec((TOP_K * tt,), lambda i: (i,), memory_space=pltpu.SMEM),
                  pl.BlockSpec((TOP_K * tt,), lambda i: (jnp.minimum(i + 1, n - 1),), memory_space=pltpu.SMEM),
                  pl.BlockSpec((tt * SUBLANES, LANES), row), pl.BlockSpec((tt, TOP_K), row),
                  pl.BlockSpec((tt, D), row),
                  pl.BlockSpec((1, D), fixed), pl.BlockSpec((1, D), fixed),
                  pl.BlockSpec(memory_space=pl.ANY)],
        out_specs=pl.BlockSpec((tt, D), row),
        out_shape=jax.ShapeDtypeStruct((T, D), F32),
        scratch_shapes=[pltpu.VMEM((2, TOP_K, tt * PACK_SUB, LANES), PACKED), pltpu.SemaphoreType.DMA((2,))],
        compiler_params=_cparams(("arbitrary",)),
        name="moe_combine_ln2",
    )(dest, dest, x1t, gates, shared, g, b, y)


def _layer(x2, mem, B, S, depth, w_in, b_f, w_pool, pool_scale, w_mem_kv, w_out, ln1_g, ln1_b,
           w_router, router_bias, w_gate, w_up, w_down, ws_gate, ws_up, ws_down, ln2_g, ln2_b):
    T, D = x2.shape
    n_win, grp = w_pool.shape[0], w_pool.shape[1]
    pw = n_win * grp
    fw = FOX_HEADS * HEAD_DIM
    mw = w_mem_kv.shape[1] // 2
    E = w_router.shape[1]
    alpha = (2 * depth) ** 0.25

    f_lo = pw + 3 * fw
    w_main = jnp.concatenate([w_in[:, :f_lo], w_in[:, f_lo + FOX_HEADS:]], axis=1).astype(BF16)
    w_f = jnp.pad(w_in[:, f_lo:f_lo + FOX_HEADS], ((0, 0), (0, LANES - FOX_HEADS))).astype(BF16)
    bf_pad = jnp.pad(b_f, (0, LANES - FOX_HEADS)).reshape(1, LANES)
    wbd = jnp.zeros((pw, pw), F32)
    for g in range(n_win):
        wbd = wbd.at[g * grp:(g + 1) * grp, g * grp:(g + 1) * grp].set(w_pool[g])
    wrt = w_router.T
    wrt_hi = wrt.astype(BF16)
    wrt_lo = (wrt - wrt_hi.astype(F32)).astype(BF16)

    u_pool, q, k, v, q_mem, f_logit = _inproj(x2, w_main, w_f, pw, fw, mw)
    y_pool = _pool(u_pool, wbd.astype(BF16), pool_scale.reshape(1, pw), B, S)
    fcum = _fgate(f_logit, bf_pad, B, S)
    y_fox = _fox(q, k, v, fcum, B, S)
    y_mem = _memattn(q_mem, mem, w_mem_kv.astype(BF16), B, S)
    x1, x1_packed = _outproj(y_pool, y_fox, y_mem, x2, w_out.astype(BF16), ln1_g.reshape(1, D),
                             ln1_b.reshape(1, D), alpha)

    eidx, gates, rank, counts = _router(x1, wrt_hi, wrt_lo, _per_expert_column(router_bias))

    cnt = counts[:, 0].astype(I32)
    padded = (cnt + ROW_BLOCK - 1) // ROW_BLOCK * ROW_BLOCK
    pad_end = jnp.cumsum(padded)
    start_pad = (pad_end - padded).astype(I32)
    n_rows = (T * TOP_K + E * (ROW_BLOCK - 1)) // ROW_BLOCK * ROW_BLOCK
    n_used = (pad_end[-1] // ROW_BLOCK).astype(I32)
    zero_blk = jnp.where(padded > 0, pad_end - ROW_BLOCK, -1).astype(I32)
    dest = _dest(eidx, rank, _per_expert_column(start_pad))

    xs, shared = _dispatch(zero_blk, dest, x1_packed, ws_gate.astype(BF16), ws_up.astype(BF16),
                           ws_down.astype(BF16), n_rows)
    y = _experts(start_pad // ROW_BLOCK, (padded // ROW_BLOCK).astype(I32), n_used.reshape(1), xs,
                 w_gate, w_up, w_down)
    return _combine(dest, x1, gates.T, shared, ln2_g.reshape(1, D), ln2_b.reshape(1, D), y, alpha)


def kernel(x, mem, w_in, b_f, w_pool, pool_scale, w_mem_kv, w_out, ln1_g, ln1_b, w_router, router_bias,
           w_gate, w_up, w_down, ws_gate, ws_up, ws_down, ln2_g, ln2_b):
    B, S, D = x.shape
    depth = w_in.shape[0]
    x2 = x.reshape(B * S, D)
    for l in range(depth):
        x2 = _layer(x2, mem, B, S, depth, w_in[l], b_f[l], w_pool[l], pool_scale[l], w_mem_kv[l], w_out[l],
                    ln1_g[l], ln1_b[l], w_router[l], router_bias[l], w_gate[l], w_up[l], w_down[l],
                    ws_gate[l], ws_up[l], ws_down[l], ln2_g[l], ln2_b[l])
    return x2.reshape(B, S, D)
```

```python
import functools

import jax
import jax.numpy as jnp
from jax import lax
from jax.experimental import pallas as pl
from jax.experimental.pallas import tpu as pltpu

F32 = jnp.float32
BF16 = jnp.bfloat16
I32 = jnp.int32
PACKED = jnp.int32

LANES = 128
POOL_WINDOWS = (2, 4, 8, 16)
POOL_HALO = 16
HEAD_DIM = 64
FOX_HEADS = 8
BIAS_TERMS = 3
MEM_HEADS = 4
TOP_K = 8
N_GROUPS = 8
TOPK_GROUPS = 4
ROUTED_SCALE = 2.5
LN_EPS = 1e-5
LOG2E = 1.4426950408889634
ROW_BLOCK = 256
EXPERT_BUFS = 8
EXPERT_REGION = 2
EXPERT_AHEAD = EXPERT_BUFS - EXPERT_REGION
VMEM_LIMIT = 48 * 1024 * 1024


def _cparams(sem):
    return pltpu.CompilerParams(dimension_semantics=sem, vmem_limit_bytes=VMEM_LIMIT)


def _layer_norm(z, g, b):
    mu = jnp.mean(z, axis=-1, keepdims=True)
    zc = z - mu
    var = jnp.mean(zc * zc, axis=-1, keepdims=True)
    return zc * lax.rsqrt(var + LN_EPS) * g + b


def _silu(x):
    return x * (1.0 / (1.0 + jnp.exp(-x)))


def _inproj_kernel(x_ref, w_ref, wf_ref, up_ref, q_ref, k_ref, v_ref, qm_ref, f_ref, *, pw, fw, mw, scale):
    xb = x_ref[...].astype(BF16)

    def proj(lo, hi):
        return jnp.dot(xb, w_ref[:, lo:hi], preferred_element_type=F32)

    up_ref[...] = proj(0, pw)
    q_ref[...] = (proj(pw, pw + fw) * (scale * LOG2E)).astype(BF16)
    k_ref[...] = proj(pw + fw, pw + 2 * fw).astype(BF16)
    v_ref[...] = proj(pw + 2 * fw, pw + 3 * fw).astype(BF16)
    qm_ref[...] = (proj(pw + 3 * fw, pw + 3 * fw + mw) * scale).astype(BF16)
    f_ref[...] = jnp.dot(xb, wf_ref[...], preferred_element_type=F32)


def _inproj(x2, w_main, w_f, pw, fw, mw, tm=512):
    T, D = x2.shape
    kern = functools.partial(_inproj_kernel, pw=pw, fw=fw, mw=mw, scale=HEAD_DIM ** -0.5)
    row = lambda i: (i, 0)
    fixed = lambda i: (0, 0)
    return pl.pallas_call(
        kern,
        grid=(T // tm,),
        in_specs=[pl.BlockSpec((tm, D), row),
                  pl.BlockSpec(w_main.shape, fixed),
                  pl.BlockSpec(w_f.shape, fixed)],
        out_specs=[pl.BlockSpec((tm, pw), row), pl.BlockSpec((tm, fw), row), pl.BlockSpec((tm, fw), row),
                   pl.BlockSpec((tm, fw), row), pl.BlockSpec((tm, mw), row), pl.BlockSpec((tm, LANES), row)],
        out_shape=[jax.ShapeDtypeStruct((T, pw), F32), jax.ShapeDtypeStruct((T, fw), BF16),
                   jax.ShapeDtypeStruct((T, fw), BF16), jax.ShapeDtypeStruct((T, fw), BF16),
                   jax.ShapeDtypeStruct((T, mw), BF16), jax.ShapeDtypeStruct((T, LANES), F32)],
        compiler_params=_cparams(("parallel",)),
        name="inproj",
    )(x2, w_main, w_f)


def _pool_kernel(u_ref, wbd_ref, sc_ref, o_ref, ext_sc, *, chunk, group):
    S, W = u_ref.shape
    ext_sc[0:POOL_HALO, :] = jnp.zeros((POOL_HALO, W), F32)
    ext_sc[POOL_HALO:, :] = u_ref[...]
    rows = chunk + POOL_HALO
    lrow = lax.broadcasted_iota(I32, (rows, W), 0)
    lane = lax.broadcasted_iota(I32, (rows, W), 1)

    def body(c, carry):
        start = pl.multiple_of(c * chunk, chunk)
        e = ext_sc[pl.ds(start, rows), :]
        posf = (lrow + (start - POOL_HALO + 1)).astype(F32)
        acc = e
        d = jnp.zeros_like(e)
        shift = 1
        for g, w in enumerate(POOL_WINDOWS):
            while shift < w:
                acc = acc + pltpu.roll(acc, shift, axis=0)
                shift *= 2
            mean = acc / jnp.minimum(posf, float(w))
            d = jnp.where((lane >= g * group) & (lane < (g + 1) * group), mean, d)
        d = (d - e)[POOL_HALO:, :]
        y = jnp.dot(d.astype(BF16), wbd_ref[...], preferred_element_type=F32) * sc_ref[...]
        o_ref[pl.ds(start, chunk), :] = y.astype(BF16)
        return carry

    lax.fori_loop(0, S // chunk, body, 0)


def _pool(u, wbd, pscale, B, S, chunk=512):
    T, W = u.shape
    kern = functools.partial(_pool_kernel, chunk=chunk, group=W // len(POOL_WINDOWS))
    return pl.pallas_call(
        kern,
        grid=(B,),
        in_specs=[pl.BlockSpec((S, W), lambda b: (b, 0)),
                  pl.BlockSpec((W, W), lambda b: (0, 0)),
                  pl.BlockSpec((1, W), lambda b: (0, 0))],
        out_specs=pl.BlockSpec((S, W), lambda b: (b, 0)),
        out_shape=jax.ShapeDtypeStruct((T, W), BF16),
        scratch_shapes=[pltpu.VMEM((S + POOL_HALO, W), F32)],
        compiler_params=_cparams(("parallel",)),
        name="pool_mixer",
    )(u, wbd, pscale)


def _fgate_kernel(f_ref, bf_ref, o_ref):
    S = f_ref.shape[0]
    z = f_ref[...] + bf_ref[...]
    x = jnp.minimum(z, 0.0) - jnp.log(1.0 + jnp.exp(-jnp.abs(z)))
    row = lax.broadcasted_iota(I32, x.shape, 0)
    lane = lax.broadcasted_iota(I32, x.shape, 1)
    shift = 1
    while shift < S:
        x = x + jnp.where(row >= shift, pltpu.roll(x, shift, axis=0), 0.0)
        shift *= 2
    x = x * LOG2E
    hi = x.astype(BF16).astype(F32)
    mid = (x - hi).astype(BF16).astype(F32)
    lo = (x - hi) - mid
    for p in range(FOX_HEADS // 2):
        out = jnp.zeros(x.shape, F32)
        for hh in range(2):
            c = 2 * p + hh
            for j, part in enumerate((hi, mid, lo)):
                out = jnp.where(lane == BIAS_TERMS * hh + j, -part[:, c:c + 1], out)
        o_ref[0, p] = out.astype(BF16)


def _fgate(f_logit, bf_pad, B, S):
    return pl.pallas_call(
        _fgate_kernel,
        grid=(B,),
        in_specs=[pl.BlockSpec((S, LANES), lambda b: (b, 0)),
                  pl.BlockSpec((1, LANES), lambda b: (0, 0))],
        out_specs=pl.BlockSpec((1, FOX_HEADS // 2, S, LANES), lambda b: (b, 0, 0, 0)),
        out_shape=jax.ShapeDtypeStruct((B, FOX_HEADS // 2, S, LANES), BF16),
        compiler_params=_cparams(("parallel",)),
        name="forget_cumsum",
    )(f_logit, bf_pad)


def _fox_kernel(q_ref, k_ref, a_ref, v_ref, o_ref, vt_sc, qa_sc, sa_sc, sb_sc, m_sc, l_sc, acc_sc, *, blk):
    qi = pl.program_id(2)
    nk = vt_sc.shape[0]
    nt = (((1,), (1,)), ((), ()))

    @pl.when(qi == 0)
    def _():
        for c in range(nk):
            vt_sc[c] = v_ref[c * blk:(c + 1) * blk, :].astype(F32).T.astype(BF16)

    q = q_ref[...].astype(F32)
    lane = lax.broadcasted_iota(I32, q.shape, 1)
    for h in range(2):
        mine = (lane >= h * HEAD_DIM) & (lane < (h + 1) * HEAD_DIM)
        bias_rows = (lane >= h * BIAS_TERMS) & (lane < (h + 1) * BIAS_TERMS)
        qa_sc[h] = jnp.concatenate([jnp.where(mine, q, 0.0), jnp.where(bias_rows, 1.0, 0.0)], axis=1).astype(BF16)
    m_sc[...] = jnp.full(m_sc.shape, -jnp.inf, F32)
    l_sc[...] = jnp.zeros(l_sc.shape, F32)
    acc_sc[...] = jnp.zeros(acc_sc.shape, F32)

    def scores(ki, dst):
        ks = pl.multiple_of(ki * blk, blk)
        kb = jnp.concatenate([k_ref[pl.ds(ks, blk), :], a_ref[0, 0, pl.ds(ks, blk), :]], axis=1)
        for h in range(2):
            dst[h] = lax.dot_general(kb, qa_sc[h], nt, preferred_element_type=F32)

    def absorb(ki, src, causal):
        for h in range(2):
            st = src[h]
            if causal:
                kpos = lax.broadcasted_iota(I32, st.shape, 0)
                qpos = lax.broadcasted_iota(I32, st.shape, 1)
                st = jnp.where(kpos <= qpos, st, -jnp.inf)
            m_prev = m_sc[h]
            m_new = jnp.maximum(m_prev, jnp.max(st, axis=0, keepdims=True))
            alpha = jnp.exp2(m_prev - m_new)
            p = jnp.exp2(st - m_new)
            l_sc[h] = alpha * l_sc[h] + jnp.sum(p, axis=0, keepdims=True)
            m_sc[h] = m_new
            vt = vt_sc[ki, h * HEAD_DIM:(h + 1) * HEAD_DIM, :]
            pv = jnp.dot(vt, p.astype(BF16), preferred_element_type=F32)
            rows = slice(h * HEAD_DIM, (h + 1) * HEAD_DIM)
            acc_sc[rows, :] = acc_sc[rows, :] * alpha + pv

    scores(0, sa_sc)

    def body(p, carry):
        scores(2 * p + 1, sb_sc)
        absorb(2 * p, sa_sc, False)
        scores(2 * p + 2, sa_sc)
        absorb(2 * p + 1, sb_sc, False)
        return carry

    lax.fori_loop(0, qi // 2, body, 0)

    @pl.when(qi % 2 == 1)
    def _():
        scores(qi, sb_sc)
        absorb(qi - 1, sa_sc, False)
        absorb(qi, sb_sc, True)

    @pl.when(qi % 2 == 0)
    def _():
        absorb(qi, sa_sc, True)

    out_t = jnp.concatenate([acc_sc[h * HEAD_DIM:(h + 1) * HEAD_DIM, :] / l_sc[h] for h in range(2)], axis=0)
    o_ref[...] = out_t.T.astype(BF16)


def _fox(q, k, v, fbias, B, S, blk=512):
    T, FW = q.shape
    nq = S // blk
    pairs = FW // LANES
    kern = functools.partial(_fox_kernel, blk=blk)
    return pl.pallas_call(
        kern,
        grid=(B, pairs, nq),
        in_specs=[pl.BlockSpec((blk, LANES), lambda b, p, i: (b * nq + i, p)),
                  pl.BlockSpec((S, LANES), lambda b, p, i: (b, p)),
                  pl.BlockSpec((1, 1, S, LANES), lambda b, p, i: (b, p, 0, 0)),
                  pl.BlockSpec((S, LANES), lambda b, p, i: (b, p))],
        out_specs=pl.BlockSpec((blk, LANES), lambda b, p, i: (b * nq + i, p)),
        out_shape=jax.ShapeDtypeStruct((T, FW), BF16),
        scratch_shapes=[pltpu.VMEM((nq, LANES, blk), BF16), pltpu.VMEM((2, blk, 2 * LANES), BF16),
                        pltpu.VMEM((2, blk, blk), F32), pltpu.VMEM((2, blk, blk), F32),
                        pltpu.VMEM((2, 1, blk), F32), pltpu.VMEM((2, 1, blk), F32), pltpu.VMEM((LANES, blk), F32)],
        compiler_params=_cparams(("parallel", "parallel", "arbitrary")),
        name="fox_attention",
    )(q, k, fbias, v)


def _memattn_kernel(qm_ref, mem_ref, wkv_ref, o_ref, k_sc, v_sc):
    MW = qm_ref.shape[1]

    @pl.when(pl.program_id(1) == 0)
    def _():
        kv = jnp.dot(mem_ref[0].astype(BF16), wkv_ref[...], preferred_element_type=F32)
        k_sc[...] = kv[:, :MW].astype(BF16)
        v_sc[...] = kv[:, MW:].astype(BF16)

    q = qm_ref[...]
    lane = lax.broadcasted_iota(I32, q.shape, 1)
    hd = MW // MEM_HEADS
    out = jnp.zeros(q.shape, F32)
    for h in range(MEM_HEADS):
        mine = (lane >= h * hd) & (lane < (h + 1) * hd)
        qh = jnp.where(mine, q, jnp.zeros_like(q))
        s = lax.dot_general(qh, k_sc[...], (((1,), (1,)), ((), ())), preferred_element_type=F32)
        p = jnp.exp(s - jnp.max(s, axis=1, keepdims=True))
        l = jnp.sum(p, axis=1, keepdims=True)
        o = jnp.dot(p.astype(BF16), v_sc[...], preferred_element_type=F32)
        out = jnp.where(mine, o / l, out)
    o_ref[...] = out.astype(BF16)


def _memattn(qm, mem, wkv, B, S, tq=512):
    T, MW = qm.shape
    M, D = mem.shape[1], mem.shape[2]
    nq = S // tq
    return pl.pallas_call(
        _memattn_kernel,
        grid=(B, nq),
        in_specs=[pl.BlockSpec((tq, MW), lambda b, i: (b * nq + i, 0)),
                  pl.BlockSpec((1, M, D), lambda b, i: (b, 0, 0)),
                  pl.BlockSpec(wkv.shape, lambda b, i: (0, 0))],
        out_specs=pl.BlockSpec((tq, MW), lambda b, i: (b * nq + i, 0)),
        out_shape=jax.ShapeDtypeStruct((T, MW), BF16),
        scratch_shapes=[pltpu.VMEM((M, MW), BF16), pltpu.VMEM((M, MW), BF16)],
        compiler_params=_cparams(("parallel", "arbitrary")),
        name="memory_attention",
    )(qm, mem, wkv)


PACK_SUB = 4


def _pack_rows(ref, val):
    n = val.shape[0]
    half = val.shape[1] // 2
    words = pltpu.pack_elementwise([val[:, :half], val[:, half:]], packed_dtype=BF16)
    for j in range(PACK_SUB):
        ref[pl.ds(j, n, stride=PACK_SUB), :] = words[:, j * LANES:(j + 1) * LANES]


def _unpack_rows(ref, n, dtype=BF16):
    slabs = [ref[pl.ds(j, n, stride=PACK_SUB), :] for j in range(PACK_SUB)]
    lo = [pltpu.unpack_elementwise(s, index=0, packed_dtype=BF16, unpacked_dtype=F32) for s in slabs]
    hi = [pltpu.unpack_elementwise(s, index=1, packed_dtype=BF16, unpacked_dtype=F32) for s in slabs]
    return jnp.concatenate(lo + hi, axis=1).astype(dtype)


def _packed_row(ref, r):
    return ref.at[pl.ds(pl.multiple_of(r * PACK_SUB, PACK_SUB), PACK_SUB), :]


def _outproj_kernel(yp_ref, yf_ref, ym_ref, x_ref, w_ref, g_ref, b_ref, o_ref, pk_ref, *, alpha):
    pw, fw = yp_ref.shape[1], yf_ref.shape[1]
    h = jnp.dot(yp_ref[...], w_ref[0:pw, :], preferred_element_type=F32)
    h = h + jnp.dot(yf_ref[...], w_ref[pw:pw + fw, :], preferred_element_type=F32)
    h = h + jnp.dot(ym_ref[...], w_ref[pw + fw:, :], preferred_element_type=F32)
    y = _layer_norm(alpha * x_ref[...] + h, g_ref[...], b_ref[...])
    o_ref[...] = y
    _pack_rows(pk_ref, y)


def _outproj(yp, yf, ym, x2, w_out, g, b, alpha, tm=512):
    T, D = x2.shape
    assert D == 2 * PACK_SUB * LANES
    row = lambda i: (i, 0)
    fixed = lambda i: (0, 0)
    return pl.pallas_call(
        functools.partial(_outproj_kernel, alpha=alpha),
        grid=(T // tm,),
        in_specs=[pl.BlockSpec((tm, yp.shape[1]), row), pl.BlockSpec((tm, yf.shape[1]), row),
                  pl.BlockSpec((tm, ym.shape[1]), row), pl.BlockSpec((tm, D), row),
                  pl.BlockSpec(w_out.shape, fixed), pl.BlockSpec((1, D), fixed), pl.BlockSpec((1, D), fixed)],
        out_specs=[pl.BlockSpec((tm, D), row),
                   pl.BlockSpec((tm * PACK_SUB, LANES), row)],
        out_shape=[jax.ShapeDtypeStruct((T, D), F32),
                   jax.ShapeDtypeStruct((T * PACK_SUB, LANES), PACKED)],
        compiler_params=_cparams(("parallel",)),
        name="outproj_ln1",
    )(yp, yf, ym, x2, w_out, g, b)


ROUTE_TILE = 512


def _per_expert_column(v, dtype=F32):
    return jnp.broadcast_to(v.astype(dtype)[:, None], (v.shape[0], ROUTE_TILE))


def _router_kernel(x_ref, wh_ref, wl_ref, bias_ref, eidx_ref, gate_ref, rank_ref, cnt_ref, carry_sc, *, tt):
    E = wh_ref.shape[0]
    gsz = E // N_GROUPS
    ninf = -jnp.inf

    @pl.when(pl.program_id(0) == 0)
    def _():
        carry_sc[...] = jnp.zeros(carry_sc.shape, F32)

    x = x_ref[...]
    xh = x.astype(BF16)
    xl = (x - xh.astype(F32)).astype(BF16)
    wh = wh_ref[...]
    nt = (((1,), (1,)), ((), ()))
    logits = lax.dot_general(wh, xh, nt, preferred_element_type=F32) + (
        lax.dot_general(wl_ref[...], xh, nt, preferred_element_type=F32)
        + lax.dot_general(wh, xl, nt, preferred_element_type=F32))
    scores = 1.0 / (1.0 + jnp.exp(-logits))
    biased = scores + bias_ref[...]
    eiota = lax.broadcasted_iota(I32, (E, tt), 0).astype(F32)
    giota = lax.broadcasted_iota(I32, (gsz, tt), 0).astype(F32)

    def cmax(a):
        return jnp.max(a, axis=0, keepdims=True)

    def first_at(a, m, iota, n):
        return jnp.min(jnp.where(a == m, iota, float(n)), axis=0, keepdims=True)

    groups = [biased[g * gsz:(g + 1) * gsz, :] for g in range(N_GROUPS)]
    gscore = []
    for blk in groups:
        m1 = cmax(blk)
        m2 = cmax(jnp.where(giota == first_at(blk, m1, giota, gsz), ninf, blk))
        gscore.append(m1 + m2)

    kept = []
    for g in range(N_GROUPS):
        ahead = jnp.zeros((1, tt), F32)
        for o in range(N_GROUPS):
            if o == g:
                continue
            beats = gscore[o] > gscore[g]
            if o < g:
                beats = beats | (gscore[o] == gscore[g])
            ahead = ahead + jnp.where(beats, 1.0, 0.0)
        kept.append(groups[g] + jnp.where(ahead < float(TOPK_GROUPS), 0.0, ninf))
    masked = jnp.concatenate(kept, axis=0)

    picks, sels = [], []
    chosen = jnp.zeros((E, tt), F32)
    for _ in range(TOP_K):
        ik = first_at(masked, cmax(masked), eiota, E)
        oh = eiota == ik
        picks.append(ik)
        sels.append(jnp.sum(jnp.where(oh, scores, 0.0), axis=0, keepdims=True))
        chosen = jnp.where(oh, 1.0, chosen)
        masked = jnp.where(oh, ninf, masked)
    denom = sels[0]
    for sk in sels[1:]:
        denom = denom + sk

    r = lax.broadcasted_iota(I32, (tt, tt), 0)
    c = lax.broadcasted_iota(I32, (tt, tt), 1)
    earlier = jnp.where(r < c, 1.0, 0.0).astype(BF16)
    chosen_b = chosen.astype(BF16)
    pos = carry_sc[...] + jnp.dot(chosen_b, earlier, preferred_element_type=F32)
    carry_sc[...] = carry_sc[...] + jnp.dot(chosen_b, jnp.ones((tt, tt), BF16), preferred_element_type=F32)
    cnt_ref[...] = carry_sc[...]

    ranks = [jnp.sum(jnp.where(eiota == ik, pos, 0.0), axis=0, keepdims=True) for ik in picks]
    eidx_ref[...] = jnp.concatenate(picks, axis=0).astype(I32)
    gate_ref[...] = jnp.concatenate([sk / denom * ROUTED_SCALE for sk in sels], axis=0)
    rank_ref[...] = jnp.concatenate(ranks, axis=0).astype(I32)


def _router(x1, wrt_hi, wrt_lo, rbias):
    T = x1.shape[0]
    E, D = wrt_hi.shape
    tt = ROUTE_TILE
    tok = lambda i: (0, i)
    fixed = lambda i: (0, 0)
    return pl.pallas_call(
        functools.partial(_router_kernel, tt=tt),
        grid=(T // tt,),
        in_specs=[pl.BlockSpec((tt, D), lambda i: (i, 0)), pl.BlockSpec((E, D), fixed),
                  pl.BlockSpec((E, D), fixed), pl.BlockSpec((E, tt), fixed)],
        out_specs=[pl.BlockSpec((TOP_K, tt), tok), pl.BlockSpec((TOP_K, tt), tok), pl.BlockSpec((TOP_K, tt), tok),
                   pl.BlockSpec((E, tt), fixed)],
        out_shape=[jax.ShapeDtypeStruct((TOP_K, T), I32), jax.ShapeDtypeStruct((TOP_K, T), F32),
                   jax.ShapeDtypeStruct((TOP_K, T), I32), jax.ShapeDtypeStruct((E, tt), F32)],
        scratch_shapes=[pltpu.VMEM((E, tt), F32)],
        compiler_params=_cparams(("arbitrary",)),
        name="router_topk",
    )(x1, wrt_hi, wrt_lo, rbias)


def _dest_kernel(e_ref, r_ref, sp_ref, d_ref):
    E = sp_ref.shape[0]
    tt = e_ref.shape[1]
    eiota = lax.broadcasted_iota(I32, (E, tt), 0)
    sp = sp_ref[...]
    rows = []
    for kk in range(TOP_K):
        start = jnp.sum(jnp.where(eiota == e_ref[kk:kk + 1, :], sp, 0.0), axis=0, keepdims=True)
        rows.append(start.astype(I32) + r_ref[kk:kk + 1, :])
    d_ref[...] = jnp.concatenate(rows, axis=0)


def _dest(eidx, rank, start_pad_b):
    K, T = eidx.shape
    tt = ROUTE_TILE
    tok = lambda i: (0, i)
    return pl.pallas_call(
        _dest_kernel,
        grid=(T // tt,),
        in_specs=[pl.BlockSpec((K, tt), tok), pl.BlockSpec((K, tt), tok),
                  pl.BlockSpec(start_pad_b.shape, lambda i: (0, 0))],
        out_specs=pl.BlockSpec((K, tt), tok),
        out_shape=jax.ShapeDtypeStruct((K, T), I32),
        compiler_params=_cparams(("parallel",)),
        name="moe_dest",
    )(eidx, rank, start_pad_b)


def _dispatch_kernel(zb_ref, d_ref, x_ref, wsg_ref, wsu_ref, wsd_ref, xs_ref, sh_ref, zero_sc, sem, *, tt, n_exp):
    blk_rows = ROW_BLOCK * PACK_SUB

    @pl.when(pl.program_id(0) == 0)
    def _():
        zero_sc[...] = jnp.zeros(zero_sc.shape, PACKED)

        def zstart(e, c):
            @pl.when(zb_ref[e] >= 0)
            def _():
                dst = xs_ref.at[pl.ds(pl.multiple_of(zb_ref[e] * PACK_SUB, blk_rows), blk_rows), :]
                pltpu.make_async_copy(zero_sc, dst, sem).start()
            return c

        def zwait(e, c):
            @pl.when(zb_ref[e] >= 0)
            def _():
                pltpu.make_async_copy(zero_sc, xs_ref.at[pl.ds(0, blk_rows), :], sem).wait()
            return c

        lax.fori_loop(0, n_exp, zstart, 0)
        lax.fori_loop(0, n_exp, zwait, 0)

    def start(j, c):
        src = _packed_row(x_ref, j)
        for kk in range(TOP_K):
            pltpu.make_async_copy(src, _packed_row(xs_ref, d_ref[kk, j]), sem).start(priority=kk % 2)
        return c

    lax.fori_loop(0, tt, start, 0)
    xb = _unpack_rows(x_ref, tt)
    hs = _silu(jnp.dot(xb, wsg_ref[...], preferred_element_type=F32)) * jnp.dot(
        xb, wsu_ref[...], preferred_element_type=F32)
    sh_ref[...] = jnp.dot(hs.astype(BF16), wsd_ref[...], preferred_element_type=F32)
    for kk in range(TOP_K):
        pltpu.make_async_copy(x_ref, xs_ref.at[pl.ds(0, tt * PACK_SUB), :], sem).wait()


def _dispatch(zero_blk, dest, xpk, wsg, wsu, wsd, n_rows, tt=1024):
    T = xpk.shape[0] // PACK_SUB
    D = wsg.shape[0]
    n_exp = zero_blk.shape[0]
    fixed = lambda i, zb: (0, 0)
    grid_spec = pltpu.PrefetchScalarGridSpec(
        num_scalar_prefetch=1,
        grid=(T // tt,),
        in_specs=[pl.BlockSpec((TOP_K, tt), lambda i, zb: (0, i), memory_space=pltpu.SMEM),
                  pl.BlockSpec((tt * PACK_SUB, LANES), lambda i, zb: (i, 0)),
                  pl.BlockSpec(wsg.shape, fixed), pl.BlockSpec(wsu.shape, fixed), pl.BlockSpec(wsd.shape, fixed)],
        out_specs=[pl.BlockSpec(memory_space=pl.ANY), pl.BlockSpec((tt, D), lambda i, zb: (i, 0))],
        scratch_shapes=[pltpu.VMEM((ROW_BLOCK * PACK_SUB, LANES), PACKED), pltpu.SemaphoreType.DMA],
    )
    return pl.pallas_call(
        functools.partial(_dispatch_kernel, tt=tt, n_exp=n_exp),
        grid_spec=grid_spec,
        out_shape=[jax.ShapeDtypeStruct((n_rows * PACK_SUB, LANES), PACKED), jax.ShapeDtypeStruct((T, D), F32)],
        compiler_params=_cparams(("arbitrary",)),
        name="moe_dispatch",
    )(zero_blk, dest, xpk, wsg, wsu, wsd)


def _expert_kernel(sb_ref, nb_ref, nu_ref, wg_ref, wu_ref, wd_ref, xs_ref, y_ref,
                   xbuf, ybuf, wg_sc, wu_sc, wd_sc, isem, osem):
    e = pl.program_id(0)
    nb = nb_ref[e]
    base = sb_ref[e]
    n_used = nu_ref[0]
    in_rows = out_rows = ROW_BLOCK * PACK_SUB

    def fetch(blk, slot):
        src = xs_ref.at[pl.ds(pl.multiple_of(blk * in_rows, in_rows), in_rows), :]
        pltpu.make_async_copy(src, xbuf.at[slot], isem.at[slot]).start()

    def wait_in(slot):
        pltpu.make_async_copy(xs_ref.at[pl.ds(0, in_rows), :], xbuf.at[slot], isem.at[slot]).wait()

    def store(blk, slot):
        dst = y_ref.at[pl.ds(pl.multiple_of(blk * out_rows, out_rows), out_rows), :]
        pltpu.make_async_copy(ybuf.at[slot], dst, osem.at[slot]).start()

    def wait_out(slot):
        pltpu.make_async_copy(ybuf.at[slot], y_ref.at[pl.ds(0, out_rows), :], osem.at[slot]).wait()

    @pl.when(nb > 0)
    def _():
        @pl.when(base == 0)
        def _():
            for g0 in range(EXPERT_AHEAD):
                @pl.when(g0 < n_used)
                def _():
                    fetch(g0, g0 % EXPERT_BUFS)

        wg_sc[...] = wg_ref[0].astype(BF16)
        wu_sc[...] = wu_ref[0].astype(BF16)
        wd_sc[...] = wd_ref[0].astype(BF16)

        def run(blocks):
            slots = [g % EXPERT_BUFS for g in blocks]
            for g, slot in zip(blocks, slots):
                wait_in(slot)

                @pl.when(g + EXPERT_AHEAD < n_used)
                def _():
                    fetch(g + EXPERT_AHEAD, (g + EXPERT_AHEAD) % EXPERT_BUFS)

                @pl.when(g >= EXPERT_BUFS)
                def _():
                    wait_out(slot)

            for g, slot in zip(blocks, slots):
                xb = _unpack_rows(xbuf.at[slot], ROW_BLOCK)
                a = jnp.dot(xb, wg_sc[...], preferred_element_type=F32)
                u = jnp.dot(xb, wu_sc[...], preferred_element_type=F32)
                h = (_silu(a) * u).astype(BF16)
                _pack_rows(ybuf.at[slot], jnp.dot(h, wd_sc[...], preferred_element_type=F32))
            for g, slot in zip(blocks, slots):
                store(g, slot)

        def region(p, c):
            run([base + EXPERT_REGION * p + i for i in range(EXPERT_REGION)])
            return c

        lax.fori_loop(0, nb // EXPERT_REGION, region, 0)
        done = nb // EXPERT_REGION * EXPERT_REGION

        @pl.when(nb % EXPERT_REGION >= 2)
        def _():
            run([base + done, base + done + 1])

        @pl.when(nb % 2 == 1)
        def _():
            run([base + nb - 1])

    @pl.when(e == pl.num_programs(0) - 1)
    def _():
        for slot in range(EXPERT_BUFS):
            @pl.when(slot < n_used)
            def _():
                wait_out(slot)


def _experts(start_blk, n_blk_e, n_used, xs, w_gate, w_up, w_down):
    n_rows = xs.shape[0] // PACK_SUB
    E, D, H = w_gate.shape
    wsel = lambda e, sb, nb, nu: (e, 0, 0)
    grid_spec = pltpu.PrefetchScalarGridSpec(
        num_scalar_prefetch=3,
        grid=(E,),
        in_specs=[pl.BlockSpec((1, D, H), wsel), pl.BlockSpec((1, D, H), wsel), pl.BlockSpec((1, H, D), wsel),
                  pl.BlockSpec(memory_space=pl.ANY)],
        out_specs=pl.BlockSpec(memory_space=pl.ANY),
        scratch_shapes=[pltpu.VMEM((EXPERT_BUFS, ROW_BLOCK * PACK_SUB, LANES), PACKED),
                        pltpu.VMEM((EXPERT_BUFS, ROW_BLOCK * PACK_SUB, LANES), PACKED),
                        pltpu.VMEM((D, H), BF16), pltpu.VMEM((D, H), BF16), pltpu.VMEM((H, D), BF16),
                        pltpu.SemaphoreType.DMA((EXPERT_BUFS,)), pltpu.SemaphoreType.DMA((EXPERT_BUFS,))],
    )
    return pl.pallas_call(
        _expert_kernel,
        grid_spec=grid_spec,
        out_shape=jax.ShapeDtypeStruct((n_rows * PACK_SUB, LANES), PACKED),
        compiler_params=_cparams(("arbitrary",)),
        name="moe_experts",
    )(start_blk, n_blk_e, n_used, w_gate, w_up, w_down, xs)


def _combine_kernel(dcur_ref, dnxt_ref, x_ref, gate_ref, sh_ref, g_ref, b_ref, y_ref, o_ref, buf, sem, *, tt, alpha):
    i = pl.program_id(0)
    n = pl.num_programs(0)

    def fetch(d_ref, slot):
        def body(j, c):
            for kk in range(TOP_K):
                src = _packed_row(y_ref, d_ref[kk * tt + j])
                pltpu.make_async_copy(src, _packed_row(buf.at[slot, kk], j), sem.at[slot]).start(priority=kk % 2)
            return c

        lax.fori_loop(0, tt, body, 0)

    def reduce(slot):
        for kk in range(TOP_K):
            pltpu.make_async_copy(y_ref.at[pl.ds(0, tt * PACK_SUB), :], buf.at[slot, kk], sem.at[slot]).wait()
        x = x_ref[...]
        moe = sh_ref[...]
        gates = gate_ref[...]
        for kk in range(TOP_K):
            moe = moe + gates[:, kk:kk + 1] * _unpack_rows(buf.at[slot, kk], tt, F32)
        o_ref[...] = _layer_norm(alpha * x + moe, g_ref[...], b_ref[...])

    @pl.when(i == 0)
    def _():
        fetch(dcur_ref, 0)

    for slot in range(2):
        @pl.when((i % 2 == slot) & (i + 1 < n))
        def _():
            fetch(dnxt_ref, 1 - slot)

    for slot in range(2):
        @pl.when(i % 2 == slot)
        def _():
            reduce(slot)


def _combine(dest, x1, gates, shared, g, b, y, alpha, tt=256):
    T, D = x1.shape
    n = T // tt
    dest = dest.reshape(TOP_K, n, tt).transpose(1, 0, 2).reshape(n * TOP_K * tt)
    row = lambda i: (i, 0)
    fixed = lambda i: (0, 0)
    return pl.pallas_call(
        functools.partial(_combine_kernel, tt=tt, alpha=alpha),
        grid=(n,),
        in_specs=[pl.BlockSpec((TOP_K * tt,), lambda i: (i,), memory_space=pltpu.SMEM),
                  pl.BlockSpec((TOP_K * tt,), lambda i: (jnp.minimum(i + 1, n - 1),), memory_space=pltpu.SMEM),
                  pl.BlockSpec((tt, D), row), pl.BlockSpec((tt, TOP_K), row),
                  pl.BlockSpec((tt, D), row),
                  pl.BlockSpec((1, D), fixed), pl.BlockSpec((1, D), fixed),
                  pl.BlockSpec(memory_space=pl.ANY)],
        out_specs=pl.BlockSpec((tt, D), row),
        out_shape=jax.ShapeDtypeStruct((T, D), F32),
        scratch_shapes=[pltpu.VMEM((2, TOP_K, tt * PACK_SUB, LANES), PACKED), pltpu.SemaphoreType.DMA((2,))],
        compiler_params=_cparams(("arbitrary",)),
        name="moe_combine_ln2",
    )(dest, dest, x1, gates, shared, g, b, y)


def _layer(x2, mem, B, S, depth, w_in, b_f, w_pool, pool_scale, w_mem_kv, w_out, ln1_g, ln1_b,
           w_router, router_bias, w_gate, w_up, w_down, ws_gate, ws_up, ws_down, ln2_g, ln2_b):
    T, D = x2.shape
    n_win, grp = w_pool.shape[0], w_pool.shape[1]
    pw = n_win * grp
    fw = FOX_HEADS * HEAD_DIM
    mw = w_mem_kv.shape[1] // 2
    E = w_router.shape[1]
    alpha = (2 * depth) ** 0.25

    f_lo = pw + 3 * fw
    w_main = jnp.concatenate([w_in[:, :f_lo], w_in[:, f_lo + FOX_HEADS:]], axis=1).astype(BF16)
    w_f = jnp.pad(w_in[:, f_lo:f_lo + FOX_HEADS], ((0, 0), (0, LANES - FOX_HEADS))).astype(BF16)
    bf_pad = jnp.pad(b_f, (0, LANES - FOX_HEADS)).reshape(1, LANES)
    wbd = jnp.zeros((pw, pw), F32)
    for g in range(n_win):
        wbd = wbd.at[g * grp:(g + 1) * grp, g * grp:(g + 1) * grp].set(w_pool[g])
    wrt = w_router.T
    wrt_hi = wrt.astype(BF16)
    wrt_lo = (wrt - wrt_hi.astype(F32)).astype(BF16)

    u_pool, q, k, v, q_mem, f_logit = _inproj(x2, w_main, w_f, pw, fw, mw)
    y_pool = _pool(u_pool, wbd.astype(BF16), pool_scale.reshape(1, pw), B, S)
    fcum = _fgate(f_logit, bf_pad, B, S)
    y_fox = _fox(q, k, v, fcum, B, S)
    y_mem = _memattn(q_mem, mem, w_mem_kv.astype(BF16), B, S)
    x1, x1_packed = _outproj(y_pool, y_fox, y_mem, x2, w_out.astype(BF16), ln1_g.reshape(1, D),
                             ln1_b.reshape(1, D), alpha)

    eidx, gates, rank, counts = _router(x1, wrt_hi, wrt_lo, _per_expert_column(router_bias))

    cnt = counts[:, 0].astype(I32)
    padded = (cnt + ROW_BLOCK - 1) // ROW_BLOCK * ROW_BLOCK
    pad_end = jnp.cumsum(padded)
    start_pad = (pad_end - padded).astype(I32)
    n_rows = (T * TOP_K + E * (ROW_BLOCK - 1)) // ROW_BLOCK * ROW_BLOCK
    n_used = (pad_end[-1] // ROW_BLOCK).astype(I32)
    zero_blk = jnp.where(padded > 0, pad_end - ROW_BLOCK, -1).astype(I32)
    dest = _dest(eidx, rank, _per_expert_column(start_pad))

    xs, shared = _dispatch(zero_blk, dest, x1_packed, ws_gate.astype(BF16), ws_up.astype(BF16),
                           ws_down.astype(BF16), n_rows)
    y = _experts(start_pad // ROW_BLOCK, (padded // ROW_BLOCK).astype(I32), n_used.reshape(1), xs,
                 w_gate, w_up, w_down)
    return _combine(dest, x1, gates.T, shared, ln2_g.reshape(1, D), ln2_b.reshape(1, D), y, alpha)


def kernel(x, mem, w_in, b_f, w_pool, pool_scale, w_mem_kv, w_out, ln1_g, ln1_b, w_router, router_bias,
           w_gate, w_up, w_down, ws_gate, ws_up, ws_down, ln2_g, ln2_b):
    B, S, D = x.shape
    depth = w_in.shape[0]
    x2 = x.reshape(B * S, D)
    for l in range(depth):
        x2 = _layer(x2, mem, B, S, depth, w_in[l], b_f[l], w_pool[l], pool_scale[l], w_mem_kv[l], w_out[l],
                    ln1_g[l], ln1_b[l], w_router[l], router_bias[l], w_gate[l], w_up[l], w_down[l],
                    ws_gate[l], ws_up[l], ws_down[l], ln2_g[l], ln2_b[l])
    return x2.reshape(B, S, D)
```

```python
import functools

import jax
import jax.numpy as jnp
from jax import lax
from jax.experimental import pallas as pl
from jax.experimental.pallas import tpu as pltpu

F32 = jnp.float32
BF16 = jnp.bfloat16
I32 = jnp.int32
PACKED = jnp.int32

LANES = 128
POOL_WINDOWS = (2, 4, 8, 16)
POOL_HALO = 16
HEAD_DIM = 64
FOX_HEADS = 8
BIAS_TERMS = 3
MEM_HEADS = 4
TOP_K = 8
N_GROUPS = 8
TOPK_GROUPS = 4
ROUTED_SCALE = 2.5
LN_EPS = 1e-5
LOG2E = 1.4426950408889634
ROW_BLOCK = 256
EXPERT_BUFS = 8
EXPERT_REGION = 2
EXPERT_AHEAD = EXPERT_BUFS - EXPERT_REGION
VMEM_LIMIT = 48 * 1024 * 1024


def _cparams(sem):
    return pltpu.CompilerParams(dimension_semantics=sem, vmem_limit_bytes=VMEM_LIMIT)


def _layer_norm(z, g, b):
    mu = jnp.mean(z, axis=-1, keepdims=True)
    zc = z - mu
    var = jnp.mean(zc * zc, axis=-1, keepdims=True)
    return zc * lax.rsqrt(var + LN_EPS) * g + b


def _silu(x):
    return x * (1.0 / (1.0 + jnp.exp(-x)))


def _inproj_kernel(x_ref, w_ref, wf_ref, up_ref, q_ref, k_ref, v_ref, qm_ref, f_ref, *, pw, fw, mw, scale):
    xb = x_ref[...].astype(BF16)

    def proj(lo, hi):
        return jnp.dot(xb, w_ref[:, lo:hi], preferred_element_type=F32)

    up_ref[...] = proj(0, pw)
    q_ref[...] = (proj(pw, pw + fw) * (scale * LOG2E)).astype(BF16)
    k_ref[...] = proj(pw + fw, pw + 2 * fw).astype(BF16)
    v_ref[...] = proj(pw + 2 * fw, pw + 3 * fw).astype(BF16)
    qm_ref[...] = (proj(pw + 3 * fw, pw + 3 * fw + mw) * scale).astype(BF16)
    f_ref[...] = jnp.dot(xb, wf_ref[...], preferred_element_type=F32)


def _inproj(x2, w_main, w_f, pw, fw, mw, tm=512):
    T, D = x2.shape
    kern = functools.partial(_inproj_kernel, pw=pw, fw=fw, mw=mw, scale=HEAD_DIM ** -0.5)
    row = lambda i: (i, 0)
    fixed = lambda i: (0, 0)
    return pl.pallas_call(
        kern,
        grid=(T // tm,),
        in_specs=[pl.BlockSpec((tm, D), row),
                  pl.BlockSpec(w_main.shape, fixed),
                  pl.BlockSpec(w_f.shape, fixed)],
        out_specs=[pl.BlockSpec((tm, pw), row), pl.BlockSpec((tm, fw), row), pl.BlockSpec((tm, fw), row),
                   pl.BlockSpec((tm, fw), row), pl.BlockSpec((tm, mw), row), pl.BlockSpec((tm, LANES), row)],
        out_shape=[jax.ShapeDtypeStruct((T, pw), F32), jax.ShapeDtypeStruct((T, fw), BF16),
                   jax.ShapeDtypeStruct((T, fw), BF16), jax.ShapeDtypeStruct((T, fw), BF16),
                   jax.ShapeDtypeStruct((T, mw), BF16), jax.ShapeDtypeStruct((T, LANES), F32)],
        compiler_params=_cparams(("parallel",)),
        name="inproj",
    )(x2, w_main, w_f)


def _pool_kernel(u_ref, wbd_ref, sc_ref, o_ref, ext_sc, *, chunk, group):
    S, W = u_ref.shape
    ext_sc[0:POOL_HALO, :] = jnp.zeros((POOL_HALO, W), F32)
    ext_sc[POOL_HALO:, :] = u_ref[...]
    rows = chunk + POOL_HALO
    lrow = lax.broadcasted_iota(I32, (rows, W), 0)
    lane = lax.broadcasted_iota(I32, (rows, W), 1)

    def body(c, carry):
        start = pl.multiple_of(c * chunk, chunk)
        e = ext_sc[pl.ds(start, rows), :]
        posf = (lrow + (start - POOL_HALO + 1)).astype(F32)
        acc = e
        d = jnp.zeros_like(e)
        shift = 1
        for g, w in enumerate(POOL_WINDOWS):
            while shift < w:
                acc = acc + pltpu.roll(acc, shift, axis=0)
                shift *= 2
            mean = acc / jnp.minimum(posf, float(w))
            d = jnp.where((lane >= g * group) & (lane < (g + 1) * group), mean, d)
        d = (d - e)[POOL_HALO:, :]
        y = jnp.dot(d.astype(BF16), wbd_ref[...], preferred_element_type=F32) * sc_ref[...]
        o_ref[pl.ds(start, chunk), :] = y.astype(BF16)
        return carry

    lax.fori_loop(0, S // chunk, body, 0)


def _pool(u, wbd, pscale, B, S, chunk=512):
    T, W = u.shape
    kern = functools.partial(_pool_kernel, chunk=chunk, group=W // len(POOL_WINDOWS))
    return pl.pallas_call(
        kern,
        grid=(B,),
        in_specs=[pl.BlockSpec((S, W), lambda b: (b, 0)),
                  pl.BlockSpec((W, W), lambda b: (0, 0)),
                  pl.BlockSpec((1, W), lambda b: (0, 0))],
        out_specs=pl.BlockSpec((S, W), lambda b: (b, 0)),
        out_shape=jax.ShapeDtypeStruct((T, W), BF16),
        scratch_shapes=[pltpu.VMEM((S + POOL_HALO, W), F32)],
        compiler_params=_cparams(("parallel",)),
        name="pool_mixer",
    )(u, wbd, pscale)


def _fgate_kernel(f_ref, bf_ref, o_ref):
    S = f_ref.shape[0]
    z = f_ref[...] + bf_ref[...]
    x = jnp.minimum(z, 0.0) - jnp.log(1.0 + jnp.exp(-jnp.abs(z)))
    row = lax.broadcasted_iota(I32, x.shape, 0)
    lane = lax.broadcasted_iota(I32, x.shape, 1)
    shift = 1
    while shift < S:
        x = x + jnp.where(row >= shift, pltpu.roll(x, shift, axis=0), 0.0)
        shift *= 2
    x = x * LOG2E
    hi = x.astype(BF16).astype(F32)
    mid = (x - hi).astype(BF16).astype(F32)
    lo = (x - hi) - mid
    for p in range(FOX_HEADS // 2):
        out = jnp.zeros(x.shape, F32)
        for hh in range(2):
            c = 2 * p + hh
            for j, part in enumerate((hi, mid, lo)):
                out = jnp.where(lane == BIAS_TERMS * hh + j, -part[:, c:c + 1], out)
        o_ref[0, p] = out.astype(BF16)


def _fgate(f_logit, bf_pad, B, S):
    return pl.pallas_call(
        _fgate_kernel,
        grid=(B,),
        in_specs=[pl.BlockSpec((S, LANES), lambda b: (b, 0)),
                  pl.BlockSpec((1, LANES), lambda b: (0, 0))],
        out_specs=pl.BlockSpec((1, FOX_HEADS // 2, S, LANES), lambda b: (b, 0, 0, 0)),
        out_shape=jax.ShapeDtypeStruct((B, FOX_HEADS // 2, S, LANES), BF16),
        compiler_params=_cparams(("parallel",)),
        name="forget_cumsum",
    )(f_logit, bf_pad)


def _fox_kernel(q_ref, k_ref, a_ref, v_ref, o_ref, vt_sc, qa_sc, sa_sc, sb_sc, m_sc, l_sc, acc_sc, *, blk):
    qi = pl.program_id(2)
    nk = vt_sc.shape[0]
    nt = (((1,), (1,)), ((), ()))

    @pl.when(qi == 0)
    def _():
        for c in range(nk):
            vt_sc[c] = v_ref[c * blk:(c + 1) * blk, :].astype(F32).T.astype(BF16)

    q = q_ref[...].astype(F32)
    lane = lax.broadcasted_iota(I32, q.shape, 1)
    for h in range(2):
        mine = (lane >= h * HEAD_DIM) & (lane < (h + 1) * HEAD_DIM)
        bias_rows = (lane >= h * BIAS_TERMS) & (lane < (h + 1) * BIAS_TERMS)
        qa_sc[h] = jnp.concatenate([jnp.where(mine, q, 0.0), jnp.where(bias_rows, 1.0, 0.0)], axis=1).astype(BF16)
    m_sc[...] = jnp.full(m_sc.shape, -jnp.inf, F32)
    l_sc[...] = jnp.zeros(l_sc.shape, F32)
    acc_sc[...] = jnp.zeros(acc_sc.shape, F32)

    def scores(ki, dst):
        ks = pl.multiple_of(ki * blk, blk)
        kb = jnp.concatenate([k_ref[pl.ds(ks, blk), :], a_ref[0, 0, pl.ds(ks, blk), :]], axis=1)
        for h in range(2):
            dst[h] = lax.dot_general(kb, qa_sc[h], nt, preferred_element_type=F32)

    def absorb(ki, src, causal):
        for h in range(2):
            st = src[h]
            if causal:
                kpos = lax.broadcasted_iota(I32, st.shape, 0)
                qpos = lax.broadcasted_iota(I32, st.shape, 1)
                st = jnp.where(kpos <= qpos, st, -jnp.inf)
            m_prev = m_sc[h]
            m_new = jnp.maximum(m_prev, jnp.max(st, axis=0, keepdims=True))
            alpha = jnp.exp2(m_prev - m_new)
            p = jnp.exp2(st - m_new)
            l_sc[h] = alpha * l_sc[h] + jnp.sum(p, axis=0, keepdims=True)
            m_sc[h] = m_new
            vt = vt_sc[ki, h * HEAD_DIM:(h + 1) * HEAD_DIM, :]
            pv = jnp.dot(vt, p.astype(BF16), preferred_element_type=F32)
            rows = slice(h * HEAD_DIM, (h + 1) * HEAD_DIM)
            acc_sc[rows, :] = acc_sc[rows, :] * alpha + pv

    scores(0, sa_sc)

    def body(p, carry):
        scores(2 * p + 1, sb_sc)
        absorb(2 * p, sa_sc, False)
        scores(2 * p + 2, sa_sc)
        absorb(2 * p + 1, sb_sc, False)
        return carry

    lax.fori_loop(0, qi // 2, body, 0)

    @pl.when(qi % 2 == 1)
    def _():
        scores(qi, sb_sc)
        absorb(qi - 1, sa_sc, False)
        absorb(qi, sb_sc, True)

    @pl.when(qi % 2 == 0)
    def _():
        absorb(qi, sa_sc, True)

    out_t = jnp.concatenate([acc_sc[h * HEAD_DIM:(h + 1) * HEAD_DIM, :] / l_sc[h] for h in range(2)], axis=0)
    o_ref[...] = out_t.T.astype(BF16)


def _fox(q, k, v, fbias, B, S, blk=512):
    T, FW = q.shape
    nq = S // blk
    pairs = FW // LANES
    kern = functools.partial(_fox_kernel, blk=blk)
    return pl.pallas_call(
        kern,
        grid=(B, pairs, nq),
        in_specs=[pl.BlockSpec((blk, LANES), lambda b, p, i: (b * nq + i, p)),
                  pl.BlockSpec((S, LANES), lambda b, p, i: (b, p)),
                  pl.BlockSpec((1, 1, S, LANES), lambda b, p, i: (b, p, 0, 0)),
                  pl.BlockSpec((S, LANES), lambda b, p, i: (b, p))],
        out_specs=pl.BlockSpec((blk, LANES), lambda b, p, i: (b * nq + i, p)),
        out_shape=jax.ShapeDtypeStruct((T, FW), BF16),
        scratch_shapes=[pltpu.VMEM((nq, LANES, blk), BF16), pltpu.VMEM((2, blk, 2 * LANES), BF16),
                        pltpu.VMEM((2, blk, blk), F32), pltpu.VMEM((2, blk, blk), F32),
                        pltpu.VMEM((2, 1, blk), F32), pltpu.VMEM((2, 1, blk), F32), pltpu.VMEM((LANES, blk), F32)],
        compiler_params=_cparams(("parallel", "parallel", "arbitrary")),
        name="fox_attention",
    )(q, k, fbias, v)


def _memattn_kernel(qm_ref, mem_ref, wkv_ref, o_ref, k_sc, v_sc):
    MW = qm_ref.shape[1]

    @pl.when(pl.program_id(1) == 0)
    def _():
        kv = jnp.dot(mem_ref[0].astype(BF16), wkv_ref[...], preferred_element_type=F32)
        k_sc[...] = kv[:, :MW].astype(BF16)
        v_sc[...] = kv[:, MW:].astype(BF16)

    q = qm_ref[...]
    lane = lax.broadcasted_iota(I32, q.shape, 1)
    hd = MW // MEM_HEADS
    out = jnp.zeros(q.shape, F32)
    for h in range(MEM_HEADS):
        mine = (lane >= h * hd) & (lane < (h + 1) * hd)
        qh = jnp.where(mine, q, jnp.zeros_like(q))
        s = lax.dot_general(qh, k_sc[...], (((1,), (1,)), ((), ())), preferred_element_type=F32)
        p = jnp.exp(s - jnp.max(s, axis=1, keepdims=True))
        l = jnp.sum(p, axis=1, keepdims=True)
        o = jnp.dot(p.astype(BF16), v_sc[...], preferred_element_type=F32)
        out = jnp.where(mine, o / l, out)
    o_ref[...] = out.astype(BF16)


def _memattn(qm, mem, wkv, B, S, tq=512):
    T, MW = qm.shape
    M, D = mem.shape[1], mem.shape[2]
    nq = S // tq
    return pl.pallas_call(
        _memattn_kernel,
        grid=(B, nq),
        in_specs=[pl.BlockSpec((tq, MW), lambda b, i: (b * nq + i, 0)),
                  pl.BlockSpec((1, M, D), lambda b, i: (b, 0, 0)),
                  pl.BlockSpec(wkv.shape, lambda b, i: (0, 0))],
        out_specs=pl.BlockSpec((tq, MW), lambda b, i: (b * nq + i, 0)),
        out_shape=jax.ShapeDtypeStruct((T, MW), BF16),
        scratch_shapes=[pltpu.VMEM((M, MW), BF16), pltpu.VMEM((M, MW), BF16)],
        compiler_params=_cparams(("parallel", "arbitrary")),
        name="memory_attention",
    )(qm, mem, wkv)


PACK_SUB = 4


def _pack_rows(ref, val):
    n = val.shape[0]
    half = val.shape[1] // 2
    words = pltpu.pack_elementwise([val[:, :half], val[:, half:]], packed_dtype=BF16)
    for j in range(PACK_SUB):
        ref[pl.ds(j, n, stride=PACK_SUB), :] = words[:, j * LANES:(j + 1) * LANES]


def _unpack_words(slabs):
    lo = [pltpu.unpack_elementwise(s, index=0, packed_dtype=BF16, unpacked_dtype=F32) for s in slabs]
    hi = [pltpu.unpack_elementwise(s, index=1, packed_dtype=BF16, unpacked_dtype=F32) for s in slabs]
    return jnp.concatenate(lo + hi, axis=1)


def _unpack_rows(ref, n, dtype=BF16):
    return _unpack_words([ref[pl.ds(j, n, stride=PACK_SUB), :] for j in range(PACK_SUB)]).astype(dtype)


def _packed_row(ref, r):
    return ref.at[pl.ds(pl.multiple_of(r * PACK_SUB, PACK_SUB), PACK_SUB), :]


def _outproj_kernel(yp_ref, yf_ref, ym_ref, x_ref, w_ref, g_ref, b_ref, o_ref, pk_ref, *, alpha):
    pw, fw = yp_ref.shape[1], yf_ref.shape[1]
    h = jnp.dot(yp_ref[...], w_ref[0:pw, :], preferred_element_type=F32)
    h = h + jnp.dot(yf_ref[...], w_ref[pw:pw + fw, :], preferred_element_type=F32)
    h = h + jnp.dot(ym_ref[...], w_ref[pw + fw:, :], preferred_element_type=F32)
    y = _layer_norm(alpha * x_ref[...] + h, g_ref[...], b_ref[...])
    o_ref[...] = y
    _pack_rows(pk_ref, y)


def _outproj(yp, yf, ym, x2, w_out, g, b, alpha, tm=512):
    T, D = x2.shape
    assert D == 2 * PACK_SUB * LANES
    row = lambda i: (i, 0)
    fixed = lambda i: (0, 0)
    return pl.pallas_call(
        functools.partial(_outproj_kernel, alpha=alpha),
        grid=(T // tm,),
        in_specs=[pl.BlockSpec((tm, yp.shape[1]), row), pl.BlockSpec((tm, yf.shape[1]), row),
                  pl.BlockSpec((tm, ym.shape[1]), row), pl.BlockSpec((tm, D), row),
                  pl.BlockSpec(w_out.shape, fixed), pl.BlockSpec((1, D), fixed), pl.BlockSpec((1, D), fixed)],
        out_specs=[pl.BlockSpec((tm, D), row),
                   pl.BlockSpec((tm * PACK_SUB, LANES), row)],
        out_shape=[jax.ShapeDtypeStruct((T, D), F32),
                   jax.ShapeDtypeStruct((T * PACK_SUB, LANES), PACKED)],
        compiler_params=_cparams(("parallel",)),
        name="outproj_ln1",
    )(yp, yf, ym, x2, w_out, g, b)


ROUTE_TILE = 512


def _per_expert_column(v, dtype=F32):
    return jnp.broadcast_to(v.astype(dtype)[:, None], (v.shape[0], ROUTE_TILE))


def _router_kernel(x_ref, wh_ref, wl_ref, bias_ref, eidx_ref, gate_ref, rank_ref, cnt_ref, carry_sc, *, tt):
    E = wh_ref.shape[0]
    gsz = E // N_GROUPS
    ninf = -jnp.inf

    @pl.when(pl.program_id(0) == 0)
    def _():
        carry_sc[...] = jnp.zeros(carry_sc.shape, F32)

    x = x_ref[...]
    xh = x.astype(BF16)
    xl = (x - xh.astype(F32)).astype(BF16)
    wh = wh_ref[...]
    nt = (((1,), (1,)), ((), ()))
    logits = lax.dot_general(wh, xh, nt, preferred_element_type=F32) + (
        lax.dot_general(wl_ref[...], xh, nt, preferred_element_type=F32)
        + lax.dot_general(wh, xl, nt, preferred_element_type=F32))
    scores = 1.0 / (1.0 + jnp.exp(-logits))
    biased = scores + bias_ref[...]
    eiota = lax.broadcasted_iota(I32, (E, tt), 0).astype(F32)
    giota = lax.broadcasted_iota(I32, (gsz, tt), 0).astype(F32)

    def cmax(a):
        return jnp.max(a, axis=0, keepdims=True)

    def first_at(a, m, iota, n):
        return jnp.min(jnp.where(a == m, iota, float(n)), axis=0, keepdims=True)

    groups = [biased[g * gsz:(g + 1) * gsz, :] for g in range(N_GROUPS)]
    gscore = []
    for blk in groups:
        m1 = cmax(blk)
        m2 = cmax(jnp.where(giota == first_at(blk, m1, giota, gsz), ninf, blk))
        gscore.append(m1 + m2)

    kept = []
    for g in range(N_GROUPS):
        ahead = jnp.zeros((1, tt), F32)
        for o in range(N_GROUPS):
            if o == g:
                continue
            beats = gscore[o] > gscore[g]
            if o < g:
                beats = beats | (gscore[o] == gscore[g])
            ahead = ahead + jnp.where(beats, 1.0, 0.0)
        kept.append(groups[g] + jnp.where(ahead < float(TOPK_GROUPS), 0.0, ninf))
    masked = jnp.concatenate(kept, axis=0)

    picks, sels = [], []
    chosen = jnp.zeros((E, tt), F32)
    for _ in range(TOP_K):
        ik = first_at(masked, cmax(masked), eiota, E)
        oh = eiota == ik
        picks.append(ik)
        sels.append(jnp.sum(jnp.where(oh, scores, 0.0), axis=0, keepdims=True))
        chosen = jnp.where(oh, 1.0, chosen)
        masked = jnp.where(oh, ninf, masked)
    denom = sels[0]
    for sk in sels[1:]:
        denom = denom + sk

    r = lax.broadcasted_iota(I32, (tt, tt), 0)
    c = lax.broadcasted_iota(I32, (tt, tt), 1)
    earlier = jnp.where(r < c, 1.0, 0.0).astype(BF16)
    chosen_b = chosen.astype(BF16)
    pos = carry_sc[...] + jnp.dot(chosen_b, earlier, preferred_element_type=F32)
    carry_sc[...] = carry_sc[...] + jnp.dot(chosen_b, jnp.ones((tt, tt), BF16), preferred_element_type=F32)
    cnt_ref[...] = carry_sc[...]

    ranks = [jnp.sum(jnp.where(eiota == ik, pos, 0.0), axis=0, keepdims=True) for ik in picks]
    eidx_ref[...] = jnp.concatenate(picks, axis=0).astype(I32)
    gate_ref[...] = jnp.concatenate([sk / denom * ROUTED_SCALE for sk in sels], axis=0)
    rank_ref[...] = jnp.concatenate(ranks, axis=0).astype(I32)


def _router(x1, wrt_hi, wrt_lo, rbias):
    T = x1.shape[0]
    E, D = wrt_hi.shape
    tt = ROUTE_TILE
    tok = lambda i: (0, i)
    fixed = lambda i: (0, 0)
    return pl.pallas_call(
        functools.partial(_router_kernel, tt=tt),
        grid=(T // tt,),
        in_specs=[pl.BlockSpec((tt, D), lambda i: (i, 0)), pl.BlockSpec((E, D), fixed),
                  pl.BlockSpec((E, D), fixed), pl.BlockSpec((E, tt), fixed)],
        out_specs=[pl.BlockSpec((TOP_K, tt), tok), pl.BlockSpec((TOP_K, tt), tok), pl.BlockSpec((TOP_K, tt), tok),
                   pl.BlockSpec((E, tt), fixed)],
        out_shape=[jax.ShapeDtypeStruct((TOP_K, T), I32), jax.ShapeDtypeStruct((TOP_K, T), F32),
                   jax.ShapeDtypeStruct((TOP_K, T), I32), jax.ShapeDtypeStruct((E, tt), F32)],
        scratch_shapes=[pltpu.VMEM((E, tt), F32)],
        compiler_params=_cparams(("arbitrary",)),
        name="router_topk",
    )(x1, wrt_hi, wrt_lo, rbias)


def _dest_kernel(e_ref, r_ref, sp_ref, d_ref):
    E = sp_ref.shape[0]
    tt = e_ref.shape[1]
    eiota = lax.broadcasted_iota(I32, (E, tt), 0)
    sp = sp_ref[...]
    rows = []
    for kk in range(TOP_K):
        start = jnp.sum(jnp.where(eiota == e_ref[kk:kk + 1, :], sp, 0.0), axis=0, keepdims=True)
        rows.append(start.astype(I32) + r_ref[kk:kk + 1, :])
    d_ref[...] = jnp.concatenate(rows, axis=0)


def _dest(eidx, rank, start_pad_b):
    K, T = eidx.shape
    tt = ROUTE_TILE
    tok = lambda i: (0, i)
    return pl.pallas_call(
        _dest_kernel,
        grid=(T // tt,),
        in_specs=[pl.BlockSpec((K, tt), tok), pl.BlockSpec((K, tt), tok),
                  pl.BlockSpec(start_pad_b.shape, lambda i: (0, 0))],
        out_specs=pl.BlockSpec((K, tt), tok),
        out_shape=jax.ShapeDtypeStruct((K, T), I32),
        compiler_params=_cparams(("parallel",)),
        name="moe_dest",
    )(eidx, rank, start_pad_b)


def _dispatch_kernel(zb_ref, d_ref, x_ref, wsg_ref, wsu_ref, wsd_ref, xs_ref, sh_ref, zero_sc, sem, *, tt, n_exp):
    blk_rows = ROW_BLOCK * PACK_SUB

    @pl.when(pl.program_id(0) == 0)
    def _():
        zero_sc[...] = jnp.zeros(zero_sc.shape, PACKED)

        def zstart(e, c):
            @pl.when(zb_ref[e] >= 0)
            def _():
                dst = xs_ref.at[pl.ds(pl.multiple_of(zb_ref[e] * PACK_SUB, blk_rows), blk_rows), :]
                pltpu.make_async_copy(zero_sc, dst, sem).start()
            return c

        def zwait(e, c):
            @pl.when(zb_ref[e] >= 0)
            def _():
                pltpu.make_async_copy(zero_sc, xs_ref.at[pl.ds(0, blk_rows), :], sem).wait()
            return c

        lax.fori_loop(0, n_exp, zstart, 0)
        lax.fori_loop(0, n_exp, zwait, 0)

    def start(j, c):
        src = _packed_row(x_ref, j)
        for kk in range(TOP_K):
            pltpu.make_async_copy(src, _packed_row(xs_ref, d_ref[kk, j]), sem).start(priority=kk % 2)
        return c

    lax.fori_loop(0, tt, start, 0)
    xb = _unpack_rows(x_ref, tt)
    hs = _silu(jnp.dot(xb, wsg_ref[...], preferred_element_type=F32)) * jnp.dot(
        xb, wsu_ref[...], preferred_element_type=F32)
    sh_ref[...] = jnp.dot(hs.astype(BF16), wsd_ref[...], preferred_element_type=F32)
    for kk in range(TOP_K):
        pltpu.make_async_copy(x_ref, xs_ref.at[pl.ds(0, tt * PACK_SUB), :], sem).wait()


def _dispatch(zero_blk, dest, xpk, wsg, wsu, wsd, n_rows, tt=1024):
    T = xpk.shape[0] // PACK_SUB
    D = wsg.shape[0]
    n_exp = zero_blk.shape[0]
    fixed = lambda i, zb: (0, 0)
    grid_spec = pltpu.PrefetchScalarGridSpec(
        num_scalar_prefetch=1,
        grid=(T // tt,),
        in_specs=[pl.BlockSpec((TOP_K, tt), lambda i, zb: (0, i), memory_space=pltpu.SMEM),
                  pl.BlockSpec((tt * PACK_SUB, LANES), lambda i, zb: (i, 0)),
                  pl.BlockSpec(wsg.shape, fixed), pl.BlockSpec(wsu.shape, fixed), pl.BlockSpec(wsd.shape, fixed)],
        out_specs=[pl.BlockSpec(memory_space=pl.ANY), pl.BlockSpec((tt, D), lambda i, zb: (i, 0))],
        scratch_shapes=[pltpu.VMEM((ROW_BLOCK * PACK_SUB, LANES), PACKED), pltpu.SemaphoreType.DMA],
    )
    return pl.pallas_call(
        functools.partial(_dispatch_kernel, tt=tt, n_exp=n_exp),
        grid_spec=grid_spec,
        out_shape=[jax.ShapeDtypeStruct((n_rows * PACK_SUB, LANES), PACKED), jax.ShapeDtypeStruct((T, D), F32)],
        compiler_params=_cparams(("arbitrary",)),
        name="moe_dispatch",
    )(zero_blk, dest, xpk, wsg, wsu, wsd)


def _expert_kernel(sb_ref, nb_ref, nu_ref, wg_ref, wu_ref, wd_ref, xs_ref, y_ref,
                   xbuf, ybuf, wg_sc, wu_sc, wd_sc, isem, osem):
    e = pl.program_id(0)
    nb = nb_ref[e]
    base = sb_ref[e]
    n_used = nu_ref[0]
    in_rows = out_rows = ROW_BLOCK * PACK_SUB

    def fetch(blk, slot):
        src = xs_ref.at[pl.ds(pl.multiple_of(blk * in_rows, in_rows), in_rows), :]
        pltpu.make_async_copy(src, xbuf.at[slot], isem.at[slot]).start()

    def wait_in(slot):
        pltpu.make_async_copy(xs_ref.at[pl.ds(0, in_rows), :], xbuf.at[slot], isem.at[slot]).wait()

    def store(blk, slot):
        dst = y_ref.at[pl.ds(pl.multiple_of(blk * out_rows, out_rows), out_rows), :]
        pltpu.make_async_copy(ybuf.at[slot], dst, osem.at[slot]).start()

    def wait_out(slot):
        pltpu.make_async_copy(ybuf.at[slot], y_ref.at[pl.ds(0, out_rows), :], osem.at[slot]).wait()

    @pl.when(nb > 0)
    def _():
        @pl.when(base == 0)
        def _():
            for g0 in range(EXPERT_AHEAD):
                @pl.when(g0 < n_used)
                def _():
                    fetch(g0, g0 % EXPERT_BUFS)

        wg_sc[...] = wg_ref[0].astype(BF16)
        wu_sc[...] = wu_ref[0].astype(BF16)
        wd_sc[...] = wd_ref[0].astype(BF16)

        def run(blocks):
            slots = [g % EXPERT_BUFS for g in blocks]
            for g, slot in zip(blocks, slots):
                wait_in(slot)

                @pl.when(g + EXPERT_AHEAD < n_used)
                def _():
                    fetch(g + EXPERT_AHEAD, (g + EXPERT_AHEAD) % EXPERT_BUFS)

                @pl.when(g >= EXPERT_BUFS)
                def _():
                    wait_out(slot)

            for g, slot in zip(blocks, slots):
                xb = _unpack_rows(xbuf.at[slot], ROW_BLOCK)
                a = jnp.dot(xb, wg_sc[...], preferred_element_type=F32)
                u = jnp.dot(xb, wu_sc[...], preferred_element_type=F32)
                h = (_silu(a) * u).astype(BF16)
                _pack_rows(ybuf.at[slot], jnp.dot(h, wd_sc[...], preferred_element_type=F32))
            for g, slot in zip(blocks, slots):
                store(g, slot)

        def region(p, c):
            run([base + EXPERT_REGION * p + i for i in range(EXPERT_REGION)])
            return c

        lax.fori_loop(0, nb // EXPERT_REGION, region, 0)
        done = nb // EXPERT_REGION * EXPERT_REGION

        @pl.when(nb % EXPERT_REGION >= 2)
        def _():
            run([base + done, base + done + 1])

        @pl.when(nb % 2 == 1)
        def _():
            run([base + nb - 1])

    @pl.when(e == pl.num_programs(0) - 1)
    def _():
        for slot in range(EXPERT_BUFS):
            @pl.when(slot < n_used)
            def _():
                wait_out(slot)


def _experts(start_blk, n_blk_e, n_used, xs, w_gate, w_up, w_down):
    n_rows = xs.shape[0] // PACK_SUB
    E, D, H = w_gate.shape
    wsel = lambda e, sb, nb, nu: (e, 0, 0)
    grid_spec = pltpu.PrefetchScalarGridSpec(
        num_scalar_prefetch=3,
        grid=(E,),
        in_specs=[pl.BlockSpec((1, D, H), wsel), pl.BlockSpec((1, D, H), wsel), pl.BlockSpec((1, H, D), wsel),
                  pl.BlockSpec(memory_space=pl.ANY)],
        out_specs=pl.BlockSpec(memory_space=pl.ANY),
        scratch_shapes=[pltpu.VMEM((EXPERT_BUFS, ROW_BLOCK * PACK_SUB, LANES), PACKED),
                        pltpu.VMEM((EXPERT_BUFS, ROW_BLOCK * PACK_SUB, LANES), PACKED),
                        pltpu.VMEM((D, H), BF16), pltpu.VMEM((D, H), BF16), pltpu.VMEM((H, D), BF16),
                        pltpu.SemaphoreType.DMA((EXPERT_BUFS,)), pltpu.SemaphoreType.DMA((EXPERT_BUFS,))],
    )
    return pl.pallas_call(
        _expert_kernel,
        grid_spec=grid_spec,
        out_shape=jax.ShapeDtypeStruct((n_rows * PACK_SUB, LANES), PACKED),
        compiler_params=_cparams(("arbitrary",)),
        name="moe_experts",
    )(start_blk, n_blk_e, n_used, w_gate, w_up, w_down, xs)


def _combine_kernel(dcur_ref, dnxt_ref, x_ref, gate_ref, sh_ref, g_ref, b_ref, y_ref, o_ref, buf, sem, *, tt, alpha):
    i = pl.program_id(0)
    n = pl.num_programs(0)

    def fetch(d_ref, slot):
        def body(j, c):
            for kk in range(TOP_K):
                src = _packed_row(y_ref, d_ref[kk * tt + j])
                pltpu.make_async_copy(src, _packed_row(buf.at[slot, kk], j), sem.at[slot]).start(priority=kk % 2)
            return c

        lax.fori_loop(0, tt, body, 0)

    def reduce(slot):
        for kk in range(TOP_K):
            pltpu.make_async_copy(y_ref.at[pl.ds(0, tt * PACK_SUB), :], buf.at[slot, kk], sem.at[slot]).wait()
        x = x_ref[...]
        moe = sh_ref[...]
        gates = gate_ref[...]
        for kk in range(TOP_K):
            moe = moe + gates[:, kk:kk + 1] * _unpack_rows(buf.at[slot, kk], tt, F32)
        o_ref[...] = _layer_norm(alpha * x + moe, g_ref[...], b_ref[...])

    @pl.when(i == 0)
    def _():
        fetch(dcur_ref, 0)

    for slot in range(2):
        @pl.when((i % 2 == slot) & (i + 1 < n))
        def _():
            fetch(dnxt_ref, 1 - slot)

    for slot in range(2):
        @pl.when(i % 2 == slot)
        def _():
            reduce(slot)


def _combine(dest, x1, gates, shared, g, b, y, alpha, tt=256):
    T, D = x1.shape
    n = T // tt
    dest = dest.reshape(TOP_K, n, tt).transpose(1, 0, 2).reshape(n * TOP_K * tt)
    row = lambda i: (i, 0)
    fixed = lambda i: (0, 0)
    return pl.pallas_call(
        functools.partial(_combine_kernel, tt=tt, alpha=alpha),
        grid=(n,),
        in_specs=[pl.BlockSpec((TOP_K * tt,), lambda i: (i,), memory_space=pltpu.SMEM),
                  pl.BlockSpec((TOP_K * tt,), lambda i: (jnp.minimum(i + 1, n - 1),), memory_space=pltpu.SMEM),
                  pl.BlockSpec((tt, D), row), pl.BlockSpec((tt, TOP_K), row),
                  pl.BlockSpec((tt, D), row),
                  pl.BlockSpec((1, D), fixed), pl.BlockSpec((1, D), fixed),
                  pl.BlockSpec(memory_space=pl.ANY)],
        out_specs=pl.BlockSpec((tt, D), row),
        out_shape=jax.ShapeDtypeStruct((T, D), F32),
        scratch_shapes=[pltpu.VMEM((2, TOP_K, tt * PACK_SUB, LANES), PACKED), pltpu.SemaphoreType.DMA((2,))],
        compiler_params=_cparams(("arbitrary",)),
        name="moe_combine_ln2",
    )(dest, dest, x1, gates, shared, g, b, y)


def _layer(x2, mem, B, S, depth, w_in, b_f, w_pool, pool_scale, w_mem_kv, w_out, ln1_g, ln1_b,
           w_router, router_bias, w_gate, w_up, w_down, ws_gate, ws_up, ws_down, ln2_g, ln2_b):
    T, D = x2.shape
    n_win, grp = w_pool.shape[0], w_pool.shape[1]
    pw = n_win * grp
    fw = FOX_HEADS * HEAD_DIM
    mw = w_mem_kv.shape[1] // 2
    E = w_router.shape[1]
    alpha = (2 * depth) ** 0.25

    f_lo = pw + 3 * fw
    w_main = jnp.concatenate([w_in[:, :f_lo], w_in[:, f_lo + FOX_HEADS:]], axis=1).astype(BF16)
    w_f = jnp.pad(w_in[:, f_lo:f_lo + FOX_HEADS], ((0, 0), (0, LANES - FOX_HEADS))).astype(BF16)
    bf_pad = jnp.pad(b_f, (0, LANES - FOX_HEADS)).reshape(1, LANES)
    wbd = jnp.zeros((pw, pw), F32)
    for g in range(n_win):
        wbd = wbd.at[g * grp:(g + 1) * grp, g * grp:(g + 1) * grp].set(w_pool[g])
    wrt = w_router.T
    wrt_hi = wrt.astype(BF16)
    wrt_lo = (wrt - wrt_hi.astype(F32)).astype(BF16)

    u_pool, q, k, v, q_mem, f_logit = _inproj(x2, w_main, w_f, pw, fw, mw)
    y_pool = _pool(u_pool, wbd.astype(BF16), pool_scale.reshape(1, pw), B, S)
    fcum = _fgate(f_logit, bf_pad, B, S)
    y_fox = _fox(q, k, v, fcum, B, S)
    y_mem = _memattn(q_mem, mem, w_mem_kv.astype(BF16), B, S)
    x1, x1_packed = _outproj(y_pool, y_fox, y_mem, x2, w_out.astype(BF16), ln1_g.reshape(1, D),
                             ln1_b.reshape(1, D), alpha)

    eidx, gates, rank, counts = _router(x1, wrt_hi, wrt_lo, _per_expert_column(router_bias))

    cnt = counts[:, 0].astype(I32)
    padded = (cnt + ROW_BLOCK - 1) // ROW_BLOCK * ROW_BLOCK
    pad_end = jnp.cumsum(padded)
    start_pad = (pad_end - padded).astype(I32)
    n_rows = (T * TOP_K + E * (ROW_BLOCK - 1)) // ROW_BLOCK * ROW_BLOCK
    n_used = (pad_end[-1] // ROW_BLOCK).astype(I32)
    zero_blk = jnp.where(padded > 0, pad_end - ROW_BLOCK, -1).astype(I32)
    dest = _dest(eidx, rank, _per_expert_column(start_pad))

    xs, shared = _dispatch(zero_blk, dest, x1_packed, ws_gate.astype(BF16), ws_up.astype(BF16),
                           ws_down.astype(BF16), n_rows)
    y = _experts(start_pad // ROW_BLOCK, (padded // ROW_BLOCK).astype(I32), n_used.reshape(1), xs,
                 w_gate, w_up, w_down)
    return _combine(dest, x1, gates.T, shared, ln2_g.reshape(1, D), ln2_b.reshape(1, D), y, alpha)


def kernel(x, mem, w_in, b_f, w_pool, pool_scale, w_mem_kv, w_out, ln1_g, ln1_b, w_router, router_bias,
           w_gate, w_up, w_down, ws_gate, ws_up, ws_down, ln2_g, ln2_b):
    B, S, D = x.shape
    depth = w_in.shape[0]
    x2 = x.reshape(B * S, D)
    for l in range(depth):
        x2 = _layer(x2, mem, B, S, depth, w_in[l], b_f[l], w_pool[l], pool_scale[l], w_mem_kv[l], w_out[l],
                    ln1_g[l], ln1_b[l], w_router[l], router_bias[l], w_gate[l], w_up[l], w_down[l],
                    ws_gate[l], ws_up[l], ws_down[l], ln2_g[l], ln2_b[l])
    return x2.reshape(B, S, D)
```

```python
import functools

import jax
import jax.numpy as jnp
from jax import lax
from jax.experimental import pallas as pl
from jax.experimental.pallas import tpu as pltpu

F32 = jnp.float32
BF16 = jnp.bfloat16
I32 = jnp.int32
PACKED = jnp.int32

LANES = 128
POOL_WINDOWS = (2, 4, 8, 16)
POOL_HALO = 16
HEAD_DIM = 64
FOX_HEADS = 8
BIAS_TERMS = 3
MEM_HEADS = 4
TOP_K = 8
N_GROUPS = 8
TOPK_GROUPS = 4
ROUTED_SCALE = 2.5
LN_EPS = 1e-5
LOG2E = 1.4426950408889634
ROW_BLOCK = 256
EXPERT_BUFS = 8
EXPERT_REGION = 2
EXPERTS_PER_STEP = 2
EXPERT_AHEAD = EXPERT_BUFS - EXPERT_REGION
VMEM_LIMIT = 48 * 1024 * 1024


def _cparams(sem):
    return pltpu.CompilerParams(dimension_semantics=sem, vmem_limit_bytes=VMEM_LIMIT)


def _layer_norm(z, g, b):
    mu = jnp.mean(z, axis=-1, keepdims=True)
    zc = z - mu
    var = jnp.mean(zc * zc, axis=-1, keepdims=True)
    return zc * lax.rsqrt(var + LN_EPS) * g + b


def _silu(x):
    return x * (1.0 / (1.0 + jnp.exp(-x)))


def _inproj_kernel(x_ref, w_ref, wf_ref, up_ref, q_ref, k_ref, v_ref, qm_ref, f_ref, *, pw, fw, mw, scale):
    xb = x_ref[...].astype(BF16)

    def proj(lo, hi):
        return jnp.dot(xb, w_ref[:, lo:hi], preferred_element_type=F32)

    up_ref[...] = proj(0, pw)
    q_ref[...] = (proj(pw, pw + fw) * (scale * LOG2E)).astype(BF16)
    k_ref[...] = proj(pw + fw, pw + 2 * fw).astype(BF16)
    v_ref[...] = proj(pw + 2 * fw, pw + 3 * fw).astype(BF16)
    qm_ref[...] = (proj(pw + 3 * fw, pw + 3 * fw + mw) * scale).astype(BF16)
    f_ref[...] = jnp.dot(xb, wf_ref[...], preferred_element_type=F32)


def _inproj(x2, w_main, w_f, pw, fw, mw, tm=512):
    T, D = x2.shape
    kern = functools.partial(_inproj_kernel, pw=pw, fw=fw, mw=mw, scale=HEAD_DIM ** -0.5)
    row = lambda i: (i, 0)
    fixed = lambda i: (0, 0)
    return pl.pallas_call(
        kern,
        grid=(T // tm,),
        in_specs=[pl.BlockSpec((tm, D), row),
                  pl.BlockSpec(w_main.shape, fixed),
                  pl.BlockSpec(w_f.shape, fixed)],
        out_specs=[pl.BlockSpec((tm, pw), row), pl.BlockSpec((tm, fw), row), pl.BlockSpec((tm, fw), row),
                   pl.BlockSpec((tm, fw), row), pl.BlockSpec((tm, mw), row), pl.BlockSpec((tm, LANES), row)],
        out_shape=[jax.ShapeDtypeStruct((T, pw), F32), jax.ShapeDtypeStruct((T, fw), BF16),
                   jax.ShapeDtypeStruct((T, fw), BF16), jax.ShapeDtypeStruct((T, fw), BF16),
                   jax.ShapeDtypeStruct((T, mw), BF16), jax.ShapeDtypeStruct((T, LANES), F32)],
        compiler_params=_cparams(("parallel",)),
        name="inproj",
    )(x2, w_main, w_f)


def _pool_kernel(u_ref, wbd_ref, sc_ref, o_ref, ext_sc, *, chunk, group):
    S, W = u_ref.shape
    ext_sc[0:POOL_HALO, :] = jnp.zeros((POOL_HALO, W), F32)
    ext_sc[POOL_HALO:, :] = u_ref[...]
    rows = chunk + POOL_HALO
    lrow = lax.broadcasted_iota(I32, (rows, W), 0)
    lane = lax.broadcasted_iota(I32, (rows, W), 1)

    def body(c, carry):
        start = pl.multiple_of(c * chunk, chunk)
        e = ext_sc[pl.ds(start, rows), :]
        posf = (lrow + (start - POOL_HALO + 1)).astype(F32)
        acc = e
        d = jnp.zeros_like(e)
        shift = 1
        for g, w in enumerate(POOL_WINDOWS):
            while shift < w:
                acc = acc + pltpu.roll(acc, shift, axis=0)
                shift *= 2
            mean = acc / jnp.minimum(posf, float(w))
            d = jnp.where((lane >= g * group) & (lane < (g + 1) * group), mean, d)
        d = (d - e)[POOL_HALO:, :]
        y = jnp.dot(d.astype(BF16), wbd_ref[...], preferred_element_type=F32) * sc_ref[...]
        o_ref[pl.ds(start, chunk), :] = y.astype(BF16)
        return carry

    lax.fori_loop(0, S // chunk, body, 0)


def _pool(u, wbd, pscale, B, S, chunk=512):
    T, W = u.shape
    kern = functools.partial(_pool_kernel, chunk=chunk, group=W // len(POOL_WINDOWS))
    return pl.pallas_call(
        kern,
        grid=(B,),
        in_specs=[pl.BlockSpec((S, W), lambda b: (b, 0)),
                  pl.BlockSpec((W, W), lambda b: (0, 0)),
                  pl.BlockSpec((1, W), lambda b: (0, 0))],
        out_specs=pl.BlockSpec((S, W), lambda b: (b, 0)),
        out_shape=jax.ShapeDtypeStruct((T, W), BF16),
        scratch_shapes=[pltpu.VMEM((S + POOL_HALO, W), F32)],
        compiler_params=_cparams(("parallel",)),
        name="pool_mixer",
    )(u, wbd, pscale)


def _fgate_kernel(f_ref, bf_ref, o_ref):
    S = f_ref.shape[0]
    z = f_ref[...] + bf_ref[...]
    x = jnp.minimum(z, 0.0) - jnp.log(1.0 + jnp.exp(-jnp.abs(z)))
    row = lax.broadcasted_iota(I32, x.shape, 0)
    lane = lax.broadcasted_iota(I32, x.shape, 1)
    shift = 1
    while shift < S:
        x = x + jnp.where(row >= shift, pltpu.roll(x, shift, axis=0), 0.0)
        shift *= 2
    x = x * LOG2E
    hi = x.astype(BF16).astype(F32)
    mid = (x - hi).astype(BF16).astype(F32)
    lo = (x - hi) - mid
    for p in range(FOX_HEADS // 2):
        out = jnp.zeros(x.shape, F32)
        for hh in range(2):
            c = 2 * p + hh
            for j, part in enumerate((hi, mid, lo)):
                out = jnp.where(lane == BIAS_TERMS * hh + j, -part[:, c:c + 1], out)
        o_ref[0, p] = out.astype(BF16)


def _fgate(f_logit, bf_pad, B, S):
    return pl.pallas_call(
        _fgate_kernel,
        grid=(B,),
        in_specs=[pl.BlockSpec((S, LANES), lambda b: (b, 0)),
                  pl.BlockSpec((1, LANES), lambda b: (0, 0))],
        out_specs=pl.BlockSpec((1, FOX_HEADS // 2, S, LANES), lambda b: (b, 0, 0, 0)),
        out_shape=jax.ShapeDtypeStruct((B, FOX_HEADS // 2, S, LANES), BF16),
        compiler_params=_cparams(("parallel",)),
        name="forget_cumsum",
    )(f_logit, bf_pad)


def _fox_kernel(q_ref, k_ref, a_ref, v_ref, o_ref, vt_sc, qa_sc, sa_sc, sb_sc, m_sc, l_sc, acc_sc, *, blk):
    qi = pl.program_id(2)
    nk = vt_sc.shape[0]
    nt = (((1,), (1,)), ((), ()))

    @pl.when(qi == 0)
    def _():
        for c in range(nk):
            vt_sc[c] = v_ref[c * blk:(c + 1) * blk, :].astype(F32).T.astype(BF16)

    q = q_ref[...].astype(F32)
    lane = lax.broadcasted_iota(I32, q.shape, 1)
    for h in range(2):
        mine = (lane >= h * HEAD_DIM) & (lane < (h + 1) * HEAD_DIM)
        bias_rows = (lane >= h * BIAS_TERMS) & (lane < (h + 1) * BIAS_TERMS)
        qa_sc[h] = jnp.concatenate([jnp.where(mine, q, 0.0), jnp.where(bias_rows, 1.0, 0.0)], axis=1).astype(BF16)
    m_sc[...] = jnp.full(m_sc.shape, -jnp.inf, F32)
    l_sc[...] = jnp.zeros(l_sc.shape, F32)
    acc_sc[...] = jnp.zeros(acc_sc.shape, F32)

    def scores(ki, dst):
        ks = pl.multiple_of(ki * blk, blk)
        kb = jnp.concatenate([k_ref[pl.ds(ks, blk), :], a_ref[0, 0, pl.ds(ks, blk), :]], axis=1)
        for h in range(2):
            dst[h] = lax.dot_general(kb, qa_sc[h], nt, preferred_element_type=F32)

    def absorb(ki, src, causal):
        for h in range(2):
            st = src[h]
            if causal:
                kpos = lax.broadcasted_iota(I32, st.shape, 0)
                qpos = lax.broadcasted_iota(I32, st.shape, 1)
                st = jnp.where(kpos <= qpos, st, -jnp.inf)
            m_prev = m_sc[h]
            m_new = jnp.maximum(m_prev, jnp.max(st, axis=0, keepdims=True))
            alpha = jnp.exp2(m_prev - m_new)
            p = jnp.exp2(st - m_new)
            l_sc[h] = alpha * l_sc[h] + jnp.sum(p, axis=0, keepdims=True)
            m_sc[h] = m_new
            vt = vt_sc[ki, h * HEAD_DIM:(h + 1) * HEAD_DIM, :]
            pv = jnp.dot(vt, p.astype(BF16), preferred_element_type=F32)
            rows = slice(h * HEAD_DIM, (h + 1) * HEAD_DIM)
            acc_sc[rows, :] = acc_sc[rows, :] * alpha + pv

    scores(0, sa_sc)

    def body(p, carry):
        scores(2 * p + 1, sb_sc)
        absorb(2 * p, sa_sc, False)
        scores(2 * p + 2, sa_sc)
        absorb(2 * p + 1, sb_sc, False)
        return carry

    lax.fori_loop(0, qi // 2, body, 0)

    @pl.when(qi % 2 == 1)
    def _():
        scores(qi, sb_sc)
        absorb(qi - 1, sa_sc, False)
        absorb(qi, sb_sc, True)

    @pl.when(qi % 2 == 0)
    def _():
        absorb(qi, sa_sc, True)

    out_t = jnp.concatenate([acc_sc[h * HEAD_DIM:(h + 1) * HEAD_DIM, :] / l_sc[h] for h in range(2)], axis=0)
    o_ref[...] = out_t.T.astype(BF16)


def _fox(q, k, v, fbias, B, S, blk=512):
    T, FW = q.shape
    nq = S // blk
    pairs = FW // LANES
    kern = functools.partial(_fox_kernel, blk=blk)
    return pl.pallas_call(
        kern,
        grid=(B, pairs, nq),
        in_specs=[pl.BlockSpec((blk, LANES), lambda b, p, i: (b * nq + i, p)),
                  pl.BlockSpec((S, LANES), lambda b, p, i: (b, p)),
                  pl.BlockSpec((1, 1, S, LANES), lambda b, p, i: (b, p, 0, 0)),
                  pl.BlockSpec((S, LANES), lambda b, p, i: (b, p))],
        out_specs=pl.BlockSpec((blk, LANES), lambda b, p, i: (b * nq + i, p)),
        out_shape=jax.ShapeDtypeStruct((T, FW), BF16),
        scratch_shapes=[pltpu.VMEM((nq, LANES, blk), BF16), pltpu.VMEM((2, blk, 2 * LANES), BF16),
                        pltpu.VMEM((2, blk, blk), F32), pltpu.VMEM((2, blk, blk), F32),
                        pltpu.VMEM((2, 1, blk), F32), pltpu.VMEM((2, 1, blk), F32), pltpu.VMEM((LANES, blk), F32)],
        compiler_params=_cparams(("parallel", "parallel", "arbitrary")),
        name="fox_attention",
    )(q, k, fbias, v)


def _memattn_kernel(qm_ref, mem_ref, wkv_ref, o_ref, k_sc, v_sc):
    MW = qm_ref.shape[1]

    @pl.when(pl.program_id(1) == 0)
    def _():
        kv = jnp.dot(mem_ref[0].astype(BF16), wkv_ref[...], preferred_element_type=F32)
        k_sc[...] = kv[:, :MW].astype(BF16)
        v_sc[...] = kv[:, MW:].astype(BF16)

    q = qm_ref[...]
    lane = lax.broadcasted_iota(I32, q.shape, 1)
    hd = MW // MEM_HEADS
    out = jnp.zeros(q.shape, F32)
    for h in range(MEM_HEADS):
        mine = (lane >= h * hd) & (lane < (h + 1) * hd)
        qh = jnp.where(mine, q, jnp.zeros_like(q))
        s = lax.dot_general(qh, k_sc[...], (((1,), (1,)), ((), ())), preferred_element_type=F32)
        p = jnp.exp(s - jnp.max(s, axis=1, keepdims=True))
        l = jnp.sum(p, axis=1, keepdims=True)
        o = jnp.dot(p.astype(BF16), v_sc[...], preferred_element_type=F32)
        out = jnp.where(mine, o / l, out)
    o_ref[...] = out.astype(BF16)


def _memattn(qm, mem, wkv, B, S, tq=512):
    T, MW = qm.shape
    M, D = mem.shape[1], mem.shape[2]
    nq = S // tq
    return pl.pallas_call(
        _memattn_kernel,
        grid=(B, nq),
        in_specs=[pl.BlockSpec((tq, MW), lambda b, i: (b * nq + i, 0)),
                  pl.BlockSpec((1, M, D), lambda b, i: (b, 0, 0)),
                  pl.BlockSpec(wkv.shape, lambda b, i: (0, 0))],
        out_specs=pl.BlockSpec((tq, MW), lambda b, i: (b * nq + i, 0)),
        out_shape=jax.ShapeDtypeStruct((T, MW), BF16),
        scratch_shapes=[pltpu.VMEM((M, MW), BF16), pltpu.VMEM((M, MW), BF16)],
        compiler_params=_cparams(("parallel", "arbitrary")),
        name="memory_attention",
    )(qm, mem, wkv)


PACK_SUB = 4


def _pack_rows(ref, val):
    n = val.shape[0]
    half = val.shape[1] // 2
    words = pltpu.pack_elementwise([val[:, :half], val[:, half:]], packed_dtype=BF16)
    for j in range(PACK_SUB):
        ref[pl.ds(j, n, stride=PACK_SUB), :] = words[:, j * LANES:(j + 1) * LANES]


def _unpack_words(slabs):
    lo = [pltpu.unpack_elementwise(s, index=0, packed_dtype=BF16, unpacked_dtype=F32) for s in slabs]
    hi = [pltpu.unpack_elementwise(s, index=1, packed_dtype=BF16, unpacked_dtype=F32) for s in slabs]
    return jnp.concatenate(lo + hi, axis=1)


def _unpack_rows(ref, n, dtype=BF16):
    return _unpack_words([ref[pl.ds(j, n, stride=PACK_SUB), :] for j in range(PACK_SUB)]).astype(dtype)


def _packed_row(ref, r):
    return ref.at[pl.ds(pl.multiple_of(r * PACK_SUB, PACK_SUB), PACK_SUB), :]


def _outproj_kernel(yp_ref, yf_ref, ym_ref, x_ref, w_ref, g_ref, b_ref, o_ref, pk_ref, *, alpha):
    pw, fw = yp_ref.shape[1], yf_ref.shape[1]
    h = jnp.dot(yp_ref[...], w_ref[0:pw, :], preferred_element_type=F32)
    h = h + jnp.dot(yf_ref[...], w_ref[pw:pw + fw, :], preferred_element_type=F32)
    h = h + jnp.dot(ym_ref[...], w_ref[pw + fw:, :], preferred_element_type=F32)
    y = _layer_norm(alpha * x_ref[...] + h, g_ref[...], b_ref[...])
    o_ref[...] = y
    _pack_rows(pk_ref, y)


def _outproj(yp, yf, ym, x2, w_out, g, b, alpha, tm=512):
    T, D = x2.shape
    assert D == 2 * PACK_SUB * LANES
    row = lambda i: (i, 0)
    fixed = lambda i: (0, 0)
    return pl.pallas_call(
        functools.partial(_outproj_kernel, alpha=alpha),
        grid=(T // tm,),
        in_specs=[pl.BlockSpec((tm, yp.shape[1]), row), pl.BlockSpec((tm, yf.shape[1]), row),
                  pl.BlockSpec((tm, ym.shape[1]), row), pl.BlockSpec((tm, D), row),
                  pl.BlockSpec(w_out.shape, fixed), pl.BlockSpec((1, D), fixed), pl.BlockSpec((1, D), fixed)],
        out_specs=[pl.BlockSpec((tm, D), row),
                   pl.BlockSpec((tm * PACK_SUB, LANES), row)],
        out_shape=[jax.ShapeDtypeStruct((T, D), F32),
                   jax.ShapeDtypeStruct((T * PACK_SUB, LANES), PACKED)],
        compiler_params=_cparams(("parallel",)),
        name="outproj_ln1",
    )(yp, yf, ym, x2, w_out, g, b)


ROUTE_TILE = 512


def _per_expert_column(v, dtype=F32):
    return jnp.broadcast_to(v.astype(dtype)[:, None], (v.shape[0], ROUTE_TILE))


def _router_kernel(x_ref, wh_ref, wl_ref, bias_ref, eidx_ref, gate_ref, rank_ref, cnt_ref, carry_sc, *, tt):
    E = wh_ref.shape[0]
    gsz = E // N_GROUPS
    ninf = -jnp.inf

    @pl.when(pl.program_id(0) == 0)
    def _():
        carry_sc[...] = jnp.zeros(carry_sc.shape, F32)

    x = x_ref[...]
    xh = x.astype(BF16)
    xl = (x - xh.astype(F32)).astype(BF16)
    wh = wh_ref[...]
    nt = (((1,), (1,)), ((), ()))
    logits = lax.dot_general(wh, xh, nt, preferred_element_type=F32) + (
        lax.dot_general(wl_ref[...], xh, nt, preferred_element_type=F32)
        + lax.dot_general(wh, xl, nt, preferred_element_type=F32))
    scores = 1.0 / (1.0 + jnp.exp(-logits))
    biased = scores + bias_ref[...]
    eiota = lax.broadcasted_iota(I32, (E, tt), 0).astype(F32)
    giota = lax.broadcasted_iota(I32, (gsz, tt), 0).astype(F32)

    def cmax(a):
        return jnp.max(a, axis=0, keepdims=True)

    def first_at(a, m, iota, n):
        return jnp.min(jnp.where(a == m, iota, float(n)), axis=0, keepdims=True)

    groups = [biased[g * gsz:(g + 1) * gsz, :] for g in range(N_GROUPS)]
    gscore = []
    for blk in groups:
        m1 = cmax(blk)
        m2 = cmax(jnp.where(giota == first_at(blk, m1, giota, gsz), ninf, blk))
        gscore.append(m1 + m2)

    kept = []
    for g in range(N_GROUPS):
        ahead = jnp.zeros((1, tt), F32)
        for o in range(N_GROUPS):
            if o == g:
                continue
            beats = gscore[o] > gscore[g]
            if o < g:
                beats = beats | (gscore[o] == gscore[g])
            ahead = ahead + jnp.where(beats, 1.0, 0.0)
        kept.append(groups[g] + jnp.where(ahead < float(TOPK_GROUPS), 0.0, ninf))
    masked = jnp.concatenate(kept, axis=0)

    picks, sels = [], []
    chosen = jnp.zeros((E, tt), F32)
    for _ in range(TOP_K):
        ik = first_at(masked, cmax(masked), eiota, E)
        oh = eiota == ik
        picks.append(ik)
        sels.append(jnp.sum(jnp.where(oh, scores, 0.0), axis=0, keepdims=True))
        chosen = jnp.where(oh, 1.0, chosen)
        masked = jnp.where(oh, ninf, masked)
    denom = sels[0]
    for sk in sels[1:]:
        denom = denom + sk

    r = lax.broadcasted_iota(I32, (tt, tt), 0)
    c = lax.broadcasted_iota(I32, (tt, tt), 1)
    earlier = jnp.where(r < c, 1.0, 0.0).astype(BF16)
    chosen_b = chosen.astype(BF16)
    pos = carry_sc[...] + jnp.dot(chosen_b, earlier, preferred_element_type=F32)
    carry_sc[...] = carry_sc[...] + jnp.dot(chosen_b, jnp.ones((tt, tt), BF16), preferred_element_type=F32)
    cnt_ref[...] = carry_sc[...]

    ranks = [jnp.sum(jnp.where(eiota == ik, pos, 0.0), axis=0, keepdims=True) for ik in picks]
    eidx_ref[...] = jnp.concatenate(picks, axis=0).astype(I32)
    gate_ref[...] = jnp.concatenate([sk / denom * ROUTED_SCALE for sk in sels], axis=0)
    rank_ref[...] = jnp.concatenate(ranks, axis=0).astype(I32)


def _router(x1, wrt_hi, wrt_lo, rbias):
    T = x1.shape[0]
    E, D = wrt_hi.shape
    tt = ROUTE_TILE
    tok = lambda i: (0, i)
    fixed = lambda i: (0, 0)
    return pl.pallas_call(
        functools.partial(_router_kernel, tt=tt),
        grid=(T // tt,),
        in_specs=[pl.BlockSpec((tt, D), lambda i: (i, 0)), pl.BlockSpec((E, D), fixed),
                  pl.BlockSpec((E, D), fixed), pl.BlockSpec((E, tt), fixed)],
        out_specs=[pl.BlockSpec((TOP_K, tt), tok), pl.BlockSpec((TOP_K, tt), tok), pl.BlockSpec((TOP_K, tt), tok),
                   pl.BlockSpec((E, tt), fixed)],
        out_shape=[jax.ShapeDtypeStruct((TOP_K, T), I32), jax.ShapeDtypeStruct((TOP_K, T), F32),
                   jax.ShapeDtypeStruct((TOP_K, T), I32), jax.ShapeDtypeStruct((E, tt), F32)],
        scratch_shapes=[pltpu.VMEM((E, tt), F32)],
        compiler_params=_cparams(("arbitrary",)),
        name="router_topk",
    )(x1, wrt_hi, wrt_lo, rbias)


def _dest_kernel(e_ref, r_ref, sp_ref, d_ref):
    E = sp_ref.shape[0]
    tt = e_ref.shape[1]
    eiota = lax.broadcasted_iota(I32, (E, tt), 0)
    sp = sp_ref[...]
    rows = []
    for kk in range(TOP_K):
        start = jnp.sum(jnp.where(eiota == e_ref[kk:kk + 1, :], sp, 0.0), axis=0, keepdims=True)
        rows.append(start.astype(I32) + r_ref[kk:kk + 1, :])
    d_ref[...] = jnp.concatenate(rows, axis=0)


def _dest(eidx, rank, start_pad_b):
    K, T = eidx.shape
    tt = ROUTE_TILE
    tok = lambda i: (0, i)
    return pl.pallas_call(
        _dest_kernel,
        grid=(T // tt,),
        in_specs=[pl.BlockSpec((K, tt), tok), pl.BlockSpec((K, tt), tok),
                  pl.BlockSpec(start_pad_b.shape, lambda i: (0, 0))],
        out_specs=pl.BlockSpec((K, tt), tok),
        out_shape=jax.ShapeDtypeStruct((K, T), I32),
        compiler_params=_cparams(("parallel",)),
        name="moe_dest",
    )(eidx, rank, start_pad_b)


def _dispatch_kernel(zb_ref, d_ref, x_ref, wsg_ref, wsu_ref, wsd_ref, xs_ref, sh_ref, zero_sc, sem, *, tt, n_exp):
    blk_rows = ROW_BLOCK * PACK_SUB

    @pl.when(pl.program_id(0) == 0)
    def _():
        zero_sc[...] = jnp.zeros(zero_sc.shape, PACKED)

        def zstart(e, c):
            @pl.when(zb_ref[e] >= 0)
            def _():
                dst = xs_ref.at[pl.ds(pl.multiple_of(zb_ref[e] * PACK_SUB, blk_rows), blk_rows), :]
                pltpu.make_async_copy(zero_sc, dst, sem).start()
            return c

        def zwait(e, c):
            @pl.when(zb_ref[e] >= 0)
            def _():
                pltpu.make_async_copy(zero_sc, xs_ref.at[pl.ds(0, blk_rows), :], sem).wait()
            return c

        lax.fori_loop(0, n_exp, zstart, 0)
        lax.fori_loop(0, n_exp, zwait, 0)

    def start(j, c):
        src = _packed_row(x_ref, j)
        for kk in range(TOP_K):
            pltpu.make_async_copy(src, _packed_row(xs_ref, d_ref[kk, j]), sem).start(priority=kk % 2)
        return c

    lax.fori_loop(0, tt, start, 0)
    xb = _unpack_rows(x_ref, tt)
    hs = _silu(jnp.dot(xb, wsg_ref[...], preferred_element_type=F32)) * jnp.dot(
        xb, wsu_ref[...], preferred_element_type=F32)
    sh_ref[...] = jnp.dot(hs.astype(BF16), wsd_ref[...], preferred_element_type=F32)
    for kk in range(TOP_K):
        pltpu.make_async_copy(x_ref, xs_ref.at[pl.ds(0, tt * PACK_SUB), :], sem).wait()


def _dispatch(zero_blk, dest, xpk, wsg, wsu, wsd, n_rows, tt=1024):
    T = xpk.shape[0] // PACK_SUB
    D = wsg.shape[0]
    n_exp = zero_blk.shape[0]
    fixed = lambda i, zb: (0, 0)
    grid_spec = pltpu.PrefetchScalarGridSpec(
        num_scalar_prefetch=1,
        grid=(T // tt,),
        in_specs=[pl.BlockSpec((TOP_K, tt), lambda i, zb: (0, i), memory_space=pltpu.SMEM),
                  pl.BlockSpec((tt * PACK_SUB, LANES), lambda i, zb: (i, 0)),
                  pl.BlockSpec(wsg.shape, fixed), pl.BlockSpec(wsu.shape, fixed), pl.BlockSpec(wsd.shape, fixed)],
        out_specs=[pl.BlockSpec(memory_space=pl.ANY), pl.BlockSpec((tt, D), lambda i, zb: (i, 0))],
        scratch_shapes=[pltpu.VMEM((ROW_BLOCK * PACK_SUB, LANES), PACKED), pltpu.SemaphoreType.DMA],
    )
    return pl.pallas_call(
        functools.partial(_dispatch_kernel, tt=tt, n_exp=n_exp),
        grid_spec=grid_spec,
        out_shape=[jax.ShapeDtypeStruct((n_rows * PACK_SUB, LANES), PACKED), jax.ShapeDtypeStruct((T, D), F32)],
        compiler_params=_cparams(("arbitrary",)),
        name="moe_dispatch",
    )(zero_blk, dest, xpk, wsg, wsu, wsd)


def _expert_kernel(sb_ref, nb_ref, nu_ref, wg_ref, wu_ref, wd_ref, xs_ref, y_ref,
                   xbuf, ybuf, wg_sc, wu_sc, wd_sc, isem, osem):
    n_used = nu_ref[0]
    in_rows = out_rows = ROW_BLOCK * PACK_SUB

    def fetch(blk, slot):
        src = xs_ref.at[pl.ds(pl.multiple_of(blk * in_rows, in_rows), in_rows), :]
        pltpu.make_async_copy(src, xbuf.at[slot], isem.at[slot]).start()

    def wait_in(slot):
        pltpu.make_async_copy(xs_ref.at[pl.ds(0, in_rows), :], xbuf.at[slot], isem.at[slot]).wait()

    def store(blk, slot):
        dst = y_ref.at[pl.ds(pl.multiple_of(blk * out_rows, out_rows), out_rows), :]
        pltpu.make_async_copy(ybuf.at[slot], dst, osem.at[slot]).start()

    def wait_out(slot):
        pltpu.make_async_copy(ybuf.at[slot], y_ref.at[pl.ds(0, out_rows), :], osem.at[slot]).wait()

    def one_expert(sub, nb, base):
        @pl.when(base == 0)
        def _():
            for g0 in range(EXPERT_AHEAD):
                @pl.when(g0 < n_used)
                def _():
                    fetch(g0, g0 % EXPERT_BUFS)

        wg_sc[...] = wg_ref[sub].astype(BF16)
        wu_sc[...] = wu_ref[sub].astype(BF16)
        wd_sc[...] = wd_ref[sub].astype(BF16)

        def run(blocks):
            slots = [g % EXPERT_BUFS for g in blocks]
            for g, slot in zip(blocks, slots):
                wait_in(slot)

                @pl.when(g + EXPERT_AHEAD < n_used)
                def _():
                    fetch(g + EXPERT_AHEAD, (g + EXPERT_AHEAD) % EXPERT_BUFS)

                @pl.when(g >= EXPERT_BUFS)
                def _():
                    wait_out(slot)

            for g, slot in zip(blocks, slots):
                xb = _unpack_rows(xbuf.at[slot], ROW_BLOCK)
                a = jnp.dot(xb, wg_sc[...], preferred_element_type=F32)
                u = jnp.dot(xb, wu_sc[...], preferred_element_type=F32)
                h = (_silu(a) * u).astype(BF16)
                _pack_rows(ybuf.at[slot], jnp.dot(h, wd_sc[...], preferred_element_type=F32))
            for g, slot in zip(blocks, slots):
                store(g, slot)

        def region(p, c):
            run([base + EXPERT_REGION * p + i for i in range(EXPERT_REGION)])
            return c

        lax.fori_loop(0, nb // EXPERT_REGION, region, 0)
        done = nb // EXPERT_REGION * EXPERT_REGION

        @pl.when(nb % EXPERT_REGION >= 2)
        def _():
            run([base + done, base + done + 1])

        @pl.when(nb % 2 == 1)
        def _():
            run([base + nb - 1])

    for sub in range(EXPERTS_PER_STEP):
        e = pl.program_id(0) * EXPERTS_PER_STEP + sub
        pl.when(nb_ref[e] > 0)(functools.partial(one_expert, sub, nb_ref[e], sb_ref[e]))

    @pl.when(pl.program_id(0) == pl.num_programs(0) - 1)
    def _():
        for slot in range(EXPERT_BUFS):
            @pl.when(slot < n_used)
            def _():
                wait_out(slot)


def _experts(start_blk, n_blk_e, n_used, xs, w_gate, w_up, w_down):
    n_rows = xs.shape[0] // PACK_SUB
    E, D, H = w_gate.shape
    wsel = lambda e, sb, nb, nu: (e, 0, 0)
    eps = EXPERTS_PER_STEP
    assert E % eps == 0
    grid_spec = pltpu.PrefetchScalarGridSpec(
        num_scalar_prefetch=3,
        grid=(E // eps,),
        in_specs=[pl.BlockSpec((eps, D, H), wsel), pl.BlockSpec((eps, D, H), wsel), pl.BlockSpec((eps, H, D), wsel),
                  pl.BlockSpec(memory_space=pl.ANY)],
        out_specs=pl.BlockSpec(memory_space=pl.ANY),
        scratch_shapes=[pltpu.VMEM((EXPERT_BUFS, ROW_BLOCK * PACK_SUB, LANES), PACKED),
                        pltpu.VMEM((EXPERT_BUFS, ROW_BLOCK * PACK_SUB, LANES), PACKED),
                        pltpu.VMEM((D, H), BF16), pltpu.VMEM((D, H), BF16), pltpu.VMEM((H, D), BF16),
                        pltpu.SemaphoreType.DMA((EXPERT_BUFS,)), pltpu.SemaphoreType.DMA((EXPERT_BUFS,))],
    )
    return pl.pallas_call(
        _expert_kernel,
        grid_spec=grid_spec,
        out_shape=jax.ShapeDtypeStruct((n_rows * PACK_SUB, LANES), PACKED),
        compiler_params=_cparams(("arbitrary",)),
        name="moe_experts",
    )(start_blk, n_blk_e, n_used, w_gate, w_up, w_down, xs)


def _combine_kernel(dcur_ref, dnxt_ref, x_ref, gate_ref, sh_ref, g_ref, b_ref, y_ref, o_ref, buf, sem, *, tt, alpha):
    i = pl.program_id(0)
    n = pl.num_programs(0)

    def fetch(d_ref, slot):
        def body(j, c):
            for kk in range(TOP_K):
                src = _packed_row(y_ref, d_ref[kk * tt + j])
                pltpu.make_async_copy(src, _packed_row(buf.at[slot, kk], j), sem.at[slot]).start(priority=kk % 2)
            return c

        lax.fori_loop(0, tt, body, 0)

    def reduce(slot):
        for kk in range(TOP_K):
            pltpu.make_async_copy(y_ref.at[pl.ds(0, tt * PACK_SUB), :], buf.at[slot, kk], sem.at[slot]).wait()
        x = x_ref[...]
        moe = sh_ref[...]
        gates = gate_ref[...]
        for kk in range(TOP_K):
            moe = moe + gates[:, kk:kk + 1] * _unpack_rows(buf.at[slot, kk], tt, F32)
        o_ref[...] = _layer_norm(alpha * x + moe, g_ref[...], b_ref[...])

    @pl.when(i == 0)
    def _():
        fetch(dcur_ref, 0)

    for slot in range(2):
        @pl.when((i % 2 == slot) & (i + 1 < n))
        def _():
            fetch(dnxt_ref, 1 - slot)

    for slot in range(2):
        @pl.when(i % 2 == slot)
        def _():
            reduce(slot)


def _combine(dest, x1, gates, shared, g, b, y, alpha, tt=256):
    T, D = x1.shape
    n = T // tt
    dest = dest.reshape(TOP_K, n, tt).transpose(1, 0, 2).reshape(n * TOP_K * tt)
    row = lambda i: (i, 0)
    fixed = lambda i: (0, 0)
    return pl.pallas_call(
        functools.partial(_combine_kernel, tt=tt, alpha=alpha),
        grid=(n,),
        in_specs=[pl.BlockSpec((TOP_K * tt,), lambda i: (i,), memory_space=pltpu.SMEM),
                  pl.BlockSpec((TOP_K * tt,), lambda i: (jnp.minimum(i + 1, n - 1),), memory_space=pltpu.SMEM),
                  pl.BlockSpec((tt, D), row), pl.BlockSpec((tt, TOP_K), row),
                  pl.BlockSpec((tt, D), row),
                  pl.BlockSpec((1, D), fixed), pl.BlockSpec((1, D), fixed),
                  pl.BlockSpec(memory_space=pl.ANY)],
        out_specs=pl.BlockSpec((tt, D), row),
        out_shape=jax.ShapeDtypeStruct((T, D), F32),
        scratch_shapes=[pltpu.VMEM((2, TOP_K, tt * PACK_SUB, LANES), PACKED), pltpu.SemaphoreType.DMA((2,))],
        compiler_params=_cparams(("arbitrary",)),
        name="moe_combine_ln2",
    )(dest, dest, x1, gates, shared, g, b, y)


def _layer(x2, mem, B, S, depth, w_in, b_f, w_pool, pool_scale, w_mem_kv, w_out, ln1_g, ln1_b,
           w_router, router_bias, w_gate, w_up, w_down, ws_gate, ws_up, ws_down, ln2_g, ln2_b):
    T, D = x2.shape
    n_win, grp = w_pool.shape[0], w_pool.shape[1]
    pw = n_win * grp
    fw = FOX_HEADS * HEAD_DIM
    mw = w_mem_kv.shape[1] // 2
    E = w_router.shape[1]
    alpha = (2 * depth) ** 0.25

    f_lo = pw + 3 * fw
    w_main = jnp.concatenate([w_in[:, :f_lo], w_in[:, f_lo + FOX_HEADS:]], axis=1).astype(BF16)
    w_f = jnp.pad(w_in[:, f_lo:f_lo + FOX_HEADS], ((0, 0), (0, LANES - FOX_HEADS))).astype(BF16)
    bf_pad = jnp.pad(b_f, (0, LANES - FOX_HEADS)).reshape(1, LANES)
    wbd = jnp.zeros((pw, pw), F32)
    for g in range(n_win):
        wbd = wbd.at[g * grp:(g + 1) * grp, g * grp:(g + 1) * grp].set(w_pool[g])
    wrt = w_router.T
    wrt_hi = wrt.astype(BF16)
    wrt_lo = (wrt - wrt_hi.astype(F32)).astype(BF16)

    u_pool, q, k, v, q_mem, f_logit = _inproj(x2, w_main, w_f, pw, fw, mw)
    y_pool = _pool(u_pool, wbd.astype(BF16), pool_scale.reshape(1, pw), B, S)
    fcum = _fgate(f_logit, bf_pad, B, S)
    y_fox = _fox(q, k, v, fcum, B, S)
    y_mem = _memattn(q_mem, mem, w_mem_kv.astype(BF16), B, S)
    x1, x1_packed = _outproj(y_pool, y_fox, y_mem, x2, w_out.astype(BF16), ln1_g.reshape(1, D),
                             ln1_b.reshape(1, D), alpha)

    eidx, gates, rank, counts = _router(x1, wrt_hi, wrt_lo, _per_expert_column(router_bias))

    cnt = counts[:, 0].astype(I32)
    padded = (cnt + ROW_BLOCK - 1) // ROW_BLOCK * ROW_BLOCK
    pad_end = jnp.cumsum(padded)
    start_pad = (pad_end - padded).astype(I32)
    n_rows = (T * TOP_K + E * (ROW_BLOCK - 1)) // ROW_BLOCK * ROW_BLOCK
    n_used = (pad_end[-1] // ROW_BLOCK).astype(I32)
    zero_blk = jnp.where(padded > 0, pad_end - ROW_BLOCK, -1).astype(I32)
    dest = _dest(eidx, rank, _per_expert_column(start_pad))

    xs, shared = _dispatch(zero_blk, dest, x1_packed, ws_gate.astype(BF16), ws_up.astype(BF16),
                           ws_down.astype(BF16), n_rows)
    y = _experts(start_pad // ROW_BLOCK, (padded // ROW_BLOCK).astype(I32), n_used.reshape(1), xs,
                 w_gate, w_up, w_down)
    return _combine(dest, x1, gates.T, shared, ln2_g.reshape(1, D), ln2_b.reshape(1, D), y, alpha)


def kernel(x, mem, w_in, b_f, w_pool, pool_scale, w_mem_kv, w_out, ln1_g, ln1_b, w_router, router_bias,
           w_gate, w_up, w_down, ws_gate, ws_up, ws_down, ln2_g, ln2_b):
    B, S, D = x.shape
    depth = w_in.shape[0]
    x2 = x.reshape(B * S, D)
    for l in range(depth):
        x2 = _layer(x2, mem, B, S, depth, w_in[l], b_f[l], w_pool[l], pool_scale[l], w_mem_kv[l], w_out[l],
                    ln1_g[l], ln1_b[l], w_router[l], router_bias[l], w_gate[l], w_up[l], w_down[l],
                    ws_gate[l], ws_up[l], ws_down[l], ln2_g[l], ln2_b[l])
    return x2.reshape(B, S, D)
```

```python
import functools

import jax
import jax.numpy as jnp
from jax import lax
from jax.experimental import pallas as pl
from jax.experimental.pallas import tpu as pltpu

F32 = jnp.float32
BF16 = jnp.bfloat16
I32 = jnp.int32
PACKED = jnp.int32

LANES = 128
POOL_WINDOWS = (2, 4, 8, 16)
POOL_HALO = 16
HEAD_DIM = 64
FOX_HEADS = 8
BIAS_TERMS = 3
MEM_HEADS = 4
TOP_K = 8
N_GROUPS = 8
TOPK_GROUPS = 4
ROUTED_SCALE = 2.5
LN_EPS = 1e-5
LOG2E = 1.4426950408889634
ROW_BLOCK = 256
EXPERT_BUFS = 8
EXPERT_REGION = 2
EXPERT_AHEAD = EXPERT_BUFS - EXPERT_REGION
VMEM_LIMIT = 48 * 1024 * 1024


def _cparams(sem):
    return pltpu.CompilerParams(dimension_semantics=sem, vmem_limit_bytes=VMEM_LIMIT)


def _layer_norm(z, g, b):
    mu = jnp.mean(z, axis=-1, keepdims=True)
    zc = z - mu
    var = jnp.mean(zc * zc, axis=-1, keepdims=True)
    return zc * lax.rsqrt(var + LN_EPS) * g + b


def _silu(x):
    return x * (1.0 / (1.0 + jnp.exp(-x)))


def _inproj_kernel(x_ref, w_ref, wf_ref, up_ref, q_ref, k_ref, v_ref, qm_ref, f_ref, *, pw, fw, mw, scale):
    xb = x_ref[...].astype(BF16)

    def proj(lo, hi):
        return jnp.dot(xb, w_ref[:, lo:hi], preferred_element_type=F32)

    up_ref[...] = proj(0, pw)
    q_ref[...] = (proj(pw, pw + fw) * (scale * LOG2E)).astype(BF16)
    k_ref[...] = proj(pw + fw, pw + 2 * fw).astype(BF16)
    v_ref[...] = proj(pw + 2 * fw, pw + 3 * fw).astype(BF16)
    qm_ref[...] = (proj(pw + 3 * fw, pw + 3 * fw + mw) * scale).astype(BF16)
    f_ref[...] = jnp.dot(xb, wf_ref[...], preferred_element_type=F32)


def _inproj(x2, w_main, w_f, pw, fw, mw, tm=512):
    T, D = x2.shape
    kern = functools.partial(_inproj_kernel, pw=pw, fw=fw, mw=mw, scale=HEAD_DIM ** -0.5)
    row = lambda i: (i, 0)
    fixed = lambda i: (0, 0)
    return pl.pallas_call(
        kern,
        grid=(T // tm,),
        in_specs=[pl.BlockSpec((tm, D), row),
                  pl.BlockSpec(w_main.shape, fixed),
                  pl.BlockSpec(w_f.shape, fixed)],
        out_specs=[pl.BlockSpec((tm, pw), row), pl.BlockSpec((tm, fw), row), pl.BlockSpec((tm, fw), row),
                   pl.BlockSpec((tm, fw), row), pl.BlockSpec((tm, mw), row), pl.BlockSpec((tm, LANES), row)],
        out_shape=[jax.ShapeDtypeStruct((T, pw), F32), jax.ShapeDtypeStruct((T, fw), BF16),
                   jax.ShapeDtypeStruct((T, fw), BF16), jax.ShapeDtypeStruct((T, fw), BF16),
                   jax.ShapeDtypeStruct((T, mw), BF16), jax.ShapeDtypeStruct((T, LANES), F32)],
        compiler_params=_cparams(("parallel",)),
        name="inproj",
    )(x2, w_main, w_f)


def _pool_kernel(u_ref, wbd_ref, sc_ref, o_ref, ext_sc, *, chunk, group):
    S, W = u_ref.shape
    ext_sc[0:POOL_HALO, :] = jnp.zeros((POOL_HALO, W), F32)
    ext_sc[POOL_HALO:, :] = u_ref[...]
    rows = chunk + POOL_HALO
    lrow = lax.broadcasted_iota(I32, (rows, W), 0)
    lane = lax.broadcasted_iota(I32, (rows, W), 1)

    def body(c, carry):
        start = pl.multiple_of(c * chunk, chunk)
        e = ext_sc[pl.ds(start, rows), :]
        posf = (lrow + (start - POOL_HALO + 1)).astype(F32)
        acc = e
        d = jnp.zeros_like(e)
        shift = 1
        for g, w in enumerate(POOL_WINDOWS):
            while shift < w:
                acc = acc + pltpu.roll(acc, shift, axis=0)
                shift *= 2
            mean = acc / jnp.minimum(posf, float(w))
            d = jnp.where((lane >= g * group) & (lane < (g + 1) * group), mean, d)
        d = (d - e)[POOL_HALO:, :]
        y = jnp.dot(d.astype(BF16), wbd_ref[...], preferred_element_type=F32) * sc_ref[...]
        o_ref[pl.ds(start, chunk), :] = y.astype(BF16)
        return carry

    lax.fori_loop(0, S // chunk, body, 0)


def _pool(u, wbd, pscale, B, S, chunk=512):
    T, W = u.shape
    kern = functools.partial(_pool_kernel, chunk=chunk, group=W // len(POOL_WINDOWS))
    return pl.pallas_call(
        kern,
        grid=(B,),
        in_specs=[pl.BlockSpec((S, W), lambda b: (b, 0)),
                  pl.BlockSpec((W, W), lambda b: (0, 0)),
                  pl.BlockSpec((1, W), lambda b: (0, 0))],
        out_specs=pl.BlockSpec((S, W), lambda b: (b, 0)),
        out_shape=jax.ShapeDtypeStruct((T, W), BF16),
        scratch_shapes=[pltpu.VMEM((S + POOL_HALO, W), F32)],
        compiler_params=_cparams(("parallel",)),
        name="pool_mixer",
    )(u, wbd, pscale)


def _fgate_kernel(f_ref, bf_ref, o_ref):
    S = f_ref.shape[0]
    z = f_ref[...] + bf_ref[...]
    x = jnp.minimum(z, 0.0) - jnp.log(1.0 + jnp.exp(-jnp.abs(z)))
    row = lax.broadcasted_iota(I32, x.shape, 0)
    lane = lax.broadcasted_iota(I32, x.shape, 1)
    shift = 1
    while shift < S:
        x = x + jnp.where(row >= shift, pltpu.roll(x, shift, axis=0), 0.0)
        shift *= 2
    x = x * LOG2E
    hi = x.astype(BF16).astype(F32)
    mid = (x - hi).astype(BF16).astype(F32)
    lo = (x - hi) - mid
    for p in range(FOX_HEADS // 2):
        out = jnp.zeros(x.shape, F32)
        for hh in range(2):
            c = 2 * p + hh
            for j, part in enumerate((hi, mid, lo)):
                out = jnp.where(lane == BIAS_TERMS * hh + j, -part[:, c:c + 1], out)
        o_ref[0, p] = out.astype(BF16)


def _fgate(f_logit, bf_pad, B, S):
    return pl.pallas_call(
        _fgate_kernel,
        grid=(B,),
        in_specs=[pl.BlockSpec((S, LANES), lambda b: (b, 0)),
                  pl.BlockSpec((1, LANES), lambda b: (0, 0))],
        out_specs=pl.BlockSpec((1, FOX_HEADS // 2, S, LANES), lambda b: (b, 0, 0, 0)),
        out_shape=jax.ShapeDtypeStruct((B, FOX_HEADS // 2, S, LANES), BF16),
        compiler_params=_cparams(("parallel",)),
        name="forget_cumsum",
    )(f_logit, bf_pad)


def _fox_kernel(q_ref, k_ref, a_ref, v_ref, o_ref, vt_sc, sa_sc, sb_sc,
                qa0_sc, m0_sc, l0_sc, acc0_sc, qa1_sc, m1_sc, l1_sc, acc1_sc, *, blk):
    j = pl.program_id(2)
    nk = vt_sc.shape[0]
    nt = (((1,), (1,)), ((), ()))
    sets = ((qa0_sc, m0_sc, l0_sc, acc0_sc), (qa1_sc, m1_sc, l1_sc, acc1_sc))

    @pl.when(j == 0)
    def _():
        for c in range(nk):
            vt_sc[c] = v_ref[c * blk:(c + 1) * blk, :].astype(F32).T.astype(BF16)

    def prepare(which):
        qa_sc, m_sc, l_sc, acc_sc = sets[which]
        q = q_ref[which * blk:(which + 1) * blk, :].astype(F32)
        lane = lax.broadcasted_iota(I32, q.shape, 1)
        for h in range(2):
            mine = (lane >= h * HEAD_DIM) & (lane < (h + 1) * HEAD_DIM)
            bias_rows = (lane >= h * BIAS_TERMS) & (lane < (h + 1) * BIAS_TERMS)
            qa_sc[h] = jnp.concatenate(
                [jnp.where(mine, q, 0.0), jnp.where(bias_rows, 1.0, 0.0)], axis=1).astype(BF16)
        m_sc[...] = jnp.full(m_sc.shape, -jnp.inf, F32)
        l_sc[...] = jnp.zeros(l_sc.shape, F32)
        acc_sc[...] = jnp.zeros(acc_sc.shape, F32)

    def scores(ki, dst, which):
        qa_sc = sets[which][0]
        ks = pl.multiple_of(ki * blk, blk)
        kb = jnp.concatenate([k_ref[pl.ds(ks, blk), :], a_ref[0, 0, pl.ds(ks, blk), :]], axis=1)
        for h in range(2):
            dst[h] = lax.dot_general(kb, qa_sc[h], nt, preferred_element_type=F32)

    def absorb(ki, src, causal, which):
        _, m_sc, l_sc, acc_sc = sets[which]
        for h in range(2):
            st = src[h]
            if causal:
                kpos = lax.broadcasted_iota(I32, st.shape, 0)
                qpos = lax.broadcasted_iota(I32, st.shape, 1)
                st = jnp.where(kpos <= qpos, st, -jnp.inf)
            m_prev = m_sc[h]
            m_new = jnp.maximum(m_prev, jnp.max(st, axis=0, keepdims=True))
            alpha = jnp.exp2(m_prev - m_new)
            p = jnp.exp2(st - m_new)
            l_sc[h] = alpha * l_sc[h] + jnp.sum(p, axis=0, keepdims=True)
            m_sc[h] = m_new
            vt = vt_sc[ki, h * HEAD_DIM:(h + 1) * HEAD_DIM, :]
            pv = jnp.dot(vt, p.astype(BF16), preferred_element_type=F32)
            rows = slice(h * HEAD_DIM, (h + 1) * HEAD_DIM)
            acc_sc[rows, :] = acc_sc[rows, :] * alpha + pv

    def finish(which):
        _, _, l_sc, acc_sc = sets[which]
        out_t = jnp.concatenate([acc_sc[h * HEAD_DIM:(h + 1) * HEAD_DIM, :] / l_sc[h] for h in range(2)], axis=0)
        o_ref[which * blk:(which + 1) * blk, :] = out_t.T.astype(BF16)

    def walk(first, other, which):
        def body(p, carry):
            scores(2 * p + 1, other, which)
            absorb(2 * p, first, False, which)
            scores(2 * p + 2, first, which)
            absorb(2 * p + 1, other, False, which)
            return carry

        lax.fori_loop(0, j, body, 0)

    prepare(0)
    scores(0, sa_sc, 0)
    walk(sa_sc, sb_sc, 0)
    prepare(1)
    scores(0, sb_sc, 1)
    absorb(2 * j, sa_sc, True, 0)
    finish(0)
    walk(sb_sc, sa_sc, 1)
    scores(2 * j + 1, sa_sc, 1)
    absorb(2 * j, sb_sc, False, 1)
    absorb(2 * j + 1, sa_sc, True, 1)
    finish(1)


def _fox(q, k, v, fbias, B, S, blk=512):
    T, FW = q.shape
    nq = S // blk
    pairs = FW // LANES
    kern = functools.partial(_fox_kernel, blk=blk)
    return pl.pallas_call(
        kern,
        grid=(B, pairs, nq // 2),
        in_specs=[pl.BlockSpec((2 * blk, LANES), lambda b, p, i: (b * (nq // 2) + i, p)),
                  pl.BlockSpec((S, LANES), lambda b, p, i: (b, p)),
                  pl.BlockSpec((1, 1, S, LANES), lambda b, p, i: (b, p, 0, 0)),
                  pl.BlockSpec((S, LANES), lambda b, p, i: (b, p))],
        out_specs=pl.BlockSpec((2 * blk, LANES), lambda b, p, i: (b * (nq // 2) + i, p)),
        out_shape=jax.ShapeDtypeStruct((T, FW), BF16),
        scratch_shapes=[pltpu.VMEM((nq, LANES, blk), BF16),
                        pltpu.VMEM((2, blk, blk), F32), pltpu.VMEM((2, blk, blk), F32)] + 2 * [
                            pltpu.VMEM((2, blk, 2 * LANES), BF16), pltpu.VMEM((2, 1, blk), F32),
                            pltpu.VMEM((2, 1, blk), F32), pltpu.VMEM((LANES, blk), F32)],
        compiler_params=_cparams(("parallel", "parallel", "arbitrary")),
        name="fox_attention",
    )(q, k, fbias, v)


def _memattn_kernel(qm_ref, mem_ref, wkv_ref, o_ref, k_sc, v_sc):
    MW = qm_ref.shape[1]

    @pl.when(pl.program_id(1) == 0)
    def _():
        kv = jnp.dot(mem_ref[0].astype(BF16), wkv_ref[...], preferred_element_type=F32)
        k_sc[...] = kv[:, :MW].astype(BF16)
        v_sc[...] = kv[:, MW:].astype(BF16)

    q = qm_ref[...]
    lane = lax.broadcasted_iota(I32, q.shape, 1)
    hd = MW // MEM_HEADS
    out = jnp.zeros(q.shape, F32)
    for h in range(MEM_HEADS):
        mine = (lane >= h * hd) & (lane < (h + 1) * hd)
        qh = jnp.where(mine, q, jnp.zeros_like(q))
        s = lax.dot_general(qh, k_sc[...], (((1,), (1,)), ((), ())), preferred_element_type=F32)
        p = jnp.exp(s - jnp.max(s, axis=1, keepdims=True))
        l = jnp.sum(p, axis=1, keepdims=True)
        o = jnp.dot(p.astype(BF16), v_sc[...], preferred_element_type=F32)
        out = jnp.where(mine, o / l, out)
    o_ref[...] = out.astype(BF16)


def _memattn(qm, mem, wkv, B, S, tq=512):
    T, MW = qm.shape
    M, D = mem.shape[1], mem.shape[2]
    nq = S // tq
    return pl.pallas_call(
        _memattn_kernel,
        grid=(B, nq),
        in_specs=[pl.BlockSpec((tq, MW), lambda b, i: (b * nq + i, 0)),
                  pl.BlockSpec((1, M, D), lambda b, i: (b, 0, 0)),
                  pl.BlockSpec(wkv.shape, lambda b, i: (0, 0))],
        out_specs=pl.BlockSpec((tq, MW), lambda b, i: (b * nq + i, 0)),
        out_shape=jax.ShapeDtypeStruct((T, MW), BF16),
        scratch_shapes=[pltpu.VMEM((M, MW), BF16), pltpu.VMEM((M, MW), BF16)],
        compiler_params=_cparams(("parallel", "arbitrary")),
        name="memory_attention",
    )(qm, mem, wkv)


PACK_SUB = 4


def _pack_rows(ref, val):
    n = val.shape[0]
    half = val.shape[1] // 2
    words = pltpu.pack_elementwise([val[:, :half], val[:, half:]], packed_dtype=BF16)
    for j in range(PACK_SUB):
        ref[pl.ds(j, n, stride=PACK_SUB), :] = words[:, j * LANES:(j + 1) * LANES]


def _unpack_words(slabs):
    lo = [pltpu.unpack_elementwise(s, index=0, packed_dtype=BF16, unpacked_dtype=F32) for s in slabs]
    hi = [pltpu.unpack_elementwise(s, index=1, packed_dtype=BF16, unpacked_dtype=F32) for s in slabs]
    return jnp.concatenate(lo + hi, axis=1)


def _unpack_rows(ref, n, dtype=BF16):
    return _unpack_words([ref[pl.ds(j, n, stride=PACK_SUB), :] for j in range(PACK_SUB)]).astype(dtype)


def _packed_row(ref, r):
    return ref.at[pl.ds(pl.multiple_of(r * PACK_SUB, PACK_SUB), PACK_SUB), :]


def _outproj_kernel(yp_ref, yf_ref, ym_ref, x_ref, w_ref, g_ref, b_ref, o_ref, pk_ref, *, alpha):
    pw, fw = yp_ref.shape[1], yf_ref.shape[1]
    h = jnp.dot(yp_ref[...], w_ref[0:pw, :], preferred_element_type=F32)
    h = h + jnp.dot(yf_ref[...], w_ref[pw:pw + fw, :], preferred_element_type=F32)
    h = h + jnp.dot(ym_ref[...], w_ref[pw + fw:, :], preferred_element_type=F32)
    y = _layer_norm(alpha * x_ref[...] + h, g_ref[...], b_ref[...])
    o_ref[...] = y
    _pack_rows(pk_ref, y)


def _outproj(yp, yf, ym, x2, w_out, g, b, alpha, tm=512):
    T, D = x2.shape
    assert D == 2 * PACK_SUB * LANES
    row = lambda i: (i, 0)
    fixed = lambda i: (0, 0)
    return pl.pallas_call(
        functools.partial(_outproj_kernel, alpha=alpha),
        grid=(T // tm,),
        in_specs=[pl.BlockSpec((tm, yp.shape[1]), row), pl.BlockSpec((tm, yf.shape[1]), row),
                  pl.BlockSpec((tm, ym.shape[1]), row), pl.BlockSpec((tm, D), row),
                  pl.BlockSpec(w_out.shape, fixed), pl.BlockSpec((1, D), fixed), pl.BlockSpec((1, D), fixed)],
        out_specs=[pl.BlockSpec((tm, D), row),
                   pl.BlockSpec((tm * PACK_SUB, LANES), row)],
        out_shape=[jax.ShapeDtypeStruct((T, D), F32),
                   jax.ShapeDtypeStruct((T * PACK_SUB, LANES), PACKED)],
        compiler_params=_cparams(("parallel",)),
        name="outproj_ln1",
    )(yp, yf, ym, x2, w_out, g, b)


ROUTE_TILE = 512


def _per_expert_column(v, dtype=F32):
    return jnp.broadcast_to(v.astype(dtype)[:, None], (v.shape[0], ROUTE_TILE))


def _router_kernel(x_ref, wh_ref, wl_ref, bias_ref, eidx_ref, gate_ref, rank_ref, cnt_ref, carry_sc, *, tt):
    E = wh_ref.shape[0]
    gsz = E // N_GROUPS
    ninf = -jnp.inf

    @pl.when(pl.program_id(0) == 0)
    def _():
        carry_sc[...] = jnp.zeros(carry_sc.shape, F32)

    x = x_ref[...]
    xh = x.astype(BF16)
    xl = (x - xh.astype(F32)).astype(BF16)
    wh = wh_ref[...]
    nt = (((1,), (1,)), ((), ()))
    logits = lax.dot_general(wh, xh, nt, preferred_element_type=F32) + (
        lax.dot_general(wl_ref[...], xh, nt, preferred_element_type=F32)
        + lax.dot_general(wh, xl, nt, preferred_element_type=F32))
    scores = 1.0 / (1.0 + jnp.exp(-logits))
    biased = scores + bias_ref[...]
    eiota = lax.broadcasted_iota(I32, (E, tt), 0).astype(F32)
    giota = lax.broadcasted_iota(I32, (gsz, tt), 0).astype(F32)

    def cmax(a):
        return jnp.max(a, axis=0, keepdims=True)

    def first_at(a, m, iota, n):
        return jnp.min(jnp.where(a == m, iota, float(n)), axis=0, keepdims=True)

    groups = [biased[g * gsz:(g + 1) * gsz, :] for g in range(N_GROUPS)]
    gscore = []
    for blk in groups:
        m1 = cmax(blk)
        m2 = cmax(jnp.where(giota == first_at(blk, m1, giota, gsz), ninf, blk))
        gscore.append(m1 + m2)

    kept = []
    for g in range(N_GROUPS):
        ahead = jnp.zeros((1, tt), F32)
        for o in range(N_GROUPS):
            if o == g:
                continue
            beats = gscore[o] > gscore[g]
            if o < g:
                beats = beats | (gscore[o] == gscore[g])
            ahead = ahead + jnp.where(beats, 1.0, 0.0)
        kept.append(groups[g] + jnp.where(ahead < float(TOPK_GROUPS), 0.0, ninf))
    masked = jnp.concatenate(kept, axis=0)

    picks, sels = [], []
    chosen = jnp.zeros((E, tt), F32)
    for _ in range(TOP_K):
        ik = first_at(masked, cmax(masked), eiota, E)
        oh = eiota == ik
        picks.append(ik)
        sels.append(jnp.sum(jnp.where(oh, scores, 0.0), axis=0, keepdims=True))
        chosen = jnp.where(oh, 1.0, chosen)
        masked = jnp.where(oh, ninf, masked)
    denom = sels[0]
    for sk in sels[1:]:
        denom = denom + sk

    r = lax.broadcasted_iota(I32, (tt, tt), 0)
    c = lax.broadcasted_iota(I32, (tt, tt), 1)
    earlier = jnp.where(r < c, 1.0, 0.0).astype(BF16)
    chosen_b = chosen.astype(BF16)
    pos = carry_sc[...] + jnp.dot(chosen_b, earlier, preferred_element_type=F32)
    carry_sc[...] = carry_sc[...] + jnp.dot(chosen_b, jnp.ones((tt, tt), BF16), preferred_element_type=F32)
    cnt_ref[...] = carry_sc[...]

    ranks = [jnp.sum(jnp.where(eiota == ik, pos, 0.0), axis=0, keepdims=True) for ik in picks]
    eidx_ref[...] = jnp.concatenate(picks, axis=0).astype(I32)
    gate_ref[...] = jnp.concatenate([sk / denom * ROUTED_SCALE for sk in sels], axis=0)
    rank_ref[...] = jnp.concatenate(ranks, axis=0).astype(I32)


def _router(x1, wrt_hi, wrt_lo, rbias):
    T = x1.shape[0]
    E, D = wrt_hi.shape
    tt = ROUTE_TILE
    tok = lambda i: (0, i)
    fixed = lambda i: (0, 0)
    return pl.pallas_call(
        functools.partial(_router_kernel, tt=tt),
        grid=(T // tt,),
        in_specs=[pl.BlockSpec((tt, D), lambda i: (i, 0)), pl.BlockSpec((E, D), fixed),
                  pl.BlockSpec((E, D), fixed), pl.BlockSpec((E, tt), fixed)],
        out_specs=[pl.BlockSpec((TOP_K, tt), tok), pl.BlockSpec((TOP_K, tt), tok), pl.BlockSpec((TOP_K, tt), tok),
                   pl.BlockSpec((E, tt), fixed)],
        out_shape=[jax.ShapeDtypeStruct((TOP_K, T), I32), jax.ShapeDtypeStruct((TOP_K, T), F32),
                   jax.ShapeDtypeStruct((TOP_K, T), I32), jax.ShapeDtypeStruct((E, tt), F32)],
        scratch_shapes=[pltpu.VMEM((E, tt), F32)],
        compiler_params=_cparams(("arbitrary",)),
        name="router_topk",
    )(x1, wrt_hi, wrt_lo, rbias)


def _dest_kernel(e_ref, r_ref, sp_ref, d_ref):
    E = sp_ref.shape[0]
    tt = e_ref.shape[1]
    eiota = lax.broadcasted_iota(I32, (E, tt), 0)
    sp = sp_ref[...]
    rows = []
    for kk in range(TOP_K):
        start = jnp.sum(jnp.where(eiota == e_ref[kk:kk + 1, :], sp, 0.0), axis=0, keepdims=True)
        rows.append(start.astype(I32) + r_ref[kk:kk + 1, :])
    d_ref[...] = jnp.concatenate(rows, axis=0)


def _dest(eidx, rank, start_pad_b):
    K, T = eidx.shape
    tt = ROUTE_TILE
    tok = lambda i: (0, i)
    return pl.pallas_call(
        _dest_kernel,
        grid=(T // tt,),
        in_specs=[pl.BlockSpec((K, tt), tok), pl.BlockSpec((K, tt), tok),
                  pl.BlockSpec(start_pad_b.shape, lambda i: (0, 0))],
        out_specs=pl.BlockSpec((K, tt), tok),
        out_shape=jax.ShapeDtypeStruct((K, T), I32),
        compiler_params=_cparams(("parallel",)),
        name="moe_dest",
    )(eidx, rank, start_pad_b)


def _dispatch_kernel(zb_ref, d_ref, x_ref, wsg_ref, wsu_ref, wsd_ref, xs_ref, sh_ref, zero_sc, sem, *, tt, n_exp):
    blk_rows = ROW_BLOCK * PACK_SUB

    @pl.when(pl.program_id(0) == 0)
    def _():
        zero_sc[...] = jnp.zeros(zero_sc.shape, PACKED)

        def zstart(e, c):
            @pl.when(zb_ref[e] >= 0)
            def _():
                dst = xs_ref.at[pl.ds(pl.multiple_of(zb_ref[e] * PACK_SUB, blk_rows), blk_rows), :]
                pltpu.make_async_copy(zero_sc, dst, sem).start()
            return c

        def zwait(e, c):
            @pl.when(zb_ref[e] >= 0)
            def _():
                pltpu.make_async_copy(zero_sc, xs_ref.at[pl.ds(0, blk_rows), :], sem).wait()
            return c

        lax.fori_loop(0, n_exp, zstart, 0)
        lax.fori_loop(0, n_exp, zwait, 0)

    def start(j, c):
        src = _packed_row(x_ref, j)
        for kk in range(TOP_K):
            pltpu.make_async_copy(src, _packed_row(xs_ref, d_ref[kk, j]), sem).start(priority=kk % 2)
        return c

    lax.fori_loop(0, tt, start, 0)
    xb = _unpack_rows(x_ref, tt)
    hs = _silu(jnp.dot(xb, wsg_ref[...], preferred_element_type=F32)) * jnp.dot(
        xb, wsu_ref[...], preferred_element_type=F32)
    sh_ref[...] = jnp.dot(hs.astype(BF16), wsd_ref[...], preferred_element_type=F32)
    for kk in range(TOP_K):
        pltpu.make_async_copy(x_ref, xs_ref.at[pl.ds(0, tt * PACK_SUB), :], sem).wait()


def _dispatch(zero_blk, dest, xpk, wsg, wsu, wsd, n_rows, tt=1024):
    T = xpk.shape[0] // PACK_SUB
    D = wsg.shape[0]
    n_exp = zero_blk.shape[0]
    fixed = lambda i, zb: (0, 0)
    grid_spec = pltpu.PrefetchScalarGridSpec(
        num_scalar_prefetch=1,
        grid=(T // tt,),
        in_specs=[pl.BlockSpec((TOP_K, tt), lambda i, zb: (0, i), memory_space=pltpu.SMEM),
                  pl.BlockSpec((tt * PACK_SUB, LANES), lambda i, zb: (i, 0)),
                  pl.BlockSpec(wsg.shape, fixed), pl.BlockSpec(wsu.shape, fixed), pl.BlockSpec(wsd.shape, fixed)],
        out_specs=[pl.BlockSpec(memory_space=pl.ANY), pl.BlockSpec((tt, D), lambda i, zb: (i, 0))],
        scratch_shapes=[pltpu.VMEM((ROW_BLOCK * PACK_SUB, LANES), PACKED), pltpu.SemaphoreType.DMA],
    )
    return pl.pallas_call(
        functools.partial(_dispatch_kernel, tt=tt, n_exp=n_exp),
        grid_spec=grid_spec,
        out_shape=[jax.ShapeDtypeStruct((n_rows * PACK_SUB, LANES), PACKED), jax.ShapeDtypeStruct((T, D), F32)],
        compiler_params=_cparams(("arbitrary",)),
        name="moe_dispatch",
    )(zero_blk, dest, xpk, wsg, wsu, wsd)


def _expert_kernel(sb_ref, nb_ref, nu_ref, wg_ref, wu_ref, wd_ref, xs_ref, y_ref,
                   xbuf, ybuf, wg_sc, wu_sc, wd_sc, isem, osem):
    e = pl.program_id(0)
    nb = nb_ref[e]
    base = sb_ref[e]
    n_used = nu_ref[0]
    in_rows = out_rows = ROW_BLOCK * PACK_SUB

    def fetch(blk, slot):
        src = xs_ref.at[pl.ds(pl.multiple_of(blk * in_rows, in_rows), in_rows), :]
        pltpu.make_async_copy(src, xbuf.at[slot], isem.at[slot]).start()

    def wait_in(slot):
        pltpu.make_async_copy(xs_ref.at[pl.ds(0, in_rows), :], xbuf.at[slot], isem.at[slot]).wait()

    def store(blk, slot):
        dst = y_ref.at[pl.ds(pl.multiple_of(blk * out_rows, out_rows), out_rows), :]
        pltpu.make_async_copy(ybuf.at[slot], dst, osem.at[slot]).start()

    def wait_out(slot):
        pltpu.make_async_copy(ybuf.at[slot], y_ref.at[pl.ds(0, out_rows), :], osem.at[slot]).wait()

    @pl.when(nb > 0)
    def _():
        @pl.when(base == 0)
        def _():
            for g0 in range(EXPERT_AHEAD):
                @pl.when(g0 < n_used)
                def _():
                    fetch(g0, g0 % EXPERT_BUFS)

        wg_sc[...] = wg_ref[0].astype(BF16)
        wu_sc[...] = wu_ref[0].astype(BF16)
        wd_sc[...] = wd_ref[0].astype(BF16)

        def run(blocks):
            slots = [g % EXPERT_BUFS for g in blocks]
            for g, slot in zip(blocks, slots):
                wait_in(slot)

                @pl.when(g + EXPERT_AHEAD < n_used)
                def _():
                    fetch(g + EXPERT_AHEAD, (g + EXPERT_AHEAD) % EXPERT_BUFS)

                @pl.when(g >= EXPERT_BUFS)
                def _():
                    wait_out(slot)

            for g, slot in zip(blocks, slots):
                xb = _unpack_rows(xbuf.at[slot], ROW_BLOCK)
                a = jnp.dot(xb, wg_sc[...], preferred_element_type=F32)
                u = jnp.dot(xb, wu_sc[...], preferred_element_type=F32)
                h = (_silu(a) * u).astype(BF16)
                _pack_rows(ybuf.at[slot], jnp.dot(h, wd_sc[...], preferred_element_type=F32))
            for g, slot in zip(blocks, slots):
                store(g, slot)

        def region(p, c):
            run([base + EXPERT_REGION * p + i for i in range(EXPERT_REGION)])
            return c

        lax.fori_loop(0, nb // EXPERT_REGION, region, 0)
        done = nb // EXPERT_REGION * EXPERT_REGION

        @pl.when(nb % EXPERT_REGION >= 2)
        def _():
            run([base + done, base + done + 1])

        @pl.when(nb % 2 == 1)
        def _():
            run([base + nb - 1])

    @pl.when(e == pl.num_programs(0) - 1)
    def _():
        for slot in range(EXPERT_BUFS):
            @pl.when(slot < n_used)
            def _():
                wait_out(slot)


def _experts(start_blk, n_blk_e, n_used, xs, w_gate, w_up, w_down):
    n_rows = xs.shape[0] // PACK_SUB
    E, D, H = w_gate.shape
    wsel = lambda e, sb, nb, nu: (e, 0, 0)
    grid_spec = pltpu.PrefetchScalarGridSpec(
        num_scalar_prefetch=3,
        grid=(E,),
        in_specs=[pl.BlockSpec((1, D, H), wsel), pl.BlockSpec((1, D, H), wsel), pl.BlockSpec((1, H, D), wsel),
                  pl.BlockSpec(memory_space=pl.ANY)],
        out_specs=pl.BlockSpec(memory_space=pl.ANY),
        scratch_shapes=[pltpu.VMEM((EXPERT_BUFS, ROW_BLOCK * PACK_SUB, LANES), PACKED),
                        pltpu.VMEM((EXPERT_BUFS, ROW_BLOCK * PACK_SUB, LANES), PACKED),
                        pltpu.VMEM((D, H), BF16), pltpu.VMEM((D, H), BF16), pltpu.VMEM((H, D), BF16),
                        pltpu.SemaphoreType.DMA((EXPERT_BUFS,)), pltpu.SemaphoreType.DMA((EXPERT_BUFS,))],
    )
    return pl.pallas_call(
        _expert_kernel,
        grid_spec=grid_spec,
        out_shape=jax.ShapeDtypeStruct((n_rows * PACK_SUB, LANES), PACKED),
        compiler_params=_cparams(("arbitrary",)),
        name="moe_experts",
    )(start_blk, n_blk_e, n_used, w_gate, w_up, w_down, xs)


def _combine_kernel(dcur_ref, dnxt_ref, x_ref, gate_ref, sh_ref, g_ref, b_ref, y_ref, o_ref, buf, sem, *, tt, alpha):
    i = pl.program_id(0)
    n = pl.num_programs(0)

    def fetch(d_ref, slot):
        def body(j, c):
            for kk in range(TOP_K):
                src = _packed_row(y_ref, d_ref[kk * tt + j])
                pltpu.make_async_copy(src, _packed_row(buf.at[slot, kk], j), sem.at[slot]).start(priority=kk % 2)
            return c

        lax.fori_loop(0, tt, body, 0)

    def reduce(slot):
        for kk in range(TOP_K):
            pltpu.make_async_copy(y_ref.at[pl.ds(0, tt * PACK_SUB), :], buf.at[slot, kk], sem.at[slot]).wait()
        x = x_ref[...]
        moe = sh_ref[...]
        gates = gate_ref[...]
        for kk in range(TOP_K):
            moe = moe + gates[:, kk:kk + 1] * _unpack_rows(buf.at[slot, kk], tt, F32)
        o_ref[...] = _layer_norm(alpha * x + moe, g_ref[...], b_ref[...])

    @pl.when(i == 0)
    def _():
        fetch(dcur_ref, 0)

    for slot in range(2):
        @pl.when((i % 2 == slot) & (i + 1 < n))
        def _():
            fetch(dnxt_ref, 1 - slot)

    for slot in range(2):
        @pl.when(i % 2 == slot)
        def _():
            reduce(slot)


def _combine(dest, x1, gates, shared, g, b, y, alpha, tt=256):
    T, D = x1.shape
    n = T // tt
    dest = dest.reshape(TOP_K, n, tt).transpose(1, 0, 2).reshape(n * TOP_K * tt)
    row = lambda i: (i, 0)
    fixed = lambda i: (0, 0)
    return pl.pallas_call(
        functools.partial(_combine_kernel, tt=tt, alpha=alpha),
        grid=(n,),
        in_specs=[pl.BlockSpec((TOP_K * tt,), lambda i: (i,), memory_space=pltpu.SMEM),
                  pl.BlockSpec((TOP_K * tt,), lambda i: (jnp.minimum(i + 1, n - 1),), memory_space=pltpu.SMEM),
                  pl.BlockSpec((tt, D), row), pl.BlockSpec((tt, TOP_K), row),
                  pl.BlockSpec((tt, D), row),
                  pl.BlockSpec((1, D), fixed), pl.BlockSpec((1, D), fixed),
                  pl.BlockSpec(memory_space=pl.ANY)],
        out_specs=pl.BlockSpec((tt, D), row),
        out_shape=jax.ShapeDtypeStruct((T, D), F32),
        scratch_shapes=[pltpu.VMEM((2, TOP_K, tt * PACK_SUB, LANES), PACKED), pltpu.SemaphoreType.DMA((2,))],
        compiler_params=_cparams(("arbitrary",)),
        name="moe_combine_ln2",
    )(dest, dest, x1, gates, shared, g, b, y)


def _layer(x2, mem, B, S, depth, w_in, b_f, w_pool, pool_scale, w_mem_kv, w_out, ln1_g, ln1_b,
           w_router, router_bias, w_gate, w_up, w_down, ws_gate, ws_up, ws_down, ln2_g, ln2_b):
    T, D = x2.shape
    n_win, grp = w_pool.shape[0], w_pool.shape[1]
    pw = n_win * grp
    fw = FOX_HEADS * HEAD_DIM
    mw = w_mem_kv.shape[1] // 2
    E = w_router.shape[1]
    alpha = (2 * depth) ** 0.25

    f_lo = pw + 3 * fw
    w_main = jnp.concatenate([w_in[:, :f_lo], w_in[:, f_lo + FOX_HEADS:]], axis=1).astype(BF16)
    w_f = jnp.pad(w_in[:, f_lo:f_lo + FOX_HEADS], ((0, 0), (0, LANES - FOX_HEADS))).astype(BF16)
    bf_pad = jnp.pad(b_f, (0, LANES - FOX_HEADS)).reshape(1, LANES)
    wbd = jnp.zeros((pw, pw), F32)
    for g in range(n_win):
        wbd = wbd.at[g * grp:(g + 1) * grp, g * grp:(g + 1) * grp].set(w_pool[g])
    wrt = w_router.T
    wrt_hi = wrt.astype(BF16)
    wrt_lo = (wrt - wrt_hi.astype(F32)).astype(BF16)

    u_pool, q, k, v, q_mem, f_logit = _inproj(x2, w_main, w_f, pw, fw, mw)
    y_pool = _pool(u_pool, wbd.astype(BF16), pool_scale.reshape(1, pw), B, S)
    fcum = _fgate(f_logit, bf_pad, B, S)
    y_fox = _fox(q, k, v, fcum, B, S)
    y_mem = _memattn(q_mem, mem, w_mem_kv.astype(BF16), B, S)
    x1, x1_packed = _outproj(y_pool, y_fox, y_mem, x2, w_out.astype(BF16), ln1_g.reshape(1, D),
                             ln1_b.reshape(1, D), alpha)

    eidx, gates, rank, counts = _router(x1, wrt_hi, wrt_lo, _per_expert_column(router_bias))

    cnt = counts[:, 0].astype(I32)
    padded = (cnt + ROW_BLOCK - 1) // ROW_BLOCK * ROW_BLOCK
    pad_end = jnp.cumsum(padded)
    start_pad = (pad_end - padded).astype(I32)
    n_rows = (T * TOP_K + E * (ROW_BLOCK - 1)) // ROW_BLOCK * ROW_BLOCK
    n_used = (pad_end[-1] // ROW_BLOCK).astype(I32)
    zero_blk = jnp.where(padded > 0, pad_end - ROW_BLOCK, -1).astype(I32)
    dest = _dest(eidx, rank, _per_expert_column(start_pad))

    xs, shared = _dispatch(zero_blk, dest, x1_packed, ws_gate.astype(BF16), ws_up.astype(BF16),
                           ws_down.astype(BF16), n_rows)
    y = _experts(start_pad // ROW_BLOCK, (padded // ROW_BLOCK).astype(I32), n_used.reshape(1), xs,
                 w_gate, w_up, w_down)
    return _combine(dest, x1, gates.T, shared, ln2_g.reshape(1, D), ln2_b.reshape(1, D), y, alpha)


def kernel(x, mem, w_in, b_f, w_pool, pool_scale, w_mem_kv, w_out, ln1_g, ln1_b, w_router, router_bias,
           w_gate, w_up, w_down, ws_gate, ws_up, ws_down, ln2_g, ln2_b):
    B, S, D = x.shape
    depth = w_in.shape[0]
    x2 = x.reshape(B * S, D)
    for l in range(depth):
        x2 = _layer(x2, mem, B, S, depth, w_in[l], b_f[l], w_pool[l], pool_scale[l], w_mem_kv[l], w_out[l],
                    ln1_g[l], ln1_b[l], w_router[l], router_bias[l], w_gate[l], w_up[l], w_down[l],
                    ws_gate[l], ws_up[l], ws_down[l], ln2_g[l], ln2_b[l])
    return x2.reshape(B, S, D)
```

```python
import functools
import math

import jax
import jax.numpy as jnp
from jax import lax
from jax.experimental import pallas as pl
from jax.experimental.pallas import tpu as pltpu

F32 = jnp.float32
BF16 = jnp.bfloat16
I32 = jnp.int32
PACKED = jnp.int32

LANES = 128
POOL_WINDOWS = (2, 4, 8, 16)
POOL_HALO = 16
HEAD_DIM = 64
FOX_HEADS = 8
BIAS_TERMS = 3
ATTN_QBLOCKS = 4
MEM_HEADS = 4
TOP_K = 8
N_GROUPS = 8
TOPK_GROUPS = 4
ROUTED_SCALE = 2.5
LN_EPS = 1e-5
LOG2E = 1.4426950408889634
ROW_BLOCK = 256
EXPERT_BUFS = 8
EXPERT_REGION = 2
EXPERT_AHEAD = EXPERT_BUFS - EXPERT_REGION
VMEM_LIMIT = 48 * 1024 * 1024


def _cparams(sem):
    return pltpu.CompilerParams(dimension_semantics=sem, vmem_limit_bytes=VMEM_LIMIT)


def _layer_norm(z, g, b):
    mu = jnp.mean(z, axis=-1, keepdims=True)
    zc = z - mu
    var = jnp.mean(zc * zc, axis=-1, keepdims=True)
    return zc * lax.rsqrt(var + LN_EPS) * g + b


def _silu(x):
    return x * (1.0 / (1.0 + jnp.exp(-x)))


def _inproj_kernel(x_ref, w_ref, wf_ref, up_ref, q_ref, k_ref, v_ref, qm_ref, f_ref, *, pw, fw, mw, scale):
    xb = x_ref[...].astype(BF16)

    def proj(lo, hi):
        return jnp.dot(xb, w_ref[:, lo:hi], preferred_element_type=F32)

    up_ref[...] = proj(0, pw)
    q_ref[...] = (proj(pw, pw + fw) * (scale * LOG2E)).astype(BF16)
    k_ref[...] = proj(pw + fw, pw + 2 * fw).astype(BF16)
    v_ref[...] = proj(pw + 2 * fw, pw + 3 * fw).astype(BF16)
    qm_ref[...] = (proj(pw + 3 * fw, pw + 3 * fw + mw) * scale).astype(BF16)
    f_ref[...] = jnp.dot(xb, wf_ref[...], preferred_element_type=F32)


def _inproj(x2, w_main, w_f, pw, fw, mw, tm=512):
    T, D = x2.shape
    kern = functools.partial(_inproj_kernel, pw=pw, fw=fw, mw=mw, scale=HEAD_DIM ** -0.5)
    row = lambda i: (i, 0)
    fixed = lambda i: (0, 0)
    return pl.pallas_call(
        kern,
        grid=(T // tm,),
        in_specs=[pl.BlockSpec((tm, D), row),
                  pl.BlockSpec(w_main.shape, fixed),
                  pl.BlockSpec(w_f.shape, fixed)],
        out_specs=[pl.BlockSpec((tm, pw), row), pl.BlockSpec((tm, fw), row), pl.BlockSpec((tm, fw), row),
                   pl.BlockSpec((tm, fw), row), pl.BlockSpec((tm, mw), row), pl.BlockSpec((tm, LANES), row)],
        out_shape=[jax.ShapeDtypeStruct((T, pw), F32), jax.ShapeDtypeStruct((T, fw), BF16),
                   jax.ShapeDtypeStruct((T, fw), BF16), jax.ShapeDtypeStruct((T, fw), BF16),
                   jax.ShapeDtypeStruct((T, mw), BF16), jax.ShapeDtypeStruct((T, LANES), F32)],
        compiler_params=_cparams(("parallel",)),
        name="inproj",
    )(x2, w_main, w_f)


def _pool_kernel(u_ref, wbd_ref, sc_ref, o_ref, ext_sc, *, chunk, group):
    S, W = u_ref.shape
    ext_sc[0:POOL_HALO, :] = jnp.zeros((POOL_HALO, W), F32)
    ext_sc[POOL_HALO:, :] = u_ref[...]
    rows = chunk + POOL_HALO
    lrow = lax.broadcasted_iota(I32, (rows, W), 0)
    lane = lax.broadcasted_iota(I32, (rows, W), 1)

    def body(c, carry):
        start = pl.multiple_of(c * chunk, chunk)
        e = ext_sc[pl.ds(start, rows), :]
        posf = (lrow + (start - POOL_HALO + 1)).astype(F32)
        acc = e
        d = jnp.zeros_like(e)
        shift = 1
        for g, w in enumerate(POOL_WINDOWS):
            while shift < w:
                acc = acc + pltpu.roll(acc, shift, axis=0)
                shift *= 2
            mean = acc / jnp.minimum(posf, float(w))
            d = jnp.where((lane >= g * group) & (lane < (g + 1) * group), mean, d)
        d = (d - e)[POOL_HALO:, :]
        y = jnp.dot(d.astype(BF16), wbd_ref[...], preferred_element_type=F32) * sc_ref[...]
        o_ref[pl.ds(start, chunk), :] = y.astype(BF16)
        return carry

    lax.fori_loop(0, S // chunk, body, 0)


def _pool(u, wbd, pscale, B, S, chunk=512):
    T, W = u.shape
    kern = functools.partial(_pool_kernel, chunk=chunk, group=W // len(POOL_WINDOWS))
    return pl.pallas_call(
        kern,
        grid=(B,),
        in_specs=[pl.BlockSpec((S, W), lambda b: (b, 0)),
                  pl.BlockSpec((W, W), lambda b: (0, 0)),
                  pl.BlockSpec((1, W), lambda b: (0, 0))],
        out_specs=pl.BlockSpec((S, W), lambda b: (b, 0)),
        out_shape=jax.ShapeDtypeStruct((T, W), BF16),
        scratch_shapes=[pltpu.VMEM((S + POOL_HALO, W), F32)],
        compiler_params=_cparams(("parallel",)),
        name="pool_mixer",
    )(u, wbd, pscale)


def _fgate_kernel(f_ref, bf_ref, o_ref):
    S = f_ref.shape[0]
    z = f_ref[...] + bf_ref[...]
    x = jnp.minimum(z, 0.0) - jnp.log(1.0 + jnp.exp(-jnp.abs(z)))
    row = lax.broadcasted_iota(I32, x.shape, 0)
    lane = lax.broadcasted_iota(I32, x.shape, 1)
    shift = 1
    while shift < S:
        x = x + jnp.where(row >= shift, pltpu.roll(x, shift, axis=0), 0.0)
        shift *= 2
    x = x * LOG2E
    hi = x.astype(BF16).astype(F32)
    mid = (x - hi).astype(BF16).astype(F32)
    lo = (x - hi) - mid
    for p in range(FOX_HEADS // 2):
        out = jnp.zeros(x.shape, F32)
        for hh in range(2):
            c = 2 * p + hh
            for j, part in enumerate((hi, mid, lo)):
                out = jnp.where(lane == BIAS_TERMS * hh + j, -part[:, c:c + 1], out)
        o_ref[0, p] = out.astype(BF16)


def _fgate(f_logit, bf_pad, B, S):
    return pl.pallas_call(
        _fgate_kernel,
        grid=(B,),
        in_specs=[pl.BlockSpec((S, LANES), lambda b: (b, 0)),
                  pl.BlockSpec((1, LANES), lambda b: (0, 0))],
        out_specs=pl.BlockSpec((1, FOX_HEADS // 2, S, LANES), lambda b: (b, 0, 0, 0)),
        out_shape=jax.ShapeDtypeStruct((B, FOX_HEADS // 2, S, LANES), BF16),
        compiler_params=_cparams(("parallel",)),
        name="forget_cumsum",
    )(f_logit, bf_pad)


def _fox_kernel(q_ref, k_ref, a_ref, v_ref, o_ref, vt_sc, sa_sc, sb_sc,
                qa0_sc, m0_sc, l0_sc, acc0_sc, qa1_sc, m1_sc, l1_sc, acc1_sc, *, blk):
    j = pl.program_id(2)
    nk = vt_sc.shape[0]
    nt = (((1,), (1,)), ((), ()))
    sets = ((qa0_sc, m0_sc, l0_sc, acc0_sc), (qa1_sc, m1_sc, l1_sc, acc1_sc))

    @pl.when(j == 0)
    def _():
        for c in range(nk):
            vt_sc[c] = v_ref[c * blk:(c + 1) * blk, :].astype(F32).T.astype(BF16)

    def prepare(which, i):
        qa_sc, m_sc, l_sc, acc_sc = sets[which]
        q = q_ref[i * blk:(i + 1) * blk, :].astype(F32)
        lane = lax.broadcasted_iota(I32, q.shape, 1)
        for h in range(2):
            mine = (lane >= h * HEAD_DIM) & (lane < (h + 1) * HEAD_DIM)
            bias_rows = (lane >= h * BIAS_TERMS) & (lane < (h + 1) * BIAS_TERMS)
            qa_sc[h] = jnp.concatenate(
                [jnp.where(mine, q, 0.0), jnp.where(bias_rows, 1.0, 0.0)], axis=1).astype(BF16)
        m_sc[...] = jnp.full(m_sc.shape, -jnp.inf, F32)
        l_sc[...] = jnp.zeros(l_sc.shape, F32)
        acc_sc[...] = jnp.zeros(acc_sc.shape, F32)

    def scores(ki, dst, which):
        qa_sc = sets[which][0]
        ks = pl.multiple_of(ki * blk, blk)
        kb = jnp.concatenate([k_ref[pl.ds(ks, blk), :], a_ref[0, 0, pl.ds(ks, blk), :]], axis=1)
        for h in range(2):
            dst[h] = lax.dot_general(kb, qa_sc[h], nt, preferred_element_type=F32)

    def absorb(ki, src, causal, which):
        _, m_sc, l_sc, acc_sc = sets[which]
        for h in range(2):
            st = src[h]
            if causal:
                kpos = lax.broadcasted_iota(I32, st.shape, 0)
                qpos = lax.broadcasted_iota(I32, st.shape, 1)
                st = jnp.where(kpos <= qpos, st, -jnp.inf)
            m_prev = m_sc[h]
            m_new = jnp.maximum(m_prev, jnp.max(st, axis=0, keepdims=True))
            alpha = jnp.exp2(m_prev - m_new)
            p = jnp.exp2(st - m_new)
            l_sc[h] = alpha * l_sc[h] + jnp.sum(p, axis=0, keepdims=True)
            m_sc[h] = m_new
            vt = vt_sc[ki, h * HEAD_DIM:(h + 1) * HEAD_DIM, :]
            pv = jnp.dot(vt, p.astype(BF16), preferred_element_type=F32)
            rows = slice(h * HEAD_DIM, (h + 1) * HEAD_DIM)
            acc_sc[rows, :] = acc_sc[rows, :] * alpha + pv

    def finish(which, i):
        _, _, l_sc, acc_sc = sets[which]
        out_t = jnp.concatenate([acc_sc[h * HEAD_DIM:(h + 1) * HEAD_DIM, :] / l_sc[h] for h in range(2)], axis=0)
        o_ref[i * blk:(i + 1) * blk, :] = out_t.T.astype(BF16)

    def walk(first, other, which, n_pairs):
        def body(p, carry):
            scores(2 * p + 1, other, which)
            absorb(2 * p, first, False, which)
            scores(2 * p + 2, first, which)
            absorb(2 * p + 1, other, False, which)
            return carry

        lax.fori_loop(0, n_pairs, body, 0)

    per_step = q_ref.shape[0] // blk
    first, other = sa_sc, sb_sc
    tail = None
    for i in range(per_step):
        which = i % 2
        qi = per_step * j + i
        prepare(which, i)
        scores(0, first, which)
        if tail is not None:
            tail()
        walk(first, other, which, (per_step // 2) * j + i // 2)
        if i % 2 == 0:
            tail = functools.partial(lambda q, buf, w, ii: (absorb(q, buf, True, w), finish(w, ii)),
                                     qi, first, which, i)
            first, other = other, first
        else:
            scores(qi, other, which)
            absorb(qi - 1, first, False, which)
            tail = functools.partial(lambda q, buf, w, ii: (absorb(q, buf, True, w), finish(w, ii)),
                                     qi, other, which, i)
    tail()


def _fox(q, k, v, fbias, B, S, blk=512):
    T, FW = q.shape
    nq = S // blk
    pairs = FW // LANES
    per = math.gcd(ATTN_QBLOCKS, nq)
    assert per % 2 == 0
    kern = functools.partial(_fox_kernel, blk=blk)
    return pl.pallas_call(
        kern,
        grid=(B, pairs, nq // per),
        in_specs=[pl.BlockSpec((per * blk, LANES), lambda b, p, i: (b * (nq // per) + i, p)),
                  pl.BlockSpec((S, LANES), lambda b, p, i: (b, p)),
                  pl.BlockSpec((1, 1, S, LANES), lambda b, p, i: (b, p, 0, 0)),
                  pl.BlockSpec((S, LANES), lambda b, p, i: (b, p))],
        out_specs=pl.BlockSpec((per * blk, LANES), lambda b, p, i: (b * (nq // per) + i, p)),
        out_shape=jax.ShapeDtypeStruct((T, FW), BF16),
        scratch_shapes=[pltpu.VMEM((nq, LANES, blk), BF16),
                        pltpu.VMEM((2, blk, blk), F32), pltpu.VMEM((2, blk, blk), F32)] + 2 * [
                            pltpu.VMEM((2, blk, 2 * LANES), BF16), pltpu.VMEM((2, 1, blk), F32),
                            pltpu.VMEM((2, 1, blk), F32), pltpu.VMEM((LANES, blk), F32)],
        compiler_params=_cparams(("parallel", "parallel", "arbitrary")),
        name="fox_attention",
    )(q, k, fbias, v)


def _memattn_kernel(qm_ref, mem_ref, wkv_ref, o_ref, k_sc, v_sc):
    MW = qm_ref.shape[1]

    @pl.when(pl.program_id(1) == 0)
    def _():
        kv = jnp.dot(mem_ref[0].astype(BF16), wkv_ref[...], preferred_element_type=F32)
        k_sc[...] = kv[:, :MW].astype(BF16)
        v_sc[...] = kv[:, MW:].astype(BF16)

    q = qm_ref[...]
    lane = lax.broadcasted_iota(I32, q.shape, 1)
    hd = MW // MEM_HEADS
    out = jnp.zeros(q.shape, F32)
    for h in range(MEM_HEADS):
        mine = (lane >= h * hd) & (lane < (h + 1) * hd)
        qh = jnp.where(mine, q, jnp.zeros_like(q))
        s = lax.dot_general(qh, k_sc[...], (((1,), (1,)), ((), ())), preferred_element_type=F32)
        p = jnp.exp(s - jnp.max(s, axis=1, keepdims=True))
        l = jnp.sum(p, axis=1, keepdims=True)
        o = jnp.dot(p.astype(BF16), v_sc[...], preferred_element_type=F32)
        out = jnp.where(mine, o / l, out)
    o_ref[...] = out.astype(BF16)


def _memattn(qm, mem, wkv, B, S, tq=512):
    T, MW = qm.shape
    M, D = mem.shape[1], mem.shape[2]
    nq = S // tq
    return pl.pallas_call(
        _memattn_kernel,
        grid=(B, nq),
        in_specs=[pl.BlockSpec((tq, MW), lambda b, i: (b * nq + i, 0)),
                  pl.BlockSpec((1, M, D), lambda b, i: (b, 0, 0)),
                  pl.BlockSpec(wkv.shape, lambda b, i: (0, 0))],
        out_specs=pl.BlockSpec((tq, MW), lambda b, i: (b * nq + i, 0)),
        out_shape=jax.ShapeDtypeStruct((T, MW), BF16),
        scratch_shapes=[pltpu.VMEM((M, MW), BF16), pltpu.VMEM((M, MW), BF16)],
        compiler_params=_cparams(("parallel", "arbitrary")),
        name="memory_attention",
    )(qm, mem, wkv)


PACK_SUB = 4


def _pack_rows(ref, val):
    n = val.shape[0]
    half = val.shape[1] // 2
    words = pltpu.pack_elementwise([val[:, :half], val[:, half:]], packed_dtype=BF16)
    for j in range(PACK_SUB):
        ref[pl.ds(j, n, stride=PACK_SUB), :] = words[:, j * LANES:(j + 1) * LANES]


def _unpack_words(slabs):
    lo = [pltpu.unpack_elementwise(s, index=0, packed_dtype=BF16, unpacked_dtype=F32) for s in slabs]
    hi = [pltpu.unpack_elementwise(s, index=1, packed_dtype=BF16, unpacked_dtype=F32) for s in slabs]
    return jnp.concatenate(lo + hi, axis=1)


def _unpack_rows(ref, n, dtype=BF16):
    return _unpack_words([ref[pl.ds(j, n, stride=PACK_SUB), :] for j in range(PACK_SUB)]).astype(dtype)


def _packed_row(ref, r):
    return ref.at[pl.ds(pl.multiple_of(r * PACK_SUB, PACK_SUB), PACK_SUB), :]


def _outproj_kernel(yp_ref, yf_ref, ym_ref, x_ref, w_ref, g_ref, b_ref, o_ref, pk_ref, *, alpha):
    pw, fw = yp_ref.shape[1], yf_ref.shape[1]
    h = jnp.dot(yp_ref[...], w_ref[0:pw, :], preferred_element_type=F32)
    h = h + jnp.dot(yf_ref[...], w_ref[pw:pw + fw, :], preferred_element_type=F32)
    h = h + jnp.dot(ym_ref[...], w_ref[pw + fw:, :], preferred_element_type=F32)
    y = _layer_norm(alpha * x_ref[...] + h, g_ref[...], b_ref[...])
    o_ref[...] = y
    _pack_rows(pk_ref, y)


def _outproj(yp, yf, ym, x2, w_out, g, b, alpha, tm=512):
    T, D = x2.shape
    assert D == 2 * PACK_SUB * LANES
    row = lambda i: (i, 0)
    fixed = lambda i: (0, 0)
    return pl.pallas_call(
        functools.partial(_outproj_kernel, alpha=alpha),
        grid=(T // tm,),
        in_specs=[pl.BlockSpec((tm, yp.shape[1]), row), pl.BlockSpec((tm, yf.shape[1]), row),
                  pl.BlockSpec((tm, ym.shape[1]), row), pl.BlockSpec((tm, D), row),
                  pl.BlockSpec(w_out.shape, fixed), pl.BlockSpec((1, D), fixed), pl.BlockSpec((1, D), fixed)],
        out_specs=[pl.BlockSpec((tm, D), row),
                   pl.BlockSpec((tm * PACK_SUB, LANES), row)],
        out_shape=[jax.ShapeDtypeStruct((T, D), F32),
                   jax.ShapeDtypeStruct((T * PACK_SUB, LANES), PACKED)],
        compiler_params=_cparams(("parallel",)),
        name="outproj_ln1",
    )(yp, yf, ym, x2, w_out, g, b)


ROUTE_TILE = 512


def _per_expert_column(v, dtype=F32):
    return jnp.broadcast_to(v.astype(dtype)[:, None], (v.shape[0], ROUTE_TILE))


def _router_kernel(x_ref, wh_ref, wl_ref, bias_ref, eidx_ref, gate_ref, rank_ref, cnt_ref, carry_sc, *, tt):
    E = wh_ref.shape[0]
    gsz = E // N_GROUPS
    ninf = -jnp.inf

    @pl.when(pl.program_id(0) == 0)
    def _():
        carry_sc[...] = jnp.zeros(carry_sc.shape, F32)

    x = x_ref[...]
    xh = x.astype(BF16)
    xl = (x - xh.astype(F32)).astype(BF16)
    wh = wh_ref[...]
    nt = (((1,), (1,)), ((), ()))
    logits = lax.dot_general(wh, xh, nt, preferred_element_type=F32) + (
        lax.dot_general(wl_ref[...], xh, nt, preferred_element_type=F32)
        + lax.dot_general(wh, xl, nt, preferred_element_type=F32))
    scores = 1.0 / (1.0 + jnp.exp(-logits))
    biased = scores + bias_ref[...]
    eiota = lax.broadcasted_iota(I32, (E, tt), 0).astype(F32)
    giota = lax.broadcasted_iota(I32, (gsz, tt), 0).astype(F32)

    def cmax(a):
        return jnp.max(a, axis=0, keepdims=True)

    def first_at(a, m, iota, n):
        return jnp.min(jnp.where(a == m, iota, float(n)), axis=0, keepdims=True)

    groups = [biased[g * gsz:(g + 1) * gsz, :] for g in range(N_GROUPS)]
    gscore = []
    for blk in groups:
        m1 = cmax(blk)
        m2 = cmax(jnp.where(giota == first_at(blk, m1, giota, gsz), ninf, blk))
        gscore.append(m1 + m2)

    kept = []
    for g in range(N_GROUPS):
        ahead = jnp.zeros((1, tt), F32)
        for o in range(N_GROUPS):
            if o == g:
                continue
            beats = gscore[o] > gscore[g]
            if o < g:
                beats = beats | (gscore[o] == gscore[g])
            ahead = ahead + jnp.where(beats, 1.0, 0.0)
        kept.append(groups[g] + jnp.where(ahead < float(TOPK_GROUPS), 0.0, ninf))
    masked = jnp.concatenate(kept, axis=0)

    picks, sels = [], []
    chosen = jnp.zeros((E, tt), F32)
    for _ in range(TOP_K):
        ik = first_at(masked, cmax(masked), eiota, E)
        oh = eiota == ik
        picks.append(ik)
        sels.append(jnp.sum(jnp.where(oh, scores, 0.0), axis=0, keepdims=True))
        chosen = jnp.where(oh, 1.0, chosen)
        masked = jnp.where(oh, ninf, masked)
    denom = sels[0]
    for sk in sels[1:]:
        denom = denom + sk

    r = lax.broadcasted_iota(I32, (tt, tt), 0)
    c = lax.broadcasted_iota(I32, (tt, tt), 1)
    earlier = jnp.where(r < c, 1.0, 0.0).astype(BF16)
    chosen_b = chosen.astype(BF16)
    pos = carry_sc[...] + jnp.dot(chosen_b, earlier, preferred_element_type=F32)
    carry_sc[...] = carry_sc[...] + jnp.dot(chosen_b, jnp.ones((tt, tt), BF16), preferred_element_type=F32)
    cnt_ref[...] = carry_sc[...]

    ranks = [jnp.sum(jnp.where(eiota == ik, pos, 0.0), axis=0, keepdims=True) for ik in picks]
    eidx_ref[...] = jnp.concatenate(picks, axis=0).astype(I32)
    gate_ref[...] = jnp.concatenate([sk / denom * ROUTED_SCALE for sk in sels], axis=0)
    rank_ref[...] = jnp.concatenate(ranks, axis=0).astype(I32)


def _router(x1, wrt_hi, wrt_lo, rbias):
    T = x1.shape[0]
    E, D = wrt_hi.shape
    tt = ROUTE_TILE
    tok = lambda i: (0, i)
    fixed = lambda i: (0, 0)
    return pl.pallas_call(
        functools.partial(_router_kernel, tt=tt),
        grid=(T // tt,),
        in_specs=[pl.BlockSpec((tt, D), lambda i: (i, 0)), pl.BlockSpec((E, D), fixed),
                  pl.BlockSpec((E, D), fixed), pl.BlockSpec((E, tt), fixed)],
        out_specs=[pl.BlockSpec((TOP_K, tt), tok), pl.BlockSpec((TOP_K, tt), tok), pl.BlockSpec((TOP_K, tt), tok),
                   pl.BlockSpec((E, tt), fixed)],
        out_shape=[jax.ShapeDtypeStruct((TOP_K, T), I32), jax.ShapeDtypeStruct((TOP_K, T), F32),
                   jax.ShapeDtypeStruct((TOP_K, T), I32), jax.ShapeDtypeStruct((E, tt), F32)],
        scratch_shapes=[pltpu.VMEM((E, tt), F32)],
        compiler_params=_cparams(("arbitrary",)),
        name="router_topk",
    )(x1, wrt_hi, wrt_lo, rbias)


def _dest_kernel(e_ref, r_ref, sp_ref, d_ref):
    E = sp_ref.shape[0]
    tt = e_ref.shape[1]
    eiota = lax.broadcasted_iota(I32, (E, tt), 0)
    sp = sp_ref[...]
    rows = []
    for kk in range(TOP_K):
        start = jnp.sum(jnp.where(eiota == e_ref[kk:kk + 1, :], sp, 0.0), axis=0, keepdims=True)
        rows.append(start.astype(I32) + r_ref[kk:kk + 1, :])
    d_ref[...] = jnp.concatenate(rows, axis=0)


def _dest(eidx, rank, start_pad_b):
    K, T = eidx.shape
    tt = ROUTE_TILE
    tok = lambda i: (0, i)
    return pl.pallas_call(
        _dest_kernel,
        grid=(T // tt,),
        in_specs=[pl.BlockSpec((K, tt), tok), pl.BlockSpec((K, tt), tok),
                  pl.BlockSpec(start_pad_b.shape, lambda i: (0, 0))],
        out_specs=pl.BlockSpec((K, tt), tok),
        out_shape=jax.ShapeDtypeStruct((K, T), I32),
        compiler_params=_cparams(("parallel",)),
        name="moe_dest",
    )(eidx, rank, start_pad_b)


def _dispatch_kernel(zb_ref, d_ref, x_ref, wsg_ref, wsu_ref, wsd_ref, xs_ref, sh_ref, zero_sc, sem, *, tt, n_exp):
    blk_rows = ROW_BLOCK * PACK_SUB

    @pl.when(pl.program_id(0) == 0)
    def _():
        zero_sc[...] = jnp.zeros(zero_sc.shape, PACKED)

        def zstart(e, c):
            @pl.when(zb_ref[e] >= 0)
            def _():
                dst = xs_ref.at[pl.ds(pl.multiple_of(zb_ref[e] * PACK_SUB, blk_rows), blk_rows), :]
                pltpu.make_async_copy(zero_sc, dst, sem).start()
            return c

        def zwait(e, c):
            @pl.when(zb_ref[e] >= 0)
            def _():
                pltpu.make_async_copy(zero_sc, xs_ref.at[pl.ds(0, blk_rows), :], sem).wait()
            return c

        lax.fori_loop(0, n_exp, zstart, 0)
        lax.fori_loop(0, n_exp, zwait, 0)

    def start(j, c):
        src = _packed_row(x_ref, j)
        for kk in range(TOP_K):
            pltpu.make_async_copy(src, _packed_row(xs_ref, d_ref[kk, j]), sem).start(priority=kk % 2)
        return c

    lax.fori_loop(0, tt, start, 0)
    xb = _unpack_rows(x_ref, tt)
    hs = _silu(jnp.dot(xb, wsg_ref[...], preferred_element_type=F32)) * jnp.dot(
        xb, wsu_ref[...], preferred_element_type=F32)
    sh_ref[...] = jnp.dot(hs.astype(BF16), wsd_ref[...], preferred_element_type=F32)
    for kk in range(TOP_K):
        pltpu.make_async_copy(x_ref, xs_ref.at[pl.ds(0, tt * PACK_SUB), :], sem).wait()


def _dispatch(zero_blk, dest, xpk, wsg, wsu, wsd, n_rows, tt=1024):
    T = xpk.shape[0] // PACK_SUB
    D = wsg.shape[0]
    n_exp = zero_blk.shape[0]
    fixed = lambda i, zb: (0, 0)
    grid_spec = pltpu.PrefetchScalarGridSpec(
        num_scalar_prefetch=1,
        grid=(T // tt,),
        in_specs=[pl.BlockSpec((TOP_K, tt), lambda i, zb: (0, i), memory_space=pltpu.SMEM),
                  pl.BlockSpec((tt * PACK_SUB, LANES), lambda i, zb: (i, 0)),
                  pl.BlockSpec(wsg.shape, fixed), pl.BlockSpec(wsu.shape, fixed), pl.BlockSpec(wsd.shape, fixed)],
        out_specs=[pl.BlockSpec(memory_space=pl.ANY), pl.BlockSpec((tt, D), lambda i, zb: (i, 0))],
        scratch_shapes=[pltpu.VMEM((ROW_BLOCK * PACK_SUB, LANES), PACKED), pltpu.SemaphoreType.DMA],
    )
    return pl.pallas_call(
        functools.partial(_dispatch_kernel, tt=tt, n_exp=n_exp),
        grid_spec=grid_spec,
        out_shape=[jax.ShapeDtypeStruct((n_rows * PACK_SUB, LANES), PACKED), jax.ShapeDtypeStruct((T, D), F32)],
        compiler_params=_cparams(("arbitrary",)),
        name="moe_dispatch",
    )(zero_blk, dest, xpk, wsg, wsu, wsd)


def _expert_kernel(sb_ref, nb_ref, nu_ref, wg_ref, wu_ref, wd_ref, xs_ref, y_ref,
                   xbuf, ybuf, wg_sc, wu_sc, wd_sc, isem, osem):
    e = pl.program_id(0)
    nb = nb_ref[e]
    base = sb_ref[e]
    n_used = nu_ref[0]
    in_rows = out_rows = ROW_BLOCK * PACK_SUB

    def fetch(blk, slot):
        src = xs_ref.at[pl.ds(pl.multiple_of(blk * in_rows, in_rows), in_rows), :]
        pltpu.make_async_copy(src, xbuf.at[slot], isem.at[slot]).start()

    def wait_in(slot):
        pltpu.make_async_copy(xs_ref.at[pl.ds(0, in_rows), :], xbuf.at[slot], isem.at[slot]).wait()

    def store(blk, slot):
        dst = y_ref.at[pl.ds(pl.multiple_of(blk * out_rows, out_rows), out_rows), :]
        pltpu.make_async_copy(ybuf.at[slot], dst, osem.at[slot]).start()

    def wait_out(slot):
        pltpu.make_async_copy(ybuf.at[slot], y_ref.at[pl.ds(0, out_rows), :], osem.at[slot]).wait()

    @pl.when(nb > 0)
    def _():
        @pl.when(base == 0)
        def _():
            for g0 in range(EXPERT_AHEAD):
                @pl.when(g0 < n_used)
                def _():
                    fetch(g0, g0 % EXPERT_BUFS)

        wg_sc[...] = wg_ref[0].astype(BF16)
        wu_sc[...] = wu_ref[0].astype(BF16)
        wd_sc[...] = wd_ref[0].astype(BF16)

        def run(blocks):
            slots = [g % EXPERT_BUFS for g in blocks]
            for g, slot in zip(blocks, slots):
                wait_in(slot)

                @pl.when(g + EXPERT_AHEAD < n_used)
                def _():
                    fetch(g + EXPERT_AHEAD, (g + EXPERT_AHEAD) % EXPERT_BUFS)

                @pl.when(g >= EXPERT_BUFS)
                def _():
                    wait_out(slot)

            for g, slot in zip(blocks, slots):
                xb = _unpack_rows(xbuf.at[slot], ROW_BLOCK)
                a = jnp.dot(xb, wg_sc[...], preferred_element_type=F32)
                u = jnp.dot(xb, wu_sc[...], preferred_element_type=F32)
                h = (_silu(a) * u).astype(BF16)
                _pack_rows(ybuf.at[slot], jnp.dot(h, wd_sc[...], preferred_element_type=F32))
            for g, slot in zip(blocks, slots):
                store(g, slot)

        def region(p, c):
            run([base + EXPERT_REGION * p + i for i in range(EXPERT_REGION)])
            return c

        lax.fori_loop(0, nb // EXPERT_REGION, region, 0)
        done = nb // EXPERT_REGION * EXPERT_REGION

        @pl.when(nb % EXPERT_REGION >= 2)
        def _():
            run([base + done, base + done + 1])

        @pl.when(nb % 2 == 1)
        def _():
            run([base + nb - 1])

    @pl.when(e == pl.num_programs(0) - 1)
    def _():
        for slot in range(EXPERT_BUFS):
            @pl.when(slot < n_used)
            def _():
                wait_out(slot)


def _experts(start_blk, n_blk_e, n_used, xs, w_gate, w_up, w_down):
    n_rows = xs.shape[0] // PACK_SUB
    E, D, H = w_gate.shape
    wsel = lambda e, sb, nb, nu: (e, 0, 0)
    grid_spec = pltpu.PrefetchScalarGridSpec(
        num_scalar_prefetch=3,
        grid=(E,),
        in_specs=[pl.BlockSpec((1, D, H), wsel), pl.BlockSpec((1, D, H), wsel), pl.BlockSpec((1, H, D), wsel),
                  pl.BlockSpec(memory_space=pl.ANY)],
        out_specs=pl.BlockSpec(memory_space=pl.ANY),
        scratch_shapes=[pltpu.VMEM((EXPERT_BUFS, ROW_BLOCK * PACK_SUB, LANES), PACKED),
                        pltpu.VMEM((EXPERT_BUFS, ROW_BLOCK * PACK_SUB, LANES), PACKED),
                        pltpu.VMEM((D, H), BF16), pltpu.VMEM((D, H), BF16), pltpu.VMEM((H, D), BF16),
                        pltpu.SemaphoreType.DMA((EXPERT_BUFS,)), pltpu.SemaphoreType.DMA((EXPERT_BUFS,))],
    )
    return pl.pallas_call(
        _expert_kernel,
        grid_spec=grid_spec,
        out_shape=jax.ShapeDtypeStruct((n_rows * PACK_SUB, LANES), PACKED),
        compiler_params=_cparams(("arbitrary",)),
        name="moe_experts",
    )(start_blk, n_blk_e, n_used, w_gate, w_up, w_down, xs)


def _combine_kernel(dcur_ref, dnxt_ref, x_ref, gate_ref, sh_ref, g_ref, b_ref, y_ref, o_ref, buf, sem, *, tt, alpha):
    i = pl.program_id(0)
    n = pl.num_programs(0)

    def fetch(d_ref, slot):
        def body(j, c):
            for kk in range(TOP_K):
                src = _packed_row(y_ref, d_ref[kk * tt + j])
                pltpu.make_async_copy(src, _packed_row(buf.at[slot, kk], j), sem.at[slot]).start(priority=kk % 2)
            return c

        lax.fori_loop(0, tt, body, 0)

    def reduce(slot):
        for kk in range(TOP_K):
            pltpu.make_async_copy(y_ref.at[pl.ds(0, tt * PACK_SUB), :], buf.at[slot, kk], sem.at[slot]).wait()
        x = x_ref[...]
        moe = sh_ref[...]
        gates = gate_ref[...]
        for kk in range(TOP_K):
            moe = moe + gates[:, kk:kk + 1] * _unpack_rows(buf.at[slot, kk], tt, F32)
        o_ref[...] = _layer_norm(alpha * x + moe, g_ref[...], b_ref[...])

    @pl.when(i == 0)
    def _():
        fetch(dcur_ref, 0)

    for slot in range(2):
        @pl.when((i % 2 == slot) & (i + 1 < n))
        def _():
            fetch(dnxt_ref, 1 - slot)

    for slot in range(2):
        @pl.when(i % 2 == slot)
        def _():
            reduce(slot)


def _combine(dest, x1, gates, shared, g, b, y, alpha, tt=256):
    T, D = x1.shape
    n = T // tt
    dest = dest.reshape(TOP_K, n, tt).transpose(1, 0, 2).reshape(n * TOP_K * tt)
    row = lambda i: (i, 0)
    fixed = lambda i: (0, 0)
    return pl.pallas_call(
        functools.partial(_combine_kernel, tt=tt, alpha=alpha),
        grid=(n,),
        in_specs=[pl.BlockSpec((TOP_K * tt,), lambda i: (i,), memory_space=pltpu.SMEM),
                  pl.BlockSpec((TOP_K * tt,), lambda i: (jnp.minimum(i + 1, n - 1),), memory_space=pltpu.SMEM),
                  pl.BlockSpec((tt, D), row), pl.BlockSpec((tt, TOP_K), row),
                  pl.BlockSpec((tt, D), row),
                  pl.BlockSpec((1, D), fixed), pl.BlockSpec((1, D), fixed),
                  pl.BlockSpec(memory_space=pl.ANY)],
        out_specs=pl.BlockSpec((tt, D), row),
        out_shape=jax.ShapeDtypeStruct((T, D), F32),
        scratch_shapes=[pltpu.VMEM((2, TOP_K, tt * PACK_SUB, LANES), PACKED), pltpu.SemaphoreType.DMA((2,))],
        compiler_params=_cparams(("arbitrary",)),
        name="moe_combine_ln2",
    )(dest, dest, x1, gates, shared, g, b, y)


def _layer(x2, mem, B, S, depth, w_in, b_f, w_pool, pool_scale, w_mem_kv, w_out, ln1_g, ln1_b,
           w_router, router_bias, w_gate, w_up, w_down, ws_gate, ws_up, ws_down, ln2_g, ln2_b):
    T, D = x2.shape
    n_win, grp = w_pool.shape[0], w_pool.shape[1]
    pw = n_win * grp
    fw = FOX_HEADS * HEAD_DIM
    mw = w_mem_kv.shape[1] // 2
    E = w_router.shape[1]
    alpha = (2 * depth) ** 0.25

    f_lo = pw + 3 * fw
    w_main = jnp.concatenate([w_in[:, :f_lo], w_in[:, f_lo + FOX_HEADS:]], axis=1).astype(BF16)
    w_f = jnp.pad(w_in[:, f_lo:f_lo + FOX_HEADS], ((0, 0), (0, LANES - FOX_HEADS))).astype(BF16)
    bf_pad = jnp.pad(b_f, (0, LANES - FOX_HEADS)).reshape(1, LANES)
    wbd = jnp.zeros((pw, pw), F32)
    for g in range(n_win):
        wbd = wbd.at[g * grp:(g + 1) * grp, g * grp:(g + 1) * grp].set(w_pool[g])
    wrt = w_router.T
    wrt_hi = wrt.astype(BF16)
    wrt_lo = (wrt - wrt_hi.astype(F32)).astype(BF16)

    u_pool, q, k, v, q_mem, f_logit = _inproj(x2, w_main, w_f, pw, fw, mw)
    y_pool = _pool(u_pool, wbd.astype(BF16), pool_scale.reshape(1, pw), B, S)
    fcum = _fgate(f_logit, bf_pad, B, S)
    y_fox = _fox(q, k, v, fcum, B, S)
    y_mem = _memattn(q_mem, mem, w_mem_kv.astype(BF16), B, S)
    x1, x1_packed = _outproj(y_pool, y_fox, y_mem, x2, w_out.astype(BF16), ln1_g.reshape(1, D),
                             ln1_b.reshape(1, D), alpha)

    eidx, gates, rank, counts = _router(x1, wrt_hi, wrt_lo, _per_expert_column(router_bias))

    cnt = counts[:, 0].astype(I32)
    padded = (cnt + ROW_BLOCK - 1) // ROW_BLOCK * ROW_BLOCK
    pad_end = jnp.cumsum(padded)
    start_pad = (pad_end - padded).astype(I32)
    n_rows = (T * TOP_K + E * (ROW_BLOCK - 1)) // ROW_BLOCK * ROW_BLOCK
    n_used = (pad_end[-1] // ROW_BLOCK).astype(I32)
    zero_blk = jnp.where(padded > 0, pad_end - ROW_BLOCK, -1).astype(I32)
    dest = _dest(eidx, rank, _per_expert_column(start_pad))

    xs, shared = _dispatch(zero_blk, dest, x1_packed, ws_gate.astype(BF16), ws_up.astype(BF16),
                           ws_down.astype(BF16), n_rows)
    y = _experts(start_pad // ROW_BLOCK, (padded // ROW_BLOCK).astype(I32), n_used.reshape(1), xs,
                 w_gate, w_up, w_down)
    return _combine(dest, x1, gates.T, shared, ln2_g.reshape(1, D), ln2_b.reshape(1, D), y, alpha)


def kernel(x, mem, w_in, b_f, w_pool, pool_scale, w_mem_kv, w_out, ln1_g, ln1_b, w_router, router_bias,
           w_gate, w_up, w_down, ws_gate, ws_up, ws_down, ln2_g, ln2_b):
    B, S, D = x.shape
    depth = w_in.shape[0]
    x2 = x.reshape(B * S, D)
    for l in range(depth):
        x2 = _layer(x2, mem, B, S, depth, w_in[l], b_f[l], w_pool[l], pool_scale[l], w_mem_kv[l], w_out[l],
                    ln1_g[l], ln1_b[l], w_router[l], router_bias[l], w_gate[l], w_up[l], w_down[l],
                    ws_gate[l], ws_up[l], ws_down[l], ln2_g[l], ln2_b[l])
    return x2.reshape(B, S, D)
```

```python
import functools
import math

import jax
import jax.numpy as jnp
from jax import lax
from jax.experimental import pallas as pl
from jax.experimental.pallas import tpu as pltpu

F32 = jnp.float32
BF16 = jnp.bfloat16
I32 = jnp.int32
PACKED = jnp.int32

LANES = 128
POOL_WINDOWS = (2, 4, 8, 16)
POOL_HALO = 16
HEAD_DIM = 64
FOX_HEADS = 8
BIAS_TERMS = 3
ATTN_QBLOCKS = 8
MEM_HEADS = 4
TOP_K = 8
N_GROUPS = 8
TOPK_GROUPS = 4
ROUTED_SCALE = 2.5
LN_EPS = 1e-5
LOG2E = 1.4426950408889634
ROW_BLOCK = 256
EXPERT_BUFS = 8
EXPERT_REGION = 2
EXPERT_AHEAD = EXPERT_BUFS - EXPERT_REGION
VMEM_LIMIT = 48 * 1024 * 1024


def _cparams(sem):
    return pltpu.CompilerParams(dimension_semantics=sem, vmem_limit_bytes=VMEM_LIMIT)


def _layer_norm(z, g, b):
    mu = jnp.mean(z, axis=-1, keepdims=True)
    zc = z - mu
    var = jnp.mean(zc * zc, axis=-1, keepdims=True)
    return zc * lax.rsqrt(var + LN_EPS) * g + b


def _silu(x):
    return x * (1.0 / (1.0 + jnp.exp(-x)))


def _inproj_kernel(x_ref, w_ref, wf_ref, up_ref, q_ref, k_ref, v_ref, qm_ref, f_ref, *, pw, fw, mw, scale):
    xb = x_ref[...].astype(BF16)

    def proj(lo, hi):
        return jnp.dot(xb, w_ref[:, lo:hi], preferred_element_type=F32)

    up_ref[...] = proj(0, pw)
    q_ref[...] = (proj(pw, pw + fw) * (scale * LOG2E)).astype(BF16)
    k_ref[...] = proj(pw + fw, pw + 2 * fw).astype(BF16)
    v_ref[...] = proj(pw + 2 * fw, pw + 3 * fw).astype(BF16)
    qm_ref[...] = (proj(pw + 3 * fw, pw + 3 * fw + mw) * scale).astype(BF16)
    f_ref[...] = jnp.dot(xb, wf_ref[...], preferred_element_type=F32)


def _inproj(x2, w_main, w_f, pw, fw, mw, tm=512):
    T, D = x2.shape
    kern = functools.partial(_inproj_kernel, pw=pw, fw=fw, mw=mw, scale=HEAD_DIM ** -0.5)
    row = lambda i: (i, 0)
    fixed = lambda i: (0, 0)
    return pl.pallas_call(
        kern,
        grid=(T // tm,),
        in_specs=[pl.BlockSpec((tm, D), row),
                  pl.BlockSpec(w_main.shape, fixed),
                  pl.BlockSpec(w_f.shape, fixed)],
        out_specs=[pl.BlockSpec((tm, pw), row), pl.BlockSpec((tm, fw), row), pl.BlockSpec((tm, fw), row),
                   pl.BlockSpec((tm, fw), row), pl.BlockSpec((tm, mw), row), pl.BlockSpec((tm, LANES), row)],
        out_shape=[jax.ShapeDtypeStruct((T, pw), F32), jax.ShapeDtypeStruct((T, fw), BF16),
                   jax.ShapeDtypeStruct((T, fw), BF16), jax.ShapeDtypeStruct((T, fw), BF16),
                   jax.ShapeDtypeStruct((T, mw), BF16), jax.ShapeDtypeStruct((T, LANES), F32)],
        compiler_params=_cparams(("parallel",)),
        name="inproj",
    )(x2, w_main, w_f)


def _pool_kernel(u_ref, wbd_ref, sc_ref, o_ref, ext_sc, *, chunk, group):
    S, W = u_ref.shape
    ext_sc[0:POOL_HALO, :] = jnp.zeros((POOL_HALO, W), F32)
    ext_sc[POOL_HALO:, :] = u_ref[...]
    rows = chunk + POOL_HALO
    lrow = lax.broadcasted_iota(I32, (rows, W), 0)
    lane = lax.broadcasted_iota(I32, (rows, W), 1)

    def body(c, carry):
        start = pl.multiple_of(c * chunk, chunk)
        e = ext_sc[pl.ds(start, rows), :]
        posf = (lrow + (start - POOL_HALO + 1)).astype(F32)
        acc = e
        d = jnp.zeros_like(e)
        shift = 1
        for g, w in enumerate(POOL_WINDOWS):
            while shift < w:
                acc = acc + pltpu.roll(acc, shift, axis=0)
                shift *= 2
            mean = acc / jnp.minimum(posf, float(w))
            d = jnp.where((lane >= g * group) & (lane < (g + 1) * group), mean, d)
        d = (d - e)[POOL_HALO:, :]
        y = jnp.dot(d.astype(BF16), wbd_ref[...], preferred_element_type=F32) * sc_ref[...]
        o_ref[pl.ds(start, chunk), :] = y.astype(BF16)
        return carry

    lax.fori_loop(0, S // chunk, body, 0)


def _pool(u, wbd, pscale, B, S, chunk=512):
    T, W = u.shape
    kern = functools.partial(_pool_kernel, chunk=chunk, group=W // len(POOL_WINDOWS))
    return pl.pallas_call(
        kern,
        grid=(B,),
        in_specs=[pl.BlockSpec((S, W), lambda b: (b, 0)),
                  pl.BlockSpec((W, W), lambda b: (0, 0)),
                  pl.BlockSpec((1, W), lambda b: (0, 0))],
        out_specs=pl.BlockSpec((S, W), lambda b: (b, 0)),
        out_shape=jax.ShapeDtypeStruct((T, W), BF16),
        scratch_shapes=[pltpu.VMEM((S + POOL_HALO, W), F32)],
        compiler_params=_cparams(("parallel",)),
        name="pool_mixer",
    )(u, wbd, pscale)


def _fgate_kernel(f_ref, bf_ref, o_ref):
    S = f_ref.shape[0]
    z = f_ref[...] + bf_ref[...]
    x = jnp.minimum(z, 0.0) - jnp.log(1.0 + jnp.exp(-jnp.abs(z)))
    row = lax.broadcasted_iota(I32, x.shape, 0)
    lane = lax.broadcasted_iota(I32, x.shape, 1)
    shift = 1
    while shift < S:
        x = x + jnp.where(row >= shift, pltpu.roll(x, shift, axis=0), 0.0)
        shift *= 2
    x = x * LOG2E
    hi = x.astype(BF16).astype(F32)
    mid = (x - hi).astype(BF16).astype(F32)
    lo = (x - hi) - mid
    for p in range(FOX_HEADS // 2):
        out = jnp.zeros(x.shape, F32)
        for hh in range(2):
            c = 2 * p + hh
            for j, part in enumerate((hi, mid, lo)):
                out = jnp.where(lane == BIAS_TERMS * hh + j, -part[:, c:c + 1], out)
        o_ref[0, p] = out.astype(BF16)


def _fgate(f_logit, bf_pad, B, S):
    return pl.pallas_call(
        _fgate_kernel,
        grid=(B,),
        in_specs=[pl.BlockSpec((S, LANES), lambda b: (b, 0)),
                  pl.BlockSpec((1, LANES), lambda b: (0, 0))],
        out_specs=pl.BlockSpec((1, FOX_HEADS // 2, S, LANES), lambda b: (b, 0, 0, 0)),
        out_shape=jax.ShapeDtypeStruct((B, FOX_HEADS // 2, S, LANES), BF16),
        compiler_params=_cparams(("parallel",)),
        name="forget_cumsum",
    )(f_logit, bf_pad)


def _fox_kernel(q_ref, k_ref, a_ref, v_ref, o_ref, vt_sc, sa_sc, sb_sc,
                qa0_sc, m0_sc, l0_sc, acc0_sc, qa1_sc, m1_sc, l1_sc, acc1_sc, *, blk):
    j = pl.program_id(2)
    nk = vt_sc.shape[0]
    nt = (((1,), (1,)), ((), ()))
    sets = ((qa0_sc, m0_sc, l0_sc, acc0_sc), (qa1_sc, m1_sc, l1_sc, acc1_sc))

    @pl.when(j == 0)
    def _():
        for c in range(nk):
            vt_sc[c] = v_ref[c * blk:(c + 1) * blk, :].astype(F32).T.astype(BF16)

    def prepare(which, i):
        qa_sc, m_sc, l_sc, acc_sc = sets[which]
        q = q_ref[i * blk:(i + 1) * blk, :].astype(F32)
        lane = lax.broadcasted_iota(I32, q.shape, 1)
        for h in range(2):
            mine = (lane >= h * HEAD_DIM) & (lane < (h + 1) * HEAD_DIM)
            bias_rows = (lane >= h * BIAS_TERMS) & (lane < (h + 1) * BIAS_TERMS)
            qa_sc[h] = jnp.concatenate(
                [jnp.where(mine, q, 0.0), jnp.where(bias_rows, 1.0, 0.0)], axis=1).astype(BF16)
        m_sc[...] = jnp.full(m_sc.shape, -jnp.inf, F32)
        l_sc[...] = jnp.zeros(l_sc.shape, F32)
        acc_sc[...] = jnp.zeros(acc_sc.shape, F32)

    def scores(ki, dst, which):
        qa_sc = sets[which][0]
        ks = pl.multiple_of(ki * blk, blk)
        kb = jnp.concatenate([k_ref[pl.ds(ks, blk), :], a_ref[0, 0, pl.ds(ks, blk), :]], axis=1)
        for h in range(2):
            dst[h] = lax.dot_general(kb, qa_sc[h], nt, preferred_element_type=F32)

    def absorb(ki, src, causal, which):
        _, m_sc, l_sc, acc_sc = sets[which]
        for h in range(2):
            st = src[h]
            if causal:
                kpos = lax.broadcasted_iota(I32, st.shape, 0)
                qpos = lax.broadcasted_iota(I32, st.shape, 1)
                st = jnp.where(kpos <= qpos, st, -jnp.inf)
            m_prev = m_sc[h]
            m_new = jnp.maximum(m_prev, jnp.max(st, axis=0, keepdims=True))
            alpha = jnp.exp2(m_prev - m_new)
            p = jnp.exp2(st - m_new)
            l_sc[h] = alpha * l_sc[h] + jnp.sum(p, axis=0, keepdims=True)
            m_sc[h] = m_new
            vt = vt_sc[ki, h * HEAD_DIM:(h + 1) * HEAD_DIM, :]
            pv = jnp.dot(vt, p.astype(BF16), preferred_element_type=F32)
            rows = slice(h * HEAD_DIM, (h + 1) * HEAD_DIM)
            acc_sc[rows, :] = acc_sc[rows, :] * alpha + pv

    def finish(which, i):
        _, _, l_sc, acc_sc = sets[which]
        out_t = jnp.concatenate([acc_sc[h * HEAD_DIM:(h + 1) * HEAD_DIM, :] / l_sc[h] for h in range(2)], axis=0)
        o_ref[i * blk:(i + 1) * blk, :] = out_t.T.astype(BF16)

    def walk(first, other, which, n_pairs):
        def body(p, carry):
            scores(2 * p + 1, other, which)
            absorb(2 * p, first, False, which)
            scores(2 * p + 2, first, which)
            absorb(2 * p + 1, other, False, which)
            return carry

        lax.fori_loop(0, n_pairs, body, 0)

    per_step = q_ref.shape[0] // blk
    first, other = sa_sc, sb_sc
    tail = None
    for i in range(per_step):
        which = i % 2
        qi = per_step * j + i
        prepare(which, i)
        scores(0, first, which)
        if tail is not None:
            tail()
        walk(first, other, which, (per_step // 2) * j + i // 2)
        if i % 2 == 0:
            tail = functools.partial(lambda q, buf, w, ii: (absorb(q, buf, True, w), finish(w, ii)),
                                     qi, first, which, i)
            first, other = other, first
        else:
            scores(qi, other, which)
            absorb(qi - 1, first, False, which)
            tail = functools.partial(lambda q, buf, w, ii: (absorb(q, buf, True, w), finish(w, ii)),
                                     qi, other, which, i)
    tail()


def _fox(q, k, v, fbias, B, S, blk=512):
    T, FW = q.shape
    nq = S // blk
    pairs = FW // LANES
    per = math.gcd(ATTN_QBLOCKS, nq)
    assert per % 2 == 0
    kern = functools.partial(_fox_kernel, blk=blk)
    return pl.pallas_call(
        kern,
        grid=(B, pairs, nq // per),
        in_specs=[pl.BlockSpec((per * blk, LANES), lambda b, p, i: (b * (nq // per) + i, p)),
                  pl.BlockSpec((S, LANES), lambda b, p, i: (b, p)),
                  pl.BlockSpec((1, 1, S, LANES), lambda b, p, i: (b, p, 0, 0)),
                  pl.BlockSpec((S, LANES), lambda b, p, i: (b, p))],
        out_specs=pl.BlockSpec((per * blk, LANES), lambda b, p, i: (b * (nq // per) + i, p)),
        out_shape=jax.ShapeDtypeStruct((T, FW), BF16),
        scratch_shapes=[pltpu.VMEM((nq, LANES, blk), BF16),
                        pltpu.VMEM((2, blk, blk), F32), pltpu.VMEM((2, blk, blk), F32)] + 2 * [
                            pltpu.VMEM((2, blk, 2 * LANES), BF16), pltpu.VMEM((2, 1, blk), F32),
                            pltpu.VMEM((2, 1, blk), F32), pltpu.VMEM((LANES, blk), F32)],
        compiler_params=_cparams(("parallel", "parallel", "arbitrary")),
        name="fox_attention",
    )(q, k, fbias, v)


def _memattn_kernel(qm_ref, mem_ref, wkv_ref, o_ref, k_sc, v_sc):
    MW = qm_ref.shape[1]

    @pl.when(pl.program_id(1) == 0)
    def _():
        kv = jnp.dot(mem_ref[0].astype(BF16), wkv_ref[...], preferred_element_type=F32)
        k_sc[...] = kv[:, :MW].astype(BF16)
        v_sc[...] = kv[:, MW:].astype(BF16)

    q = qm_ref[...]
    lane = lax.broadcasted_iota(I32, q.shape, 1)
    hd = MW // MEM_HEADS
    out = jnp.zeros(q.shape, F32)
    for h in range(MEM_HEADS):
        mine = (lane >= h * hd) & (lane < (h + 1) * hd)
        qh = jnp.where(mine, q, jnp.zeros_like(q))
        s = lax.dot_general(qh, k_sc[...], (((1,), (1,)), ((), ())), preferred_element_type=F32)
        p = jnp.exp(s - jnp.max(s, axis=1, keepdims=True))
        l = jnp.sum(p, axis=1, keepdims=True)
        o = jnp.dot(p.astype(BF16), v_sc[...], preferred_element_type=F32)
        out = jnp.where(mine, o / l, out)
    o_ref[...] = out.astype(BF16)


def _memattn(qm, mem, wkv, B, S, tq=512):
    T, MW = qm.shape
    M, D = mem.shape[1], mem.shape[2]
    nq = S // tq
    return pl.pallas_call(
        _memattn_kernel,
        grid=(B, nq),
        in_specs=[pl.BlockSpec((tq, MW), lambda b, i: (b * nq + i, 0)),
                  pl.BlockSpec((1, M, D), lambda b, i: (b, 0, 0)),
                  pl.BlockSpec(wkv.shape, lambda b, i: (0, 0))],
        out_specs=pl.BlockSpec((tq, MW), lambda b, i: (b * nq + i, 0)),
        out_shape=jax.ShapeDtypeStruct((T, MW), BF16),
        scratch_shapes=[pltpu.VMEM((M, MW), BF16), pltpu.VMEM((M, MW), BF16)],
        compiler_params=_cparams(("parallel", "arbitrary")),
        name="memory_attention",
    )(qm, mem, wkv)


PACK_SUB = 4


def _pack_rows(ref, val):
    n = val.shape[0]
    half = val.shape[1] // 2
    words = pltpu.pack_elementwise([val[:, :half], val[:, half:]], packed_dtype=BF16)
    for j in range(PACK_SUB):
        ref[pl.ds(j, n, stride=PACK_SUB), :] = words[:, j * LANES:(j + 1) * LANES]


def _unpack_words(slabs):
    lo = [pltpu.unpack_elementwise(s, index=0, packed_dtype=BF16, unpacked_dtype=F32) for s in slabs]
    hi = [pltpu.unpack_elementwise(s, index=1, packed_dtype=BF16, unpacked_dtype=F32) for s in slabs]
    return jnp.concatenate(lo + hi, axis=1)


def _unpack_rows(ref, n, dtype=BF16):
    return _unpack_words([ref[pl.ds(j, n, stride=PACK_SUB), :] for j in range(PACK_SUB)]).astype(dtype)


def _packed_row(ref, r):
    return ref.at[pl.ds(pl.multiple_of(r * PACK_SUB, PACK_SUB), PACK_SUB), :]


def _outproj_kernel(yp_ref, yf_ref, ym_ref, x_ref, w_ref, g_ref, b_ref, o_ref, pk_ref, *, alpha):
    pw, fw = yp_ref.shape[1], yf_ref.shape[1]
    h = jnp.dot(yp_ref[...], w_ref[0:pw, :], preferred_element_type=F32)
    h = h + jnp.dot(yf_ref[...], w_ref[pw:pw + fw, :], preferred_element_type=F32)
    h = h + jnp.dot(ym_ref[...], w_ref[pw + fw:, :], preferred_element_type=F32)
    y = _layer_norm(alpha * x_ref[...] + h, g_ref[...], b_ref[...])
    o_ref[...] = y
    _pack_rows(pk_ref, y)


def _outproj(yp, yf, ym, x2, w_out, g, b, alpha, tm=512):
    T, D = x2.shape
    assert D == 2 * PACK_SUB * LANES
    row = lambda i: (i, 0)
    fixed = lambda i: (0, 0)
    return pl.pallas_call(
        functools.partial(_outproj_kernel, alpha=alpha),
        grid=(T // tm,),
        in_specs=[pl.BlockSpec((tm, yp.shape[1]), row), pl.BlockSpec((tm, yf.shape[1]), row),
                  pl.BlockSpec((tm, ym.shape[1]), row), pl.BlockSpec((tm, D), row),
                  pl.BlockSpec(w_out.shape, fixed), pl.BlockSpec((1, D), fixed), pl.BlockSpec((1, D), fixed)],
        out_specs=[pl.BlockSpec((tm, D), row),
                   pl.BlockSpec((tm * PACK_SUB, LANES), row)],
        out_shape=[jax.ShapeDtypeStruct((T, D), F32),
                   jax.ShapeDtypeStruct((T * PACK_SUB, LANES), PACKED)],
        compiler_params=_cparams(("parallel",)),
        name="outproj_ln1",
    )(yp, yf, ym, x2, w_out, g, b)


ROUTE_TILE = 512


def _per_expert_column(v, dtype=F32):
    return jnp.broadcast_to(v.astype(dtype)[:, None], (v.shape[0], ROUTE_TILE))


def _router_kernel(x_ref, wh_ref, wl_ref, bias_ref, eidx_ref, gate_ref, rank_ref, cnt_ref, carry_sc, *, tt):
    E = wh_ref.shape[0]
    gsz = E // N_GROUPS
    ninf = -jnp.inf

    @pl.when(pl.program_id(0) == 0)
    def _():
        carry_sc[...] = jnp.zeros(carry_sc.shape, F32)

    x = x_ref[...]
    xh = x.astype(BF16)
    xl = (x - xh.astype(F32)).astype(BF16)
    wh = wh_ref[...]
    nt = (((1,), (1,)), ((), ()))
    logits = lax.dot_general(wh, xh, nt, preferred_element_type=F32) + (
        lax.dot_general(wl_ref[...], xh, nt, preferred_element_type=F32)
        + lax.dot_general(wh, xl, nt, preferred_element_type=F32))
    scores = 1.0 / (1.0 + jnp.exp(-logits))
    biased = scores + bias_ref[...]
    eiota = lax.broadcasted_iota(I32, (E, tt), 0).astype(F32)
    giota = lax.broadcasted_iota(I32, (gsz, tt), 0).astype(F32)

    def cmax(a):
        return jnp.max(a, axis=0, keepdims=True)

    def first_at(a, m, iota, n):
        return jnp.min(jnp.where(a == m, iota, float(n)), axis=0, keepdims=True)

    groups = [biased[g * gsz:(g + 1) * gsz, :] for g in range(N_GROUPS)]
    gscore = []
    for blk in groups:
        m1 = cmax(blk)
        m2 = cmax(jnp.where(giota == first_at(blk, m1, giota, gsz), ninf, blk))
        gscore.append(m1 + m2)

    kept = []
    for g in range(N_GROUPS):
        ahead = jnp.zeros((1, tt), F32)
        for o in range(N_GROUPS):
            if o == g:
                continue
            beats = gscore[o] > gscore[g]
            if o < g:
                beats = beats | (gscore[o] == gscore[g])
            ahead = ahead + jnp.where(beats, 1.0, 0.0)
        kept.append(groups[g] + jnp.where(ahead < float(TOPK_GROUPS), 0.0, ninf))
    masked = jnp.concatenate(kept, axis=0)

    picks, sels = [], []
    chosen = jnp.zeros((E, tt), F32)
    for _ in range(TOP_K):
        ik = first_at(masked, cmax(masked), eiota, E)
        oh = eiota == ik
        picks.append(ik)
        sels.append(jnp.sum(jnp.where(oh, scores, 0.0), axis=0, keepdims=True))
        chosen = jnp.where(oh, 1.0, chosen)
        masked = jnp.where(oh, ninf, masked)
    denom = sels[0]
    for sk in sels[1:]:
        denom = denom + sk

    r = lax.broadcasted_iota(I32, (tt, tt), 0)
    c = lax.broadcasted_iota(I32, (tt, tt), 1)
    earlier = jnp.where(r < c, 1.0, 0.0).astype(BF16)
    chosen_b = chosen.astype(BF16)
    pos = carry_sc[...] + jnp.dot(chosen_b, earlier, preferred_element_type=F32)
    carry_sc[...] = carry_sc[...] + jnp.dot(chosen_b, jnp.ones((tt, tt), BF16), preferred_element_type=F32)
    cnt_ref[...] = carry_sc[...]

    ranks = [jnp.sum(jnp.where(eiota == ik, pos, 0.0), axis=0, keepdims=True) for ik in picks]
    eidx_ref[...] = jnp.concatenate(picks, axis=0).astype(I32)
    gate_ref[...] = jnp.concatenate([sk / denom * ROUTED_SCALE for sk in sels], axis=0)
    rank_ref[...] = jnp.concatenate(ranks, axis=0).astype(I32)


def _router(x1, wrt_hi, wrt_lo, rbias):
    T = x1.shape[0]
    E, D = wrt_hi.shape
    tt = ROUTE_TILE
    tok = lambda i: (0, i)
    fixed = lambda i: (0, 0)
    return pl.pallas_call(
        functools.partial(_router_kernel, tt=tt),
        grid=(T // tt,),
        in_specs=[pl.BlockSpec((tt, D), lambda i: (i, 0)), pl.BlockSpec((E, D), fixed),
                  pl.BlockSpec((E, D), fixed), pl.BlockSpec((E, tt), fixed)],
        out_specs=[pl.BlockSpec((TOP_K, tt), tok), pl.BlockSpec((TOP_K, tt), tok), pl.BlockSpec((TOP_K, tt), tok),
                   pl.BlockSpec((E, tt), fixed)],
        out_shape=[jax.ShapeDtypeStruct((TOP_K, T), I32), jax.ShapeDtypeStruct((TOP_K, T), F32),
                   jax.ShapeDtypeStruct((TOP_K, T), I32), jax.ShapeDtypeStruct((E, tt), F32)],
        scratch_shapes=[pltpu.VMEM((E, tt), F32)],
        compiler_params=_cparams(("arbitrary",)),
        name="router_topk",
    )(x1, wrt_hi, wrt_lo, rbias)


def _dest_kernel(e_ref, r_ref, sp_ref, d_ref):
    E = sp_ref.shape[0]
    tt = e_ref.shape[1]
    eiota = lax.broadcasted_iota(I32, (E, tt), 0)
    sp = sp_ref[...]
    rows = []
    for kk in range(TOP_K):
        start = jnp.sum(jnp.where(eiota == e_ref[kk:kk + 1, :], sp, 0.0), axis=0, keepdims=True)
        rows.append(start.astype(I32) + r_ref[kk:kk + 1, :])
    d_ref[...] = jnp.concatenate(rows, axis=0)


def _dest(eidx, rank, start_pad_b):
    K, T = eidx.shape
    tt = ROUTE_TILE
    tok = lambda i: (0, i)
    return pl.pallas_call(
        _dest_kernel,
        grid=(T // tt,),
        in_specs=[pl.BlockSpec((K, tt), tok), pl.BlockSpec((K, tt), tok),
                  pl.BlockSpec(start_pad_b.shape, lambda i: (0, 0))],
        out_specs=pl.BlockSpec((K, tt), tok),
        out_shape=jax.ShapeDtypeStruct((K, T), I32),
        compiler_params=_cparams(("parallel",)),
        name="moe_dest",
    )(eidx, rank, start_pad_b)


def _dispatch_kernel(zb_ref, d_ref, x_ref, wsg_ref, wsu_ref, wsd_ref, xs_ref, sh_ref, zero_sc, sem, *, tt, n_exp):
    blk_rows = ROW_BLOCK * PACK_SUB

    @pl.when(pl.program_id(0) == 0)
    def _():
        zero_sc[...] = jnp.zeros(zero_sc.shape, PACKED)

        def zstart(e, c):
            @pl.when(zb_ref[e] >= 0)
            def _():
                dst = xs_ref.at[pl.ds(pl.multiple_of(zb_ref[e] * PACK_SUB, blk_rows), blk_rows), :]
                pltpu.make_async_copy(zero_sc, dst, sem).start()
            return c

        def zwait(e, c):
            @pl.when(zb_ref[e] >= 0)
            def _():
                pltpu.make_async_copy(zero_sc, xs_ref.at[pl.ds(0, blk_rows), :], sem).wait()
            return c

        lax.fori_loop(0, n_exp, zstart, 0)
        lax.fori_loop(0, n_exp, zwait, 0)

    def start(j, c):
        src = _packed_row(x_ref, j)
        for kk in range(TOP_K):
            pltpu.make_async_copy(src, _packed_row(xs_ref, d_ref[kk, j]), sem).start(priority=kk % 2)
        return c

    lax.fori_loop(0, tt, start, 0)
    xb = _unpack_rows(x_ref, tt)
    hs = _silu(jnp.dot(xb, wsg_ref[...], preferred_element_type=F32)) * jnp.dot(
        xb, wsu_ref[...], preferred_element_type=F32)
    sh_ref[...] = jnp.dot(hs.astype(BF16), wsd_ref[...], preferred_element_type=F32)
    for kk in range(TOP_K):
        pltpu.make_async_copy(x_ref, xs_ref.at[pl.ds(0, tt * PACK_SUB), :], sem).wait()


def _dispatch(zero_blk, dest, xpk, wsg, wsu, wsd, n_rows, tt=1024):
    T = xpk.shape[0] // PACK_SUB
    D = wsg.shape[0]
    n_exp = zero_blk.shape[0]
    fixed = lambda i, zb: (0, 0)
    grid_spec = pltpu.PrefetchScalarGridSpec(
        num_scalar_prefetch=1,
        grid=(T // tt,),
        in_specs=[pl.BlockSpec((TOP_K, tt), lambda i, zb: (0, i), memory_space=pltpu.SMEM),
                  pl.BlockSpec((tt * PACK_SUB, LANES), lambda i, zb: (i, 0)),
                  pl.BlockSpec(wsg.shape, fixed), pl.BlockSpec(wsu.shape, fixed), pl.BlockSpec(wsd.shape, fixed)],
        out_specs=[pl.BlockSpec(memory_space=pl.ANY), pl.BlockSpec((tt, D), lambda i, zb: (i, 0))],
        scratch_shapes=[pltpu.VMEM((ROW_BLOCK * PACK_SUB, LANES), PACKED), pltpu.SemaphoreType.DMA],
    )
    return pl.pallas_call(
        functools.partial(_dispatch_kernel, tt=tt, n_exp=n_exp),
        grid_spec=grid_spec,
        out_shape=[jax.ShapeDtypeStruct((n_rows * PACK_SUB, LANES), PACKED), jax.ShapeDtypeStruct((T, D), F32)],
        compiler_params=_cparams(("arbitrary",)),
        name="moe_dispatch",
    )(zero_blk, dest, xpk, wsg, wsu, wsd)


def _expert_kernel(sb_ref, nb_ref, nu_ref, wg_ref, wu_ref, wd_ref, xs_ref, y_ref,
                   xbuf, ybuf, wg_sc, wu_sc, wd_sc, isem, osem):
    e = pl.program_id(0)
    nb = nb_ref[e]
    base = sb_ref[e]
    n_used = nu_ref[0]
    in_rows = out_rows = ROW_BLOCK * PACK_SUB

    def fetch(blk, slot):
        src = xs_ref.at[pl.ds(pl.multiple_of(blk * in_rows, in_rows), in_rows), :]
        pltpu.make_async_copy(src, xbuf.at[slot], isem.at[slot]).start()

    def wait_in(slot):
        pltpu.make_async_copy(xs_ref.at[pl.ds(0, in_rows), :], xbuf.at[slot], isem.at[slot]).wait()

    def store(blk, slot):
        dst = y_ref.at[pl.ds(pl.multiple_of(blk * out_rows, out_rows), out_rows), :]
        pltpu.make_async_copy(ybuf.at[slot], dst, osem.at[slot]).start()

    def wait_out(slot):
        pltpu.make_async_copy(ybuf.at[slot], y_ref.at[pl.ds(0, out_rows), :], osem.at[slot]).wait()

    @pl.when(nb > 0)
    def _():
        @pl.when(base == 0)
        def _():
            for g0 in range(EXPERT_AHEAD):
                @pl.when(g0 < n_used)
                def _():
                    fetch(g0, g0 % EXPERT_BUFS)

        wg_sc[...] = wg_ref[0].astype(BF16)
        wu_sc[...] = wu_ref[0].astype(BF16)
        wd_sc[...] = wd_ref[0].astype(BF16)

        def run(blocks):
            slots = [g % EXPERT_BUFS for g in blocks]
            for g, slot in zip(blocks, slots):
                wait_in(slot)

                @pl.when(g + EXPERT_AHEAD < n_used)
                def _():
                    fetch(g + EXPERT_AHEAD, (g + EXPERT_AHEAD) % EXPERT_BUFS)

                @pl.when(g >= EXPERT_BUFS)
                def _():
                    wait_out(slot)

            for g, slot in zip(blocks, slots):
                xb = _unpack_rows(xbuf.at[slot], ROW_BLOCK)
                a = jnp.dot(xb, wg_sc[...], preferred_element_type=F32)
                u = jnp.dot(xb, wu_sc[...], preferred_element_type=F32)
                h = (_silu(a) * u).astype(BF16)
                _pack_rows(ybuf.at[slot], jnp.dot(h, wd_sc[...], preferred_element_type=F32))
            for g, slot in zip(blocks, slots):
                store(g, slot)

        def region(p, c):
            run([base + EXPERT_REGION * p + i for i in range(EXPERT_REGION)])
            return c

        lax.fori_loop(0, nb // EXPERT_REGION, region, 0)
        done = nb // EXPERT_REGION * EXPERT_REGION

        @pl.when(nb % EXPERT_REGION >= 2)
        def _():
            run([base + done, base + done + 1])

        @pl.when(nb % 2 == 1)
        def _():
            run([base + nb - 1])

    @pl.when(e == pl.num_programs(0) - 1)
    def _():
        for slot in range(EXPERT_BUFS):
            @pl.when(slot < n_used)
            def _():
                wait_out(slot)


def _experts(start_blk, n_blk_e, n_used, xs, w_gate, w_up, w_down):
    n_rows = xs.shape[0] // PACK_SUB
    E, D, H = w_gate.shape
    wsel = lambda e, sb, nb, nu: (e, 0, 0)
    grid_spec = pltpu.PrefetchScalarGridSpec(
        num_scalar_prefetch=3,
        grid=(E,),
        in_specs=[pl.BlockSpec((1, D, H), wsel), pl.BlockSpec((1, D, H), wsel), pl.BlockSpec((1, H, D), wsel),
                  pl.BlockSpec(memory_space=pl.ANY)],
        out_specs=pl.BlockSpec(memory_space=pl.ANY),
        scratch_shapes=[pltpu.VMEM((EXPERT_BUFS, ROW_BLOCK * PACK_SUB, LANES), PACKED),
                        pltpu.VMEM((EXPERT_BUFS, ROW_BLOCK * PACK_SUB, LANES), PACKED),
                        pltpu.VMEM((D, H), BF16), pltpu.VMEM((D, H), BF16), pltpu.VMEM((H, D), BF16),
                        pltpu.SemaphoreType.DMA((EXPERT_BUFS,)), pltpu.SemaphoreType.DMA((EXPERT_BUFS,))],
    )
    return pl.pallas_call(
        _expert_kernel,
        grid_spec=grid_spec,
        out_shape=jax.ShapeDtypeStruct((n_rows * PACK_SUB, LANES), PACKED),
        compiler_params=_cparams(("arbitrary",)),
        name="moe_experts",
    )(start_blk, n_blk_e, n_used, w_gate, w_up, w_down, xs)


def _combine_kernel(dcur_ref, dnxt_ref, x_ref, gate_ref, sh_ref, g_ref, b_ref, y_ref, o_ref, buf, sem, *, tt, alpha):
    i = pl.program_id(0)
    n = pl.num_programs(0)

    def fetch(d_ref, slot):
        def body(j, c):
            for kk in range(TOP_K):
                src = _packed_row(y_ref, d_ref[kk * tt + j])
                pltpu.make_async_copy(src, _packed_row(buf.at[slot, kk], j), sem.at[slot]).start(priority=kk % 2)
            return c

        lax.fori_loop(0, tt, body, 0)

    def reduce(slot):
        for kk in range(TOP_K):
            pltpu.make_async_copy(y_ref.at[pl.ds(0, tt * PACK_SUB), :], buf.at[slot, kk], sem.at[slot]).wait()
        x = x_ref[...]
        moe = sh_ref[...]
        gates = gate_ref[...]
        for kk in range(TOP_K):
            moe = moe + gates[:, kk:kk + 1] * _unpack_rows(buf.at[slot, kk], tt, F32)
        o_ref[...] = _layer_norm(alpha * x + moe, g_ref[...], b_ref[...])

    @pl.when(i == 0)
    def _():
        fetch(dcur_ref, 0)

    for slot in range(2):
        @pl.when((i % 2 == slot) & (i + 1 < n))
        def _():
            fetch(dnxt_ref, 1 - slot)

    for slot in range(2):
        @pl.when(i % 2 == slot)
        def _():
            reduce(slot)


def _combine(dest, x1, gates, shared, g, b, y, alpha, tt=256):
    T, D = x1.shape
    n = T // tt
    dest = dest.reshape(TOP_K, n, tt).transpose(1, 0, 2).reshape(n * TOP_K * tt)
    row = lambda i: (i, 0)
    fixed = lambda i: (0, 0)
    return pl.pallas_call(
        functools.partial(_combine_kernel, tt=tt, alpha=alpha),
        grid=(n,),
        in_specs=[pl.BlockSpec((TOP_K * tt,), lambda i: (i,), memory_space=pltpu.SMEM),
                  pl.BlockSpec((TOP_K * tt,), lambda i: (jnp.minimum(i + 1, n - 1),), memory_space=pltpu.SMEM),
                  pl.BlockSpec((tt, D), row), pl.BlockSpec((tt, TOP_K), row),
                  pl.BlockSpec((tt, D), row),
                  pl.BlockSpec((1, D), fixed), pl.BlockSpec((1, D), fixed),
                  pl.BlockSpec(memory_space=pl.ANY)],
        out_specs=pl.BlockSpec((tt, D), row),
        out_shape=jax.ShapeDtypeStruct((T, D), F32),
        scratch_shapes=[pltpu.VMEM((2, TOP_K, tt * PACK_SUB, LANES), PACKED), pltpu.SemaphoreType.DMA((2,))],
        compiler_params=_cparams(("arbitrary",)),
        name="moe_combine_ln2",
    )(dest, dest, x1, gates, shared, g, b, y)


def _layer(x2, mem, B, S, depth, w_in, b_f, w_pool, pool_scale, w_mem_kv, w_out, ln1_g, ln1_b,
           w_router, router_bias, w_gate, w_up, w_down, ws_gate, ws_up, ws_down, ln2_g, ln2_b):
    T, D = x2.shape
    n_win, grp = w_pool.shape[0], w_pool.shape[1]
    pw = n_win * grp
    fw = FOX_HEADS * HEAD_DIM
    mw = w_mem_kv.shape[1] // 2
    E = w_router.shape[1]
    alpha = (2 * depth) ** 0.25

    f_lo = pw + 3 * fw
    w_main = jnp.concatenate([w_in[:, :f_lo], w_in[:, f_lo + FOX_HEADS:]], axis=1).astype(BF16)
    w_f = jnp.pad(w_in[:, f_lo:f_lo + FOX_HEADS], ((0, 0), (0, LANES - FOX_HEADS))).astype(BF16)
    bf_pad = jnp.pad(b_f, (0, LANES - FOX_HEADS)).reshape(1, LANES)
    wbd = jnp.zeros((pw, pw), F32)
    for g in range(n_win):
        wbd = wbd.at[g * grp:(g + 1) * grp, g * grp:(g + 1) * grp].set(w_pool[g])
    wrt = w_router.T
    wrt_hi = wrt.astype(BF16)
    wrt_lo = (wrt - wrt_hi.astype(F32)).astype(BF16)

    u_pool, q, k, v, q_mem, f_logit = _inproj(x2, w_main, w_f, pw, fw, mw)
    y_pool = _pool(u_pool, wbd.astype(BF16), pool_scale.reshape(1, pw), B, S)
    fcum = _fgate(f_logit, bf_pad, B, S)
    y_fox = _fox(q, k, v, fcum, B, S)
    y_mem = _memattn(q_mem, mem, w_mem_kv.astype(BF16), B, S)
    x1, x1_packed = _outproj(y_pool, y_fox, y_mem, x2, w_out.astype(BF16), ln1_g.reshape(1, D),
                             ln1_b.reshape(1, D), alpha)

    eidx, gates, rank, counts = _router(x1, wrt_hi, wrt_lo, _per_expert_column(router_bias))

    cnt = counts[:, 0].astype(I32)
    padded = (cnt + ROW_BLOCK - 1) // ROW_BLOCK * ROW_BLOCK
    pad_end = jnp.cumsum(padded)
    start_pad = (pad_end - padded).astype(I32)
    n_rows = (T * TOP_K + E * (ROW_BLOCK - 1)) // ROW_BLOCK * ROW_BLOCK
    n_used = (pad_end[-1] // ROW_BLOCK).astype(I32)
    zero_blk = jnp.where(padded > 0, pad_end - ROW_BLOCK, -1).astype(I32)
    dest = _dest(eidx, rank, _per_expert_column(start_pad))

    xs, shared = _dispatch(zero_blk, dest, x1_packed, ws_gate.astype(BF16), ws_up.astype(BF16),
                           ws_down.astype(BF16), n_rows)
    y = _experts(start_pad // ROW_BLOCK, (padded // ROW_BLOCK).astype(I32), n_used.reshape(1), xs,
                 w_gate, w_up, w_down)
    return _combine(dest, x1, gates.T, shared, ln2_g.reshape(1, D), ln2_b.reshape(1, D), y, alpha)


def kernel(x, mem, w_in, b_f, w_pool, pool_scale, w_mem_kv, w_out, ln1_g, ln1_b, w_router, router_bias,
           w_gate, w_up, w_down, ws_gate, ws_up, ws_down, ln2_g, ln2_b):
    B, S, D = x.shape
    depth = w_in.shape[0]
    x2 = x.reshape(B * S, D)
    for l in range(depth):
        x2 = _layer(x2, mem, B, S, depth, w_in[l], b_f[l], w_pool[l], pool_scale[l], w_mem_kv[l], w_out[l],
                    ln1_g[l], ln1_b[l], w_router[l], router_bias[l], w_gate[l], w_up[l], w_down[l],
                    ws_gate[l], ws_up[l], ws_down[l], ln2_g[l], ln2_b[l])
    return x2.reshape(B, S, D)
```

```python
import functools
import math

import jax
import jax.numpy as jnp
from jax import lax
from jax.experimental import pallas as pl
from jax.experimental.pallas import tpu as pltpu

F32 = jnp.float32
BF16 = jnp.bfloat16
I32 = jnp.int32
PACKED = jnp.int32

LANES = 128
POOL_WINDOWS = (2, 4, 8, 16)
POOL_HALO = 16
HEAD_DIM = 64
FOX_HEADS = 8
BIAS_TERMS = 3
ATTN_QBLOCKS = 8
MEM_HEADS = 4
TOP_K = 8
N_GROUPS = 8
TOPK_GROUPS = 4
ROUTED_SCALE = 2.5
LN_EPS = 1e-5
LOG2E = 1.4426950408889634
ROW_BLOCK = 256
EXPERT_BUFS = 8
EXPERT_REGION = 2
EXPERT_AHEAD = EXPERT_BUFS - EXPERT_REGION
VMEM_LIMIT = 48 * 1024 * 1024


def _cparams(sem):
    return pltpu.CompilerParams(dimension_semantics=sem, vmem_limit_bytes=VMEM_LIMIT)


def _layer_norm(z, g, b):
    mu = jnp.mean(z, axis=-1, keepdims=True)
    zc = z - mu
    var = jnp.mean(zc * zc, axis=-1, keepdims=True)
    return zc * lax.rsqrt(var + LN_EPS) * g + b


def _silu(x):
    return x * (1.0 / (1.0 + jnp.exp(-x)))


def _inproj_kernel(x_ref, w_ref, wf_ref, up_ref, q_ref, k_ref, v_ref, qm_ref, f_ref, *, pw, fw, mw, scale):
    xb = x_ref[...].astype(BF16)

    def proj(lo, hi):
        return jnp.dot(xb, w_ref[:, lo:hi], preferred_element_type=F32)

    up_ref[...] = proj(0, pw)
    q_ref[...] = (proj(pw, pw + fw) * (scale * LOG2E)).astype(BF16)
    k_ref[...] = proj(pw + fw, pw + 2 * fw).astype(BF16)
    v_ref[...] = proj(pw + 2 * fw, pw + 3 * fw).astype(BF16)
    qm_ref[...] = (proj(pw + 3 * fw, pw + 3 * fw + mw) * scale).astype(BF16)
    f_ref[...] = jnp.dot(xb, wf_ref[...], preferred_element_type=F32)


def _inproj(x2, w_main, w_f, pw, fw, mw, tm=512):
    T, D = x2.shape
    kern = functools.partial(_inproj_kernel, pw=pw, fw=fw, mw=mw, scale=HEAD_DIM ** -0.5)
    row = lambda i: (i, 0)
    fixed = lambda i: (0, 0)
    return pl.pallas_call(
        kern,
        grid=(T // tm,),
        in_specs=[pl.BlockSpec((tm, D), row),
                  pl.BlockSpec(w_main.shape, fixed),
                  pl.BlockSpec(w_f.shape, fixed)],
        out_specs=[pl.BlockSpec((tm, pw), row), pl.BlockSpec((tm, fw), row), pl.BlockSpec((tm, fw), row),
                   pl.BlockSpec((tm, fw), row), pl.BlockSpec((tm, mw), row), pl.BlockSpec((tm, LANES), row)],
        out_shape=[jax.ShapeDtypeStruct((T, pw), F32), jax.ShapeDtypeStruct((T, fw), BF16),
                   jax.ShapeDtypeStruct((T, fw), BF16), jax.ShapeDtypeStruct((T, fw), BF16),
                   jax.ShapeDtypeStruct((T, mw), BF16), jax.ShapeDtypeStruct((T, LANES), F32)],
        compiler_params=_cparams(("parallel",)),
        name="inproj",
    )(x2, w_main, w_f)


def _pool_kernel(u_ref, wbd_ref, sc_ref, o_ref, ext_sc, *, chunk, group):
    S, W = u_ref.shape
    ext_sc[0:POOL_HALO, :] = jnp.zeros((POOL_HALO, W), F32)
    ext_sc[POOL_HALO:, :] = u_ref[...]
    rows = chunk + POOL_HALO
    lrow = lax.broadcasted_iota(I32, (rows, W), 0)
    lane = lax.broadcasted_iota(I32, (rows, W), 1)

    def body(c, carry):
        start = pl.multiple_of(c * chunk, chunk)
        e = ext_sc[pl.ds(start, rows), :]
        posf = (lrow + (start - POOL_HALO + 1)).astype(F32)
        acc = e
        d = jnp.zeros_like(e)
        shift = 1
        for g, w in enumerate(POOL_WINDOWS):
            while shift < w:
                acc = acc + pltpu.roll(acc, shift, axis=0)
                shift *= 2
            mean = acc / jnp.minimum(posf, float(w))
            d = jnp.where((lane >= g * group) & (lane < (g + 1) * group), mean, d)
        d = (d - e)[POOL_HALO:, :]
        y = jnp.dot(d.astype(BF16), wbd_ref[...], preferred_element_type=F32) * sc_ref[...]
        o_ref[pl.ds(start, chunk), :] = y.astype(BF16)
        return carry

    lax.fori_loop(0, S // chunk, body, 0)


def _pool(u, wbd, pscale, B, S, chunk=512):
    T, W = u.shape
    kern = functools.partial(_pool_kernel, chunk=chunk, group=W // len(POOL_WINDOWS))
    return pl.pallas_call(
        kern,
        grid=(B,),
        in_specs=[pl.BlockSpec((S, W), lambda b: (b, 0)),
                  pl.BlockSpec((W, W), lambda b: (0, 0)),
                  pl.BlockSpec((1, W), lambda b: (0, 0))],
        out_specs=pl.BlockSpec((S, W), lambda b: (b, 0)),
        out_shape=jax.ShapeDtypeStruct((T, W), BF16),
        scratch_shapes=[pltpu.VMEM((S + POOL_HALO, W), F32)],
        compiler_params=_cparams(("parallel",)),
        name="pool_mixer",
    )(u, wbd, pscale)


def _fgate_kernel(f_ref, bf_ref, o_ref):
    S = f_ref.shape[0]
    z = f_ref[...] + bf_ref[...]
    x = jnp.minimum(z, 0.0) - jnp.log(1.0 + jnp.exp(-jnp.abs(z)))
    row = lax.broadcasted_iota(I32, x.shape, 0)
    lane = lax.broadcasted_iota(I32, x.shape, 1)
    shift = 1
    while shift < S:
        x = x + jnp.where(row >= shift, pltpu.roll(x, shift, axis=0), 0.0)
        shift *= 2
    x = x * LOG2E
    hi = x.astype(BF16).astype(F32)
    mid = (x - hi).astype(BF16).astype(F32)
    lo = (x - hi) - mid
    for p in range(FOX_HEADS // 2):
        out = jnp.zeros(x.shape, F32)
        for hh in range(2):
            c = 2 * p + hh
            for j, part in enumerate((hi, mid, lo)):
                out = jnp.where(lane == BIAS_TERMS * hh + j, -part[:, c:c + 1], out)
        o_ref[0, p] = out.astype(BF16)


def _fgate(f_logit, bf_pad, B, S):
    return pl.pallas_call(
        _fgate_kernel,
        grid=(B,),
        in_specs=[pl.BlockSpec((S, LANES), lambda b: (b, 0)),
                  pl.BlockSpec((1, LANES), lambda b: (0, 0))],
        out_specs=pl.BlockSpec((1, FOX_HEADS // 2, S, LANES), lambda b: (b, 0, 0, 0)),
        out_shape=jax.ShapeDtypeStruct((B, FOX_HEADS // 2, S, LANES), BF16),
        compiler_params=_cparams(("parallel",)),
        name="forget_cumsum",
    )(f_logit, bf_pad)


def _fox_kernel(q_ref, k_ref, a_ref, v_ref, o_ref, vt_sc, sa_sc, sb_sc,
                qa0_sc, m0_sc, l0_sc, acc0_sc, qa1_sc, m1_sc, l1_sc, acc1_sc, *, blk):
    j = pl.program_id(2)
    nk = vt_sc.shape[0]
    nt = (((1,), (1,)), ((), ()))
    sets = ((qa0_sc, m0_sc, l0_sc, acc0_sc), (qa1_sc, m1_sc, l1_sc, acc1_sc))

    @pl.when(j == 0)
    def _():
        for c in range(nk):
            vt_sc[c] = v_ref[c * blk:(c + 1) * blk, :].astype(F32).T.astype(BF16)

    def prepare(which, i):
        qa_sc, m_sc, l_sc, acc_sc = sets[which]
        q = q_ref[i * blk:(i + 1) * blk, :].astype(F32)
        lane = lax.broadcasted_iota(I32, q.shape, 1)
        for h in range(2):
            mine = (lane >= h * HEAD_DIM) & (lane < (h + 1) * HEAD_DIM)
            bias_rows = (lane >= h * BIAS_TERMS) & (lane < (h + 1) * BIAS_TERMS)
            qa_sc[h] = jnp.concatenate(
                [jnp.where(mine, q, 0.0), jnp.where(bias_rows, 1.0, 0.0)], axis=1).astype(BF16)
        m_sc[...] = jnp.full(m_sc.shape, -jnp.inf, F32)
        l_sc[...] = jnp.zeros(l_sc.shape, F32)
        acc_sc[...] = jnp.zeros(acc_sc.shape, F32)

    def scores(ki, dst, which):
        qa_sc = sets[which][0]
        ks = pl.multiple_of(ki * blk, blk)
        kb = jnp.concatenate([k_ref[pl.ds(ks, blk), :], a_ref[0, 0, pl.ds(ks, blk), :]], axis=1)
        for h in range(2):
            dst[h] = lax.dot_general(kb, qa_sc[h], nt, preferred_element_type=F32)

    def absorb(ki, src, causal, which):
        _, m_sc, l_sc, acc_sc = sets[which]
        for h in range(2):
            st = src[h]
            if causal:
                kpos = lax.broadcasted_iota(I32, st.shape, 0)
                qpos = lax.broadcasted_iota(I32, st.shape, 1)
                st = jnp.where(kpos <= qpos, st, -jnp.inf)
            m_prev = m_sc[h]
            m_new = jnp.maximum(m_prev, jnp.max(st, axis=0, keepdims=True))
            alpha = jnp.exp2(m_prev - m_new)
            p = jnp.exp2(st - m_new)
            l_sc[h] = alpha * l_sc[h] + jnp.sum(p, axis=0, keepdims=True)
            m_sc[h] = m_new
            vt = vt_sc[ki, h * HEAD_DIM:(h + 1) * HEAD_DIM, :]
            pv = jnp.dot(vt, p.astype(BF16), preferred_element_type=F32)
            rows = slice(h * HEAD_DIM, (h + 1) * HEAD_DIM)
            acc_sc[rows, :] = acc_sc[rows, :] * alpha + pv

    def finish(which, i):
        _, _, l_sc, acc_sc = sets[which]
        out_t = jnp.concatenate([acc_sc[h * HEAD_DIM:(h + 1) * HEAD_DIM, :] / l_sc[h] for h in range(2)], axis=0)
        o_ref[i * blk:(i + 1) * blk, :] = out_t.T.astype(BF16)

    def walk(first, other, which, n_pairs):
        def body(p, carry):
            scores(2 * p + 1, other, which)
            absorb(2 * p, first, False, which)
            scores(2 * p + 2, first, which)
            absorb(2 * p + 1, other, False, which)
            return carry

        lax.fori_loop(0, n_pairs, body, 0)

    per_step = q_ref.shape[0] // blk
    first, other = sa_sc, sb_sc
    tail = None
    for i in range(per_step):
        which = i % 2
        qi = per_step * j + i
        prepare(which, i)
        scores(0, first, which)
        if tail is not None:
            tail()
        walk(first, other, which, (per_step // 2) * j + i // 2)
        if i % 2 == 0:
            tail = functools.partial(lambda q, buf, w, ii: (absorb(q, buf, True, w), finish(w, ii)),
                                     qi, first, which, i)
            first, other = other, first
        else:
            scores(qi, other, which)
            absorb(qi - 1, first, False, which)
            tail = functools.partial(lambda q, buf, w, ii: (absorb(q, buf, True, w), finish(w, ii)),
                                     qi, other, which, i)
    tail()


def _fox(q, k, v, fbias, B, S, blk=512):
    T, FW = q.shape
    nq = S // blk
    pairs = FW // LANES
    per = math.gcd(ATTN_QBLOCKS, nq)
    assert per % 2 == 0
    kern = functools.partial(_fox_kernel, blk=blk)
    return pl.pallas_call(
        kern,
        grid=(B, pairs, nq // per),
        in_specs=[pl.BlockSpec((per * blk, LANES), lambda b, p, i: (b * (nq // per) + i, p)),
                  pl.BlockSpec((S, LANES), lambda b, p, i: (b, p)),
                  pl.BlockSpec((1, 1, S, LANES), lambda b, p, i: (b, p, 0, 0)),
                  pl.BlockSpec((S, LANES), lambda b, p, i: (b, p))],
        out_specs=pl.BlockSpec((per * blk, LANES), lambda b, p, i: (b * (nq // per) + i, p)),
        out_shape=jax.ShapeDtypeStruct((T, FW), BF16),
        scratch_shapes=[pltpu.VMEM((nq, LANES, blk), BF16),
                        pltpu.VMEM((2, blk, blk), F32), pltpu.VMEM((2, blk, blk), F32)] + 2 * [
                            pltpu.VMEM((2, blk, 2 * LANES), BF16), pltpu.VMEM((2, 1, blk), F32),
                            pltpu.VMEM((2, 1, blk), F32), pltpu.VMEM((LANES, blk), F32)],
        compiler_params=_cparams(("parallel", "parallel", "arbitrary")),
        name="fox_attention",
    )(q, k, fbias, v)


def _memattn_kernel(qm_ref, mem_ref, wkv_ref, o_ref, k_sc, v_sc):
    MW = qm_ref.shape[1]

    @pl.when(pl.program_id(1) == 0)
    def _():
        kv = jnp.dot(mem_ref[0].astype(BF16), wkv_ref[...], preferred_element_type=F32)
        k_sc[...] = kv[:, :MW].astype(BF16)
        v_sc[...] = kv[:, MW:].astype(BF16)

    q = qm_ref[...]
    lane = lax.broadcasted_iota(I32, q.shape, 1)
    hd = MW // MEM_HEADS
    out = jnp.zeros(q.shape, F32)
    for h in range(MEM_HEADS):
        mine = (lane >= h * hd) & (lane < (h + 1) * hd)
        qh = jnp.where(mine, q, jnp.zeros_like(q))
        s = lax.dot_general(qh, k_sc[...], (((1,), (1,)), ((), ())), preferred_element_type=F32)
        p = jnp.exp(s - jnp.max(s, axis=1, keepdims=True))
        l = jnp.sum(p, axis=1, keepdims=True)
        o = jnp.dot(p.astype(BF16), v_sc[...], preferred_element_type=F32)
        out = jnp.where(mine, o / l, out)
    o_ref[...] = out.astype(BF16)


def _memattn(qm, mem, wkv, B, S, tq=512):
    T, MW = qm.shape
    M, D = mem.shape[1], mem.shape[2]
    nq = S // tq
    return pl.pallas_call(
        _memattn_kernel,
        grid=(B, nq),
        in_specs=[pl.BlockSpec((tq, MW), lambda b, i: (b * nq + i, 0)),
                  pl.BlockSpec((1, M, D), lambda b, i: (b, 0, 0)),
                  pl.BlockSpec(wkv.shape, lambda b, i: (0, 0))],
        out_specs=pl.BlockSpec((tq, MW), lambda b, i: (b * nq + i, 0)),
        out_shape=jax.ShapeDtypeStruct((T, MW), BF16),
        scratch_shapes=[pltpu.VMEM((M, MW), BF16), pltpu.VMEM((M, MW), BF16)],
        compiler_params=_cparams(("parallel", "arbitrary")),
        name="memory_attention",
    )(qm, mem, wkv)


PACK_SUB = 4


def _pack_rows(ref, val):
    n = val.shape[0]
    half = val.shape[1] // 2
    words = pltpu.pack_elementwise([val[:, :half], val[:, half:]], packed_dtype=BF16)
    for j in range(PACK_SUB):
        ref[pl.ds(j, n, stride=PACK_SUB), :] = words[:, j * LANES:(j + 1) * LANES]


def _unpack_words(slabs):
    lo = [pltpu.unpack_elementwise(s, index=0, packed_dtype=BF16, unpacked_dtype=F32) for s in slabs]
    hi = [pltpu.unpack_elementwise(s, index=1, packed_dtype=BF16, unpacked_dtype=F32) for s in slabs]
    return jnp.concatenate(lo + hi, axis=1)


def _unpack_rows(ref, n, dtype=BF16):
    return _unpack_words([ref[pl.ds(j, n, stride=PACK_SUB), :] for j in range(PACK_SUB)]).astype(dtype)


def _packed_row(ref, r):
    return ref.at[pl.ds(pl.multiple_of(r * PACK_SUB, PACK_SUB), PACK_SUB), :]


def _outproj_kernel(yp_ref, yf_ref, ym_ref, x_ref, w_ref, g_ref, b_ref, o_ref, pk_ref, *, alpha):
    pw, fw = yp_ref.shape[1], yf_ref.shape[1]
    h = jnp.dot(yp_ref[...], w_ref[0:pw, :], preferred_element_type=F32)
    h = h + jnp.dot(yf_ref[...], w_ref[pw:pw + fw, :], preferred_element_type=F32)
    h = h + jnp.dot(ym_ref[...], w_ref[pw + fw:, :], preferred_element_type=F32)
    y = _layer_norm(alpha * x_ref[...] + h, g_ref[...], b_ref[...])
    o_ref[...] = y
    _pack_rows(pk_ref, y)


def _outproj(yp, yf, ym, x2, w_out, g, b, alpha, tm=512):
    T, D = x2.shape
    assert D == 2 * PACK_SUB * LANES
    row = lambda i: (i, 0)
    fixed = lambda i: (0, 0)
    return pl.pallas_call(
        functools.partial(_outproj_kernel, alpha=alpha),
        grid=(T // tm,),
        in_specs=[pl.BlockSpec((tm, yp.shape[1]), row), pl.BlockSpec((tm, yf.shape[1]), row),
                  pl.BlockSpec((tm, ym.shape[1]), row), pl.BlockSpec((tm, D), row),
                  pl.BlockSpec(w_out.shape, fixed), pl.BlockSpec((1, D), fixed), pl.BlockSpec((1, D), fixed)],
        out_specs=[pl.BlockSpec((tm, D), row),
                   pl.BlockSpec((tm * PACK_SUB, LANES), row)],
        out_shape=[jax.ShapeDtypeStruct((T, D), F32),
                   jax.ShapeDtypeStruct((T * PACK_SUB, LANES), PACKED)],
        compiler_params=_cparams(("parallel",)),
        name="outproj_ln1",
    )(yp, yf, ym, x2, w_out, g, b)


ROUTE_TILE = 512


def _per_expert_column(v, dtype=F32):
    return jnp.broadcast_to(v.astype(dtype)[:, None], (v.shape[0], ROUTE_TILE))


def _router_kernel(x_ref, wh_ref, wl_ref, bias_ref, eidx_ref, gate_ref, rank_ref, cnt_ref, carry_sc, *, tt):
    E = wh_ref.shape[0]
    gsz = E // N_GROUPS
    ninf = -jnp.inf

    @pl.when(pl.program_id(0) == 0)
    def _():
        carry_sc[...] = jnp.zeros(carry_sc.shape, F32)

    x = x_ref[...]
    xh = x.astype(BF16)
    xl = (x - xh.astype(F32)).astype(BF16)
    wh = wh_ref[...]
    nt = (((1,), (1,)), ((), ()))
    logits = lax.dot_general(wh, xh, nt, preferred_element_type=F32) + (
        lax.dot_general(wl_ref[...], xh, nt, preferred_element_type=F32)
        + lax.dot_general(wh, xl, nt, preferred_element_type=F32))
    scores = 1.0 / (1.0 + jnp.exp(-logits))
    biased = scores + bias_ref[...]
    eiota = lax.broadcasted_iota(I32, (E, tt), 0).astype(F32)
    giota = lax.broadcasted_iota(I32, (gsz, tt), 0).astype(F32)

    def cmax(a):
        return jnp.max(a, axis=0, keepdims=True)

    def first_at(a, m, iota, n):
        return jnp.min(jnp.where(a == m, iota, float(n)), axis=0, keepdims=True)

    groups = [biased[g * gsz:(g + 1) * gsz, :] for g in range(N_GROUPS)]
    gscore = []
    for blk in groups:
        m1 = cmax(blk)
        m2 = cmax(jnp.where(giota == first_at(blk, m1, giota, gsz), ninf, blk))
        gscore.append(m1 + m2)

    kept = []
    for g in range(N_GROUPS):
        ahead = jnp.zeros((1, tt), F32)
        for o in range(N_GROUPS):
            if o == g:
                continue
            beats = gscore[o] > gscore[g]
            if o < g:
                beats = beats | (gscore[o] == gscore[g])
            ahead = ahead + jnp.where(beats, 1.0, 0.0)
        kept.append(groups[g] + jnp.where(ahead < float(TOPK_GROUPS), 0.0, ninf))
    masked = jnp.concatenate(kept, axis=0)

    picks, sels = [], []
    chosen = jnp.zeros((E, tt), F32)
    for _ in range(TOP_K):
        ik = first_at(masked, cmax(masked), eiota, E)
        oh = eiota == ik
        picks.append(ik)
        sels.append(jnp.sum(jnp.where(oh, scores, 0.0), axis=0, keepdims=True))
        chosen = jnp.where(oh, 1.0, chosen)
        masked = jnp.where(oh, ninf, masked)
    denom = sels[0]
    for sk in sels[1:]:
        denom = denom + sk

    r = lax.broadcasted_iota(I32, (tt, tt), 0)
    c = lax.broadcasted_iota(I32, (tt, tt), 1)
    earlier = jnp.where(r < c, 1.0, 0.0).astype(BF16)
    chosen_b = chosen.astype(BF16)
    pos = carry_sc[...] + jnp.dot(chosen_b, earlier, preferred_element_type=F32)
    carry_sc[...] = carry_sc[...] + jnp.dot(chosen_b, jnp.ones((tt, tt), BF16), preferred_element_type=F32)
    cnt_ref[...] = carry_sc[...]

    ranks = [jnp.sum(jnp.where(eiota == ik, pos, 0.0), axis=0, keepdims=True) for ik in picks]
    eidx_ref[...] = jnp.concatenate(picks, axis=0).astype(I32)
    gate_ref[...] = jnp.concatenate([sk / denom * ROUTED_SCALE for sk in sels], axis=0)
    rank_ref[...] = jnp.concatenate(ranks, axis=0).astype(I32)


def _router(x1, wrt_hi, wrt_lo, rbias):
    T = x1.shape[0]
    E, D = wrt_hi.shape
    tt = ROUTE_TILE
    tok = lambda i: (0, i)
    fixed = lambda i: (0, 0)
    return pl.pallas_call(
        functools.partial(_router_kernel, tt=tt),
        grid=(T // tt,),
        in_specs=[pl.BlockSpec((tt, D), lambda i: (i, 0)), pl.BlockSpec((E, D), fixed),
                  pl.BlockSpec((E, D), fixed), pl.BlockSpec((E, tt), fixed)],
        out_specs=[pl.BlockSpec((TOP_K, tt), tok), pl.BlockSpec((TOP_K, tt), tok), pl.BlockSpec((TOP_K, tt), tok),
                   pl.BlockSpec((E, tt), fixed)],
        out_shape=[jax.ShapeDtypeStruct((TOP_K, T), I32), jax.ShapeDtypeStruct((TOP_K, T), F32),
                   jax.ShapeDtypeStruct((TOP_K, T), I32), jax.ShapeDtypeStruct((E, tt), F32)],
        scratch_shapes=[pltpu.VMEM((E, tt), F32)],
        compiler_params=_cparams(("arbitrary",)),
        name="router_topk",
    )(x1, wrt_hi, wrt_lo, rbias)


def _dest_kernel(e_ref, r_ref, sp_ref, d_ref):
    E = sp_ref.shape[0]
    tt = e_ref.shape[1]
    eiota = lax.broadcasted_iota(I32, (E, tt), 0)
    sp = sp_ref[...]
    rows = []
    for kk in range(TOP_K):
        start = jnp.sum(jnp.where(eiota == e_ref[kk:kk + 1, :], sp, 0.0), axis=0, keepdims=True)
        rows.append(start.astype(I32) + r_ref[kk:kk + 1, :])
    d_ref[...] = jnp.concatenate(rows, axis=0)


def _dest(eidx, rank, start_pad_b):
    K, T = eidx.shape
    tt = ROUTE_TILE
    tok = lambda i: (0, i)
    return pl.pallas_call(
        _dest_kernel,
        grid=(T // tt,),
        in_specs=[pl.BlockSpec((K, tt), tok), pl.BlockSpec((K, tt), tok),
                  pl.BlockSpec(start_pad_b.shape, lambda i: (0, 0))],
        out_specs=pl.BlockSpec((K, tt), tok),
        out_shape=jax.ShapeDtypeStruct((K, T), I32),
        compiler_params=_cparams(("parallel",)),
        name="moe_dest",
    )(eidx, rank, start_pad_b)


def _dispatch_kernel(zb_ref, d_ref, x_ref, wsg_ref, wsu_ref, wsd_ref, xs_ref, sh_ref, zero_sc, sem, *, tt, n_exp):
    blk_rows = ROW_BLOCK * PACK_SUB

    @pl.when(pl.program_id(0) == 0)
    def _():
        zero_sc[...] = jnp.zeros(zero_sc.shape, PACKED)

        def zstart(e, c):
            @pl.when(zb_ref[e] >= 0)
            def _():
                dst = xs_ref.at[pl.ds(pl.multiple_of(zb_ref[e] * PACK_SUB, blk_rows), blk_rows), :]
                pltpu.make_async_copy(zero_sc, dst, sem).start()
            return c

        def zwait(e, c):
            @pl.when(zb_ref[e] >= 0)
            def _():
                pltpu.make_async_copy(zero_sc, xs_ref.at[pl.ds(0, blk_rows), :], sem).wait()
            return c

        lax.fori_loop(0, n_exp, zstart, 0)
        lax.fori_loop(0, n_exp, zwait, 0)

    def start(j, c):
        src = _packed_row(x_ref, j)
        for kk in range(TOP_K):
            pltpu.make_async_copy(src, _packed_row(xs_ref, d_ref[kk, j]), sem).start(priority=kk % 2)
        return c

    lax.fori_loop(0, tt, start, 0)
    xb = _unpack_rows(x_ref, tt)
    hs = _silu(jnp.dot(xb, wsg_ref[...], preferred_element_type=F32)) * jnp.dot(
        xb, wsu_ref[...], preferred_element_type=F32)
    sh_ref[...] = jnp.dot(hs.astype(BF16), wsd_ref[...], preferred_element_type=F32)
    for kk in range(TOP_K):
        pltpu.make_async_copy(x_ref, xs_ref.at[pl.ds(0, tt * PACK_SUB), :], sem).wait()


def _dispatch(zero_blk, dest, xpk, wsg, wsu, wsd, n_rows, tt=2048):
    T = xpk.shape[0] // PACK_SUB
    D = wsg.shape[0]
    n_exp = zero_blk.shape[0]
    fixed = lambda i, zb: (0, 0)
    grid_spec = pltpu.PrefetchScalarGridSpec(
        num_scalar_prefetch=1,
        grid=(T // tt,),
        in_specs=[pl.BlockSpec((TOP_K, tt), lambda i, zb: (0, i), memory_space=pltpu.SMEM),
                  pl.BlockSpec((tt * PACK_SUB, LANES), lambda i, zb: (i, 0)),
                  pl.BlockSpec(wsg.shape, fixed), pl.BlockSpec(wsu.shape, fixed), pl.BlockSpec(wsd.shape, fixed)],
        out_specs=[pl.BlockSpec(memory_space=pl.ANY), pl.BlockSpec((tt, D), lambda i, zb: (i, 0))],
        scratch_shapes=[pltpu.VMEM((ROW_BLOCK * PACK_SUB, LANES), PACKED), pltpu.SemaphoreType.DMA],
    )
    return pl.pallas_call(
        functools.partial(_dispatch_kernel, tt=tt, n_exp=n_exp),
        grid_spec=grid_spec,
        out_shape=[jax.ShapeDtypeStruct((n_rows * PACK_SUB, LANES), PACKED), jax.ShapeDtypeStruct((T, D), F32)],
        compiler_params=_cparams(("arbitrary",)),
        name="moe_dispatch",
    )(zero_blk, dest, xpk, wsg, wsu, wsd)


def _expert_kernel(sb_ref, nb_ref, nu_ref, wg_ref, wu_ref, wd_ref, xs_ref, y_ref,
                   xbuf, ybuf, wg_sc, wu_sc, wd_sc, isem, osem):
    e = pl.program_id(0)
    nb = nb_ref[e]
    base = sb_ref[e]
    n_used = nu_ref[0]
    in_rows = out_rows = ROW_BLOCK * PACK_SUB

    def fetch(blk, slot):
        src = xs_ref.at[pl.ds(pl.multiple_of(blk * in_rows, in_rows), in_rows), :]
        pltpu.make_async_copy(src, xbuf.at[slot], isem.at[slot]).start()

    def wait_in(slot):
        pltpu.make_async_copy(xs_ref.at[pl.ds(0, in_rows), :], xbuf.at[slot], isem.at[slot]).wait()

    def store(blk, slot):
        dst = y_ref.at[pl.ds(pl.multiple_of(blk * out_rows, out_rows), out_rows), :]
        pltpu.make_async_copy(ybuf.at[slot], dst, osem.at[slot]).start()

    def wait_out(slot):
        pltpu.make_async_copy(ybuf.at[slot], y_ref.at[pl.ds(0, out_rows), :], osem.at[slot]).wait()

    @pl.when(nb > 0)
    def _():
        @pl.when(base == 0)
        def _():
            for g0 in range(EXPERT_AHEAD):
                @pl.when(g0 < n_used)
                def _():
                    fetch(g0, g0 % EXPERT_BUFS)

        wg_sc[...] = wg_ref[0].astype(BF16)
        wu_sc[...] = wu_ref[0].astype(BF16)
        wd_sc[...] = wd_ref[0].astype(BF16)

        def run(blocks):
            slots = [g % EXPERT_BUFS for g in blocks]
            for g, slot in zip(blocks, slots):
                wait_in(slot)

                @pl.when(g + EXPERT_AHEAD < n_used)
                def _():
                    fetch(g + EXPERT_AHEAD, (g + EXPERT_AHEAD) % EXPERT_BUFS)

                @pl.when(g >= EXPERT_BUFS)
                def _():
                    wait_out(slot)

            for g, slot in zip(blocks, slots):
                xb = _unpack_rows(xbuf.at[slot], ROW_BLOCK)
                a = jnp.dot(xb, wg_sc[...], preferred_element_type=F32)
                u = jnp.dot(xb, wu_sc[...], preferred_element_type=F32)
                h = (_silu(a) * u).astype(BF16)
                _pack_rows(ybuf.at[slot], jnp.dot(h, wd_sc[...], preferred_element_type=F32))
            for g, slot in zip(blocks, slots):
                store(g, slot)

        def region(p, c):
            run([base + EXPERT_REGION * p + i for i in range(EXPERT_REGION)])
            return c

        lax.fori_loop(0, nb // EXPERT_REGION, region, 0)
        done = nb // EXPERT_REGION * EXPERT_REGION

        @pl.when(nb % EXPERT_REGION >= 2)
        def _():
            run([base + done, base + done + 1])

        @pl.when(nb % 2 == 1)
        def _():
            run([base + nb - 1])

    @pl.when(e == pl.num_programs(0) - 1)
    def _():
        for slot in range(EXPERT_BUFS):
            @pl.when(slot < n_used)
            def _():
                wait_out(slot)


def _experts(start_blk, n_blk_e, n_used, xs, w_gate, w_up, w_down):
    n_rows = xs.shape[0] // PACK_SUB
    E, D, H = w_gate.shape
    wsel = lambda e, sb, nb, nu: (e, 0, 0)
    grid_spec = pltpu.PrefetchScalarGridSpec(
        num_scalar_prefetch=3,
        grid=(E,),
        in_specs=[pl.BlockSpec((1, D, H), wsel), pl.BlockSpec((1, D, H), wsel), pl.BlockSpec((1, H, D), wsel),
                  pl.BlockSpec(memory_space=pl.ANY)],
        out_specs=pl.BlockSpec(memory_space=pl.ANY),
        scratch_shapes=[pltpu.VMEM((EXPERT_BUFS, ROW_BLOCK * PACK_SUB, LANES), PACKED),
                        pltpu.VMEM((EXPERT_BUFS, ROW_BLOCK * PACK_SUB, LANES), PACKED),
                        pltpu.VMEM((D, H), BF16), pltpu.VMEM((D, H), BF16), pltpu.VMEM((H, D), BF16),
                        pltpu.SemaphoreType.DMA((EXPERT_BUFS,)), pltpu.SemaphoreType.DMA((EXPERT_BUFS,))],
    )
    return pl.pallas_call(
        _expert_kernel,
        grid_spec=grid_spec,
        out_shape=jax.ShapeDtypeStruct((n_rows * PACK_SUB, LANES), PACKED),
        compiler_params=_cparams(("arbitrary",)),
        name="moe_experts",
    )(start_blk, n_blk_e, n_used, w_gate, w_up, w_down, xs)


def _combine_kernel(dcur_ref, dnxt_ref, x_ref, gate_ref, sh_ref, g_ref, b_ref, y_ref, o_ref, buf, sem, *, tt, alpha):
    i = pl.program_id(0)
    n = pl.num_programs(0)

    def fetch(d_ref, slot):
        def body(j, c):
            for kk in range(TOP_K):
                src = _packed_row(y_ref, d_ref[kk * tt + j])
                pltpu.make_async_copy(src, _packed_row(buf.at[slot, kk], j), sem.at[slot]).start(priority=kk % 2)
            return c

        lax.fori_loop(0, tt, body, 0)

    def reduce(slot):
        for kk in range(TOP_K):
            pltpu.make_async_copy(y_ref.at[pl.ds(0, tt * PACK_SUB), :], buf.at[slot, kk], sem.at[slot]).wait()
        x = x_ref[...]
        moe = sh_ref[...]
        gates = gate_ref[...]
        for kk in range(TOP_K):
            moe = moe + gates[:, kk:kk + 1] * _unpack_rows(buf.at[slot, kk], tt, F32)
        o_ref[...] = _layer_norm(alpha * x + moe, g_ref[...], b_ref[...])

    @pl.when(i == 0)
    def _():
        fetch(dcur_ref, 0)

    for slot in range(2):
        @pl.when((i % 2 == slot) & (i + 1 < n))
        def _():
            fetch(dnxt_ref, 1 - slot)

    for slot in range(2):
        @pl.when(i % 2 == slot)
        def _():
            reduce(slot)


def _combine(dest, x1, gates, shared, g, b, y, alpha, tt=256):
    T, D = x1.shape
    n = T // tt
    dest = dest.reshape(TOP_K, n, tt).transpose(1, 0, 2).reshape(n * TOP_K * tt)
    row = lambda i: (i, 0)
    fixed = lambda i: (0, 0)
    return pl.pallas_call(
        functools.partial(_combine_kernel, tt=tt, alpha=alpha),
        grid=(n,),
        in_specs=[pl.BlockSpec((TOP_K * tt,), lambda i: (i,), memory_space=pltpu.SMEM),
                  pl.BlockSpec((TOP_K * tt,), lambda i: (jnp.minimum(i + 1, n - 1),), memory_space=pltpu.SMEM),
                  pl.BlockSpec((tt, D), row), pl.BlockSpec((tt, TOP_K), row),
                  pl.BlockSpec((tt, D), row),
                  pl.BlockSpec((1, D), fixed), pl.BlockSpec((1, D), fixed),
                  pl.BlockSpec(memory_space=pl.ANY)],
        out_specs=pl.BlockSpec((tt, D), row),
        out_shape=jax.ShapeDtypeStruct((T, D), F32),
        scratch_shapes=[pltpu.VMEM((2, TOP_K, tt * PACK_SUB, LANES), PACKED), pltpu.SemaphoreType.DMA((2,))],
        compiler_params=_cparams(("arbitrary",)),
        name="moe_combine_ln2",
    )(dest, dest, x1, gates, shared, g, b, y)


def _layer(x2, mem, B, S, depth, w_in, b_f, w_pool, pool_scale, w_mem_kv, w_out, ln1_g, ln1_b,
           w_router, router_bias, w_gate, w_up, w_down, ws_gate, ws_up, ws_down, ln2_g, ln2_b):
    T, D = x2.shape
    n_win, grp = w_pool.shape[0], w_pool.shape[1]
    pw = n_win * grp
    fw = FOX_HEADS * HEAD_DIM
    mw = w_mem_kv.shape[1] // 2
    E = w_router.shape[1]
    alpha = (2 * depth) ** 0.25

    f_lo = pw + 3 * fw
    w_main = jnp.concatenate([w_in[:, :f_lo], w_in[:, f_lo + FOX_HEADS:]], axis=1).astype(BF16)
    w_f = jnp.pad(w_in[:, f_lo:f_lo + FOX_HEADS], ((0, 0), (0, LANES - FOX_HEADS))).astype(BF16)
    bf_pad = jnp.pad(b_f, (0, LANES - FOX_HEADS)).reshape(1, LANES)
    wbd = jnp.zeros((pw, pw), F32)
    for g in range(n_win):
        wbd = wbd.at[g * grp:(g + 1) * grp, g * grp:(g + 1) * grp].set(w_pool[g])
    wrt = w_router.T
    wrt_hi = wrt.astype(BF16)
    wrt_lo = (wrt - wrt_hi.astype(F32)).astype(BF16)

    u_pool, q, k, v, q_mem, f_logit = _inproj(x2, w_main, w_f, pw, fw, mw)
    y_pool = _pool(u_pool, wbd.astype(BF16), pool_scale.reshape(1, pw), B, S)
    fcum = _fgate(f_logit, bf_pad, B, S)
    y_fox = _fox(q, k, v, fcum, B, S)
    y_mem = _memattn(q_mem, mem, w_mem_kv.astype(BF16), B, S)
    x1, x1_packed = _outproj(y_pool, y_fox, y_mem, x2, w_out.astype(BF16), ln1_g.reshape(1, D),
                             ln1_b.reshape(1, D), alpha)

    eidx, gates, rank, counts = _router(x1, wrt_hi, wrt_lo, _per_expert_column(router_bias))

    cnt = counts[:, 0].astype(I32)
    padded = (cnt + ROW_BLOCK - 1) // ROW_BLOCK * ROW_BLOCK
    pad_end = jnp.cumsum(padded)
    start_pad = (pad_end - padded).astype(I32)
    n_rows = (T * TOP_K + E * (ROW_BLOCK - 1)) // ROW_BLOCK * ROW_BLOCK
    n_used = (pad_end[-1] // ROW_BLOCK).astype(I32)
    zero_blk = jnp.where(padded > 0, pad_end - ROW_BLOCK, -1).astype(I32)
    dest = _dest(eidx, rank, _per_expert_column(start_pad))

    xs, shared = _dispatch(zero_blk, dest, x1_packed, ws_gate.astype(BF16), ws_up.astype(BF16),
                           ws_down.astype(BF16), n_rows)
    y = _experts(start_pad // ROW_BLOCK, (padded // ROW_BLOCK).astype(I32), n_used.reshape(1), xs,
                 w_gate, w_up, w_down)
    return _combine(dest, x1, gates.T, shared, ln2_g.reshape(1, D), ln2_b.reshape(1, D), y, alpha)


def kernel(x, mem, w_in, b_f, w_pool, pool_scale, w_mem_kv, w_out, ln1_g, ln1_b, w_router, router_bias,
           w_gate, w_up, w_down, ws_gate, ws_up, ws_down, ln2_g, ln2_b):
    B, S, D = x.shape
    depth = w_in.shape[0]
    x2 = x.reshape(B * S, D)
    for l in range(depth):
        x2 = _layer(x2, mem, B, S, depth, w_in[l], b_f[l], w_pool[l], pool_scale[l], w_mem_kv[l], w_out[l],
                    ln1_g[l], ln1_b[l], w_router[l], router_bias[l], w_gate[l], w_up[l], w_down[l],
                    ws_gate[l], ws_up[l], ws_down[l], ln2_g[l], ln2_b[l])
    return x2.reshape(B, S, D)
```
